```python
import math
import jax
import jax.numpy as jnp
from jax import lax
import numpy as np

D_MODEL = 1024
BATCH = 8
SEQ = 2048
DEPTH = 2

N_META = 16
BLOCK_Q = 128
MIX_WIDTH = D_MODEL // 2
V_HEAD_DIM = 64
MLA_HEADS = MIX_WIDTH // V_HEAD_DIM
QK_NOPE_DIM = 64
QK_ROPE_DIM = 32
QK_HEAD_DIM = QK_NOPE_DIM + QK_ROPE_DIM
Q_LORA_RANK = 3 * D_MODEL // 8
KV_LORA_RANK = D_MODEL // 4
ROPE_BASE = 10000.0
CONV_WIDTH = MIX_WIDTH
CONV_K = 3
S5_WIDTH = MIX_WIDTH
S5_GROUP = 16
S5_GROUPS = S5_WIDTH // S5_GROUP
S5_STATE = 64
N_BRANCH = 3
D_FF = 128 * ((8 * D_MODEL // 3 + 127) // 128)
ALPHA = (2.0 * DEPTH) ** 0.25
BETA = (8.0 * DEPTH) ** -0.25
LN_EPS = 1e-5
RMS_EPS = 1e-6
IN_SPLITS = (Q_LORA_RANK, KV_LORA_RANK, QK_ROPE_DIM, CONV_WIDTH, CONV_WIDTH, CONV_WIDTH, S5_WIDTH, N_BRANCH * D_MODEL)
D_IN = sum(IN_SPLITS)

kernel_name = "hybrid_mla_shortconv_s5_deepnorm_macaron"


def layer_norm(x, g, b):
    xf = x.astype(jnp.float32)
    mu = jnp.mean(xf, axis=-1, keepdims=True)
    var = jnp.mean(jnp.square(xf - mu), axis=-1, keepdims=True)
    y = (xf - mu) * lax.rsqrt(var + LN_EPS) * g.astype(jnp.float32) + b.astype(jnp.float32)
    return y.astype(x.dtype)


def rms_norm(x, g):
    xf = x.astype(jnp.float32)
    y = xf * lax.rsqrt(jnp.mean(jnp.square(xf), axis=-1, keepdims=True) + RMS_EPS) * g.astype(jnp.float32)
    return y.astype(x.dtype)


def swiglu(x, w_gate, w_up, w_down):
    return (jax.nn.silu(x @ w_gate) * (x @ w_up)) @ w_down


def rope(x, pos):
    d = x.shape[-1]
    inv_freq = ROPE_BASE ** (-jnp.arange(0, d, 2, dtype=jnp.float32) / d)
    ang = pos.astype(jnp.float32)[:, None] * inv_freq[None, :]
    cos = jnp.cos(ang)[None, :, None, :]
    sin = jnp.sin(ang)[None, :, None, :]
    xf = x.astype(jnp.float32)
    x1, x2 = xf[..., : d // 2], xf[..., d // 2:]
    return jnp.concatenate([x1 * cos - x2 * sin, x2 * cos + x1 * sin], axis=-1).astype(x.dtype)


def mla_branch(c_q_raw, c_kv_raw, k_rope_raw, q_norm_g, w_uq, kv_norm_g, w_ukv, w_out):
    B, L, _ = c_q_raw.shape
    pos = jnp.arange(L)
    q = (rms_norm(c_q_raw, q_norm_g) @ w_uq).reshape(B, L, MLA_HEADS, QK_HEAD_DIM)
    kv = (rms_norm(c_kv_raw, kv_norm_g) @ w_ukv).reshape(B, L, MLA_HEADS, QK_NOPE_DIM + V_HEAD_DIM)
    q = jnp.concatenate([q[..., :QK_NOPE_DIM], rope(q[..., QK_NOPE_DIM:], pos)], axis=-1)
    k_rope = rope(k_rope_raw[:, :, None, :], pos)
    k = jnp.concatenate([kv[..., :QK_NOPE_DIM], jnp.broadcast_to(k_rope, (B, L, MLA_HEADS, QK_ROPE_DIM))], axis=-1)
    v = kv[..., QK_NOPE_DIM:]
    pad = (-L) % BLOCK_Q
    padw = ((0, 0), (pad, 0), (0, 0), (0, 0))
    q, k, v = jnp.pad(q, padw), jnp.pad(k, padw), jnp.pad(v, padw)
    n_blocks = (L + pad) // BLOCK_Q
    scale = QK_HEAD_DIM ** -0.5
    outs = []
    for i in range(n_blocks):
        kend = (i + 1) * BLOCK_Q
        qs = q[:, i * BLOCK_Q: kend]
        s = jnp.einsum('bqhd,bkhd->bhqk', qs, k[:, :kend]).astype(jnp.float32) * scale
        qi = i * BLOCK_Q + jnp.arange(BLOCK_Q)
        ki = jnp.arange(kend)
        mask = (ki[None, :] <= qi[:, None]) & (ki[None, :] >= pad)
        s = jnp.where(mask[None, None], s, -1e30)
        p = jax.nn.softmax(s, axis=-1).astype(v.dtype)
        outs.append(jnp.einsum('bhqk,bkhd->bqhd', p, v[:, :kend]))
    o = jnp.concatenate(outs, axis=1)[:, pad:]
    return o.reshape(B, L, MLA_HEADS * V_HEAD_DIM) @ w_out


def short_conv_branch(xbar, b_gate, c_gate, conv_w, conv_b, w_out):
    L = xbar.shape[1]
    u = c_gate * xbar
    up = jnp.pad(u, ((0, 0), (CONV_K - 1, 0), (0, 0)))
    y = conv_b + sum(conv_w[j] * up[:, j: j + L] for j in range(CONV_K))
    return (b_gate * y) @ w_out


def _ssm_combine(e1, e2):
    a1r, a1i, b1r, b1i = e1
    a2r, a2i, b2r, b2i = e2
    ar = a2r * a1r - a2i * a1i
    ai = a2r * a1i + a2i * a1r
    br = a2r * b1r - a2i * b1i + b2r
    bi = a2r * b1i + a2i * b1r + b2i
    return (ar, ai, br, bi)


def s5_branch(u, a_re, a_im, log_dt, b_re, b_im, c_re, c_im, d, w_glu, b_glu, w_out):
    B, L, _ = u.shape
    f32 = jnp.float32
    uf = u.astype(f32).reshape(B, L, S5_GROUPS, S5_GROUP)
    a_re, a_im = a_re.astype(f32), a_im.astype(f32)
    dt = jnp.exp(log_dt.astype(f32))[:, None]
    mag = jnp.exp(dt * a_re)
    ab_re, ab_im = mag * jnp.cos(dt * a_im), mag * jnp.sin(dt * a_im)
    den = a_re * a_re + a_im * a_im
    nr, ni = ab_re - 1.0, ab_im
    coef_re = (nr * a_re + ni * a_im) / den
    coef_im = (ni * a_re - nr * a_im) / den
    b_re, b_im = b_re.astype(f32), b_im.astype(f32)
    bb_re = coef_re[..., None] * b_re - coef_im[..., None] * b_im
    bb_im = coef_re[..., None] * b_im + coef_im[..., None] * b_re
    bu_re = jnp.einsum('gnh,blgh->blgn', bb_re, uf)
    bu_im = jnp.einsum('gnh,blgh->blgn', bb_im, uf)
    shp = (1, L, S5_GROUPS, S5_STATE)
    elems = (jnp.broadcast_to(ab_re[None, None], shp), jnp.broadcast_to(ab_im[None, None], shp), bu_re, bu_im)
    _, _, xr, xi = lax.associative_scan(_ssm_combine, elems, axis=1)
    y = (jnp.einsum('ghn,blgn->blgh', c_re.astype(f32), xr)
         - jnp.einsum('ghn,blgn->blgh', c_im.astype(f32), xi)
         + d.astype(f32) * uf)
    y = jax.nn.gelu(y.reshape(B, L, S5_WIDTH).astype(u.dtype))
    y = y * jax.nn.sigmoid(y @ w_glu + b_glu)
    return y @ w_out


def hybrid_layer(x, ffn1_w_gate, ffn1_w_up, ffn1_w_down, ln1_g, ln1_b, w_in,
                 mla_q_norm_g, mla_w_uq, mla_kv_norm_g, mla_w_ukv, mla_w_o,
                 conv_w, conv_b, conv_w_out,
                 s5_a_re, s5_a_im, s5_log_dt, s5_b_re, s5_b_im, s5_c_re, s5_c_im, s5_d, s5_w_glu, s5_b_glu, s5_w_out,
                 w_o, ln2_g, ln2_b, ffn2_w_gate, ffn2_w_up, ffn2_w_down, ln3_g, ln3_b):
    B, L, _ = x.shape
    x = layer_norm(ALPHA * x + 0.5 * swiglu(x, ffn1_w_gate, ffn1_w_up, ffn1_w_down), ln1_g, ln1_b)
    proj = x @ w_in
    c_q, c_kv, k_rope, xbar, b_gate, c_gate, u_s5, gates = jnp.split(proj, np.cumsum(IN_SPLITS)[:-1].tolist(), axis=-1)
    y_a = mla_branch(c_q, c_kv, k_rope, mla_q_norm_g, mla_w_uq, mla_kv_norm_g, mla_w_ukv, mla_w_o)
    y_b = short_conv_branch(xbar, b_gate, c_gate, conv_w, conv_b, conv_w_out)
    y_c = s5_branch(u_s5, s5_a_re, s5_a_im, s5_log_dt, s5_b_re, s5_b_im, s5_c_re, s5_c_im, s5_d, s5_w_glu, s5_b_glu, s5_w_out)
    g = jax.nn.sigmoid(gates.reshape(B, L, N_BRANCH, D_MODEL))
    mixed = g[:, :, 0] * y_a + g[:, :, 1] * y_b + g[:, :, 2] * y_c
    x = layer_norm(ALPHA * x + mixed @ w_o, ln2_g, ln2_b)
    x = layer_norm(ALPHA * x + 0.5 * swiglu(x, ffn2_w_gate, ffn2_w_up, ffn2_w_down), ln3_g, ln3_b)
    return x


def _fwd_setup_inputs(seed: int = 0) -> dict:
    key = jax.random.key(seed)
    keys = iter(jax.random.split(key, 48))
    f32 = jnp.float32

    def nrm(shape, scale):
        return jax.random.normal(next(keys), shape, f32) * scale

    def gain(shape):
        return 1.0 + nrm(shape, 0.01)

    Dp = DEPTH
    n_idx = jnp.arange(S5_STATE, dtype=f32)
    inp = {}
    inp["x"] = nrm((BATCH, SEQ, D_MODEL), 1.0)
    inp["meta"] = nrm((N_META, D_MODEL), 1.0)
    inp["ffn1_w_gate"] = nrm((Dp, D_MODEL, D_FF), D_MODEL ** -0.5)
    inp["ffn1_w_up"] = nrm((Dp, D_MODEL, D_FF), D_MODEL ** -0.5)
    inp["ffn1_w_down"] = nrm((Dp, D_FF, D_MODEL), BETA * D_FF ** -0.5)
    inp["ln1_g"] = gain((Dp, D_MODEL))
    inp["ln1_b"] = nrm((Dp, D_MODEL), 0.01)
    inp["w_in"] = nrm((Dp, D_MODEL, D_IN), D_MODEL ** -0.5)
    inp["mla_q_norm_g"] = gain((Dp, Q_LORA_RANK))
    inp["mla_w_uq"] = nrm((Dp, Q_LORA_RANK, MLA_HEADS * QK_HEAD_DIM), Q_LORA_RANK ** -0.5)
    inp["mla_kv_norm_g"] = gain((Dp, KV_LORA_RANK))
    inp["mla_w_ukv"] = nrm((Dp, KV_LORA_RANK, MLA_HEADS * (QK_NOPE_DIM + V_HEAD_DIM)), KV_LORA_RANK ** -0.5)
    inp["mla_w_o"] = nrm((Dp, MLA_HEADS * V_HEAD_DIM, D_MODEL), (MLA_HEADS * V_HEAD_DIM) ** -0.5)
    inp["conv_w"] = nrm((Dp, CONV_K, CONV_WIDTH), CONV_K ** -0.5)
    inp["conv_b"] = nrm((Dp, CONV_WIDTH), 0.01)
    inp["conv_w_out"] = nrm((Dp, CONV_WIDTH, D_MODEL), CONV_WIDTH ** -0.5)
    inp["s5_a_re"] = -0.5 * jnp.exp(nrm((Dp, S5_GROUPS, S5_STATE), 0.01))
    inp["s5_a_im"] = math.pi * n_idx + nrm((Dp, S5_GROUPS, S5_STATE), 0.01)
    inp["s5_log_dt"] = jax.random.uniform(next(keys), (Dp, S5_GROUPS), f32, math.log(1e-3), math.log(1e-1))
    inp["s5_b_re"] = nrm((Dp, S5_GROUPS, S5_STATE, S5_GROUP), (2 * S5_GROUP) ** -0.5)
    inp["s5_b_im"] = nrm((Dp, S5_GROUPS, S5_STATE, S5_GROUP), (2 * S5_GROUP) ** -0.5)
    inp["s5_c_re"] = nrm((Dp, S5_GROUPS, S5_GROUP, S5_STATE), (2 * S5_STATE) ** -0.5)
    inp["s5_c_im"] = nrm((Dp, S5_GROUPS, S5_GROUP, S5_STATE), (2 * S5_STATE) ** -0.5)
    inp["s5_d"] = nrm((Dp, S5_GROUPS, S5_GROUP), 1.0)
    inp["s5_w_glu"] = nrm((Dp, S5_WIDTH, S5_WIDTH), S5_WIDTH ** -0.5)
    inp["s5_b_glu"] = nrm((Dp, S5_WIDTH), 0.01)
    inp["s5_w_out"] = nrm((Dp, S5_WIDTH, D_MODEL), S5_WIDTH ** -0.5)
    inp["w_o"] = nrm((Dp, D_MODEL, D_MODEL), BETA * D_MODEL ** -0.5)
    inp["ln2_g"] = gain((Dp, D_MODEL))
    inp["ln2_b"] = nrm((Dp, D_MODEL), 0.01)
    inp["ffn2_w_gate"] = nrm((Dp, D_MODEL, D_FF), D_MODEL ** -0.5)
    inp["ffn2_w_up"] = nrm((Dp, D_MODEL, D_FF), D_MODEL ** -0.5)
    inp["ffn2_w_down"] = nrm((Dp, D_FF, D_MODEL), BETA * D_FF ** -0.5)
    inp["ln3_g"] = gain((Dp, D_MODEL))
    inp["ln3_b"] = nrm((Dp, D_MODEL), 0.01)
    return inp


def _fwd_reference(x, meta, ffn1_w_gate, ffn1_w_up, ffn1_w_down, ln1_g, ln1_b, w_in,
              mla_q_norm_g, mla_w_uq, mla_kv_norm_g, mla_w_ukv, mla_w_o,
              conv_w, conv_b, conv_w_out,
              s5_a_re, s5_a_im, s5_log_dt, s5_b_re, s5_b_im, s5_c_re, s5_c_im, s5_d, s5_w_glu, s5_b_glu, s5_w_out,
              w_o, ln2_g, ln2_b, ffn2_w_gate, ffn2_w_up, ffn2_w_down, ln3_g, ln3_b):
    B = x.shape[0]
    h = jnp.concatenate([jnp.broadcast_to(meta[None].astype(x.dtype), (B, N_META, D_MODEL)), x], axis=1)
    for i in range(DEPTH):
        h = hybrid_layer(h, ffn1_w_gate[i], ffn1_w_up[i], ffn1_w_down[i], ln1_g[i], ln1_b[i], w_in[i],
                         mla_q_norm_g[i], mla_w_uq[i], mla_kv_norm_g[i], mla_w_ukv[i], mla_w_o[i],
                         conv_w[i], conv_b[i], conv_w_out[i],
                         s5_a_re[i], s5_a_im[i], s5_log_dt[i], s5_b_re[i], s5_b_im[i], s5_c_re[i], s5_c_im[i],
                         s5_d[i], s5_w_glu[i], s5_b_glu[i], s5_w_out[i],
                         w_o[i], ln2_g[i], ln2_b[i], ffn2_w_gate[i], ffn2_w_up[i], ffn2_w_down[i], ln3_g[i], ln3_b[i])
    return h[:, N_META:]


import jax as _jax
import jax.numpy as _jnp

TWIN_FORMAT = 'train_step'
FWD_PARAMS = ['x', 'meta', 'ffn1_w_gate', 'ffn1_w_up', 'ffn1_w_down', 'ln1_g', 'ln1_b', 'w_in', 'mla_q_norm_g', 'mla_w_uq', 'mla_kv_norm_g', 'mla_w_ukv', 'mla_w_o', 'conv_w', 'conv_b', 'conv_w_out', 's5_a_re', 's5_a_im', 's5_log_dt', 's5_b_re', 's5_b_im', 's5_c_re', 's5_c_im', 's5_d', 's5_w_glu', 's5_b_glu', 's5_w_out', 'w_o', 'ln2_g', 'ln2_b', 'ffn2_w_gate', 'ffn2_w_up', 'ffn2_w_down', 'ln3_g', 'ln3_b']
TWIN_WEIGHTS = ['meta', 'ffn1_w_gate', 'ffn1_w_up', 'ffn1_w_down', 'ln1_g', 'ln1_b', 'w_in', 'mla_q_norm_g', 'mla_w_uq', 'mla_kv_norm_g', 'mla_w_ukv', 'mla_w_o', 'conv_w', 'conv_b', 'conv_w_out', 's5_a_re', 's5_a_im', 's5_log_dt', 's5_b_re', 's5_b_im', 's5_c_re', 's5_c_im', 's5_d', 's5_w_glu', 's5_b_glu', 's5_w_out', 'w_o', 'ln2_g', 'ln2_b', 'ffn2_w_gate', 'ffn2_w_up', 'ffn2_w_down', 'ln3_g', 'ln3_b']
TWIN_DIFF_INPUT = 'x'
TWIN_INPUTS = ['x', 'meta', 'ffn1_w_gate', 'ffn1_w_up', 'ffn1_w_down', 'ln1_g', 'ln1_b', 'w_in', 'mla_q_norm_g', 'mla_w_uq', 'mla_kv_norm_g', 'mla_w_ukv', 'mla_w_o', 'conv_w', 'conv_b', 'conv_w_out', 's5_a_re', 's5_a_im', 's5_log_dt', 's5_b_re', 's5_b_im', 's5_c_re', 's5_c_im', 's5_d', 's5_w_glu', 's5_b_glu', 's5_w_out', 'w_o', 'ln2_g', 'ln2_b', 'ffn2_w_gate', 'ffn2_w_up', 'ffn2_w_down', 'ln3_g', 'ln3_b', 'loss_target', 'm_meta', 'm_ffn1_w_gate', 'm_ffn1_w_up', 'm_ffn1_w_down', 'm_ln1_g', 'm_ln1_b', 'm_w_in', 'm_mla_q_norm_g', 'm_mla_w_uq', 'm_mla_kv_norm_g', 'm_mla_w_ukv', 'm_mla_w_o', 'm_conv_w', 'm_conv_b', 'm_conv_w_out', 'm_s5_a_re', 'm_s5_a_im', 'm_s5_log_dt', 'm_s5_b_re', 'm_s5_b_im', 'm_s5_c_re', 'm_s5_c_im', 'm_s5_d', 'm_s5_w_glu', 'm_s5_b_glu', 'm_s5_w_out', 'm_w_o', 'm_ln2_g', 'm_ln2_b', 'm_ffn2_w_gate', 'm_ffn2_w_up', 'm_ffn2_w_down', 'm_ln3_g', 'm_ln3_b', 'v_meta', 'v_ffn1_w_gate', 'v_ffn1_w_up', 'v_ffn1_w_down', 'v_ln1_g', 'v_ln1_b', 'v_w_in', 'v_mla_q_norm_g', 'v_mla_w_uq', 'v_mla_kv_norm_g', 'v_mla_w_ukv', 'v_mla_w_o', 'v_conv_w', 'v_conv_b', 'v_conv_w_out', 'v_s5_a_re', 'v_s5_a_im', 'v_s5_log_dt', 'v_s5_b_re', 'v_s5_b_im', 'v_s5_c_re', 'v_s5_c_im', 'v_s5_d', 'v_s5_w_glu', 'v_s5_b_glu', 'v_s5_w_out', 'v_w_o', 'v_ln2_g', 'v_ln2_b', 'v_ffn2_w_gate', 'v_ffn2_w_up', 'v_ffn2_w_down', 'v_ln3_g', 'v_ln3_b']
TWIN_OUTPUTS = ['loss', 'grad_x', 'grad_meta', 'grad_ffn1_w_gate', 'grad_ffn1_w_up', 'grad_ffn1_w_down', 'grad_ln1_g', 'grad_ln1_b', 'grad_w_in', 'grad_mla_q_norm_g', 'grad_mla_w_uq', 'grad_mla_kv_norm_g', 'grad_mla_w_ukv', 'grad_mla_w_o', 'grad_conv_w', 'grad_conv_b', 'grad_conv_w_out', 'grad_s5_a_re', 'grad_s5_a_im', 'grad_s5_log_dt', 'grad_s5_b_re', 'grad_s5_b_im', 'grad_s5_c_re', 'grad_s5_c_im', 'grad_s5_d', 'grad_s5_w_glu', 'grad_s5_b_glu', 'grad_s5_w_out', 'grad_w_o', 'grad_ln2_g', 'grad_ln2_b', 'grad_ffn2_w_gate', 'grad_ffn2_w_up', 'grad_ffn2_w_down', 'grad_ln3_g', 'grad_ln3_b', 'delta_meta', 'delta_ffn1_w_gate', 'delta_ffn1_w_up', 'delta_ffn1_w_down', 'delta_ln1_g', 'delta_ln1_b', 'delta_w_in', 'delta_mla_q_norm_g', 'delta_mla_w_uq', 'delta_mla_kv_norm_g', 'delta_mla_w_ukv', 'delta_mla_w_o', 'delta_conv_w', 'delta_conv_b', 'delta_conv_w_out', 'delta_s5_a_re', 'delta_s5_a_im', 'delta_s5_log_dt', 'delta_s5_b_re', 'delta_s5_b_im', 'delta_s5_c_re', 'delta_s5_c_im', 'delta_s5_d', 'delta_s5_w_glu', 'delta_s5_b_glu', 'delta_s5_w_out', 'delta_w_o', 'delta_ln2_g', 'delta_ln2_b', 'delta_ffn2_w_gate', 'delta_ffn2_w_up', 'delta_ffn2_w_down', 'delta_ln3_g', 'delta_ln3_b', 'new_m_meta', 'new_m_ffn1_w_gate', 'new_m_ffn1_w_up', 'new_m_ffn1_w_down', 'new_m_ln1_g', 'new_m_ln1_b', 'new_m_w_in', 'new_m_mla_q_norm_g', 'new_m_mla_w_uq', 'new_m_mla_kv_norm_g', 'new_m_mla_w_ukv', 'new_m_mla_w_o', 'new_m_conv_w', 'new_m_conv_b', 'new_m_conv_w_out', 'new_m_s5_a_re', 'new_m_s5_a_im', 'new_m_s5_log_dt', 'new_m_s5_b_re', 'new_m_s5_b_im', 'new_m_s5_c_re', 'new_m_s5_c_im', 'new_m_s5_d', 'new_m_s5_w_glu', 'new_m_s5_b_glu', 'new_m_s5_w_out', 'new_m_w_o', 'new_m_ln2_g', 'new_m_ln2_b', 'new_m_ffn2_w_gate', 'new_m_ffn2_w_up', 'new_m_ffn2_w_down', 'new_m_ln3_g', 'new_m_ln3_b', 'new_v_meta', 'new_v_ffn1_w_gate', 'new_v_ffn1_w_up', 'new_v_ffn1_w_down', 'new_v_ln1_g', 'new_v_ln1_b', 'new_v_w_in', 'new_v_mla_q_norm_g', 'new_v_mla_w_uq', 'new_v_mla_kv_norm_g', 'new_v_mla_w_ukv', 'new_v_mla_w_o', 'new_v_conv_w', 'new_v_conv_b', 'new_v_conv_w_out', 'new_v_s5_a_re', 'new_v_s5_a_im', 'new_v_s5_log_dt', 'new_v_s5_b_re', 'new_v_s5_b_im', 'new_v_s5_c_re', 'new_v_s5_c_im', 'new_v_s5_d', 'new_v_s5_w_glu', 'new_v_s5_b_glu', 'new_v_s5_w_out', 'new_v_w_o', 'new_v_ln2_g', 'new_v_ln2_b', 'new_v_ffn2_w_gate', 'new_v_ffn2_w_up', 'new_v_ffn2_w_down', 'new_v_ln3_g', 'new_v_ln3_b']
TWIN_LEAF_KINDS = {'loss': 'loss', 'grad_x': 'grad_x', 'grad_meta': 'grad_w', 'grad_ffn1_w_gate': 'grad_w', 'grad_ffn1_w_up': 'grad_w', 'grad_ffn1_w_down': 'grad_w', 'grad_ln1_g': 'grad_w', 'grad_ln1_b': 'grad_w', 'grad_w_in': 'grad_w', 'grad_mla_q_norm_g': 'grad_w', 'grad_mla_w_uq': 'grad_w', 'grad_mla_kv_norm_g': 'grad_w', 'grad_mla_w_ukv': 'grad_w', 'grad_mla_w_o': 'grad_w', 'grad_conv_w': 'grad_w', 'grad_conv_b': 'grad_w', 'grad_conv_w_out': 'grad_w', 'grad_s5_a_re': 'grad_w', 'grad_s5_a_im': 'grad_w', 'grad_s5_log_dt': 'grad_w', 'grad_s5_b_re': 'grad_w', 'grad_s5_b_im': 'grad_w', 'grad_s5_c_re': 'grad_w', 'grad_s5_c_im': 'grad_w', 'grad_s5_d': 'grad_w', 'grad_s5_w_glu': 'grad_w', 'grad_s5_b_glu': 'grad_w', 'grad_s5_w_out': 'grad_w', 'grad_w_o': 'grad_w', 'grad_ln2_g': 'grad_w', 'grad_ln2_b': 'grad_w', 'grad_ffn2_w_gate': 'grad_w', 'grad_ffn2_w_up': 'grad_w', 'grad_ffn2_w_down': 'grad_w', 'grad_ln3_g': 'grad_w', 'grad_ln3_b': 'grad_w', 'delta_meta': 'delta_w', 'delta_ffn1_w_gate': 'delta_w', 'delta_ffn1_w_up': 'delta_w', 'delta_ffn1_w_down': 'delta_w', 'delta_ln1_g': 'delta_w', 'delta_ln1_b': 'delta_w', 'delta_w_in': 'delta_w', 'delta_mla_q_norm_g': 'delta_w', 'delta_mla_w_uq': 'delta_w', 'delta_mla_kv_norm_g': 'delta_w', 'delta_mla_w_ukv': 'delta_w', 'delta_mla_w_o': 'delta_w', 'delta_conv_w': 'delta_w', 'delta_conv_b': 'delta_w', 'delta_conv_w_out': 'delta_w', 'delta_s5_a_re': 'delta_w', 'delta_s5_a_im': 'delta_w', 'delta_s5_log_dt': 'delta_w', 'delta_s5_b_re': 'delta_w', 'delta_s5_b_im': 'delta_w', 'delta_s5_c_re': 'delta_w', 'delta_s5_c_im': 'delta_w', 'delta_s5_d': 'delta_w', 'delta_s5_w_glu': 'delta_w', 'delta_s5_b_glu': 'delta_w', 'delta_s5_w_out': 'delta_w', 'delta_w_o': 'delta_w', 'delta_ln2_g': 'delta_w', 'delta_ln2_b': 'delta_w', 'delta_ffn2_w_gate': 'delta_w', 'delta_ffn2_w_up': 'delta_w', 'delta_ffn2_w_down': 'delta_w', 'delta_ln3_g': 'delta_w', 'delta_ln3_b': 'delta_w', 'new_m_meta': 'new_m', 'new_m_ffn1_w_gate': 'new_m', 'new_m_ffn1_w_up': 'new_m', 'new_m_ffn1_w_down': 'new_m', 'new_m_ln1_g': 'new_m', 'new_m_ln1_b': 'new_m', 'new_m_w_in': 'new_m', 'new_m_mla_q_norm_g': 'new_m', 'new_m_mla_w_uq': 'new_m', 'new_m_mla_kv_norm_g': 'new_m', 'new_m_mla_w_ukv': 'new_m', 'new_m_mla_w_o': 'new_m', 'new_m_conv_w': 'new_m', 'new_m_conv_b': 'new_m', 'new_m_conv_w_out': 'new_m', 'new_m_s5_a_re': 'new_m', 'new_m_s5_a_im': 'new_m', 'new_m_s5_log_dt': 'new_m', 'new_m_s5_b_re': 'new_m', 'new_m_s5_b_im': 'new_m', 'new_m_s5_c_re': 'new_m', 'new_m_s5_c_im': 'new_m', 'new_m_s5_d': 'new_m', 'new_m_s5_w_glu': 'new_m', 'new_m_s5_b_glu': 'new_m', 'new_m_s5_w_out': 'new_m', 'new_m_w_o': 'new_m', 'new_m_ln2_g': 'new_m', 'new_m_ln2_b': 'new_m', 'new_m_ffn2_w_gate': 'new_m', 'new_m_ffn2_w_up': 'new_m', 'new_m_ffn2_w_down': 'new_m', 'new_m_ln3_g': 'new_m', 'new_m_ln3_b': 'new_m', 'new_v_meta': 'new_v', 'new_v_ffn1_w_gate': 'new_v', 'new_v_ffn1_w_up': 'new_v', 'new_v_ffn1_w_down': 'new_v', 'new_v_ln1_g': 'new_v', 'new_v_ln1_b': 'new_v', 'new_v_w_in': 'new_v', 'new_v_mla_q_norm_g': 'new_v', 'new_v_mla_w_uq': 'new_v', 'new_v_mla_kv_norm_g': 'new_v', 'new_v_mla_w_ukv': 'new_v', 'new_v_mla_w_o': 'new_v', 'new_v_conv_w': 'new_v', 'new_v_conv_b': 'new_v', 'new_v_conv_w_out': 'new_v', 'new_v_s5_a_re': 'new_v', 'new_v_s5_a_im': 'new_v', 'new_v_s5_log_dt': 'new_v', 'new_v_s5_b_re': 'new_v', 'new_v_s5_b_im': 'new_v', 'new_v_s5_c_re': 'new_v', 'new_v_s5_c_im': 'new_v', 'new_v_s5_d': 'new_v', 'new_v_s5_w_glu': 'new_v', 'new_v_s5_b_glu': 'new_v', 'new_v_s5_w_out': 'new_v', 'new_v_w_o': 'new_v', 'new_v_ln2_g': 'new_v', 'new_v_ln2_b': 'new_v', 'new_v_ffn2_w_gate': 'new_v', 'new_v_ffn2_w_up': 'new_v', 'new_v_ffn2_w_down': 'new_v', 'new_v_ln3_g': 'new_v', 'new_v_ln3_b': 'new_v'}


def _forward(args):
    return _fwd_reference(*[args[k] for k in FWD_PARAMS])


def _output_shape():
    out = _jax.eval_shape(lambda: _forward(_fwd_setup_inputs(0)))
    return out.shape, out.dtype

N_MICROBATCH = 1
ADAM_LR = 0.001
ADAM_B1 = 0.9
ADAM_B2 = 0.999
ADAM_EPS = 1e-08
ADAM_WD = 0.01
ADAM_STEP = 10
PER_EXAMPLE_BATCH_AXIS = {'x': 0, 'loss_target': 0}
SHARED_INPUTS = []
_WEIGHT_DTYPES = {'meta': _jnp.float32, 'ffn1_w_gate': _jnp.float32, 'ffn1_w_up': _jnp.float32, 'ffn1_w_down': _jnp.float32, 'ln1_g': _jnp.float32, 'ln1_b': _jnp.float32, 'w_in': _jnp.float32, 'mla_q_norm_g': _jnp.float32, 'mla_w_uq': _jnp.float32, 'mla_kv_norm_g': _jnp.float32, 'mla_w_ukv': _jnp.float32, 'mla_w_o': _jnp.float32, 'conv_w': _jnp.float32, 'conv_b': _jnp.float32, 'conv_w_out': _jnp.float32, 's5_a_re': _jnp.float32, 's5_a_im': _jnp.float32, 's5_log_dt': _jnp.float32, 's5_b_re': _jnp.float32, 's5_b_im': _jnp.float32, 's5_c_re': _jnp.float32, 's5_c_im': _jnp.float32, 's5_d': _jnp.float32, 's5_w_glu': _jnp.float32, 's5_b_glu': _jnp.float32, 's5_w_out': _jnp.float32, 'w_o': _jnp.float32, 'ln2_g': _jnp.float32, 'ln2_b': _jnp.float32, 'ffn2_w_gate': _jnp.float32, 'ffn2_w_up': _jnp.float32, 'ffn2_w_down': _jnp.float32, 'ln3_g': _jnp.float32, 'ln3_b': _jnp.float32}
MOMENT_SCALE = {'meta': 1.346732e-03, 'ffn1_w_gate': 8.748648e-03, 'ffn1_w_up': 8.476085e-03, 'ffn1_w_down': 2.813079e-02, 'ln1_g': 2.603893e-01, 'ln1_b': 1.655241e-01, 'w_in': 1.880089e-02, 'mla_q_norm_g': 6.421094e-03, 'mla_w_uq': 4.566108e-03, 'mla_kv_norm_g': 1.111747e-02, 'mla_w_ukv': 5.450406e-03, 'mla_w_o': 4.318570e-03, 'conv_w': 3.433752e-02, 'conv_b': 3.535607e-02, 'conv_w_out': 2.425857e-02, 's5_a_re': 6.249089e-04, 's5_a_im': 6.713774e-04, 's5_log_dt': 3.427460e-01, 's5_b_re': 4.428750e-04, 's5_b_im': 4.342216e-04, 's5_c_re': 8.462167e-04, 's5_c_im': 8.669128e-04, 's5_d': 1.380641e-02, 's5_w_glu': 3.496247e-03, 's5_b_glu': 5.442566e-03, 's5_w_out': 8.767453e-03, 'w_o': 5.222454e-02, 'ln2_g': 2.699366e-01, 'ln2_b': 1.659061e-01, 'ffn2_w_gate': 8.366583e-03, 'ffn2_w_up': 8.094040e-03, 'ffn2_w_down': 2.683887e-02, 'ln3_g': 1.130982e+01, 'ln3_b': 4.918372e-01}


def _to_microbatches(a, axis):
    t = _jnp.moveaxis(a, axis, 0)
    t = t.reshape((N_MICROBATCH, t.shape[0] // N_MICROBATCH) + t.shape[1:])
    return _jnp.moveaxis(t, 1, axis + 1)


def setup_inputs(seed: int = 0) -> dict:
    inp = _fwd_setup_inputs(seed)
    key = _jax.random.fold_in(_jax.random.key(seed), 7919)
    shape, _ = _output_shape()
    out = dict(inp)
    out["loss_target"] = _jax.random.normal(_jax.random.fold_in(key, 0), shape, _jnp.float32)
    for i, name in enumerate(TWIN_WEIGHTS):
        w = inp[name].astype(_jnp.float32)
        if MOMENT_SCALE is None:
            s = _jnp.sqrt(_jnp.mean(_jnp.square(w)) + 1e-30)
        else:
            s = MOMENT_SCALE[name]
        km, kv = _jax.random.split(_jax.random.fold_in(key, i + 1))
        out[name] = w
        out["m_" + name] = s * _jax.random.normal(km, w.shape, _jnp.float32)
        out["v_" + name] = (s * s) * _jax.random.uniform(kv, w.shape, _jnp.float32, 0.5, 1.5)
    if N_MICROBATCH > 1:
        for name, axis in PER_EXAMPLE_BATCH_AXIS.items():
            out[name] = _to_microbatches(out[name], axis)
    return {'x': out['x'], 'meta': out['meta'], 'ffn1_w_gate': out['ffn1_w_gate'], 'ffn1_w_up': out['ffn1_w_up'], 'ffn1_w_down': out['ffn1_w_down'], 'ln1_g': out['ln1_g'], 'ln1_b': out['ln1_b'], 'w_in': out['w_in'], 'mla_q_norm_g': out['mla_q_norm_g'], 'mla_w_uq': out['mla_w_uq'], 'mla_kv_norm_g': out['mla_kv_norm_g'], 'mla_w_ukv': out['mla_w_ukv'], 'mla_w_o': out['mla_w_o'], 'conv_w': out['conv_w'], 'conv_b': out['conv_b'], 'conv_w_out': out['conv_w_out'], 's5_a_re': out['s5_a_re'], 's5_a_im': out['s5_a_im'], 's5_log_dt': out['s5_log_dt'], 's5_b_re': out['s5_b_re'], 's5_b_im': out['s5_b_im'], 's5_c_re': out['s5_c_re'], 's5_c_im': out['s5_c_im'], 's5_d': out['s5_d'], 's5_w_glu': out['s5_w_glu'], 's5_b_glu': out['s5_b_glu'], 's5_w_out': out['s5_w_out'], 'w_o': out['w_o'], 'ln2_g': out['ln2_g'], 'ln2_b': out['ln2_b'], 'ffn2_w_gate': out['ffn2_w_gate'], 'ffn2_w_up': out['ffn2_w_up'], 'ffn2_w_down': out['ffn2_w_down'], 'ln3_g': out['ln3_g'], 'ln3_b': out['ln3_b'], 'loss_target': out['loss_target'], 'm_meta': out['m_meta'], 'm_ffn1_w_gate': out['m_ffn1_w_gate'], 'm_ffn1_w_up': out['m_ffn1_w_up'], 'm_ffn1_w_down': out['m_ffn1_w_down'], 'm_ln1_g': out['m_ln1_g'], 'm_ln1_b': out['m_ln1_b'], 'm_w_in': out['m_w_in'], 'm_mla_q_norm_g': out['m_mla_q_norm_g'], 'm_mla_w_uq': out['m_mla_w_uq'], 'm_mla_kv_norm_g': out['m_mla_kv_norm_g'], 'm_mla_w_ukv': out['m_mla_w_ukv'], 'm_mla_w_o': out['m_mla_w_o'], 'm_conv_w': out['m_conv_w'], 'm_conv_b': out['m_conv_b'], 'm_conv_w_out': out['m_conv_w_out'], 'm_s5_a_re': out['m_s5_a_re'], 'm_s5_a_im': out['m_s5_a_im'], 'm_s5_log_dt': out['m_s5_log_dt'], 'm_s5_b_re': out['m_s5_b_re'], 'm_s5_b_im': out['m_s5_b_im'], 'm_s5_c_re': out['m_s5_c_re'], 'm_s5_c_im': out['m_s5_c_im'], 'm_s5_d': out['m_s5_d'], 'm_s5_w_glu': out['m_s5_w_glu'], 'm_s5_b_glu': out['m_s5_b_glu'], 'm_s5_w_out': out['m_s5_w_out'], 'm_w_o': out['m_w_o'], 'm_ln2_g': out['m_ln2_g'], 'm_ln2_b': out['m_ln2_b'], 'm_ffn2_w_gate': out['m_ffn2_w_gate'], 'm_ffn2_w_up': out['m_ffn2_w_up'], 'm_ffn2_w_down': out['m_ffn2_w_down'], 'm_ln3_g': out['m_ln3_g'], 'm_ln3_b': out['m_ln3_b'], 'v_meta': out['v_meta'], 'v_ffn1_w_gate': out['v_ffn1_w_gate'], 'v_ffn1_w_up': out['v_ffn1_w_up'], 'v_ffn1_w_down': out['v_ffn1_w_down'], 'v_ln1_g': out['v_ln1_g'], 'v_ln1_b': out['v_ln1_b'], 'v_w_in': out['v_w_in'], 'v_mla_q_norm_g': out['v_mla_q_norm_g'], 'v_mla_w_uq': out['v_mla_w_uq'], 'v_mla_kv_norm_g': out['v_mla_kv_norm_g'], 'v_mla_w_ukv': out['v_mla_w_ukv'], 'v_mla_w_o': out['v_mla_w_o'], 'v_conv_w': out['v_conv_w'], 'v_conv_b': out['v_conv_b'], 'v_conv_w_out': out['v_conv_w_out'], 'v_s5_a_re': out['v_s5_a_re'], 'v_s5_a_im': out['v_s5_a_im'], 'v_s5_log_dt': out['v_s5_log_dt'], 'v_s5_b_re': out['v_s5_b_re'], 'v_s5_b_im': out['v_s5_b_im'], 'v_s5_c_re': out['v_s5_c_re'], 'v_s5_c_im': out['v_s5_c_im'], 'v_s5_d': out['v_s5_d'], 'v_s5_w_glu': out['v_s5_w_glu'], 'v_s5_b_glu': out['v_s5_b_glu'], 'v_s5_w_out': out['v_s5_w_out'], 'v_w_o': out['v_w_o'], 'v_ln2_g': out['v_ln2_g'], 'v_ln2_b': out['v_ln2_b'], 'v_ffn2_w_gate': out['v_ffn2_w_gate'], 'v_ffn2_w_up': out['v_ffn2_w_up'], 'v_ffn2_w_down': out['v_ffn2_w_down'], 'v_ln3_g': out['v_ln3_g'], 'v_ln3_b': out['v_ln3_b']}


def _loss(weights, diff, rest, loss_target):
    with _jax.named_scope("forward"):
        args = {**rest, TWIN_DIFF_INPUT: diff, **{k: w.astype(_WEIGHT_DTYPES[k]) for k, w in weights.items()}}
        y = _forward(args)
    with _jax.named_scope("loss_head"):
        err = _jnp.square(y.astype(_jnp.float32) - loss_target)
        return 0.5 * _jnp.sum(_jnp.mean(err, axis=-1)) if err.ndim else 0.5 * err


def _adamw(w, g, m, v):
    m = ADAM_B1 * m + (1.0 - ADAM_B1) * g
    v = ADAM_B2 * v + (1.0 - ADAM_B2) * _jnp.square(g)
    m_hat = m / (1.0 - ADAM_B1 ** ADAM_STEP)
    v_hat = v / (1.0 - ADAM_B2 ** ADAM_STEP)
    delta = -ADAM_LR * (m_hat / (_jnp.sqrt(v_hat) + ADAM_EPS) + ADAM_WD * w)
    return delta, m, v


def reference(x, meta, ffn1_w_gate, ffn1_w_up, ffn1_w_down, ln1_g, ln1_b, w_in, mla_q_norm_g, mla_w_uq, mla_kv_norm_g, mla_w_ukv, mla_w_o, conv_w, conv_b, conv_w_out, s5_a_re, s5_a_im, s5_log_dt, s5_b_re, s5_b_im, s5_c_re, s5_c_im, s5_d, s5_w_glu, s5_b_glu, s5_w_out, w_o, ln2_g, ln2_b, ffn2_w_gate, ffn2_w_up, ffn2_w_down, ln3_g, ln3_b, loss_target, m_meta, m_ffn1_w_gate, m_ffn1_w_up, m_ffn1_w_down, m_ln1_g, m_ln1_b, m_w_in, m_mla_q_norm_g, m_mla_w_uq, m_mla_kv_norm_g, m_mla_w_ukv, m_mla_w_o, m_conv_w, m_conv_b, m_conv_w_out, m_s5_a_re, m_s5_a_im, m_s5_log_dt, m_s5_b_re, m_s5_b_im, m_s5_c_re, m_s5_c_im, m_s5_d, m_s5_w_glu, m_s5_b_glu, m_s5_w_out, m_w_o, m_ln2_g, m_ln2_b, m_ffn2_w_gate, m_ffn2_w_up, m_ffn2_w_down, m_ln3_g, m_ln3_b, v_meta, v_ffn1_w_gate, v_ffn1_w_up, v_ffn1_w_down, v_ln1_g, v_ln1_b, v_w_in, v_mla_q_norm_g, v_mla_w_uq, v_mla_kv_norm_g, v_mla_w_ukv, v_mla_w_o, v_conv_w, v_conv_b, v_conv_w_out, v_s5_a_re, v_s5_a_im, v_s5_log_dt, v_s5_b_re, v_s5_b_im, v_s5_c_re, v_s5_c_im, v_s5_d, v_s5_w_glu, v_s5_b_glu, v_s5_w_out, v_w_o, v_ln2_g, v_ln2_b, v_ffn2_w_gate, v_ffn2_w_up, v_ffn2_w_down, v_ln3_g, v_ln3_b):
    given = dict(x=x, meta=meta, ffn1_w_gate=ffn1_w_gate, ffn1_w_up=ffn1_w_up, ffn1_w_down=ffn1_w_down, ln1_g=ln1_g, ln1_b=ln1_b, w_in=w_in, mla_q_norm_g=mla_q_norm_g, mla_w_uq=mla_w_uq, mla_kv_norm_g=mla_kv_norm_g, mla_w_ukv=mla_w_ukv, mla_w_o=mla_w_o, conv_w=conv_w, conv_b=conv_b, conv_w_out=conv_w_out, s5_a_re=s5_a_re, s5_a_im=s5_a_im, s5_log_dt=s5_log_dt, s5_b_re=s5_b_re, s5_b_im=s5_b_im, s5_c_re=s5_c_re, s5_c_im=s5_c_im, s5_d=s5_d, s5_w_glu=s5_w_glu, s5_b_glu=s5_b_glu, s5_w_out=s5_w_out, w_o=w_o, ln2_g=ln2_g, ln2_b=ln2_b, ffn2_w_gate=ffn2_w_gate, ffn2_w_up=ffn2_w_up, ffn2_w_down=ffn2_w_down, ln3_g=ln3_g, ln3_b=ln3_b, loss_target=loss_target, m_meta=m_meta, m_ffn1_w_gate=m_ffn1_w_gate, m_ffn1_w_up=m_ffn1_w_up, m_ffn1_w_down=m_ffn1_w_down, m_ln1_g=m_ln1_g, m_ln1_b=m_ln1_b, m_w_in=m_w_in, m_mla_q_norm_g=m_mla_q_norm_g, m_mla_w_uq=m_mla_w_uq, m_mla_kv_norm_g=m_mla_kv_norm_g, m_mla_w_ukv=m_mla_w_ukv, m_mla_w_o=m_mla_w_o, m_conv_w=m_conv_w, m_conv_b=m_conv_b, m_conv_w_out=m_conv_w_out, m_s5_a_re=m_s5_a_re, m_s5_a_im=m_s5_a_im, m_s5_log_dt=m_s5_log_dt, m_s5_b_re=m_s5_b_re, m_s5_b_im=m_s5_b_im, m_s5_c_re=m_s5_c_re, m_s5_c_im=m_s5_c_im, m_s5_d=m_s5_d, m_s5_w_glu=m_s5_w_glu, m_s5_b_glu=m_s5_b_glu, m_s5_w_out=m_s5_w_out, m_w_o=m_w_o, m_ln2_g=m_ln2_g, m_ln2_b=m_ln2_b, m_ffn2_w_gate=m_ffn2_w_gate, m_ffn2_w_up=m_ffn2_w_up, m_ffn2_w_down=m_ffn2_w_down, m_ln3_g=m_ln3_g, m_ln3_b=m_ln3_b, v_meta=v_meta, v_ffn1_w_gate=v_ffn1_w_gate, v_ffn1_w_up=v_ffn1_w_up, v_ffn1_w_down=v_ffn1_w_down, v_ln1_g=v_ln1_g, v_ln1_b=v_ln1_b, v_w_in=v_w_in, v_mla_q_norm_g=v_mla_q_norm_g, v_mla_w_uq=v_mla_w_uq, v_mla_kv_norm_g=v_mla_kv_norm_g, v_mla_w_ukv=v_mla_w_ukv, v_mla_w_o=v_mla_w_o, v_conv_w=v_conv_w, v_conv_b=v_conv_b, v_conv_w_out=v_conv_w_out, v_s5_a_re=v_s5_a_re, v_s5_a_im=v_s5_a_im, v_s5_log_dt=v_s5_log_dt, v_s5_b_re=v_s5_b_re, v_s5_b_im=v_s5_b_im, v_s5_c_re=v_s5_c_re, v_s5_c_im=v_s5_c_im, v_s5_d=v_s5_d, v_s5_w_glu=v_s5_w_glu, v_s5_b_glu=v_s5_b_glu, v_s5_w_out=v_s5_w_out, v_w_o=v_w_o, v_ln2_g=v_ln2_g, v_ln2_b=v_ln2_b, v_ffn2_w_gate=v_ffn2_w_gate, v_ffn2_w_up=v_ffn2_w_up, v_ffn2_w_down=v_ffn2_w_down, v_ln3_g=v_ln3_g, v_ln3_b=v_ln3_b)
    weights = {n: given[n] for n in TWIN_WEIGHTS}
    shared = {n: given[n] for n in SHARED_INPUTS}
    per_example = {n: given[n] for n in ['x']}
    grad_fn = _jax.value_and_grad(_loss, argnums=(0, 1))

    def one_microbatch(ex, loss_target):
        ex = dict(ex)
        diff = ex.pop(TWIN_DIFF_INPUT)
        return grad_fn(weights, diff, {**shared, **ex}, loss_target)

    if N_MICROBATCH == 1:
        loss, (grad_w, grad_x) = one_microbatch(per_example, given["loss_target"])
    else:
        def body(carry, xs):
            loss_sum, grad_sum = carry
            l_k, (gw_k, gx_k) = one_microbatch(xs[0], xs[1])
            with _jax.named_scope("update"):
                return (loss_sum + l_k, _jax.tree.map(_jnp.add, grad_sum, gw_k)), gx_k

        init = (_jnp.zeros((), _jnp.float32), _jax.tree.map(_jnp.zeros_like, weights))
        (loss, grad_w), grad_x = _jax.lax.scan(body, init, (per_example, given["loss_target"]))
    with _jax.named_scope("update"):
        delta_w, new_m, new_v = {}, {}, {}
        for n in TWIN_WEIGHTS:
            delta_w[n], new_m[n], new_v[n] = _adamw(weights[n], grad_w[n], given["m_" + n], given["v_" + n])
    return (loss, grad_x, *[grad_w[n] for n in TWIN_WEIGHTS], *[delta_w[n] for n in TWIN_WEIGHTS],
            *[new_m[n] for n in TWIN_WEIGHTS], *[new_v[n] for n in TWIN_WEIGHTS])
```

```python
import functools
import math

import jax
import jax.numpy as jnp
import numpy as np
from jax import lax
from jax.experimental import pallas as pl
from jax.experimental.pallas import tpu as pltpu

F32 = jnp.float32
BF16 = jnp.bfloat16

D_MODEL = 1024
DEPTH = 2
N_META = 16
HEADS = 8
V_DIM = 64
NOPE = 64
ROPE = 32
HALF_ROPE = ROPE // 2
QK_DIM = NOPE + ROPE
Q_RANK = 384
KV_RANK = 256
MIX = 512
CONV_K = 3
S5_GROUPS = 32
S5_GROUP = 16
S5_STATE = 64
S5_CH = S5_GROUPS * S5_STATE
D_FF = 2816
ALPHA = (2.0 * DEPTH) ** 0.25
LN_EPS = 1e-5
RMS_EPS = 1e-6
ROPE_BASE = 10000.0
IN_SPLITS = (Q_RANK, KV_RANK, ROPE, MIX, MIX, MIX, MIX, 3 * D_MODEL)
D_IN = sum(IN_SPLITS)
ADAM_LR, ADAM_B1, ADAM_B2, ADAM_EPS, ADAM_WD, ADAM_STEP = 0.001, 0.9, 0.999, 1e-08, 0.01, 10

N_DEV = 8
AXES = ("x", "y", "c")
LANES = 128
PACK_COLS = 1024
HEAD_PAD = 128
VMEM_LIMIT = 48 * 1024 * 1024

PIN_CQ, PIN_CKV, PIN_KR1, PIN_KR2, PIN_XBAR, PIN_BG, PIN_CG, PIN_U, PIN_GATES, PIN_END = (
    0, 384, 640, 768, 896, 1408, 1920, 2432, 2944, 6016)
D_IN_PAD = 6144

WEIGHT_NAMES = ['meta', 'ffn1_w_gate', 'ffn1_w_up', 'ffn1_w_down', 'ln1_g', 'ln1_b', 'w_in', 'mla_q_norm_g', 'mla_w_uq',
                'mla_kv_norm_g', 'mla_w_ukv', 'mla_w_o', 'conv_w', 'conv_b', 'conv_w_out', 's5_a_re', 's5_a_im',
                's5_log_dt', 's5_b_re', 's5_b_im', 's5_c_re', 's5_c_im', 's5_d', 's5_w_glu', 's5_b_glu', 's5_w_out',
                'w_o', 'ln2_g', 'ln2_b', 'ffn2_w_gate', 'ffn2_w_up', 'ffn2_w_down', 'ln3_g', 'ln3_b']
BIG = {'ffn1_w_gate': 2, 'ffn1_w_up': 2, 'ffn1_w_down': 1, 'w_in': 2, 'mla_w_uq': 2, 'mla_w_ukv': 2, 'mla_w_o': 2,
       'conv_w_out': 2, 's5_w_glu': 1, 's5_w_out': 2, 'w_o': 1, 'ffn2_w_gate': 2, 'ffn2_w_up': 2, 'ffn2_w_down': 1}
SMALL_SHARDED = ('meta', 'conv_w')
SMALL_NAMES = [n for n in WEIGHT_NAMES if n not in BIG]


def _divisor_tile(n, limit, mult):
    best = None
    for t in range(mult, min(n, limit) + 1, mult):
        if n % t == 0:
            best = t
    return best if best is not None else n


def _params(*sem):
    return pltpu.CompilerParams(dimension_semantics=sem, vmem_limit_bytes=VMEM_LIMIT)


def _matmul(a, b, *, ta=False, tb=False, out_dtype=F32, name):
    m, k = (a.shape[1], a.shape[0]) if ta else a.shape
    n = b.shape[0] if tb else b.shape[1]
    assert (b.shape[1] if tb else b.shape[0]) == k, (a.shape, b.shape, ta, tb)
    tm = _divisor_tile(m, 1408, LANES) if ta else _divisor_tile(m, 1088, 16)
    tn = _divisor_tile(n, 512, LANES)
    tk = _divisor_tile(k, 1408, 16 if ta else LANES)
    nk = k // tk
    dims = (((0 if ta else 1,), (1 if tb else 0,)), ((), ()))

    def kern(a_ref, b_ref, o_ref, acc_ref):
        kk = pl.program_id(2)
        part = lax.dot_general(a_ref[...].astype(BF16), b_ref[...].astype(BF16), dims, preferred_element_type=F32)

        @pl.when(kk == 0)
        def _():
            acc_ref[...] = part

        @pl.when(kk > 0)
        def _():
            acc_ref[...] += part

        @pl.when(kk == nk - 1)
        def _():
            o_ref[...] = acc_ref[...].astype(o_ref.dtype)

    a_spec = pl.BlockSpec((tk, tm), lambda i, j, kk: (kk, i)) if ta else pl.BlockSpec((tm, tk), lambda i, j, kk: (i, kk))
    b_spec = pl.BlockSpec((tn, tk), lambda i, j, kk: (j, kk)) if tb else pl.BlockSpec((tk, tn), lambda i, j, kk: (kk, j))
    return pl.pallas_call(
        kern, name=name, grid=(m // tm, n // tn, nk),
        in_specs=[a_spec, b_spec], out_specs=pl.BlockSpec((tm, tn), lambda i, j, kk: (i, j)),
        out_shape=jax.ShapeDtypeStruct((m, n), out_dtype),
        scratch_shapes=[pltpu.VMEM((tm, tn), F32)],
        compiler_params=_params("parallel", "parallel", "arbitrary"),
    )(a, b)


def _make_mm(name):
    @jax.custom_vjp
    def mm(x, w, wz):
        return _matmul(x, w, name=name + "_fwd")

    def fwd(x, w, wz):
        return _matmul(x, w, name=name + "_fwd"), (x, w)

    def bwd(res, dy):
        x, w = res
        dx = _matmul(dy, w, tb=True, out_dtype=x.dtype, name=name + "_dx")
        dw = _matmul(x, dy, ta=True, name=name + "_dw")
        return dx, jnp.zeros_like(w), dw

    mm.defvjp(fwd, bwd)
    return mm


def _make_mm_f32w(name):
    @jax.custom_vjp
    def mm(x, w):
        return _matmul(x, w, name=name + "_fwd")

    def fwd(x, w):
        return _matmul(x, w, name=name + "_fwd"), (x, w)

    def bwd(res, dy):
        x, w = res
        return (_matmul(dy, w, tb=True, out_dtype=x.dtype, name=name + "_dx"),
                _matmul(x, dy, ta=True, name=name + "_dw"))

    mm.defvjp(fwd, bwd)
    return mm


def _row_tile(rows, widths):
    limit = max(16, (6 * 1024 * 1024 // 4) // max(1, sum(widths)))
    return _divisor_tile(rows, limit, 16)


def _make_rowwise(f, n_rows, n_pars, out_dtypes, name, n_nodiff=0):
    n_out = len(out_dtypes)
    n_diff = n_rows - n_nodiff

    def run_fwd(rows, pars):
        length = rows[0].shape[0]
        shapes = jax.eval_shape(lambda *a: f(*a), *[jax.ShapeDtypeStruct((16, r.shape[1]), F32) for r in rows],
                                *[jax.ShapeDtypeStruct(p.shape, F32) for p in pars])
        widths = [s.shape[1] for s in shapes]
        tm = _row_tile(length, [r.shape[1] for r in rows] + widths)

        def kern(*refs):
            ins = [r[...].astype(F32) for r in refs[:n_rows + n_pars]]
            outs = f(*ins)
            for o_ref, o in zip(refs[n_rows + n_pars:], outs):
                o_ref[...] = o.astype(o_ref.dtype)

        return pl.pallas_call(
            kern, name=name + "_fwd", grid=(length // tm,),
            in_specs=[pl.BlockSpec((tm, r.shape[1]), lambda i: (i, 0)) for r in rows]
            + [pl.BlockSpec(p.shape, lambda i: (0, 0)) for p in pars],
            out_specs=[pl.BlockSpec((tm, w), lambda i: (i, 0)) for w in widths],
            out_shape=[jax.ShapeDtypeStruct((length, w), dt) for w, dt in zip(widths, out_dtypes)],
            compiler_params=_params("parallel"),
        )(*rows, *pars)

    def run_bwd(rows, pars, cts):
        length = rows[0].shape[0]
        tm = _row_tile(length, [r.shape[1] for r in rows] * 2 + [c.shape[1] for c in cts] * 2)

        def kern(*refs):
            ins = [r[...].astype(F32) for r in refs[:n_rows + n_pars]]
            ct = [r[...].astype(F32) for r in refs[n_rows + n_pars:n_rows + n_pars + n_out]]
            out_refs = refs[n_rows + n_pars + n_out:]
            nodiff = ins[n_diff:n_rows]
            _, vjp = jax.vjp(lambda *a: f(*a[:n_diff], *nodiff, *a[n_diff:]), *ins[:n_diff], *ins[n_rows:])
            grads = vjp(tuple(ct))
            for o_ref, g in zip(out_refs[:n_diff], grads[:n_diff]):
                o_ref[...] = g.astype(o_ref.dtype)
            first = pl.program_id(0) == 0
            for o_ref, g in zip(out_refs[n_diff:], grads[n_diff:]):
                @pl.when(first)
                def _(o_ref=o_ref, g=g):
                    o_ref[...] = g

                @pl.when(jnp.logical_not(first))
                def _(o_ref=o_ref, g=g):
                    o_ref[...] += g

        return pl.pallas_call(
            kern, name=name + "_bwd", grid=(length // tm,),
            in_specs=[pl.BlockSpec((tm, r.shape[1]), lambda i: (i, 0)) for r in rows]
            + [pl.BlockSpec(p.shape, lambda i: (0, 0)) for p in pars]
            + [pl.BlockSpec((tm, c.shape[1]), lambda i: (i, 0)) for c in cts],
            out_specs=[pl.BlockSpec((tm, r.shape[1]), lambda i: (i, 0)) for r in rows[:n_diff]]
            + [pl.BlockSpec(p.shape, lambda i: (0, 0)) for p in pars],
            out_shape=[jax.ShapeDtypeStruct(r.shape, r.dtype) for r in rows[:n_diff]]
            + [jax.ShapeDtypeStruct(p.shape, F32) for p in pars],
            compiler_params=_params("arbitrary"),
        )(*rows, *pars, *cts)

    @jax.custom_vjp
    def op(*args):
        return tuple(run_fwd(args[:n_rows], args[n_rows:]))

    def fwd(*args):
        return tuple(run_fwd(args[:n_rows], args[n_rows:])), args

    def bwd(args, cts):
        grads = run_bwd(args[:n_rows], args[n_rows:], cts)
        zeros = [jnp.zeros_like(r) for r in args[n_diff:n_rows]]
        return (*grads[:n_diff], *zeros, *grads[n_diff:])

    op.defvjp(fwd, bwd)
    return op


def _layer_norm(z, g, b):
    mu = jnp.mean(z, axis=-1, keepdims=True)
    d = z - mu
    var = jnp.mean(d * d, axis=-1, keepdims=True)
    return d * lax.rsqrt(var + LN_EPS) * g + b


def _f_ln_half(h, f, g, b):
    return (_layer_norm(ALPHA * h + 0.5 * f, g, b),)


def _f_ln_full(h, f, g, b):
    return (_layer_norm(ALPHA * h + f, g, b),)


def _f_swiglu(gate, up):
    return (jax.nn.silu(gate) * up,)


def _f_rms(x, g):
    return (x * lax.rsqrt(jnp.mean(x * x, axis=-1, keepdims=True) + RMS_EPS) * g,)


def _f_rope(x1, x2, cos, sin):
    return x1 * cos - x2 * sin, x2 * cos + x1 * sin


def _f_gelu_skip(y, u, d):
    return (jax.nn.gelu(y + d * u),)


def _f_glu(z, t, b):
    return (z * jax.nn.sigmoid(t + b),)


def _f_merge(ga, gb, gc, ya, yb, yc):
    return (jax.nn.sigmoid(ga) * ya + jax.nn.sigmoid(gb) * yb + jax.nn.sigmoid(gc) * yc,)


def _attn_scores(q, k, q_block, tq):
    length = k.shape[0]
    s = lax.dot_general(q, k, (((1,), (1,)), ((), ())), preferred_element_type=F32) * (QK_DIM ** -0.5)
    row = q_block * tq + lax.broadcasted_iota(jnp.int32, (tq, length), 0)
    col = lax.broadcasted_iota(jnp.int32, (tq, length), 1)
    s = jnp.where(col <= row, s, -1e30)
    e = jnp.exp(s - jnp.max(s, axis=1, keepdims=True))
    return e / jnp.sum(e, axis=1, keepdims=True)


ATTN_TQ = 128


def _attn_fwd(q3, k3, v3):
    heads, length, _ = q3.shape
    tq = ATTN_TQ

    def kern(q_ref, k_ref, v_ref, o_ref):
        p = _attn_scores(q_ref[0], k_ref[0], pl.program_id(1), tq)
        o_ref[0] = jnp.dot(p.astype(BF16), v_ref[0], preferred_element_type=F32).astype(o_ref.dtype)

    return pl.pallas_call(
        kern, name="attn_fwd", grid=(heads, length // tq),
        in_specs=[pl.BlockSpec((1, tq, HEAD_PAD), lambda h, i: (h, i, 0)),
                  pl.BlockSpec((1, length, HEAD_PAD), lambda h, i: (h, 0, 0)),
                  pl.BlockSpec((1, length, V_DIM), lambda h, i: (h, 0, 0))],
        out_specs=pl.BlockSpec((1, tq, V_DIM), lambda h, i: (h, i, 0)),
        out_shape=jax.ShapeDtypeStruct((heads, length, V_DIM), F32),
        compiler_params=_params("parallel", "parallel"),
    )(q3, k3, v3)


def _attn_bwd(q3, k3, v3, do3):
    heads, length, _ = q3.shape
    tq = ATTN_TQ

    def kern(q_ref, k_ref, v_ref, do_ref, dq_ref, dk_ref, dv_ref):
        i = pl.program_id(1)
        q, k, v, do = q_ref[0], k_ref[0], v_ref[0], do_ref[0].astype(BF16)
        p = _attn_scores(q, k, i, tq)
        dp = lax.dot_general(do, v, (((1,), (1,)), ((), ())), preferred_element_type=F32)
        ds = (p * (dp - jnp.sum(p * dp, axis=1, keepdims=True)) * (QK_DIM ** -0.5)).astype(BF16)
        dq_ref[0] = jnp.dot(ds, k, preferred_element_type=F32)
        dk = lax.dot_general(ds, q, (((0,), (0,)), ((), ())), preferred_element_type=F32)
        dv = lax.dot_general(p.astype(BF16), do, (((0,), (0,)), ((), ())), preferred_element_type=F32)

        @pl.when(i == 0)
        def _():
            dk_ref[0] = dk
            dv_ref[0] = dv

        @pl.when(i > 0)
        def _():
            dk_ref[0] += dk
            dv_ref[0] += dv

    return pl.pallas_call(
        kern, name="attn_bwd", grid=(heads, length // tq),
        in_specs=[pl.BlockSpec((1, tq, HEAD_PAD), lambda h, i: (h, i, 0)),
                  pl.BlockSpec((1, length, HEAD_PAD), lambda h, i: (h, 0, 0)),
                  pl.BlockSpec((1, length, V_DIM), lambda h, i: (h, 0, 0)),
                  pl.BlockSpec((1, tq, V_DIM), lambda h, i: (h, i, 0))],
        out_specs=[pl.BlockSpec((1, tq, HEAD_PAD), lambda h, i: (h, i, 0)),
                   pl.BlockSpec((1, length, HEAD_PAD), lambda h, i: (h, 0, 0)),
                   pl.BlockSpec((1, length, V_DIM), lambda h, i: (h, 0, 0))],
        out_shape=[jax.ShapeDtypeStruct((heads, length, HEAD_PAD), F32),
                   jax.ShapeDtypeStruct((heads, length, HEAD_PAD), F32),
                   jax.ShapeDtypeStruct((heads, length, V_DIM), F32)],
        compiler_params=_params("parallel", "arbitrary"),
    )(q3, k3, v3, do3)


@jax.custom_vjp
def _attention(q3, k3, v3):
    return _attn_fwd(q3, k3, v3)


def _attention_fwd(q3, k3, v3):
    return _attn_fwd(q3, k3, v3), (q3, k3, v3)


def _attention_bwd(res, do3):
    q3, k3, v3 = res
    dq, dk, dv = _attn_bwd(q3, k3, v3, do3)
    return dq.astype(q3.dtype), dk.astype(k3.dtype), dv.astype(v3.dtype)


_attention.defvjp(_attention_fwd, _attention_bwd)


def _conv_terms(x, c, w_ref, cb):
    u = c * x
    row = lax.broadcasted_iota(jnp.int32, u.shape, 0)
    u1 = jnp.where(row >= 1, pltpu.roll(u, 1, 0), 0.0)
    u2 = jnp.where(row >= 2, pltpu.roll(u, 2, 0), 0.0)
    y = cb + w_ref[0:1, :] * u2 + w_ref[1:2, :] * u1 + w_ref[2:3, :] * u
    return u, u1, u2, y


def _conv_specs(length):
    col = pl.BlockSpec((length, LANES), lambda j: (0, j))
    return col, pl.BlockSpec((CONV_K, LANES), lambda j: (0, j)), pl.BlockSpec((1, LANES), lambda j: (0, j))


def _conv_fwd(x, b, c, w, cb):
    length = x.shape[0]

    def kern(x_ref, b_ref, c_ref, w_ref, cb_ref, o_ref):
        _, _, _, y = _conv_terms(x_ref[...], c_ref[...], w_ref, cb_ref[...])
        o_ref[...] = b_ref[...] * y

    col, wspec, bspec = _conv_specs(length)
    return pl.pallas_call(
        kern, name="conv_fwd", grid=(MIX // LANES,), in_specs=[col, col, col, wspec, bspec], out_specs=col,
        out_shape=jax.ShapeDtypeStruct((length, MIX), F32), compiler_params=_params("parallel"),
    )(x, b, c, w, cb)


def _conv_bwd(x, b, c, w, cb, do):
    length = x.shape[0]

    def kern(x_ref, b_ref, c_ref, w_ref, cb_ref, do_ref, dx_ref, db_ref, dc_ref, dw_ref, dcb_ref):
        x, c, do = x_ref[...], c_ref[...], do_ref[...]
        u, u1, u2, y = _conv_terms(x, c, w_ref, cb_ref[...])
        db_ref[...] = do * y
        dy = do * b_ref[...]
        row = lax.broadcasted_iota(jnp.int32, dy.shape, 0)
        dy1 = jnp.where(row < length - 1, pltpu.roll(dy, length - 1, 0), 0.0)
        dy2 = jnp.where(row < length - 2, pltpu.roll(dy, length - 2, 0), 0.0)
        du = w_ref[2:3, :] * dy + w_ref[1:2, :] * dy1 + w_ref[0:1, :] * dy2
        dx_ref[...] = du * c
        dc_ref[...] = du * x
        dw_ref[0:1, :] = jnp.sum(dy * u2, axis=0, keepdims=True)
        dw_ref[1:2, :] = jnp.sum(dy * u1, axis=0, keepdims=True)
        dw_ref[2:3, :] = jnp.sum(dy * u, axis=0, keepdims=True)
        dcb_ref[...] = jnp.sum(dy, axis=0, keepdims=True)

    col, wspec, bspec = _conv_specs(length)
    big = jax.ShapeDtypeStruct((length, MIX), F32)
    return pl.pallas_call(
        kern, name="conv_bwd", grid=(MIX // LANES,), in_specs=[col, col, col, wspec, bspec, col],
        out_specs=[col, col, col, wspec, bspec],
        out_shape=[big, big, big, jax.ShapeDtypeStruct((CONV_K, MIX), F32), jax.ShapeDtypeStruct((1, MIX), F32)],
        compiler_params=_params("parallel"),
    )(x, b, c, w, cb, do)


@jax.custom_vjp
def _short_conv(x, b, c, w, cb):
    return _conv_fwd(x, b, c, w, cb)


def _short_conv_fwd(x, b, c, w, cb):
    return _conv_fwd(x, b, c, w, cb), (x, b, c, w, cb)


def _short_conv_bwd(res, do):
    return tuple(_conv_bwd(*res, do))


_short_conv.defvjp(_short_conv_fwd, _short_conv_bwd)


SCAN_ROWS = S5_CH // LANES
SCAN_TC = 136


def _scan_fwd(ar, ai, b):
    length = b.shape[0]
    tc = _divisor_tile(length, SCAN_TC, 8)

    def kern(ar_ref, ai_ref, b_ref, x_ref, sr, si):
        @pl.when(pl.program_id(0) == 0)
        def _():
            sr[...] = jnp.zeros_like(sr)
            si[...] = jnp.zeros_like(si)

        a_re, a_im = ar_ref[...], ai_ref[...]

        def body(t, carry):
            xr, xi = carry
            nr = a_re * xr - a_im * xi + b_ref[t, 0:SCAN_ROWS, :]
            ni = a_re * xi + a_im * xr + b_ref[t, SCAN_ROWS:2 * SCAN_ROWS, :]
            x_ref[t, 0:SCAN_ROWS, :] = nr
            x_ref[t, SCAN_ROWS:2 * SCAN_ROWS, :] = ni
            return nr, ni

        xr, xi = lax.fori_loop(0, tc, body, (sr[...], si[...]), unroll=4)
        sr[...] = xr
        si[...] = xi

    par = pl.BlockSpec((SCAN_ROWS, LANES), lambda i: (0, 0))
    blk = pl.BlockSpec((tc, 2 * SCAN_ROWS, LANES), lambda i: (i, 0, 0))
    return pl.pallas_call(
        kern, name="s5_scan_fwd", grid=(length // tc,), in_specs=[par, par, blk], out_specs=blk,
        out_shape=jax.ShapeDtypeStruct(b.shape, F32), scratch_shapes=[pltpu.VMEM((SCAN_ROWS, LANES), F32)] * 2,
        compiler_params=_params("arbitrary"),
    )(ar, ai, b)


def _scan_bwd(ar, ai, x, dx):
    length = x.shape[0]
    tc = _divisor_tile(length, SCAN_TC, 8)
    n_blk = length // tc
    re, im = slice(0, SCAN_ROWS), slice(SCAN_ROWS, 2 * SCAN_ROWS)

    def kern(ar_ref, ai_ref, x_ref, dx_ref, db_ref, dar_ref, dai_ref, lr_s, li_s):
        @pl.when(pl.program_id(0) == 0)
        def _():
            lr_s[...] = jnp.zeros_like(lr_s)
            li_s[...] = jnp.zeros_like(li_s)
            dar_ref[...] = jnp.zeros_like(dar_ref)
            dai_ref[...] = jnp.zeros_like(dai_ref)

        a_re, a_im = ar_ref[...], ai_ref[...]

        def body(j, carry):
            t = tc - 1 - j
            lr, li, gr, gi = carry
            x_re, x_im = x_ref[t, re, :], x_ref[t, im, :]
            gr = gr + (lr * x_re + li * x_im)
            gi = gi + (li * x_re - lr * x_im)
            nlr = dx_ref[t, re, :] + (a_re * lr + a_im * li)
            nli = dx_ref[t, im, :] + (a_re * li - a_im * lr)
            db_ref[t, re, :] = nlr
            db_ref[t, im, :] = nli
            return nlr, nli, gr, gi

        lr, li, gr, gi = lax.fori_loop(0, tc, body, (lr_s[...], li_s[...], dar_ref[...], dai_ref[...]), unroll=4)
        lr_s[...] = lr
        li_s[...] = li
        dar_ref[...] = gr
        dai_ref[...] = gi

    par = pl.BlockSpec((SCAN_ROWS, LANES), lambda i: (0, 0))
    blk = pl.BlockSpec((tc, 2 * SCAN_ROWS, LANES), lambda i: (n_blk - 1 - i, 0, 0))
    pout = jax.ShapeDtypeStruct((SCAN_ROWS, LANES), F32)
    return pl.pallas_call(
        kern, name="s5_scan_bwd", grid=(n_blk,), in_specs=[par, par, blk, blk],
        out_specs=[blk, par, par], out_shape=[jax.ShapeDtypeStruct(x.shape, F32), pout, pout],
        scratch_shapes=[pltpu.VMEM((SCAN_ROWS, LANES), F32)] * 2, compiler_params=_params("arbitrary"),
    )(ar, ai, x, dx)


@jax.custom_vjp
def _s5_scan(ar, ai, b):
    return _scan_fwd(ar, ai, b)


def _s5_scan_fwd(ar, ai, b):
    x = _scan_fwd(ar, ai, b)
    return x, (ar, ai, x)


def _s5_scan_bwd(res, dx):
    ar, ai, x = res
    db, dar, dai = _scan_bwd(ar, ai, x, dx)
    return dar, dai, db


_s5_scan.defvjp(_s5_scan_fwd, _s5_scan_bwd)


def _loss_call(y, target, n_real):
    length = y.shape[0]
    tm = _divisor_tile(length, 544, 16)

    def kern(y_ref, t_ref, loss_ref, dy_ref):
        i = pl.program_id(0)
        row = i * tm + lax.broadcasted_iota(jnp.int32, (tm, 1), 0)
        keep = jnp.logical_and(row >= N_META, row < n_real)
        err = jnp.where(keep, y_ref[...] - t_ref[...], 0.0)
        dy_ref[...] = err * (1.0 / D_MODEL)
        part = 0.5 * jnp.sum(jnp.mean(err * err, axis=-1, keepdims=True), axis=0, keepdims=True)

        @pl.when(i == 0)
        def _():
            loss_ref[...] = jnp.zeros_like(loss_ref)

        loss_ref[...] += part

    blk = pl.BlockSpec((tm, D_MODEL), lambda i: (i, 0))
    return pl.pallas_call(
        kern, name="loss_head", grid=(length // tm,), in_specs=[blk, blk],
        out_specs=[pl.BlockSpec((8, LANES), lambda i: (0, 0)), blk],
        out_shape=[jax.ShapeDtypeStruct((8, LANES), F32), jax.ShapeDtypeStruct(y.shape, F32)],
        compiler_params=_params("arbitrary"),
    )(y, target)


def _make_loss(n_real):
    @jax.custom_vjp
    def loss(y, target):
        return _loss_call(y, target, n_real)[0][0, 0]

    def fwd(y, target):
        total, dy = _loss_call(y, target, n_real)
        return total[0, 0], dy

    def bwd(dy, ct):
        return dy * ct, jnp.zeros_like(dy)

    loss.defvjp(fwd, bwd)
    return loss


HBM_SPEC = pl.BlockSpec(memory_space=pl.ANY)
MESH_ID = pl.DeviceIdType.MESH


def _all_gather(shards, name):
    n = len(shards)

    def body(*refs):
        x_refs, out_refs = refs[:n], refs[n:2 * n]
        send_sems, recv_sems, local_sems = refs[2 * n:]
        x, y, c = lax.axis_index("x"), lax.axis_index("y"), lax.axis_index("c")
        me, sibling = (x, y, c), (x, y, 1 - c)
        chips = [(1 - x, y), (x, 1 - y), (1 - x, 1 - y)]

        def copy(b, k, block, to, from_input=False):
            px, py, pc = block
            slot = out_refs[b].at[4 * px + 2 * py + pc]
            return pltpu.make_async_remote_copy(
                src_ref=x_refs[b] if from_input else slot, dst_ref=slot,
                send_sem=send_sems.at[7 * b + k], recv_sem=recv_sems.at[7 * b + k], device_id=to, device_id_type=MESH_ID)

        mine = [pltpu.make_async_copy(x_refs[b], out_refs[b].at[4 * x + 2 * y + c], local_sems.at[b]) for b in range(n)]
        for cp in mine:
            cp.start()
        first = []
        for b in range(n):
            first.append(copy(b, 0, me, sibling, from_input=True))
            first += [copy(b, 1 + j, me, (*chip, c), from_input=True) for j, chip in enumerate(chips)]
        for cp in first:
            cp.start()
        passed = []
        for j, chip in enumerate(chips):
            for b in range(n):
                copy(b, 1 + j, (*chip, c), me).wait_recv()
                passed.append(copy(b, 4 + j, (*chip, c), sibling))
                passed[-1].start()
        for b in range(n):
            copy(b, 0, sibling, me).wait_recv()
            for j, chip in enumerate(chips):
                copy(b, 4 + j, (*chip, 1 - c), me).wait_recv()
        for cp in first + passed:
            cp.wait_send()
        for cp in mine:
            cp.wait()

    return pl.pallas_call(
        body, name=name, out_shape=[jax.ShapeDtypeStruct((N_DEV, *s.shape), s.dtype) for s in shards],
        in_specs=[HBM_SPEC] * n, out_specs=[HBM_SPEC] * n,
        scratch_shapes=[pltpu.SemaphoreType.DMA((7 * n,)), pltpu.SemaphoreType.DMA((7 * n,)), pltpu.SemaphoreType.DMA((n,))],
    )(*shards)


def _sibling_exchange(bufs, name):
    def body(*refs):
        ins, outs = refs[:len(bufs)], refs[len(bufs):2 * len(bufs)]
        send_sems, recv_sems = refs[2 * len(bufs):]
        x, y, c = lax.axis_index("x"), lax.axis_index("y"), lax.axis_index("c")
        copies = []
        for b, (src, dst) in enumerate(zip(ins, outs)):
            for chip in range(4):
                copies.append(pltpu.make_async_remote_copy(
                    src_ref=src.at[2 * chip + (1 - c)], dst_ref=dst.at[chip],
                    send_sem=send_sems.at[4 * b + chip], recv_sem=recv_sems.at[4 * b + chip],
                    device_id=(x, y, 1 - c), device_id_type=MESH_ID))
        for cp in copies:
            cp.start()
        for cp in copies:
            cp.wait()

    n = 4 * len(bufs)
    return pl.pallas_call(
        body, name=name, out_shape=[jax.ShapeDtypeStruct((4, *b.shape[1:]), b.dtype) for b in bufs],
        in_specs=[HBM_SPEC] * len(bufs), out_specs=[HBM_SPEC] * len(bufs),
        scratch_shapes=[pltpu.SemaphoreType.DMA((n,)), pltpu.SemaphoreType.DMA((n,))],
    )(*bufs)


def _chip_exchange(bufs, name):
    def body(*refs):
        ins, outs = refs[:len(bufs)], refs[len(bufs):2 * len(bufs)]
        send_sems, recv_sems = refs[2 * len(bufs):]
        x, y, c = lax.axis_index("x"), lax.axis_index("y"), lax.axis_index("c")
        chips = [(1 - x, y), (x, 1 - y), (1 - x, 1 - y)]
        copies = []
        for b, (src, dst) in enumerate(zip(ins, outs)):
            for j, (px, py) in enumerate(chips):
                copies.append(pltpu.make_async_remote_copy(
                    src_ref=src.at[2 * px + py], dst_ref=dst.at[j],
                    send_sem=send_sems.at[3 * b + j], recv_sem=recv_sems.at[3 * b + j],
                    device_id=(px, py, c), device_id_type=MESH_ID))
        for cp in copies:
            cp.start()
        for cp in copies:
            cp.wait()

    n = 3 * len(bufs)
    return pl.pallas_call(
        body, name=name, out_shape=[jax.ShapeDtypeStruct((3, *b.shape[1:]), b.dtype) for b in bufs],
        in_specs=[HBM_SPEC] * len(bufs), out_specs=[HBM_SPEC] * len(bufs),
        scratch_shapes=[pltpu.SemaphoreType.DMA((n,)), pltpu.SemaphoreType.DMA((n,))],
    )(*bufs)


def _pair_add(full, got, core, name):
    _, rows, cols = full.shape
    tm = _divisor_tile(rows, 640, 8)

    def kern(core_ref, a_ref, b_ref, o_ref):
        o_ref[...] = a_ref[...] + b_ref[...]

    return pl.pallas_call(
        kern, name=name,
        grid_spec=pltpu.PrefetchScalarGridSpec(
            num_scalar_prefetch=1, grid=(4, rows // tm),
            in_specs=[pl.BlockSpec((1, tm, cols), lambda k, i, core_ref: (2 * k + core_ref[0], i, 0)),
                      pl.BlockSpec((1, tm, cols), lambda k, i, core_ref: (k, i, 0))],
            out_specs=pl.BlockSpec((1, tm, cols), lambda k, i, core_ref: (k, i, 0))),
        out_shape=jax.ShapeDtypeStruct((4, rows, cols), F32), compiler_params=_params("parallel", "parallel"),
    )(core, full, got)


def _chip_add(part, got, chip, name):
    _, rows, cols = part.shape
    tm = _divisor_tile(rows, 640, 8)

    def kern(chip_ref, a_ref, b_ref, o_ref):
        o_ref[...] = ((a_ref[0] + b_ref[0]) + b_ref[1]) + b_ref[2]

    return pl.pallas_call(
        kern, name=name,
        grid_spec=pltpu.PrefetchScalarGridSpec(
            num_scalar_prefetch=1, grid=(rows // tm,),
            in_specs=[pl.BlockSpec((1, tm, cols), lambda i, chip_ref: (chip_ref[0], i, 0)),
                      pl.BlockSpec((3, tm, cols), lambda i, chip_ref: (0, i, 0))],
            out_specs=pl.BlockSpec((tm, cols), lambda i, chip_ref: (i, 0))),
        out_shape=jax.ShapeDtypeStruct((rows, cols), F32), compiler_params=_params("parallel"),
    )(chip, part, got)


def _reduce_scatter(bufs, tag):
    core = lax.axis_index("c").astype(jnp.int32).reshape(1)
    chip = (2 * lax.axis_index("x") + lax.axis_index("y")).astype(jnp.int32).reshape(1)
    got = _sibling_exchange(bufs, f"rs_sibling_exchange_{tag}")
    parts = [_pair_add(b, g, core, f"rs_pair_add_{tag}{i}") for i, (b, g) in enumerate(zip(bufs, got))]
    arrived = _chip_exchange(parts, f"rs_chip_exchange_{tag}")
    return [_chip_add(p, a, chip, f"rs_chip_add_{tag}{i}") for i, (p, a) in enumerate(zip(parts, arrived))]


def _adamw(w, g, m, v, name):
    rows, cols = w.shape
    tm = _divisor_tile(rows, max(8, (512 * 1024) // cols // 8 * 8), 8)

    def kern(w_ref, g_ref, m_ref, v_ref, d_ref, nm_ref, nv_ref):
        g = g_ref[...]
        m = ADAM_B1 * m_ref[...] + (1.0 - ADAM_B1) * g
        v = ADAM_B2 * v_ref[...] + (1.0 - ADAM_B2) * (g * g)
        m_hat = m / (1.0 - ADAM_B1 ** ADAM_STEP)
        v_hat = v / (1.0 - ADAM_B2 ** ADAM_STEP)
        d_ref[...] = -ADAM_LR * (m_hat / (jnp.sqrt(v_hat) + ADAM_EPS) + ADAM_WD * w_ref[...])
        nm_ref[...] = m
        nv_ref[...] = v

    blk = pl.BlockSpec((tm, cols), lambda i: (i, 0))
    out = jax.ShapeDtypeStruct(w.shape, F32)
    return pl.pallas_call(
        kern, name=name, grid=(rows // tm,), in_specs=[blk] * 4, out_specs=[blk] * 3, out_shape=[out] * 3,
        compiler_params=_params("parallel"),
    )(w, g, m, v)


GROUPS = (('ffn1_w_gate', 'ffn1_w_up', 'ffn2_w_gate', 'ffn2_w_up'), ('ffn1_w_down', 'ffn2_w_down', 'w_o'), ('w_in',),
          ('mla_w_ukv', 'mla_w_o', 'conv_w_out', 's5_w_out'), ('mla_w_uq',), ('s5_w_glu',))
PIN_CUTS = (PIN_CQ, PIN_CKV, PIN_KR1, PIN_KR2, PIN_XBAR, PIN_BG, PIN_CG, PIN_U, PIN_GATES, PIN_GATES + D_MODEL,
            PIN_GATES + 2 * D_MODEL, PIN_END, D_IN_PAD)


def _make_split(cuts):
    @jax.custom_vjp
    def split(t):
        return tuple(t[:, a:b] for a, b in zip(cuts[:-1], cuts[1:]))

    def fwd(t):
        return split(t), None

    def bwd(_, cts):
        return (jnp.concatenate(cts, axis=1),)

    split.defvjp(fwd, bwd)
    return split


def _pack_groups(tensors, dtype):
    return [jnp.concatenate([tensors[n].reshape(-1, tensors[n].shape[-1]).astype(dtype) for n in grp], axis=0)
            for grp in GROUPS]


def _unpack_groups(bufs, shard_shapes):
    out = {}
    for buf, grp in zip(bufs, GROUPS):
        at = 0
        for n in grp:
            rows = int(np.prod(shard_shapes[n][:-1]))
            out[n] = buf[..., at:at + rows, :].reshape(*buf.shape[:-2], *shard_shapes[n])
            at += rows
    return out


def _pack_rows(arrays):
    flat = jnp.concatenate([a.reshape(-1) for a in arrays])
    rows = -(-flat.shape[0] // PACK_COLS)
    rows = -(-rows // 8) * 8
    return jnp.pad(flat, (0, rows * PACK_COLS - flat.shape[0])).reshape(rows, PACK_COLS)


def _unpack_rows(buf, shapes):
    flat = buf.reshape(-1)
    out, at = [], 0
    for s in shapes:
        n = int(np.prod(s))
        out.append(flat[at:at + n].reshape(s))
        at += n
    return out


def _full_weight(t, axis):
    if axis == 1:
        return jnp.moveaxis(t, 0, 1).reshape(t.shape[1], N_DEV * t.shape[2], t.shape[3])
    return jnp.moveaxis(t, 0, 2).reshape(t.shape[1], t.shape[2], N_DEV * t.shape[3])


def _shards_of(full, axis):
    depth, rows, cols = full.shape
    if axis == 1:
        return full.reshape(depth, N_DEV, rows // N_DEV, cols).transpose(1, 0, 2, 3)
    return full.reshape(depth, rows, N_DEV, cols // N_DEV).transpose(2, 0, 1, 3)


def _disassemble(layers):
    per_layer = []
    for d in layers:
        w_in = d['w_in']
        pieces = [(PIN_CQ, Q_RANK), (PIN_CKV, KV_RANK), (PIN_KR1, HALF_ROPE), (PIN_KR2, HALF_ROPE), (PIN_XBAR, 4 * MIX),
                  (PIN_GATES, 3 * D_MODEL)]
        uq, ukv = d['w_uq'], d['w_ukv']
        heads = lambda t: t.reshape(t.shape[0], HEADS, -1)
        per_layer.append(dict(
            ffn1_w_gate=d['ffn1_gu'][:, :D_FF], ffn1_w_up=d['ffn1_gu'][:, D_FF:], ffn1_w_down=d['ffn1_down'],
            ffn2_w_gate=d['ffn2_gu'][:, :D_FF], ffn2_w_up=d['ffn2_gu'][:, D_FF:], ffn2_w_down=d['ffn2_down'],
            w_in=jnp.concatenate([w_in[:, a:a + n] for a, n in pieces], axis=1),
            mla_w_uq=jnp.concatenate([heads(uq[:, :HEADS * NOPE]), heads(uq[:, HEADS * NOPE:HEADS * NOPE + LANES]),
                                      heads(uq[:, HEADS * NOPE + LANES:])], axis=2).reshape(Q_RANK, HEADS * QK_DIM),
            mla_w_ukv=jnp.concatenate([heads(ukv[:, :HEADS * NOPE]), heads(ukv[:, HEADS * NOPE:])],
                                      axis=2).reshape(KV_RANK, HEADS * (NOPE + V_DIM)),
            mla_w_o=d['mla_w_o'], conv_w_out=d['conv_w_out'], s5_w_glu=d['s5_w_glu'], s5_w_out=d['s5_w_out'], w_o=d['w_o']))
    shards = {n: _shards_of(jnp.stack([p[n] for p in per_layer]), BIG[n]) for n in BIG}
    return [jnp.concatenate([shards[n].reshape(N_DEV, -1, shards[n].shape[-1]) for n in grp], axis=1) for grp in GROUPS]


def _assemble(gathered, shard_shapes):
    full = {n: _full_weight(t, BIG[n]) for n, t in _unpack_groups(gathered, shard_shapes).items()}
    layers = []
    for l in range(DEPTH):
        w_in = full['w_in'][l]
        cuts = np.cumsum((0,) + IN_SPLITS)
        cq, ckv, kr, xbar, bg, cg, u, gates = [w_in[:, a:b] for a, b in zip(cuts[:-1], cuts[1:])]
        pad = lambda t, n: jnp.pad(t, ((0, 0), (0, n - t.shape[1])))
        w_in_packed = jnp.concatenate(
            [cq, ckv, pad(kr[:, :HALF_ROPE], LANES), pad(kr[:, HALF_ROPE:], LANES), xbar, bg, cg, u, gates,
             jnp.zeros((D_MODEL, D_IN_PAD - PIN_END), w_in.dtype)], axis=1)
        uq = full['mla_w_uq'][l].reshape(Q_RANK, HEADS, QK_DIM)
        w_uq = jnp.concatenate([uq[:, :, :NOPE].reshape(Q_RANK, HEADS * NOPE),
                                uq[:, :, NOPE:NOPE + HALF_ROPE].reshape(Q_RANK, HEADS * HALF_ROPE),
                                uq[:, :, NOPE + HALF_ROPE:].reshape(Q_RANK, HEADS * HALF_ROPE)], axis=1)
        ukv = full['mla_w_ukv'][l].reshape(KV_RANK, HEADS, NOPE + V_DIM)
        w_ukv = jnp.concatenate([ukv[:, :, :NOPE].reshape(KV_RANK, HEADS * NOPE),
                                 ukv[:, :, NOPE:].reshape(KV_RANK, HEADS * V_DIM)], axis=1)
        layers.append(dict(
            ffn1_gu=jnp.concatenate([full['ffn1_w_gate'][l], full['ffn1_w_up'][l]], axis=1),
            ffn1_down=full['ffn1_w_down'][l],
            w_in=w_in_packed, w_uq=w_uq, w_ukv=w_ukv, mla_w_o=full['mla_w_o'][l], conv_w_out=full['conv_w_out'][l],
            s5_w_glu=full['s5_w_glu'][l], s5_w_out=full['s5_w_out'][l], w_o=full['w_o'][l],
            ffn2_gu=jnp.concatenate([full['ffn2_w_gate'][l], full['ffn2_w_up'][l]], axis=1),
            ffn2_down=full['ffn2_w_down'][l]))
    return layers


def _s5_discretize(a_re, a_im, log_dt, b_re, b_im, c_re, c_im):
    dt = jnp.exp(log_dt)[:, None]
    mag = jnp.exp(dt * a_re)
    ab_re, ab_im = mag * jnp.cos(dt * a_im), mag * jnp.sin(dt * a_im)
    den = a_re * a_re + a_im * a_im
    nr, ni = ab_re - 1.0, ab_im
    coef_re = (nr * a_re + ni * a_im) / den
    coef_im = (ni * a_re - nr * a_im) / den
    bb_re = coef_re[..., None] * b_re - coef_im[..., None] * b_im
    bb_im = coef_re[..., None] * b_im + coef_im[..., None] * b_re
    eye = jnp.eye(S5_GROUPS, dtype=F32)
    spread_b = lambda bb: jnp.einsum('gnh,gk->ghkn', bb, eye).reshape(MIX, S5_CH)
    spread_c = lambda cc: jnp.einsum('ghn,gk->gnkh', cc, eye).reshape(S5_CH, MIX)
    b_map = jnp.concatenate([spread_b(bb_re), spread_b(bb_im)], axis=1)
    c_map = jnp.concatenate([spread_c(c_re), -spread_c(c_im)], axis=0)
    return ab_re.reshape(SCAN_ROWS, LANES), ab_im.reshape(SCAN_ROWS, LANES), b_map, c_map


def _rope_tables(length):
    inv_freq = ROPE_BASE ** (-jnp.arange(0, ROPE, 2, dtype=F32) / ROPE)
    ang = jnp.arange(length).astype(F32)[:, None] * inv_freq[None, :]
    return jnp.tile(jnp.cos(ang), (1, LANES // HALF_ROPE)), jnp.tile(jnp.sin(ang), (1, LANES // HALF_ROPE))


def _heads_first(t):
    return t.reshape(t.shape[0], HEADS, -1).transpose(1, 0, 2)


def _local_loss(diff, big, n_real):
    small, wz = diff['small'], diff['wz']
    h = diff['h0']
    length = h.shape[0]
    cos, sin = _rope_tables(length)
    ln_half = lambda tag: _make_rowwise(_f_ln_half, 2, 2, (F32,), tag)
    row2 = lambda v: v.reshape(1, -1)
    for l in range(DEPTH):
        w, z = big[l], wz[l]
        p = {k: small[k][l] for k in small if k != 'meta'}

        def ffn(h, gu, down, g, b, tag):
            gate, up = _make_split((0, D_FF, 2 * D_FF))(_make_mm(f"{tag}_gu")(h, gu[0], gu[1]))
            act, = _make_rowwise(_f_swiglu, 2, 0, (BF16,), f"{tag}_swiglu")(gate, up)
            f = _make_mm(f"{tag}_down")(act, down[0], down[1])
            return ln_half(f"{tag}_ln")(h, f, row2(g), row2(b))[0]

        h = ffn(h, (w['ffn1_gu'], z['ffn1_gu']), (w['ffn1_down'], z['ffn1_down']), p['ln1_g'], p['ln1_b'], "ffn1")
        cq, ckv, kr1, kr2, xbar, bg, cg, u, gate_a, gate_b, gate_c, _ = _make_split(PIN_CUTS)(
            _make_mm("w_in")(h, w['w_in'], z['w_in']))
        qn, = _make_rowwise(_f_rms, 1, 1, (BF16,), "q_rms")(cq, row2(p['mla_q_norm_g']))
        kvn, = _make_rowwise(_f_rms, 1, 1, (BF16,), "kv_rms")(ckv, row2(p['mla_kv_norm_g']))
        q_nope, q1, q2 = _make_split((0, HEADS * NOPE, HEADS * NOPE + LANES, HEADS * NOPE + 2 * LANES))(
            _make_mm("w_uq")(qn, w['w_uq'], z['w_uq']))
        k_nope, val = _make_split((0, HEADS * NOPE, HEADS * (NOPE + V_DIM)))(_make_mm("w_ukv")(kvn, w['w_ukv'], z['w_ukv']))
        rope = _make_rowwise(_f_rope, 4, 0, (BF16, BF16), "rope", n_nodiff=2)
        q1, q2 = rope(q1, q2, cos, sin)
        k1, k2 = rope(kr1, kr2, cos, sin)
        hpad = jnp.zeros((HEADS, length, HEAD_PAD - QK_DIM), BF16)
        q3 = jnp.concatenate([_heads_first(q_nope.astype(BF16)), _heads_first(q1), _heads_first(q2), hpad], -1)
        shared = lambda t: jnp.broadcast_to(t[None, :, :HALF_ROPE], (HEADS, length, HALF_ROPE))
        k3 = jnp.concatenate([_heads_first(k_nope.astype(BF16)), shared(k1), shared(k2), hpad], -1)
        v3 = _heads_first(val.astype(BF16))
        o3 = _attention(q3, k3, v3)
        y_a = _make_mm("mla_w_o")(o3.transpose(1, 0, 2).reshape(length, MIX), w['mla_w_o'], z['mla_w_o'])
        conv = _short_conv(xbar, bg, cg, p['conv_w_full'], row2(p['conv_b']))
        y_b = _make_mm("conv_w_out")(conv, w['conv_w_out'], z['conv_w_out'])
        ar, ai, b_map, c_map = _s5_discretize(p['s5_a_re'], p['s5_a_im'], p['s5_log_dt'], p['s5_b_re'], p['s5_b_im'],
                                              p['s5_c_re'], p['s5_c_im'])
        bu = _make_mm_f32w("s5_b")(u, b_map)
        states = _s5_scan(ar, ai, bu.reshape(length, 2 * SCAN_ROWS, LANES)).reshape(length, 2 * S5_CH)
        y_ssm = _make_mm_f32w("s5_c")(states, c_map)
        zed, = _make_rowwise(_f_gelu_skip, 2, 1, (F32,), "s5_gelu")(y_ssm, u, row2(p['s5_d']))
        t = _make_mm("s5_w_glu")(zed, w['s5_w_glu'], z['s5_w_glu'])
        glu, = _make_rowwise(_f_glu, 2, 1, (BF16,), "s5_glu")(zed, t, row2(p['s5_b_glu']))
        y_c = _make_mm("s5_w_out")(glu, w['s5_w_out'], z['s5_w_out'])
        mixed, = _make_rowwise(_f_merge, 6, 0, (BF16,), "merge")(gate_a, gate_b, gate_c, y_a, y_b, y_c)
        mix_out = _make_mm("w_o")(mixed, w['w_o'], z['w_o'])
        h, = _make_rowwise(_f_ln_full, 2, 2, (F32,), "mix_ln")(h, mix_out, row2(p['ln2_g']), row2(p['ln2_b']))
        h = ffn(h, (w['ffn2_gu'], z['ffn2_gu']), (w['ffn2_down'], z['ffn2_down']), p['ln3_g'], p['ln3_b'], "ffn2")
    return _make_loss(n_real)(h, diff['target'])


def kernel(x, meta, ffn1_w_gate, ffn1_w_up, ffn1_w_down, ln1_g, ln1_b, w_in, mla_q_norm_g, mla_w_uq, mla_kv_norm_g, mla_w_ukv, mla_w_o, conv_w, conv_b, conv_w_out, s5_a_re, s5_a_im, s5_log_dt, s5_b_re, s5_b_im, s5_c_re, s5_c_im, s5_d, s5_w_glu, s5_b_glu, s5_w_out, w_o, ln2_g, ln2_b, ffn2_w_gate, ffn2_w_up, ffn2_w_down, ln3_g, ln3_b, loss_target, m_meta, m_ffn1_w_gate, m_ffn1_w_up, m_ffn1_w_down, m_ln1_g, m_ln1_b, m_w_in, m_mla_q_norm_g, m_mla_w_uq, m_mla_kv_norm_g, m_mla_w_ukv, m_mla_w_o, m_conv_w, m_conv_b, m_conv_w_out, m_s5_a_re, m_s5_a_im, m_s5_log_dt, m_s5_b_re, m_s5_b_im, m_s5_c_re, m_s5_c_im, m_s5_d, m_s5_w_glu, m_s5_b_glu, m_s5_w_out, m_w_o, m_ln2_g, m_ln2_b, m_ffn2_w_gate, m_ffn2_w_up, m_ffn2_w_down, m_ln3_g, m_ln3_b, v_meta, v_ffn1_w_gate, v_ffn1_w_up, v_ffn1_w_down, v_ln1_g, v_ln1_b, v_w_in, v_mla_q_norm_g, v_mla_w_uq, v_mla_kv_norm_g, v_mla_w_ukv, v_mla_w_o, v_conv_w, v_conv_b, v_conv_w_out, v_s5_a_re, v_s5_a_im, v_s5_log_dt, v_s5_b_re, v_s5_b_im, v_s5_c_re, v_s5_c_im, v_s5_d, v_s5_w_glu, v_s5_b_glu, v_s5_w_out, v_w_o, v_ln2_g, v_ln2_b, v_ffn2_w_gate, v_ffn2_w_up, v_ffn2_w_down, v_ln3_g, v_ln3_b):
    args = locals()
    w = {n: args[n] for n in WEIGHT_NAMES}
    m = {n: args["m_" + n] for n in WEIGHT_NAMES}
    v = {n: args["v_" + n] for n in WEIGHT_NAMES}
    me = 4 * lax.axis_index("x") + 2 * lax.axis_index("y") + lax.axis_index("c")
    seq = x.shape[1]
    n_real = N_META + seq
    length = -(-n_real // LANES) * LANES

    shard_shapes = {n: w[n].shape for n in BIG}
    gathered = _all_gather(_pack_groups(w, BF16), "gather_weights")
    big = _assemble(gathered, shard_shapes)
    small_shards = _all_gather([_pack_rows([w[n] for n in SMALL_SHARDED])], "gather_small")[0].reshape(N_DEV, -1)
    meta_full = _full_weight(small_shards[:, :meta.size].reshape(N_DEV, 1, *meta.shape), 2)[0]
    conv_w_full = _full_weight(small_shards[:, meta.size:meta.size + conv_w.size].reshape(N_DEV, *conv_w.shape), 2)

    small = {n: w[n] for n in SMALL_NAMES if n not in SMALL_SHARDED}
    small['conv_w_full'] = conv_w_full
    small['meta'] = meta_full
    wz = jax.tree.map(lambda t: jnp.zeros(t.shape, F32), big)
    pad_rows = length - n_real

    def loss_fn(diff):
        h0 = jnp.concatenate([diff['small']['meta'], diff['x'], jnp.zeros((pad_rows, D_MODEL), F32)], axis=0)
        target = jnp.pad(loss_target[0], ((N_META, pad_rows), (0, 0)))
        return _local_loss(dict(h0=h0, small=diff['small'], wz=diff['wz'], target=target), big, n_real)

    loss_local, grads = jax.value_and_grad(loss_fn)(dict(x=x[0], small=small, wz=wz))
    loss = lax.psum(loss_local, AXES)

    small_names = [n for n in SMALL_NAMES if n not in SMALL_SHARDED] + ['conv_w_full', 'meta']
    small_flat = _pack_rows([grads['small'][n] for n in small_names])
    rows_each = -(-small_flat.shape[0] // (8 * N_DEV)) * 8
    small_flat = jnp.pad(small_flat, ((0, rows_each * N_DEV - small_flat.shape[0]), (0, 0)))
    *big_sums, small_sum = _reduce_scatter(
        _disassemble(grads['wz']) + [small_flat.reshape(N_DEV, rows_each, PACK_COLS)], "grads")
    small_all, = _all_gather([small_sum], "gather_small_grads")
    small_grads = dict(zip(small_names, _unpack_rows(small_all, [grads['small'][n].shape for n in small_names])))
    g = _unpack_groups(big_sums, shard_shapes)
    for name in SMALL_NAMES:
        if name == 'meta':
            g[name] = lax.dynamic_slice_in_dim(small_grads['meta'], me * meta.shape[1], meta.shape[1], axis=1)
        elif name == 'conv_w':
            g[name] = lax.dynamic_slice_in_dim(small_grads['conv_w_full'], me * conv_w.shape[2], conv_w.shape[2], axis=2)
        else:
            g[name] = small_grads[name]

    delta, new_m, new_v = {}, {}, {}
    for name in BIG:
        two_d = lambda t: t.reshape(-1, t.shape[-1])
        d, nm, nv = _adamw(two_d(w[name]), two_d(g[name]), two_d(m[name]), two_d(v[name]), f"adamw_{name}")
        delta[name], new_m[name], new_v[name] = (t.reshape(w[name].shape) for t in (d, nm, nv))
    shapes = [w[n].shape for n in SMALL_NAMES]
    d, nm, nv = _adamw(*[_pack_rows([t[n] for n in SMALL_NAMES]) for t in (w, g, m, v)], "adamw_small")
    for out, buf in ((delta, d), (new_m, nm), (new_v, nv)):
        out.update(zip(SMALL_NAMES, _unpack_rows(buf, shapes)))

    return (loss, grads['x'][None], *[g[n] for n in WEIGHT_NAMES], *[delta[n] for n in WEIGHT_NAMES],
            *[new_m[n] for n in WEIGHT_NAMES], *[new_v[n] for n in WEIGHT_NAMES])
```

```python
import functools
import math

import jax
import jax.numpy as jnp
import numpy as np
from jax import lax
from jax.experimental import pallas as pl
from jax.experimental.pallas import tpu as pltpu

F32 = jnp.float32
BF16 = jnp.bfloat16

D_MODEL = 1024
DEPTH = 2
N_META = 16
HEADS = 8
V_DIM = 64
NOPE = 64
ROPE = 32
HALF_ROPE = ROPE // 2
QK_DIM = NOPE + ROPE
Q_RANK = 384
KV_RANK = 256
MIX = 512
CONV_K = 3
S5_GROUPS = 32
S5_GROUP = 16
S5_STATE = 64
S5_CH = S5_GROUPS * S5_STATE
D_FF = 2816
ALPHA = (2.0 * DEPTH) ** 0.25
LN_EPS = 1e-5
RMS_EPS = 1e-6
ROPE_BASE = 10000.0
IN_SPLITS = (Q_RANK, KV_RANK, ROPE, MIX, MIX, MIX, MIX, 3 * D_MODEL)
D_IN = sum(IN_SPLITS)
ADAM_LR, ADAM_B1, ADAM_B2, ADAM_EPS, ADAM_WD, ADAM_STEP = 0.001, 0.9, 0.999, 1e-08, 0.01, 10

N_DEV = 8
AXES = ("x", "y", "c")
LANES = 128
PACK_COLS = 1024
HEAD_PAD = 128
VMEM_LIMIT = 48 * 1024 * 1024

PIN_CQ, PIN_CKV, PIN_KR1, PIN_KR2, PIN_XBAR, PIN_BG, PIN_CG, PIN_U, PIN_GATES, PIN_END = (
    0, 384, 640, 768, 896, 1408, 1920, 2432, 2944, 6016)
D_IN_PAD = 6144

WEIGHT_NAMES = ['meta', 'ffn1_w_gate', 'ffn1_w_up', 'ffn1_w_down', 'ln1_g', 'ln1_b', 'w_in', 'mla_q_norm_g', 'mla_w_uq',
                'mla_kv_norm_g', 'mla_w_ukv', 'mla_w_o', 'conv_w', 'conv_b', 'conv_w_out', 's5_a_re', 's5_a_im',
                's5_log_dt', 's5_b_re', 's5_b_im', 's5_c_re', 's5_c_im', 's5_d', 's5_w_glu', 's5_b_glu', 's5_w_out',
                'w_o', 'ln2_g', 'ln2_b', 'ffn2_w_gate', 'ffn2_w_up', 'ffn2_w_down', 'ln3_g', 'ln3_b']
BIG = {'ffn1_w_gate': 2, 'ffn1_w_up': 2, 'ffn1_w_down': 1, 'w_in': 2, 'mla_w_uq': 2, 'mla_w_ukv': 2, 'mla_w_o': 2,
       'conv_w_out': 2, 's5_w_glu': 1, 's5_w_out': 2, 'w_o': 1, 'ffn2_w_gate': 2, 'ffn2_w_up': 2, 'ffn2_w_down': 1}
SMALL_SHARDED = ('meta', 'conv_w')
SMALL_NAMES = [n for n in WEIGHT_NAMES if n not in BIG]


def _divisor_tile(n, limit, mult):
    best = None
    for t in range(mult, min(n, limit) + 1, mult):
        if n % t == 0:
            best = t
    return best if best is not None else n


def _params(*sem):
    return pltpu.CompilerParams(dimension_semantics=sem, vmem_limit_bytes=VMEM_LIMIT)


def _matmul(a, b, *, ta=False, tb=False, out_dtype=F32, name):
    m, k = (a.shape[1], a.shape[0]) if ta else a.shape
    n = b.shape[0] if tb else b.shape[1]
    assert (b.shape[1] if tb else b.shape[0]) == k, (a.shape, b.shape, ta, tb)
    tm = _divisor_tile(m, 1408, LANES) if ta else _divisor_tile(m, 1088, 16)
    tn = _divisor_tile(n, 512, LANES)
    tk = _divisor_tile(k, 1408, 16 if ta else LANES)
    nk = k // tk
    dims = (((0 if ta else 1,), (1 if tb else 0,)), ((), ()))

    def kern(a_ref, b_ref, o_ref, acc_ref):
        kk = pl.program_id(2)
        part = lax.dot_general(a_ref[...].astype(BF16), b_ref[...].astype(BF16), dims, preferred_element_type=F32)

        @pl.when(kk == 0)
        def _():
            acc_ref[...] = part

        @pl.when(kk > 0)
        def _():
            acc_ref[...] += part

        @pl.when(kk == nk - 1)
        def _():
            o_ref[...] = acc_ref[...].astype(o_ref.dtype)

    a_spec = pl.BlockSpec((tk, tm), lambda i, j, kk: (kk, i)) if ta else pl.BlockSpec((tm, tk), lambda i, j, kk: (i, kk))
    b_spec = pl.BlockSpec((tn, tk), lambda i, j, kk: (j, kk)) if tb else pl.BlockSpec((tk, tn), lambda i, j, kk: (kk, j))
    return pl.pallas_call(
        kern, name=name, grid=(m // tm, n // tn, nk),
        in_specs=[a_spec, b_spec], out_specs=pl.BlockSpec((tm, tn), lambda i, j, kk: (i, j)),
        out_shape=jax.ShapeDtypeStruct((m, n), out_dtype),
        scratch_shapes=[pltpu.VMEM((tm, tn), F32)],
        compiler_params=_params("parallel", "parallel", "arbitrary"),
    )(a, b)


def _make_mm(name, wt=False):
    @jax.custom_vjp
    def mm(x, w, wz):
        return _matmul(x, w, tb=wt, name=name + "_fwd")

    def fwd(x, w, wz):
        return _matmul(x, w, tb=wt, name=name + "_fwd"), (x, w)

    def bwd(res, dy):
        x, w = res
        dx = _matmul(dy, w, tb=not wt, out_dtype=x.dtype, name=name + "_dx")
        dw = _matmul(dy, x, ta=True, name=name + "_dw") if wt else _matmul(x, dy, ta=True, name=name + "_dw")
        return dx, jnp.zeros_like(w), dw

    mm.defvjp(fwd, bwd)
    return mm


def _make_mm_f32w(name):
    @jax.custom_vjp
    def mm(x, w):
        return _matmul(x, w, name=name + "_fwd")

    def fwd(x, w):
        return _matmul(x, w, name=name + "_fwd"), (x, w)

    def bwd(res, dy):
        x, w = res
        return (_matmul(dy, w, tb=True, out_dtype=x.dtype, name=name + "_dx"),
                _matmul(x, dy, ta=True, name=name + "_dw"))

    mm.defvjp(fwd, bwd)
    return mm


def _row_tile(rows, widths):
    limit = max(16, (6 * 1024 * 1024 // 4) // max(1, sum(widths)))
    return _divisor_tile(rows, limit, 16)


def _make_rowwise(f, n_rows, n_pars, out_dtypes, name, n_nodiff=0):
    n_out = len(out_dtypes)
    n_diff = n_rows - n_nodiff

    def run_fwd(rows, pars):
        length = rows[0].shape[0]
        shapes = jax.eval_shape(lambda *a: f(*a), *[jax.ShapeDtypeStruct((16, r.shape[1]), F32) for r in rows],
                                *[jax.ShapeDtypeStruct(p.shape, F32) for p in pars])
        widths = [s.shape[1] for s in shapes]
        tm = _row_tile(length, [r.shape[1] for r in rows] + widths)

        def kern(*refs):
            ins = [r[...].astype(F32) for r in refs[:n_rows + n_pars]]
            outs = f(*ins)
            for o_ref, o in zip(refs[n_rows + n_pars:], outs):
                o_ref[...] = o.astype(o_ref.dtype)

        return pl.pallas_call(
            kern, name=name + "_fwd", grid=(length // tm,),
            in_specs=[pl.BlockSpec((tm, r.shape[1]), lambda i: (i, 0)) for r in rows]
            + [pl.BlockSpec(p.shape, lambda i: (0, 0)) for p in pars],
            out_specs=[pl.BlockSpec((tm, w), lambda i: (i, 0)) for w in widths],
            out_shape=[jax.ShapeDtypeStruct((length, w), dt) for w, dt in zip(widths, out_dtypes)],
            compiler_params=_params("parallel"),
        )(*rows, *pars)

    def run_bwd(rows, pars, cts):
        length = rows[0].shape[0]
        tm = _row_tile(length, [r.shape[1] for r in rows] * 2 + [c.shape[1] for c in cts] * 2)

        def kern(*refs):
            ins = [r[...].astype(F32) for r in refs[:n_rows + n_pars]]
            ct = [r[...].astype(F32) for r in refs[n_rows + n_pars:n_rows + n_pars + n_out]]
            out_refs = refs[n_rows + n_pars + n_out:]
            nodiff = ins[n_diff:n_rows]
            _, vjp = jax.vjp(lambda *a: f(*a[:n_diff], *nodiff, *a[n_diff:]), *ins[:n_diff], *ins[n_rows:])
            grads = vjp(tuple(ct))
            for o_ref, g in zip(out_refs[:n_diff], grads[:n_diff]):
                o_ref[...] = g.astype(o_ref.dtype)
            first = pl.program_id(0) == 0
            for o_ref, g in zip(out_refs[n_diff:], grads[n_diff:]):
                @pl.when(first)
                def _(o_ref=o_ref, g=g):
                    o_ref[...] = g

                @pl.when(jnp.logical_not(first))
                def _(o_ref=o_ref, g=g):
                    o_ref[...] += g

        return pl.pallas_call(
            kern, name=name + "_bwd", grid=(length // tm,),
            in_specs=[pl.BlockSpec((tm, r.shape[1]), lambda i: (i, 0)) for r in rows]
            + [pl.BlockSpec(p.shape, lambda i: (0, 0)) for p in pars]
            + [pl.BlockSpec((tm, c.shape[1]), lambda i: (i, 0)) for c in cts],
            out_specs=[pl.BlockSpec((tm, r.shape[1]), lambda i: (i, 0)) for r in rows[:n_diff]]
            + [pl.BlockSpec(p.shape, lambda i: (0, 0)) for p in pars],
            out_shape=[jax.ShapeDtypeStruct(r.shape, r.dtype) for r in rows[:n_diff]]
            + [jax.ShapeDtypeStruct(p.shape, F32) for p in pars],
            compiler_params=_params("arbitrary"),
        )(*rows, *pars, *cts)

    @jax.custom_vjp
    def op(*args):
        return tuple(run_fwd(args[:n_rows], args[n_rows:]))

    def fwd(*args):
        return tuple(run_fwd(args[:n_rows], args[n_rows:])), args

    def bwd(args, cts):
        grads = run_bwd(args[:n_rows], args[n_rows:], cts)
        zeros = [jnp.zeros_like(r) for r in args[n_diff:n_rows]]
        return (*grads[:n_diff], *zeros, *grads[n_diff:])

    op.defvjp(fwd, bwd)
    return op


def _layer_norm(z, g, b):
    mu = jnp.mean(z, axis=-1, keepdims=True)
    d = z - mu
    var = jnp.mean(d * d, axis=-1, keepdims=True)
    return d * lax.rsqrt(var + LN_EPS) * g + b


def _f_ln_half(h, f, g, b):
    return (_layer_norm(ALPHA * h + 0.5 * f, g, b),)


def _f_ln_full(h, f, g, b):
    return (_layer_norm(ALPHA * h + f, g, b),)


def _f_rms(x, g):
    return (x * lax.rsqrt(jnp.mean(x * x, axis=-1, keepdims=True) + RMS_EPS) * g,)


def _f_rope(x1, x2, cos, sin):
    return x1 * cos - x2 * sin, x2 * cos + x1 * sin


def _f_gelu_skip(y, u, d):
    return (jax.nn.gelu(y + d * u),)


def _f_glu(z, t, b):
    return (z * jax.nn.sigmoid(t + b),)


def _f_merge(ga, gb, gc, ya, yb, yc):
    return (jax.nn.sigmoid(ga) * ya + jax.nn.sigmoid(gb) * yb + jax.nn.sigmoid(gc) * yc,)


def _make_swiglu(name):
    def run_fwd(gu):
        length = gu.shape[0]
        tm = _row_tile(length, [3 * D_FF])

        def kern(g_ref, u_ref, o_ref):
            o_ref[...] = (jax.nn.silu(g_ref[...]) * u_ref[...]).astype(o_ref.dtype)

        half = lambda j: pl.BlockSpec((tm, D_FF), lambda i: (i, j))
        return pl.pallas_call(
            kern, name=name + "_fwd", grid=(length // tm,), in_specs=[half(0), half(1)], out_specs=half(0),
            out_shape=jax.ShapeDtypeStruct((length, D_FF), BF16), compiler_params=_params("parallel"),
        )(gu, gu)

    def run_bwd(gu, dact):
        length = gu.shape[0]
        tm = _row_tile(length, [6 * D_FF])

        def kern(g_ref, u_ref, d_ref, o_ref):
            _, vjp = jax.vjp(lambda g, u: jax.nn.silu(g) * u, g_ref[...], u_ref[...])
            dg, du = vjp(d_ref[...].astype(F32))
            o_ref[:, :D_FF] = dg
            o_ref[:, D_FF:] = du

        half = lambda j: pl.BlockSpec((tm, D_FF), lambda i: (i, j))
        return pl.pallas_call(
            kern, name=name + "_bwd", grid=(length // tm,), in_specs=[half(0), half(1), half(0)],
            out_specs=pl.BlockSpec((tm, 2 * D_FF), lambda i: (i, 0)),
            out_shape=jax.ShapeDtypeStruct(gu.shape, F32), compiler_params=_params("parallel"),
        )(gu, gu, dact)

    @jax.custom_vjp
    def op(gu):
        return run_fwd(gu)

    def fwd(gu):
        return run_fwd(gu), gu

    def bwd(gu, dact):
        return (run_bwd(gu, dact),)

    op.defvjp(fwd, bwd)
    return op


def _attn_scores(q, k, q_block, tq):
    length = k.shape[0]
    s = lax.dot_general(q, k, (((1,), (1,)), ((), ())), preferred_element_type=F32) * (QK_DIM ** -0.5)
    row = q_block * tq + lax.broadcasted_iota(jnp.int32, (tq, length), 0)
    col = lax.broadcasted_iota(jnp.int32, (tq, length), 1)
    s = jnp.where(col <= row, s, -1e30)
    e = jnp.exp(s - jnp.max(s, axis=1, keepdims=True))
    return e / jnp.sum(e, axis=1, keepdims=True)


ATTN_TQ = 128


def _attn_fwd(q3, k3, v3):
    heads, length, _ = q3.shape
    tq = ATTN_TQ

    def kern(q_ref, k_ref, v_ref, o_ref):
        p = _attn_scores(q_ref[0], k_ref[0], pl.program_id(1), tq)
        o_ref[0] = jnp.dot(p.astype(BF16), v_ref[0], preferred_element_type=F32).astype(o_ref.dtype)

    return pl.pallas_call(
        kern, name="attn_fwd", grid=(heads, length // tq),
        in_specs=[pl.BlockSpec((1, tq, HEAD_PAD), lambda h, i: (h, i, 0)),
                  pl.BlockSpec((1, length, HEAD_PAD), lambda h, i: (h, 0, 0)),
                  pl.BlockSpec((1, length, V_DIM), lambda h, i: (h, 0, 0))],
        out_specs=pl.BlockSpec((1, tq, V_DIM), lambda h, i: (h, i, 0)),
        out_shape=jax.ShapeDtypeStruct((heads, length, V_DIM), F32),
        compiler_params=_params("parallel", "parallel"),
    )(q3, k3, v3)


def _attn_bwd(q3, k3, v3, do3):
    heads, length, _ = q3.shape
    tq = ATTN_TQ

    def kern(q_ref, k_ref, v_ref, do_ref, dq_ref, dk_ref, dv_ref):
        i = pl.program_id(1)
        q, k, v, do = q_ref[0], k_ref[0], v_ref[0], do_ref[0].astype(BF16)
        p = _attn_scores(q, k, i, tq)
        dp = lax.dot_general(do, v, (((1,), (1,)), ((), ())), preferred_element_type=F32)
        ds = (p * (dp - jnp.sum(p * dp, axis=1, keepdims=True)) * (QK_DIM ** -0.5)).astype(BF16)
        dq_ref[0] = jnp.dot(ds, k, preferred_element_type=F32)
        dk = lax.dot_general(ds, q, (((0,), (0,)), ((), ())), preferred_element_type=F32)
        dv = lax.dot_general(p.astype(BF16), do, (((0,), (0,)), ((), ())), preferred_element_type=F32)

        @pl.when(i == 0)
        def _():
            dk_ref[0] = dk
            dv_ref[0] = dv

        @pl.when(i > 0)
        def _():
            dk_ref[0] += dk
            dv_ref[0] += dv

    return pl.pallas_call(
        kern, name="attn_bwd", grid=(heads, length // tq),
        in_specs=[pl.BlockSpec((1, tq, HEAD_PAD), lambda h, i: (h, i, 0)),
                  pl.BlockSpec((1, length, HEAD_PAD), lambda h, i: (h, 0, 0)),
                  pl.BlockSpec((1, length, V_DIM), lambda h, i: (h, 0, 0)),
                  pl.BlockSpec((1, tq, V_DIM), lambda h, i: (h, i, 0))],
        out_specs=[pl.BlockSpec((1, tq, HEAD_PAD), lambda h, i: (h, i, 0)),
                   pl.BlockSpec((1, length, HEAD_PAD), lambda h, i: (h, 0, 0)),
                   pl.BlockSpec((1, length, V_DIM), lambda h, i: (h, 0, 0))],
        out_shape=[jax.ShapeDtypeStruct((heads, length, HEAD_PAD), F32),
                   jax.ShapeDtypeStruct((heads, length, HEAD_PAD), F32),
                   jax.ShapeDtypeStruct((heads, length, V_DIM), F32)],
        compiler_params=_params("parallel", "arbitrary"),
    )(q3, k3, v3, do3)


@jax.custom_vjp
def _attention(q3, k3, v3):
    return _attn_fwd(q3, k3, v3)


def _attention_fwd(q3, k3, v3):
    return _attn_fwd(q3, k3, v3), (q3, k3, v3)


def _attention_bwd(res, do3):
    q3, k3, v3 = res
    dq, dk, dv = _attn_bwd(q3, k3, v3, do3)
    return dq.astype(q3.dtype), dk.astype(k3.dtype), dv.astype(v3.dtype)


_attention.defvjp(_attention_fwd, _attention_bwd)


def _conv_terms(x, c, w_ref, cb):
    u = c * x
    row = lax.broadcasted_iota(jnp.int32, u.shape, 0)
    u1 = jnp.where(row >= 1, pltpu.roll(u, 1, 0), 0.0)
    u2 = jnp.where(row >= 2, pltpu.roll(u, 2, 0), 0.0)
    y = cb + w_ref[0:1, :] * u2 + w_ref[1:2, :] * u1 + w_ref[2:3, :] * u
    return u, u1, u2, y


def _conv_specs(length):
    col = pl.BlockSpec((length, LANES), lambda j: (0, j))
    return col, pl.BlockSpec((CONV_K, LANES), lambda j: (0, j)), pl.BlockSpec((1, LANES), lambda j: (0, j))


def _conv_fwd(x, b, c, w, cb):
    length = x.shape[0]

    def kern(x_ref, b_ref, c_ref, w_ref, cb_ref, o_ref):
        _, _, _, y = _conv_terms(x_ref[...], c_ref[...], w_ref, cb_ref[...])
        o_ref[...] = b_ref[...] * y

    col, wspec, bspec = _conv_specs(length)
    return pl.pallas_call(
        kern, name="conv_fwd", grid=(MIX // LANES,), in_specs=[col, col, col, wspec, bspec], out_specs=col,
        out_shape=jax.ShapeDtypeStruct((length, MIX), F32), compiler_params=_params("parallel"),
    )(x, b, c, w, cb)


def _conv_bwd(x, b, c, w, cb, do):
    length = x.shape[0]

    def kern(x_ref, b_ref, c_ref, w_ref, cb_ref, do_ref, dx_ref, db_ref, dc_ref, dw_ref, dcb_ref):
        x, c, do = x_ref[...], c_ref[...], do_ref[...]
        u, u1, u2, y = _conv_terms(x, c, w_ref, cb_ref[...])
        db_ref[...] = do * y
        dy = do * b_ref[...]
        row = lax.broadcasted_iota(jnp.int32, dy.shape, 0)
        dy1 = jnp.where(row < length - 1, pltpu.roll(dy, length - 1, 0), 0.0)
        dy2 = jnp.where(row < length - 2, pltpu.roll(dy, length - 2, 0), 0.0)
        du = w_ref[2:3, :] * dy + w_ref[1:2, :] * dy1 + w_ref[0:1, :] * dy2
        dx_ref[...] = du * c
        dc_ref[...] = du * x
        dw_ref[0:1, :] = jnp.sum(dy * u2, axis=0, keepdims=True)
        dw_ref[1:2, :] = jnp.sum(dy * u1, axis=0, keepdims=True)
        dw_ref[2:3, :] = jnp.sum(dy * u, axis=0, keepdims=True)
        dcb_ref[...] = jnp.sum(dy, axis=0, keepdims=True)

    col, wspec, bspec = _conv_specs(length)
    big = jax.ShapeDtypeStruct((length, MIX), F32)
    return pl.pallas_call(
        kern, name="conv_bwd", grid=(MIX // LANES,), in_specs=[col, col, col, wspec, bspec, col],
        out_specs=[col, col, col, wspec, bspec],
        out_shape=[big, big, big, jax.ShapeDtypeStruct((CONV_K, MIX), F32), jax.ShapeDtypeStruct((1, MIX), F32)],
        compiler_params=_params("parallel"),
    )(x, b, c, w, cb, do)


@jax.custom_vjp
def _short_conv(x, b, c, w, cb):
    return _conv_fwd(x, b, c, w, cb)


def _short_conv_fwd(x, b, c, w, cb):
    return _conv_fwd(x, b, c, w, cb), (x, b, c, w, cb)


def _short_conv_bwd(res, do):
    return tuple(_conv_bwd(*res, do))


_short_conv.defvjp(_short_conv_fwd, _short_conv_bwd)


SCAN_ROWS = S5_CH // LANES
SCAN_TC = 136


def _scan_fwd(ar, ai, b):
    length = b.shape[0]
    tc = _divisor_tile(length, SCAN_TC, 8)

    def kern(ar_ref, ai_ref, b_ref, x_ref, sr, si):
        @pl.when(pl.program_id(0) == 0)
        def _():
            sr[...] = jnp.zeros_like(sr)
            si[...] = jnp.zeros_like(si)

        a_re, a_im = ar_ref[...], ai_ref[...]

        def body(t, carry):
            xr, xi = carry
            nr = a_re * xr - a_im * xi + b_ref[t, 0:SCAN_ROWS, :]
            ni = a_re * xi + a_im * xr + b_ref[t, SCAN_ROWS:2 * SCAN_ROWS, :]
            x_ref[t, 0:SCAN_ROWS, :] = nr
            x_ref[t, SCAN_ROWS:2 * SCAN_ROWS, :] = ni
            return nr, ni

        xr, xi = lax.fori_loop(0, tc, body, (sr[...], si[...]), unroll=4)
        sr[...] = xr
        si[...] = xi

    par = pl.BlockSpec((SCAN_ROWS, LANES), lambda i: (0, 0))
    blk = pl.BlockSpec((tc, 2 * SCAN_ROWS, LANES), lambda i: (i, 0, 0))
    return pl.pallas_call(
        kern, name="s5_scan_fwd", grid=(length // tc,), in_specs=[par, par, blk], out_specs=blk,
        out_shape=jax.ShapeDtypeStruct(b.shape, F32), scratch_shapes=[pltpu.VMEM((SCAN_ROWS, LANES), F32)] * 2,
        compiler_params=_params("arbitrary"),
    )(ar, ai, b)


def _scan_bwd(ar, ai, x, dx):
    length = x.shape[0]
    tc = _divisor_tile(length, SCAN_TC, 8)
    n_blk = length // tc
    re, im = slice(0, SCAN_ROWS), slice(SCAN_ROWS, 2 * SCAN_ROWS)

    def kern(ar_ref, ai_ref, x_ref, dx_ref, db_ref, dar_ref, dai_ref, lr_s, li_s):
        @pl.when(pl.program_id(0) == 0)
        def _():
            lr_s[...] = jnp.zeros_like(lr_s)
            li_s[...] = jnp.zeros_like(li_s)
            dar_ref[...] = jnp.zeros_like(dar_ref)
            dai_ref[...] = jnp.zeros_like(dai_ref)

        a_re, a_im = ar_ref[...], ai_ref[...]

        def body(j, carry):
            t = tc - 1 - j
            lr, li, gr, gi = carry
            x_re, x_im = x_ref[t, re, :], x_ref[t, im, :]
            gr = gr + (lr * x_re + li * x_im)
            gi = gi + (li * x_re - lr * x_im)
            nlr = dx_ref[t, re, :] + (a_re * lr + a_im * li)
            nli = dx_ref[t, im, :] + (a_re * li - a_im * lr)
            db_ref[t, re, :] = nlr
            db_ref[t, im, :] = nli
            return nlr, nli, gr, gi

        lr, li, gr, gi = lax.fori_loop(0, tc, body, (lr_s[...], li_s[...], dar_ref[...], dai_ref[...]), unroll=4)
        lr_s[...] = lr
        li_s[...] = li
        dar_ref[...] = gr
        dai_ref[...] = gi

    par = pl.BlockSpec((SCAN_ROWS, LANES), lambda i: (0, 0))
    blk = pl.BlockSpec((tc, 2 * SCAN_ROWS, LANES), lambda i: (n_blk - 1 - i, 0, 0))
    pout = jax.ShapeDtypeStruct((SCAN_ROWS, LANES), F32)
    return pl.pallas_call(
        kern, name="s5_scan_bwd", grid=(n_blk,), in_specs=[par, par, blk, blk],
        out_specs=[blk, par, par], out_shape=[jax.ShapeDtypeStruct(x.shape, F32), pout, pout],
        scratch_shapes=[pltpu.VMEM((SCAN_ROWS, LANES), F32)] * 2, compiler_params=_params("arbitrary"),
    )(ar, ai, x, dx)


@jax.custom_vjp
def _s5_scan(ar, ai, b):
    return _scan_fwd(ar, ai, b)


def _s5_scan_fwd(ar, ai, b):
    x = _scan_fwd(ar, ai, b)
    return x, (ar, ai, x)


def _s5_scan_bwd(res, dx):
    ar, ai, x = res
    db, dar, dai = _scan_bwd(ar, ai, x, dx)
    return dar, dai, db


_s5_scan.defvjp(_s5_scan_fwd, _s5_scan_bwd)


def _loss_call(y, target, n_real):
    length = y.shape[0]
    tm = _divisor_tile(length, 544, 16)

    def kern(y_ref, t_ref, loss_ref, dy_ref):
        i = pl.program_id(0)
        row = i * tm + lax.broadcasted_iota(jnp.int32, (tm, 1), 0)
        keep = jnp.logical_and(row >= N_META, row < n_real)
        err = jnp.where(keep, y_ref[...] - t_ref[...], 0.0)
        dy_ref[...] = err * (1.0 / D_MODEL)
        part = 0.5 * jnp.sum(jnp.mean(err * err, axis=-1, keepdims=True), axis=0, keepdims=True)

        @pl.when(i == 0)
        def _():
            loss_ref[...] = jnp.zeros_like(loss_ref)

        loss_ref[...] += part

    blk = pl.BlockSpec((tm, D_MODEL), lambda i: (i, 0))
    return pl.pallas_call(
        kern, name="loss_head", grid=(length // tm,), in_specs=[blk, blk],
        out_specs=[pl.BlockSpec((8, LANES), lambda i: (0, 0)), blk],
        out_shape=[jax.ShapeDtypeStruct((8, LANES), F32), jax.ShapeDtypeStruct(y.shape, F32)],
        compiler_params=_params("arbitrary"),
    )(y, target)


def _make_loss(n_real):
    @jax.custom_vjp
    def loss(y, target):
        return _loss_call(y, target, n_real)[0][0, 0]

    def fwd(y, target):
        total, dy = _loss_call(y, target, n_real)
        return total[0, 0], dy

    def bwd(dy, ct):
        return dy * ct, jnp.zeros_like(dy)

    loss.defvjp(fwd, bwd)
    return loss


HBM_SPEC = pl.BlockSpec(memory_space=pl.ANY)
MESH_ID = pl.DeviceIdType.MESH


def _all_gather(shards, name):
    n = len(shards)

    def body(*refs):
        x_refs, out_refs = refs[:n], refs[n:2 * n]
        send_sems, recv_sems, local_sems = refs[2 * n:]
        x, y, c = lax.axis_index("x"), lax.axis_index("y"), lax.axis_index("c")
        me, sibling = (x, y, c), (x, y, 1 - c)
        chips = [(1 - x, y), (x, 1 - y), (1 - x, 1 - y)]

        def copy(b, k, block, to, from_input=False):
            px, py, pc = block
            slot = out_refs[b].at[4 * px + 2 * py + pc]
            return pltpu.make_async_remote_copy(
                src_ref=x_refs[b] if from_input else slot, dst_ref=slot,
                send_sem=send_sems.at[7 * b + k], recv_sem=recv_sems.at[7 * b + k], device_id=to, device_id_type=MESH_ID)

        mine = [pltpu.make_async_copy(x_refs[b], out_refs[b].at[4 * x + 2 * y + c], local_sems.at[b]) for b in range(n)]
        for cp in mine:
            cp.start()
        first = []
        for b in range(n):
            first.append(copy(b, 0, me, sibling, from_input=True))
            first += [copy(b, 1 + j, me, (*chip, c), from_input=True) for j, chip in enumerate(chips)]
        for cp in first:
            cp.start()
        passed = []
        for j, chip in enumerate(chips):
            for b in range(n):
                copy(b, 1 + j, (*chip, c), me).wait_recv()
                passed.append(copy(b, 4 + j, (*chip, c), sibling))
                passed[-1].start()
        for b in range(n):
            copy(b, 0, sibling, me).wait_recv()
            for j, chip in enumerate(chips):
                copy(b, 4 + j, (*chip, 1 - c), me).wait_recv()
        for cp in first + passed:
            cp.wait_send()
        for cp in mine:
            cp.wait()

    return pl.pallas_call(
        body, name=name, out_shape=[jax.ShapeDtypeStruct((N_DEV, *s.shape), s.dtype) for s in shards],
        in_specs=[HBM_SPEC] * n, out_specs=[HBM_SPEC] * n,
        scratch_shapes=[pltpu.SemaphoreType.DMA((7 * n,)), pltpu.SemaphoreType.DMA((7 * n,)), pltpu.SemaphoreType.DMA((n,))],
    )(*shards)


def _sibling_exchange(bufs, name):
    def body(*refs):
        ins, outs = refs[:len(bufs)], refs[len(bufs):2 * len(bufs)]
        send_sems, recv_sems = refs[2 * len(bufs):]
        x, y, c = lax.axis_index("x"), lax.axis_index("y"), lax.axis_index("c")
        copies = []
        for b, (src, dst) in enumerate(zip(ins, outs)):
            for chip in range(4):
                copies.append(pltpu.make_async_remote_copy(
                    src_ref=src.at[2 * chip + (1 - c)], dst_ref=dst.at[chip],
                    send_sem=send_sems.at[4 * b + chip], recv_sem=recv_sems.at[4 * b + chip],
                    device_id=(x, y, 1 - c), device_id_type=MESH_ID))
        for cp in copies:
            cp.start()
        for cp in copies:
            cp.wait()

    n = 4 * len(bufs)
    return pl.pallas_call(
        body, name=name, out_shape=[jax.ShapeDtypeStruct((4, *b.shape[1:]), b.dtype) for b in bufs],
        in_specs=[HBM_SPEC] * len(bufs), out_specs=[HBM_SPEC] * len(bufs),
        scratch_shapes=[pltpu.SemaphoreType.DMA((n,)), pltpu.SemaphoreType.DMA((n,))],
    )(*bufs)


def _chip_exchange(bufs, name):
    def body(*refs):
        ins, outs = refs[:len(bufs)], refs[len(bufs):2 * len(bufs)]
        send_sems, recv_sems = refs[2 * len(bufs):]
        x, y, c = lax.axis_index("x"), lax.axis_index("y"), lax.axis_index("c")
        chips = [(1 - x, y), (x, 1 - y), (1 - x, 1 - y)]
        copies = []
        for b, (src, dst) in enumerate(zip(ins, outs)):
            for j, (px, py) in enumerate(chips):
                copies.append(pltpu.make_async_remote_copy(
                    src_ref=src.at[2 * px + py], dst_ref=dst.at[j],
                    send_sem=send_sems.at[3 * b + j], recv_sem=recv_sems.at[3 * b + j],
                    device_id=(px, py, c), device_id_type=MESH_ID))
        for cp in copies:
            cp.start()
        for cp in copies:
            cp.wait()

    n = 3 * len(bufs)
    return pl.pallas_call(
        body, name=name, out_shape=[jax.ShapeDtypeStruct((3, *b.shape[1:]), b.dtype) for b in bufs],
        in_specs=[HBM_SPEC] * len(bufs), out_specs=[HBM_SPEC] * len(bufs),
        scratch_shapes=[pltpu.SemaphoreType.DMA((n,)), pltpu.SemaphoreType.DMA((n,))],
    )(*bufs)


def _pair_add(full, got, core, out_dtype, name):
    _, rows, cols = full.shape
    tm = _divisor_tile(rows, 640, 16)

    def kern(core_ref, a_ref, b_ref, o_ref):
        o_ref[...] = (a_ref[...] + b_ref[...]).astype(o_ref.dtype)

    return pl.pallas_call(
        kern, name=name,
        grid_spec=pltpu.PrefetchScalarGridSpec(
            num_scalar_prefetch=1, grid=(4, rows // tm),
            in_specs=[pl.BlockSpec((1, tm, cols), lambda k, i, core_ref: (2 * k + core_ref[0], i, 0)),
                      pl.BlockSpec((1, tm, cols), lambda k, i, core_ref: (k, i, 0))],
            out_specs=pl.BlockSpec((1, tm, cols), lambda k, i, core_ref: (k, i, 0))),
        out_shape=jax.ShapeDtypeStruct((4, rows, cols), out_dtype), compiler_params=_params("parallel", "parallel"),
    )(core, full, got)


def _chip_add(part, got, chip, name):
    _, rows, cols = part.shape
    tm = _divisor_tile(rows, 640, 16)

    def kern(chip_ref, a_ref, b_ref, o_ref):
        o_ref[...] = ((a_ref[0].astype(F32) + b_ref[0].astype(F32)) + b_ref[1].astype(F32)) + b_ref[2].astype(F32)

    return pl.pallas_call(
        kern, name=name,
        grid_spec=pltpu.PrefetchScalarGridSpec(
            num_scalar_prefetch=1, grid=(rows // tm,),
            in_specs=[pl.BlockSpec((1, tm, cols), lambda i, chip_ref: (chip_ref[0], i, 0)),
                      pl.BlockSpec((3, tm, cols), lambda i, chip_ref: (0, i, 0))],
            out_specs=pl.BlockSpec((tm, cols), lambda i, chip_ref: (i, 0))),
        out_shape=jax.ShapeDtypeStruct((rows, cols), F32), compiler_params=_params("parallel"),
    )(chip, part, got)


def _reduce_scatter(bufs, wire_dtypes, tag):
    core = lax.axis_index("c").astype(jnp.int32).reshape(1)
    chip = (2 * lax.axis_index("x") + lax.axis_index("y")).astype(jnp.int32).reshape(1)
    got = _sibling_exchange(bufs, f"rs_sibling_exchange_{tag}")
    parts = [_pair_add(b, g, core, dt, f"rs_pair_add_{tag}{i}")
             for i, (b, g, dt) in enumerate(zip(bufs, got, wire_dtypes))]
    arrived = _chip_exchange(parts, f"rs_chip_exchange_{tag}")
    return [_chip_add(p, a, chip, f"rs_chip_add_{tag}{i}") for i, (p, a) in enumerate(zip(parts, arrived))]


def _adamw(w, g, m, v, name):
    rows, cols = w.shape
    tm = _divisor_tile(rows, max(8, (512 * 1024) // cols // 8 * 8), 8)

    def kern(w_ref, g_ref, m_ref, v_ref, d_ref, nm_ref, nv_ref):
        g = g_ref[...]
        m = ADAM_B1 * m_ref[...] + (1.0 - ADAM_B1) * g
        v = ADAM_B2 * v_ref[...] + (1.0 - ADAM_B2) * (g * g)
        m_hat = m / (1.0 - ADAM_B1 ** ADAM_STEP)
        v_hat = v / (1.0 - ADAM_B2 ** ADAM_STEP)
        d_ref[...] = -ADAM_LR * (m_hat / (jnp.sqrt(v_hat) + ADAM_EPS) + ADAM_WD * w_ref[...])
        nm_ref[...] = m
        nv_ref[...] = v

    blk = pl.BlockSpec((tm, cols), lambda i: (i, 0))
    out = jax.ShapeDtypeStruct(w.shape, F32)
    return pl.pallas_call(
        kern, name=name, grid=(rows // tm,), in_specs=[blk] * 4, out_specs=[blk] * 3, out_shape=[out] * 3,
        compiler_params=_params("parallel"),
    )(w, g, m, v)


TRANSPOSED = ('ffn1_w_gate', 'ffn1_w_up', 'ffn2_w_gate', 'ffn2_w_up', 'w_in', 'mla_w_uq', 'mla_w_ukv', 'mla_w_o',
              'conv_w_out', 's5_w_out')
GROUPS = (('ffn1_w_gate', 'ffn1_w_up', 'ffn1_w_down', 'ffn2_w_gate', 'ffn2_w_up', 'ffn2_w_down', 'w_o'), ('w_in',),
          ('mla_w_o', 'conv_w_out', 's5_w_out', 's5_w_glu'), ('mla_w_uq',), ('mla_w_ukv',))
PIN_CUTS = (PIN_CQ, PIN_CKV, PIN_KR1, PIN_KR2, PIN_XBAR, PIN_BG, PIN_CG, PIN_U, PIN_GATES, PIN_GATES + D_MODEL,
            PIN_GATES + 2 * D_MODEL, PIN_END, D_IN_PAD)
PIN_PIECES = ((PIN_CQ, Q_RANK), (PIN_CKV, KV_RANK), (PIN_KR1, HALF_ROPE), (PIN_KR2, HALF_ROPE), (PIN_XBAR, 4 * MIX),
              (PIN_GATES, 3 * D_MODEL))


def _make_split(cuts):
    @jax.custom_vjp
    def split(t):
        return tuple(t[:, a:b] for a, b in zip(cuts[:-1], cuts[1:]))

    def fwd(t):
        return split(t), None

    def bwd(_, cts):
        return (jnp.concatenate(cts, axis=1),)

    split.defvjp(fwd, bwd)
    return split


def _travel_shape(name, shape):
    return (shape[0], shape[2], shape[1]) if name in TRANSPOSED else tuple(shape)


def _pack_groups(tensors, dtype):
    def view(n):
        t = tensors[n].transpose(0, 2, 1) if n in TRANSPOSED else tensors[n]
        return t.reshape(-1, t.shape[-1]).astype(dtype)
    return [jnp.concatenate([view(n) for n in grp], axis=0) for grp in GROUPS]


def _unpack_groups(bufs, shard_shapes):
    out = {}
    for buf, grp in zip(bufs, GROUPS):
        at = 0
        for n in grp:
            depth, r, c = _travel_shape(n, shard_shapes[n])
            out[n] = buf[..., at:at + depth * r, :].reshape(*buf.shape[:-2], depth, r, c)
            at += depth * r
    return out


def _pack_rows(arrays):
    flat = jnp.concatenate([a.reshape(-1) for a in arrays])
    rows = -(-flat.shape[0] // PACK_COLS)
    rows = -(-rows // 8) * 8
    return jnp.pad(flat, (0, rows * PACK_COLS - flat.shape[0])).reshape(rows, PACK_COLS)


def _unpack_rows(buf, shapes):
    flat = buf.reshape(-1)
    out, at = [], 0
    for s in shapes:
        n = int(np.prod(s))
        out.append(flat[at:at + n].reshape(s))
        at += n
    return out


def _full_weight(t, axis):
    if axis == 1:
        return jnp.moveaxis(t, 0, 1).reshape(t.shape[1], N_DEV * t.shape[2], t.shape[3])
    return jnp.moveaxis(t, 0, 2).reshape(t.shape[1], t.shape[2], N_DEV * t.shape[3])


def _disassemble(layers):
    per_layer = []
    for d in layers:
        w_in, uq, ukv = d['w_in'], d['w_uq'], d['w_ukv']
        heads = lambda t: t.reshape(HEADS, -1, t.shape[1])
        per_layer.append(dict(
            ffn1_w_gate=d['ffn1_gu'][:D_FF], ffn1_w_up=d['ffn1_gu'][D_FF:], ffn1_w_down=d['ffn1_down'],
            ffn2_w_gate=d['ffn2_gu'][:D_FF], ffn2_w_up=d['ffn2_gu'][D_FF:], ffn2_w_down=d['ffn2_down'],
            w_in=jnp.concatenate([w_in[a:a + n] for a, n in PIN_PIECES], axis=0),
            mla_w_uq=jnp.concatenate([heads(uq[:HEADS * NOPE]), heads(uq[HEADS * NOPE:HEADS * NOPE + LANES]),
                                      heads(uq[HEADS * NOPE + LANES:])], axis=1).reshape(HEADS * QK_DIM, Q_RANK),
            mla_w_ukv=jnp.concatenate([heads(ukv[:HEADS * NOPE]), heads(ukv[HEADS * NOPE:])],
                                      axis=1).reshape(HEADS * (NOPE + V_DIM), KV_RANK),
            mla_w_o=d['mla_w_o'], conv_w_out=d['conv_w_out'], s5_w_glu=d['s5_w_glu'], s5_w_out=d['s5_w_out'], w_o=d['w_o']))
    shards = {n: jnp.stack([p[n].reshape(N_DEV, -1, p[n].shape[-1]) for p in per_layer], axis=1) for n in BIG}
    return [jnp.concatenate([shards[n].reshape(N_DEV, -1, shards[n].shape[-1]) for n in grp], axis=1) for grp in GROUPS]


def _assemble(gathered, shard_shapes):
    travel = _unpack_groups(gathered, shard_shapes)
    layers = []
    for l in range(DEPTH):
        full = {n: t[:, l].reshape(N_DEV * t.shape[2], t.shape[3]) for n, t in travel.items()}
        w_in = full['w_in']
        cuts = np.cumsum((0,) + IN_SPLITS)
        cq, ckv, kr, xbar, bg, cg, u, gates = [w_in[a:b] for a, b in zip(cuts[:-1], cuts[1:])]
        pad = lambda t, n: jnp.pad(t, ((0, n - t.shape[0]), (0, 0)))
        w_in_packed = jnp.concatenate(
            [cq, ckv, pad(kr[:HALF_ROPE], LANES), pad(kr[HALF_ROPE:], LANES), xbar, bg, cg, u, gates,
             jnp.zeros((D_IN_PAD - PIN_END, D_MODEL), w_in.dtype)], axis=0)
        uq = full['mla_w_uq'].reshape(HEADS, QK_DIM, Q_RANK)
        w_uq = jnp.concatenate([uq[:, :NOPE].reshape(HEADS * NOPE, Q_RANK),
                                uq[:, NOPE:NOPE + HALF_ROPE].reshape(HEADS * HALF_ROPE, Q_RANK),
                                uq[:, NOPE + HALF_ROPE:].reshape(HEADS * HALF_ROPE, Q_RANK)], axis=0)
        ukv = full['mla_w_ukv'].reshape(HEADS, NOPE + V_DIM, KV_RANK)
        w_ukv = jnp.concatenate([ukv[:, :NOPE].reshape(HEADS * NOPE, KV_RANK),
                                 ukv[:, NOPE:].reshape(HEADS * V_DIM, KV_RANK)], axis=0)
        layers.append(dict(
            ffn1_gu=jnp.concatenate([full['ffn1_w_gate'], full['ffn1_w_up']], axis=0), ffn1_down=full['ffn1_w_down'],
            w_in=w_in_packed, w_uq=w_uq, w_ukv=w_ukv, mla_w_o=full['mla_w_o'], conv_w_out=full['conv_w_out'],
            s5_w_glu=full['s5_w_glu'], s5_w_out=full['s5_w_out'], w_o=full['w_o'],
            ffn2_gu=jnp.concatenate([full['ffn2_w_gate'], full['ffn2_w_up']], axis=0), ffn2_down=full['ffn2_w_down']))
    return layers


def _s5_discretize(a_re, a_im, log_dt, b_re, b_im, c_re, c_im):
    dt = jnp.exp(log_dt)[:, None]
    mag = jnp.exp(dt * a_re)
    ab_re, ab_im = mag * jnp.cos(dt * a_im), mag * jnp.sin(dt * a_im)
    den = a_re * a_re + a_im * a_im
    nr, ni = ab_re - 1.0, ab_im
    coef_re = (nr * a_re + ni * a_im) / den
    coef_im = (ni * a_re - nr * a_im) / den
    bb_re = coef_re[..., None] * b_re - coef_im[..., None] * b_im
    bb_im = coef_re[..., None] * b_im + coef_im[..., None] * b_re
    eye = jnp.eye(S5_GROUPS, dtype=F32)
    spread_b = lambda bb: jnp.einsum('gnh,gk->ghkn', bb, eye).reshape(MIX, S5_CH)
    spread_c = lambda cc: jnp.einsum('ghn,gk->gnkh', cc, eye).reshape(S5_CH, MIX)
    b_map = jnp.concatenate([spread_b(bb_re), spread_b(bb_im)], axis=1)
    c_map = jnp.concatenate([spread_c(c_re), -spread_c(c_im)], axis=0)
    return ab_re.reshape(SCAN_ROWS, LANES), ab_im.reshape(SCAN_ROWS, LANES), b_map, c_map


def _rope_tables(length):
    inv_freq = ROPE_BASE ** (-jnp.arange(0, ROPE, 2, dtype=F32) / ROPE)
    ang = jnp.arange(length).astype(F32)[:, None] * inv_freq[None, :]
    return jnp.tile(jnp.cos(ang), (1, LANES // HALF_ROPE)), jnp.tile(jnp.sin(ang), (1, LANES // HALF_ROPE))


def _heads_first(t):
    return t.reshape(t.shape[0], HEADS, -1).transpose(1, 0, 2)


def _local_loss(diff, big, n_real):
    small, wz = diff['small'], diff['wz']
    h = diff['h0']
    length = h.shape[0]
    cos, sin = _rope_tables(length)
    ln_half = lambda tag: _make_rowwise(_f_ln_half, 2, 2, (F32,), tag)
    row2 = lambda v: v.reshape(1, -1)
    for l in range(DEPTH):
        w, z = big[l], wz[l]
        p = {k: small[k][l] for k in small if k != 'meta'}

        def ffn(h, gu, down, g, b, tag):
            act = _make_swiglu(f"{tag}_swiglu")(_make_mm(f"{tag}_gu", wt=True)(h, gu[0], gu[1]))
            f = _make_mm(f"{tag}_down")(act, down[0], down[1])
            return ln_half(f"{tag}_ln")(h, f, row2(g), row2(b))[0]

        h = ffn(h, (w['ffn1_gu'], z['ffn1_gu']), (w['ffn1_down'], z['ffn1_down']), p['ln1_g'], p['ln1_b'], "ffn1")
        cq, ckv, kr1, kr2, xbar, bg, cg, u, gate_a, gate_b, gate_c, _ = _make_split(PIN_CUTS)(
            _make_mm("w_in", wt=True)(h, w['w_in'], z['w_in']))
        qn, = _make_rowwise(_f_rms, 1, 1, (BF16,), "q_rms")(cq, row2(p['mla_q_norm_g']))
        kvn, = _make_rowwise(_f_rms, 1, 1, (BF16,), "kv_rms")(ckv, row2(p['mla_kv_norm_g']))
        q_nope, q1, q2 = _make_split((0, HEADS * NOPE, HEADS * NOPE + LANES, HEADS * NOPE + 2 * LANES))(
            _make_mm("w_uq", wt=True)(qn, w['w_uq'], z['w_uq']))
        k_nope, val = _make_split((0, HEADS * NOPE, HEADS * (NOPE + V_DIM)))(
            _make_mm("w_ukv", wt=True)(kvn, w['w_ukv'], z['w_ukv']))
        rope = _make_rowwise(_f_rope, 4, 0, (BF16, BF16), "rope", n_nodiff=2)
        q1, q2 = rope(q1, q2, cos, sin)
        k1, k2 = rope(kr1, kr2, cos, sin)
        hpad = jnp.zeros((HEADS, length, HEAD_PAD - QK_DIM), BF16)
        q3 = jnp.concatenate([_heads_first(q_nope.astype(BF16)), _heads_first(q1), _heads_first(q2), hpad], -1)
        shared = lambda t: jnp.broadcast_to(t[None, :, :HALF_ROPE], (HEADS, length, HALF_ROPE))
        k3 = jnp.concatenate([_heads_first(k_nope.astype(BF16)), shared(k1), shared(k2), hpad], -1)
        v3 = _heads_first(val.astype(BF16))
        o3 = _attention(q3, k3, v3)
        y_a = _make_mm("mla_w_o", wt=True)(o3.transpose(1, 0, 2).reshape(length, MIX), w['mla_w_o'], z['mla_w_o'])
        conv = _short_conv(xbar, bg, cg, p['conv_w_full'], row2(p['conv_b']))
        y_b = _make_mm("conv_w_out", wt=True)(conv, w['conv_w_out'], z['conv_w_out'])
        ar, ai, b_map, c_map = _s5_discretize(p['s5_a_re'], p['s5_a_im'], p['s5_log_dt'], p['s5_b_re'], p['s5_b_im'],
                                              p['s5_c_re'], p['s5_c_im'])
        bu = _make_mm_f32w("s5_b")(u, b_map)
        states = _s5_scan(ar, ai, bu.reshape(length, 2 * SCAN_ROWS, LANES)).reshape(length, 2 * S5_CH)
        y_ssm = _make_mm_f32w("s5_c")(states, c_map)
        zed, = _make_rowwise(_f_gelu_skip, 2, 1, (F32,), "s5_gelu")(y_ssm, u, row2(p['s5_d']))
        t = _make_mm("s5_w_glu")(zed, w['s5_w_glu'], z['s5_w_glu'])
        glu, = _make_rowwise(_f_glu, 2, 1, (BF16,), "s5_glu")(zed, t, row2(p['s5_b_glu']))
        y_c = _make_mm("s5_w_out", wt=True)(glu, w['s5_w_out'], z['s5_w_out'])
        mixed, = _make_rowwise(_f_merge, 6, 0, (BF16,), "merge")(gate_a, gate_b, gate_c, y_a, y_b, y_c)
        mix_out = _make_mm("w_o")(mixed, w['w_o'], z['w_o'])
        h, = _make_rowwise(_f_ln_full, 2, 2, (F32,), "mix_ln")(h, mix_out, row2(p['ln2_g']), row2(p['ln2_b']))
        h = ffn(h, (w['ffn2_gu'], z['ffn2_gu']), (w['ffn2_down'], z['ffn2_down']), p['ln3_g'], p['ln3_b'], "ffn2")
    return _make_loss(n_real)(h, diff['target'])


def kernel(x, meta, ffn1_w_gate, ffn1_w_up, ffn1_w_down, ln1_g, ln1_b, w_in, mla_q_norm_g, mla_w_uq, mla_kv_norm_g, mla_w_ukv, mla_w_o, conv_w, conv_b, conv_w_out, s5_a_re, s5_a_im, s5_log_dt, s5_b_re, s5_b_im, s5_c_re, s5_c_im, s5_d, s5_w_glu, s5_b_glu, s5_w_out, w_o, ln2_g, ln2_b, ffn2_w_gate, ffn2_w_up, ffn2_w_down, ln3_g, ln3_b, loss_target, m_meta, m_ffn1_w_gate, m_ffn1_w_up, m_ffn1_w_down, m_ln1_g, m_ln1_b, m_w_in, m_mla_q_norm_g, m_mla_w_uq, m_mla_kv_norm_g, m_mla_w_ukv, m_mla_w_o, m_conv_w, m_conv_b, m_conv_w_out, m_s5_a_re, m_s5_a_im, m_s5_log_dt, m_s5_b_re, m_s5_b_im, m_s5_c_re, m_s5_c_im, m_s5_d, m_s5_w_glu, m_s5_b_glu, m_s5_w_out, m_w_o, m_ln2_g, m_ln2_b, m_ffn2_w_gate, m_ffn2_w_up, m_ffn2_w_down, m_ln3_g, m_ln3_b, v_meta, v_ffn1_w_gate, v_ffn1_w_up, v_ffn1_w_down, v_ln1_g, v_ln1_b, v_w_in, v_mla_q_norm_g, v_mla_w_uq, v_mla_kv_norm_g, v_mla_w_ukv, v_mla_w_o, v_conv_w, v_conv_b, v_conv_w_out, v_s5_a_re, v_s5_a_im, v_s5_log_dt, v_s5_b_re, v_s5_b_im, v_s5_c_re, v_s5_c_im, v_s5_d, v_s5_w_glu, v_s5_b_glu, v_s5_w_out, v_w_o, v_ln2_g, v_ln2_b, v_ffn2_w_gate, v_ffn2_w_up, v_ffn2_w_down, v_ln3_g, v_ln3_b):
    args = locals()
    w = {n: args[n] for n in WEIGHT_NAMES}
    m = {n: args["m_" + n] for n in WEIGHT_NAMES}
    v = {n: args["v_" + n] for n in WEIGHT_NAMES}
    me = 4 * lax.axis_index("x") + 2 * lax.axis_index("y") + lax.axis_index("c")
    seq = x.shape[1]
    n_real = N_META + seq
    length = -(-n_real // LANES) * LANES

    shard_shapes = {n: w[n].shape for n in BIG}
    gathered = _all_gather(_pack_groups(w, BF16), "gather_weights")
    big = _assemble(gathered, shard_shapes)
    small_shards = _all_gather([_pack_rows([w[n] for n in SMALL_SHARDED])], "gather_small")[0].reshape(N_DEV, -1)
    meta_full = _full_weight(small_shards[:, :meta.size].reshape(N_DEV, 1, *meta.shape), 2)[0]
    conv_w_full = _full_weight(small_shards[:, meta.size:meta.size + conv_w.size].reshape(N_DEV, *conv_w.shape), 2)

    small = {n: w[n] for n in SMALL_NAMES if n not in SMALL_SHARDED}
    small['conv_w_full'] = conv_w_full
    small['meta'] = meta_full
    wz = jax.tree.map(lambda t: jnp.zeros(t.shape, F32), big)
    pad_rows = length - n_real

    def loss_fn(diff):
        h0 = jnp.concatenate([diff['small']['meta'], diff['x'], jnp.zeros((pad_rows, D_MODEL), F32)], axis=0)
        target = jnp.pad(loss_target[0], ((N_META, pad_rows), (0, 0)))
        return _local_loss(dict(h0=h0, small=diff['small'], wz=diff['wz'], target=target), big, n_real)

    loss_local, grads = jax.value_and_grad(loss_fn)(dict(x=x[0], small=small, wz=wz))
    loss = lax.psum(loss_local, AXES)

    small_names = [n for n in SMALL_NAMES if n not in SMALL_SHARDED] + ['conv_w_full', 'meta']
    small_flat = _pack_rows([grads['small'][n] for n in small_names])
    rows_each = -(-small_flat.shape[0] // (8 * N_DEV)) * 8
    small_flat = jnp.pad(small_flat, ((0, rows_each * N_DEV - small_flat.shape[0]), (0, 0)))
    *big_sums, small_sum = _reduce_scatter(
        _disassemble(grads['wz']) + [small_flat.reshape(N_DEV, rows_each, PACK_COLS)],
        [BF16] * len(GROUPS) + [F32], "grads")
    small_all, = _all_gather([small_sum], "gather_small_grads")
    small_grads = dict(zip(small_names, _unpack_rows(small_all, [grads['small'][n].shape for n in small_names])))
    g = {n: t.transpose(0, 2, 1) if n in TRANSPOSED else t for n, t in _unpack_groups(big_sums, shard_shapes).items()}
    for name in SMALL_NAMES:
        if name == 'meta':
            g[name] = lax.dynamic_slice_in_dim(small_grads['meta'], me * meta.shape[1], meta.shape[1], axis=1)
        elif name == 'conv_w':
            g[name] = lax.dynamic_slice_in_dim(small_grads['conv_w_full'], me * conv_w.shape[2], conv_w.shape[2], axis=2)
        else:
            g[name] = small_grads[name]

    delta, new_m, new_v = {}, {}, {}
    for name in BIG:
        two_d = lambda t: t.reshape(-1, t.shape[-1])
        d, nm, nv = _adamw(two_d(w[name]), two_d(g[name]), two_d(m[name]), two_d(v[name]), f"adamw_{name}")
        delta[name], new_m[name], new_v[name] = (t.reshape(w[name].shape) for t in (d, nm, nv))
    shapes = [w[n].shape for n in SMALL_NAMES]
    d, nm, nv = _adamw(*[_pack_rows([t[n] for n in SMALL_NAMES]) for t in (w, g, m, v)], "adamw_small")
    for out, buf in ((delta, d), (new_m, nm), (new_v, nv)):
        out.update(zip(SMALL_NAMES, _unpack_rows(buf, shapes)))

    return (loss, grads['x'][None], *[g[n] for n in WEIGHT_NAMES], *[delta[n] for n in WEIGHT_NAMES],
            *[new_m[n] for n in WEIGHT_NAMES], *[new_v[n] for n in WEIGHT_NAMES])
```

```python
import functools
import math

import jax
import jax.numpy as jnp
import numpy as np
from jax import lax
from jax.experimental import pallas as pl
from jax.experimental.pallas import tpu as pltpu
from jax.experimental.pallas import tpu_sc as plsc

F32 = jnp.float32
BF16 = jnp.bfloat16

D_MODEL = 1024
DEPTH = 2
N_META = 16
HEADS = 8
V_DIM = 64
NOPE = 64
ROPE = 32
HALF_ROPE = ROPE // 2
QK_DIM = NOPE + ROPE
Q_RANK = 384
KV_RANK = 256
MIX = 512
CONV_K = 3
S5_GROUPS = 32
S5_GROUP = 16
S5_STATE = 64
S5_CH = S5_GROUPS * S5_STATE
D_FF = 2816
ALPHA = (2.0 * DEPTH) ** 0.25
LN_EPS = 1e-5
RMS_EPS = 1e-6
ROPE_BASE = 10000.0
IN_SPLITS = (Q_RANK, KV_RANK, ROPE, MIX, MIX, MIX, MIX, 3 * D_MODEL)
D_IN = sum(IN_SPLITS)
ADAM_LR, ADAM_B1, ADAM_B2, ADAM_EPS, ADAM_WD, ADAM_STEP = 0.001, 0.9, 0.999, 1e-08, 0.01, 10

N_DEV = 8
AXES = ("x", "y", "c")
LANES = 128
PACK_COLS = 1024
HEAD_PAD = 128
VMEM_LIMIT = 48 * 1024 * 1024

PIN_CQ, PIN_CKV, PIN_KR1, PIN_KR2, PIN_XBAR, PIN_BG, PIN_CG, PIN_U, PIN_GATES, PIN_END = (
    0, 384, 640, 768, 896, 1408, 1920, 2432, 2944, 6016)
D_IN_PAD = 6144

WEIGHT_NAMES = ['meta', 'ffn1_w_gate', 'ffn1_w_up', 'ffn1_w_down', 'ln1_g', 'ln1_b', 'w_in', 'mla_q_norm_g', 'mla_w_uq',
                'mla_kv_norm_g', 'mla_w_ukv', 'mla_w_o', 'conv_w', 'conv_b', 'conv_w_out', 's5_a_re', 's5_a_im',
                's5_log_dt', 's5_b_re', 's5_b_im', 's5_c_re', 's5_c_im', 's5_d', 's5_w_glu', 's5_b_glu', 's5_w_out',
                'w_o', 'ln2_g', 'ln2_b', 'ffn2_w_gate', 'ffn2_w_up', 'ffn2_w_down', 'ln3_g', 'ln3_b']
BIG = {'ffn1_w_gate': 2, 'ffn1_w_up': 2, 'ffn1_w_down': 1, 'w_in': 2, 'mla_w_uq': 2, 'mla_w_ukv': 2, 'mla_w_o': 2,
       'conv_w_out': 2, 's5_w_glu': 1, 's5_w_out': 2, 'w_o': 1, 'ffn2_w_gate': 2, 'ffn2_w_up': 2, 'ffn2_w_down': 1}
SMALL_SHARDED = ('meta', 'conv_w')
SMALL_NAMES = [n for n in WEIGHT_NAMES if n not in BIG]


def _divisor_tile(n, limit, mult):
    best = None
    for t in range(mult, min(n, limit) + 1, mult):
        if n % t == 0:
            best = t
    return best if best is not None else n


def _params(*sem):
    return pltpu.CompilerParams(dimension_semantics=sem, vmem_limit_bytes=VMEM_LIMIT)


def _matmul(a, b, *, ta=False, tb=False, out_dtype=F32, name):
    m, k = (a.shape[1], a.shape[0]) if ta else a.shape
    n = b.shape[0] if tb else b.shape[1]
    assert (b.shape[1] if tb else b.shape[0]) == k, (a.shape, b.shape, ta, tb)
    tm = _divisor_tile(m, 1408, LANES) if ta else _divisor_tile(m, 1088, 16)
    tn = _divisor_tile(n, 512, LANES)
    tk = _divisor_tile(k, 1408, 16 if ta else LANES)
    nk = k // tk
    dims = (((0 if ta else 1,), (1 if tb else 0,)), ((), ()))

    def kern(a_ref, b_ref, o_ref, acc_ref):
        kk = pl.program_id(2)
        part = lax.dot_general(a_ref[...].astype(BF16), b_ref[...].astype(BF16), dims, preferred_element_type=F32)

        @pl.when(kk == 0)
        def _():
            acc_ref[...] = part

        @pl.when(kk > 0)
        def _():
            acc_ref[...] += part

        @pl.when(kk == nk - 1)
        def _():
            o_ref[...] = acc_ref[...].astype(o_ref.dtype)

    a_spec = pl.BlockSpec((tk, tm), lambda i, j, kk: (kk, i)) if ta else pl.BlockSpec((tm, tk), lambda i, j, kk: (i, kk))
    b_spec = pl.BlockSpec((tn, tk), lambda i, j, kk: (j, kk)) if tb else pl.BlockSpec((tk, tn), lambda i, j, kk: (kk, j))
    return pl.pallas_call(
        kern, name=name, grid=(m // tm, n // tn, nk),
        in_specs=[a_spec, b_spec], out_specs=pl.BlockSpec((tm, tn), lambda i, j, kk: (i, j)),
        out_shape=jax.ShapeDtypeStruct((m, n), out_dtype),
        scratch_shapes=[pltpu.VMEM((tm, tn), F32)],
        compiler_params=_params("parallel", "parallel", "arbitrary"),
    )(a, b)


def _make_mm(name, wt=False):
    @jax.custom_vjp
    def mm(x, w, wz):
        return _matmul(x, w, tb=wt, name=name + "_fwd")

    def fwd(x, w, wz):
        return _matmul(x, w, tb=wt, name=name + "_fwd"), (x, w)

    def bwd(res, dy):
        x, w = res
        dx = _matmul(dy, w, tb=not wt, out_dtype=x.dtype, name=name + "_dx")
        dw = _matmul(dy, x, ta=True, name=name + "_dw") if wt else _matmul(x, dy, ta=True, name=name + "_dw")
        return dx, jnp.zeros_like(w), dw

    mm.defvjp(fwd, bwd)
    return mm


def _make_mm_f32w(name):
    @jax.custom_vjp
    def mm(x, w):
        return _matmul(x, w, name=name + "_fwd")

    def fwd(x, w):
        return _matmul(x, w, name=name + "_fwd"), (x, w)

    def bwd(res, dy):
        x, w = res
        return (_matmul(dy, w, tb=True, out_dtype=x.dtype, name=name + "_dx"),
                _matmul(x, dy, ta=True, name=name + "_dw"))

    mm.defvjp(fwd, bwd)
    return mm


def _row_tile(rows, widths):
    limit = max(16, (6 * 1024 * 1024 // 4) // max(1, sum(widths)))
    return _divisor_tile(rows, limit, 16)


def _make_rowwise(f, n_rows, n_pars, out_dtypes, name, n_nodiff=0):
    n_out = len(out_dtypes)
    n_diff = n_rows - n_nodiff

    def run_fwd(rows, pars):
        length = rows[0].shape[0]
        shapes = jax.eval_shape(lambda *a: f(*a), *[jax.ShapeDtypeStruct((16, r.shape[1]), F32) for r in rows],
                                *[jax.ShapeDtypeStruct(p.shape, F32) for p in pars])
        widths = [s.shape[1] for s in shapes]
        tm = _row_tile(length, [r.shape[1] for r in rows] + widths)

        def kern(*refs):
            ins = [r[...].astype(F32) for r in refs[:n_rows + n_pars]]
            outs = f(*ins)
            for o_ref, o in zip(refs[n_rows + n_pars:], outs):
                o_ref[...] = o.astype(o_ref.dtype)

        return pl.pallas_call(
            kern, name=name + "_fwd", grid=(length // tm,),
            in_specs=[pl.BlockSpec((tm, r.shape[1]), lambda i: (i, 0)) for r in rows]
            + [pl.BlockSpec(p.shape, lambda i: (0, 0)) for p in pars],
            out_specs=[pl.BlockSpec((tm, w), lambda i: (i, 0)) for w in widths],
            out_shape=[jax.ShapeDtypeStruct((length, w), dt) for w, dt in zip(widths, out_dtypes)],
            compiler_params=_params("parallel"),
        )(*rows, *pars)

    def run_bwd(rows, pars, cts):
        length = rows[0].shape[0]
        tm = _row_tile(length, [r.shape[1] for r in rows] * 2 + [c.shape[1] for c in cts] * 2)

        def kern(*refs):
            ins = [r[...].astype(F32) for r in refs[:n_rows + n_pars]]
            ct = [r[...].astype(F32) for r in refs[n_rows + n_pars:n_rows + n_pars + n_out]]
            out_refs = refs[n_rows + n_pars + n_out:]
            nodiff = ins[n_diff:n_rows]
            _, vjp = jax.vjp(lambda *a: f(*a[:n_diff], *nodiff, *a[n_diff:]), *ins[:n_diff], *ins[n_rows:])
            grads = vjp(tuple(ct))
            for o_ref, g in zip(out_refs[:n_diff], grads[:n_diff]):
                o_ref[...] = g.astype(o_ref.dtype)
            first = pl.program_id(0) == 0
            for o_ref, g in zip(out_refs[n_diff:], grads[n_diff:]):
                @pl.when(first)
                def _(o_ref=o_ref, g=g):
                    o_ref[...] = g

                @pl.when(jnp.logical_not(first))
                def _(o_ref=o_ref, g=g):
                    o_ref[...] += g

        return pl.pallas_call(
            kern, name=name + "_bwd", grid=(length // tm,),
            in_specs=[pl.BlockSpec((tm, r.shape[1]), lambda i: (i, 0)) for r in rows]
            + [pl.BlockSpec(p.shape, lambda i: (0, 0)) for p in pars]
            + [pl.BlockSpec((tm, c.shape[1]), lambda i: (i, 0)) for c in cts],
            out_specs=[pl.BlockSpec((tm, r.shape[1]), lambda i: (i, 0)) for r in rows[:n_diff]]
            + [pl.BlockSpec(p.shape, lambda i: (0, 0)) for p in pars],
            out_shape=[jax.ShapeDtypeStruct(r.shape, r.dtype) for r in rows[:n_diff]]
            + [jax.ShapeDtypeStruct(p.shape, F32) for p in pars],
            compiler_params=_params("arbitrary"),
        )(*rows, *pars, *cts)

    @jax.custom_vjp
    def op(*args):
        return tuple(run_fwd(args[:n_rows], args[n_rows:]))

    def fwd(*args):
        return tuple(run_fwd(args[:n_rows], args[n_rows:])), args

    def bwd(args, cts):
        grads = run_bwd(args[:n_rows], args[n_rows:], cts)
        zeros = [jnp.zeros_like(r) for r in args[n_diff:n_rows]]
        return (*grads[:n_diff], *zeros, *grads[n_diff:])

    op.defvjp(fwd, bwd)
    return op


def _layer_norm(z, g, b):
    mu = jnp.mean(z, axis=-1, keepdims=True)
    d = z - mu
    var = jnp.mean(d * d, axis=-1, keepdims=True)
    return d * lax.rsqrt(var + LN_EPS) * g + b


def _f_ln_half(h, f, g, b):
    return (_layer_norm(ALPHA * h + 0.5 * f, g, b),)


def _f_ln_full(h, f, g, b):
    return (_layer_norm(ALPHA * h + f, g, b),)


def _f_rms(x, g):
    return (x * lax.rsqrt(jnp.mean(x * x, axis=-1, keepdims=True) + RMS_EPS) * g,)


def _f_rope(x1, x2, cos, sin):
    return x1 * cos - x2 * sin, x2 * cos + x1 * sin


def _f_gelu_skip(y, u, d):
    return (jax.nn.gelu(y + d * u),)


def _f_glu(z, t, b):
    return (z * jax.nn.sigmoid(t + b),)


def _f_merge(ga, gb, gc, ya, yb, yc):
    return (jax.nn.sigmoid(ga) * ya + jax.nn.sigmoid(gb) * yb + jax.nn.sigmoid(gc) * yc,)


def _make_swiglu(name):
    def run_fwd(gu):
        length = gu.shape[0]
        tm = _row_tile(length, [3 * D_FF])

        def kern(g_ref, u_ref, o_ref):
            o_ref[...] = (jax.nn.silu(g_ref[...]) * u_ref[...]).astype(o_ref.dtype)

        half = lambda j: pl.BlockSpec((tm, D_FF), lambda i: (i, j))
        return pl.pallas_call(
            kern, name=name + "_fwd", grid=(length // tm,), in_specs=[half(0), half(1)], out_specs=half(0),
            out_shape=jax.ShapeDtypeStruct((length, D_FF), BF16), compiler_params=_params("parallel"),
        )(gu, gu)

    def run_bwd(gu, dact):
        length = gu.shape[0]
        tm = _row_tile(length, [6 * D_FF])

        def kern(g_ref, u_ref, d_ref, o_ref):
            _, vjp = jax.vjp(lambda g, u: jax.nn.silu(g) * u, g_ref[...], u_ref[...])
            dg, du = vjp(d_ref[...].astype(F32))
            o_ref[:, :D_FF] = dg
            o_ref[:, D_FF:] = du

        half = lambda j: pl.BlockSpec((tm, D_FF), lambda i: (i, j))
        return pl.pallas_call(
            kern, name=name + "_bwd", grid=(length // tm,), in_specs=[half(0), half(1), half(0)],
            out_specs=pl.BlockSpec((tm, 2 * D_FF), lambda i: (i, 0)),
            out_shape=jax.ShapeDtypeStruct(gu.shape, F32), compiler_params=_params("parallel"),
        )(gu, gu, dact)

    @jax.custom_vjp
    def op(gu):
        return run_fwd(gu)

    def fwd(gu):
        return run_fwd(gu), gu

    def bwd(gu, dact):
        return (run_bwd(gu, dact),)

    op.defvjp(fwd, bwd)
    return op


def _attn_scores(q, k, q_block, tq):
    length = k.shape[0]
    s = lax.dot_general(q, k, (((1,), (1,)), ((), ())), preferred_element_type=F32) * (QK_DIM ** -0.5)
    row = q_block * tq + lax.broadcasted_iota(jnp.int32, (tq, length), 0)
    col = lax.broadcasted_iota(jnp.int32, (tq, length), 1)
    s = jnp.where(col <= row, s, -1e30)
    e = jnp.exp(s - jnp.max(s, axis=1, keepdims=True))
    return e / jnp.sum(e, axis=1, keepdims=True)


ATTN_TQ = 128


def _attn_fwd(q3, k3, v3):
    heads, length, _ = q3.shape
    tq = ATTN_TQ

    def kern(q_ref, k_ref, v_ref, o_ref):
        p = _attn_scores(q_ref[0], k_ref[0], pl.program_id(1), tq)
        o_ref[0] = jnp.dot(p.astype(BF16), v_ref[0], preferred_element_type=F32).astype(o_ref.dtype)

    return pl.pallas_call(
        kern, name="attn_fwd", grid=(heads, length // tq),
        in_specs=[pl.BlockSpec((1, tq, HEAD_PAD), lambda h, i: (h, i, 0)),
                  pl.BlockSpec((1, length, HEAD_PAD), lambda h, i: (h, 0, 0)),
                  pl.BlockSpec((1, length, V_DIM), lambda h, i: (h, 0, 0))],
        out_specs=pl.BlockSpec((1, tq, V_DIM), lambda h, i: (h, i, 0)),
        out_shape=jax.ShapeDtypeStruct((heads, length, V_DIM), F32),
        compiler_params=_params("parallel", "parallel"),
    )(q3, k3, v3)


def _attn_bwd(q3, k3, v3, do3):
    heads, length, _ = q3.shape
    tq = ATTN_TQ

    def kern(q_ref, k_ref, v_ref, do_ref, dq_ref, dk_ref, dv_ref):
        i = pl.program_id(1)
        q, k, v, do = q_ref[0], k_ref[0], v_ref[0], do_ref[0].astype(BF16)
        p = _attn_scores(q, k, i, tq)
        dp = lax.dot_general(do, v, (((1,), (1,)), ((), ())), preferred_element_type=F32)
        ds = (p * (dp - jnp.sum(p * dp, axis=1, keepdims=True)) * (QK_DIM ** -0.5)).astype(BF16)
        dq_ref[0] = jnp.dot(ds, k, preferred_element_type=F32)
        dk = lax.dot_general(ds, q, (((0,), (0,)), ((), ())), preferred_element_type=F32)
        dv = lax.dot_general(p.astype(BF16), do, (((0,), (0,)), ((), ())), preferred_element_type=F32)

        @pl.when(i == 0)
        def _():
            dk_ref[0] = dk
            dv_ref[0] = dv

        @pl.when(i > 0)
        def _():
            dk_ref[0] += dk
            dv_ref[0] += dv

    return pl.pallas_call(
        kern, name="attn_bwd", grid=(heads, length // tq),
        in_specs=[pl.BlockSpec((1, tq, HEAD_PAD), lambda h, i: (h, i, 0)),
                  pl.BlockSpec((1, length, HEAD_PAD), lambda h, i: (h, 0, 0)),
                  pl.BlockSpec((1, length, V_DIM), lambda h, i: (h, 0, 0)),
                  pl.BlockSpec((1, tq, V_DIM), lambda h, i: (h, i, 0))],
        out_specs=[pl.BlockSpec((1, tq, HEAD_PAD), lambda h, i: (h, i, 0)),
                   pl.BlockSpec((1, length, HEAD_PAD), lambda h, i: (h, 0, 0)),
                   pl.BlockSpec((1, length, V_DIM), lambda h, i: (h, 0, 0))],
        out_shape=[jax.ShapeDtypeStruct((heads, length, HEAD_PAD), F32),
                   jax.ShapeDtypeStruct((heads, length, HEAD_PAD), F32),
                   jax.ShapeDtypeStruct((heads, length, V_DIM), F32)],
        compiler_params=_params("parallel", "arbitrary"),
    )(q3, k3, v3, do3)


@jax.custom_vjp
def _attention(q3, k3, v3):
    return _attn_fwd(q3, k3, v3)


def _attention_fwd(q3, k3, v3):
    return _attn_fwd(q3, k3, v3), (q3, k3, v3)


def _attention_bwd(res, do3):
    q3, k3, v3 = res
    dq, dk, dv = _attn_bwd(q3, k3, v3, do3)
    return dq.astype(q3.dtype), dk.astype(k3.dtype), dv.astype(v3.dtype)


_attention.defvjp(_attention_fwd, _attention_bwd)


def _conv_terms(x, c, w_ref, cb):
    u = c * x
    row = lax.broadcasted_iota(jnp.int32, u.shape, 0)
    u1 = jnp.where(row >= 1, pltpu.roll(u, 1, 0), 0.0)
    u2 = jnp.where(row >= 2, pltpu.roll(u, 2, 0), 0.0)
    y = cb + w_ref[0:1, :] * u2 + w_ref[1:2, :] * u1 + w_ref[2:3, :] * u
    return u, u1, u2, y


def _conv_specs(length):
    col = pl.BlockSpec((length, LANES), lambda j: (0, j))
    return col, pl.BlockSpec((CONV_K, LANES), lambda j: (0, j)), pl.BlockSpec((1, LANES), lambda j: (0, j))


def _conv_fwd(x, b, c, w, cb):
    length = x.shape[0]

    def kern(x_ref, b_ref, c_ref, w_ref, cb_ref, o_ref):
        _, _, _, y = _conv_terms(x_ref[...], c_ref[...], w_ref, cb_ref[...])
        o_ref[...] = b_ref[...] * y

    col, wspec, bspec = _conv_specs(length)
    return pl.pallas_call(
        kern, name="conv_fwd", grid=(MIX // LANES,), in_specs=[col, col, col, wspec, bspec], out_specs=col,
        out_shape=jax.ShapeDtypeStruct((length, MIX), F32), compiler_params=_params("parallel"),
    )(x, b, c, w, cb)


def _conv_bwd(x, b, c, w, cb, do):
    length = x.shape[0]

    def kern(x_ref, b_ref, c_ref, w_ref, cb_ref, do_ref, dx_ref, db_ref, dc_ref, dw_ref, dcb_ref):
        x, c, do = x_ref[...], c_ref[...], do_ref[...]
        u, u1, u2, y = _conv_terms(x, c, w_ref, cb_ref[...])
        db_ref[...] = do * y
        dy = do * b_ref[...]
        row = lax.broadcasted_iota(jnp.int32, dy.shape, 0)
        dy1 = jnp.where(row < length - 1, pltpu.roll(dy, length - 1, 0), 0.0)
        dy2 = jnp.where(row < length - 2, pltpu.roll(dy, length - 2, 0), 0.0)
        du = w_ref[2:3, :] * dy + w_ref[1:2, :] * dy1 + w_ref[0:1, :] * dy2
        dx_ref[...] = du * c
        dc_ref[...] = du * x
        dw_ref[0:1, :] = jnp.sum(dy * u2, axis=0, keepdims=True)
        dw_ref[1:2, :] = jnp.sum(dy * u1, axis=0, keepdims=True)
        dw_ref[2:3, :] = jnp.sum(dy * u, axis=0, keepdims=True)
        dcb_ref[...] = jnp.sum(dy, axis=0, keepdims=True)

    col, wspec, bspec = _conv_specs(length)
    big = jax.ShapeDtypeStruct((length, MIX), F32)
    return pl.pallas_call(
        kern, name="conv_bwd", grid=(MIX // LANES,), in_specs=[col, col, col, wspec, bspec, col],
        out_specs=[col, col, col, wspec, bspec],
        out_shape=[big, big, big, jax.ShapeDtypeStruct((CONV_K, MIX), F32), jax.ShapeDtypeStruct((1, MIX), F32)],
        compiler_params=_params("parallel"),
    )(x, b, c, w, cb, do)


@jax.custom_vjp
def _short_conv(x, b, c, w, cb):
    return _conv_fwd(x, b, c, w, cb)


def _short_conv_fwd(x, b, c, w, cb):
    return _conv_fwd(x, b, c, w, cb), (x, b, c, w, cb)


def _short_conv_bwd(res, do):
    return tuple(_conv_bwd(*res, do))


_short_conv.defvjp(_short_conv_fwd, _short_conv_bwd)


SCAN_ROWS = S5_CH // LANES
SCAN_TC = 136


def _scan_fwd(ar, ai, b):
    length = b.shape[0]
    tc = _divisor_tile(length, SCAN_TC, 8)

    def kern(ar_ref, ai_ref, b_ref, x_ref, sr, si):
        @pl.when(pl.program_id(0) == 0)
        def _():
            sr[...] = jnp.zeros_like(sr)
            si[...] = jnp.zeros_like(si)

        a_re, a_im = ar_ref[...], ai_ref[...]

        def body(t, carry):
            xr, xi = carry
            nr = a_re * xr - a_im * xi + b_ref[t, 0:SCAN_ROWS, :]
            ni = a_re * xi + a_im * xr + b_ref[t, SCAN_ROWS:2 * SCAN_ROWS, :]
            x_ref[t, 0:SCAN_ROWS, :] = nr
            x_ref[t, SCAN_ROWS:2 * SCAN_ROWS, :] = ni
            return nr, ni

        xr, xi = lax.fori_loop(0, tc, body, (sr[...], si[...]), unroll=4)
        sr[...] = xr
        si[...] = xi

    par = pl.BlockSpec((SCAN_ROWS, LANES), lambda i: (0, 0))
    blk = pl.BlockSpec((tc, 2 * SCAN_ROWS, LANES), lambda i: (i, 0, 0))
    return pl.pallas_call(
        kern, name="s5_scan_fwd", grid=(length // tc,), in_specs=[par, par, blk], out_specs=blk,
        out_shape=jax.ShapeDtypeStruct(b.shape, F32), scratch_shapes=[pltpu.VMEM((SCAN_ROWS, LANES), F32)] * 2,
        compiler_params=_params("arbitrary"),
    )(ar, ai, b)


def _scan_bwd(ar, ai, x, dx):
    length = x.shape[0]
    tc = _divisor_tile(length, SCAN_TC, 8)
    n_blk = length // tc
    re, im = slice(0, SCAN_ROWS), slice(SCAN_ROWS, 2 * SCAN_ROWS)

    def kern(ar_ref, ai_ref, x_ref, dx_ref, db_ref, dar_ref, dai_ref, lr_s, li_s):
        @pl.when(pl.program_id(0) == 0)
        def _():
            lr_s[...] = jnp.zeros_like(lr_s)
            li_s[...] = jnp.zeros_like(li_s)
            dar_ref[...] = jnp.zeros_like(dar_ref)
            dai_ref[...] = jnp.zeros_like(dai_ref)

        a_re, a_im = ar_ref[...], ai_ref[...]

        def body(j, carry):
            t = tc - 1 - j
            lr, li, gr, gi = carry
            x_re, x_im = x_ref[t, re, :], x_ref[t, im, :]
            gr = gr + (lr * x_re + li * x_im)
            gi = gi + (li * x_re - lr * x_im)
            nlr = dx_ref[t, re, :] + (a_re * lr + a_im * li)
            nli = dx_ref[t, im, :] + (a_re * li - a_im * lr)
            db_ref[t, re, :] = nlr
            db_ref[t, im, :] = nli
            return nlr, nli, gr, gi

        lr, li, gr, gi = lax.fori_loop(0, tc, body, (lr_s[...], li_s[...], dar_ref[...], dai_ref[...]), unroll=4)
        lr_s[...] = lr
        li_s[...] = li
        dar_ref[...] = gr
        dai_ref[...] = gi

    par = pl.BlockSpec((SCAN_ROWS, LANES), lambda i: (0, 0))
    blk = pl.BlockSpec((tc, 2 * SCAN_ROWS, LANES), lambda i: (n_blk - 1 - i, 0, 0))
    pout = jax.ShapeDtypeStruct((SCAN_ROWS, LANES), F32)
    return pl.pallas_call(
        kern, name="s5_scan_bwd", grid=(n_blk,), in_specs=[par, par, blk, blk],
        out_specs=[blk, par, par], out_shape=[jax.ShapeDtypeStruct(x.shape, F32), pout, pout],
        scratch_shapes=[pltpu.VMEM((SCAN_ROWS, LANES), F32)] * 2, compiler_params=_params("arbitrary"),
    )(ar, ai, x, dx)


@jax.custom_vjp
def _s5_scan(ar, ai, b):
    return _scan_fwd(ar, ai, b)


def _s5_scan_fwd(ar, ai, b):
    x = _scan_fwd(ar, ai, b)
    return x, (ar, ai, x)


def _s5_scan_bwd(res, dx):
    ar, ai, x = res
    db, dar, dai = _scan_bwd(ar, ai, x, dx)
    return dar, dai, db


_s5_scan.defvjp(_s5_scan_fwd, _s5_scan_bwd)


def _loss_call(y, target, n_real):
    length = y.shape[0]
    tm = _divisor_tile(length, 544, 16)

    def kern(y_ref, t_ref, loss_ref, dy_ref):
        i = pl.program_id(0)
        row = i * tm + lax.broadcasted_iota(jnp.int32, (tm, 1), 0)
        keep = jnp.logical_and(row >= N_META, row < n_real)
        err = jnp.where(keep, y_ref[...] - t_ref[...], 0.0)
        dy_ref[...] = err * (1.0 / D_MODEL)
        part = 0.5 * jnp.sum(jnp.mean(err * err, axis=-1, keepdims=True), axis=0, keepdims=True)

        @pl.when(i == 0)
        def _():
            loss_ref[...] = jnp.zeros_like(loss_ref)

        loss_ref[...] += part

    blk = pl.BlockSpec((tm, D_MODEL), lambda i: (i, 0))
    return pl.pallas_call(
        kern, name="loss_head", grid=(length // tm,), in_specs=[blk, blk],
        out_specs=[pl.BlockSpec((8, LANES), lambda i: (0, 0)), blk],
        out_shape=[jax.ShapeDtypeStruct((8, LANES), F32), jax.ShapeDtypeStruct(y.shape, F32)],
        compiler_params=_params("arbitrary"),
    )(y, target)


def _make_loss(n_real):
    @jax.custom_vjp
    def loss(y, target):
        return _loss_call(y, target, n_real)[0][0, 0]

    def fwd(y, target):
        total, dy = _loss_call(y, target, n_real)
        return total[0, 0], dy

    def bwd(dy, ct):
        return dy * ct, jnp.zeros_like(dy)

    loss.defvjp(fwd, bwd)
    return loss


HBM_SPEC = pl.BlockSpec(memory_space=pl.ANY)
MESH_ID = pl.DeviceIdType.MESH


SC_MESH = dict(axis_name="sequencer", num_cores=1)
GATHER_ID, SIBLING_ID, CHIP_ID = 1, 2, 3


def _handshake(peers):
    barrier = pltpu.get_barrier_semaphore()
    for peer in peers:
        pl.semaphore_signal(barrier, inc=1, device_id=peer, device_id_type=MESH_ID)
    pl.semaphore_wait(barrier, len(peers))


def _exchange_call(body, name, ins, out_types, n_sems, sequencer_id):
    n_in, n_out = len(ins), len(out_types)
    sems = [pltpu.SemaphoreType.DMA((n_sems,)), pltpu.SemaphoreType.DMA((n_sems,)), pltpu.SemaphoreType.DMA((n_in,))]
    if sequencer_id is None:
        def on_core(*refs):
            body(lambda peers: None, refs[:n_in], refs[n_in:n_in + n_out], *refs[n_in + n_out:])

        return pl.pallas_call(on_core, name=name, out_shape=out_types, in_specs=[HBM_SPEC] * n_in,
                              out_specs=[HBM_SPEC] * n_out, scratch_shapes=sems)(*ins)

    def on_sequencer(*refs):
        body(_handshake, refs[:n_in], refs[n_in:n_in + n_out], *refs[n_in + n_out:])

    return pl.kernel(on_sequencer, name=name, out_type=out_types, mesh=plsc.ScalarSubcoreMesh(**SC_MESH),
                     scratch_types=sems, compiler_params=pltpu.CompilerParams(collective_id=sequencer_id))(*ins)


def _all_gather(shards, name, sequencer=False):
    n = len(shards)

    def body(handshake, x_refs, out_refs, send_sems, recv_sems, local_sems):
        x, y, c = lax.axis_index("x"), lax.axis_index("y"), lax.axis_index("c")
        me, sibling = (x, y, c), (x, y, 1 - c)
        chips = [(1 - x, y), (x, 1 - y), (1 - x, 1 - y)]
        handshake([sibling] + [(*chip, c) for chip in chips])

        def copy(b, k, block, to, from_input=False):
            px, py, pc = block
            slot = out_refs[b].at[4 * px + 2 * py + pc]
            return pltpu.make_async_remote_copy(
                src_ref=x_refs[b] if from_input else slot, dst_ref=slot,
                send_sem=send_sems.at[7 * b + k], recv_sem=recv_sems.at[7 * b + k], device_id=to, device_id_type=MESH_ID)

        mine = [pltpu.make_async_copy(x_refs[b], out_refs[b].at[4 * x + 2 * y + c], local_sems.at[b]) for b in range(n)]
        for cp in mine:
            cp.start()
        first = []
        for b in range(n):
            first.append(copy(b, 0, me, sibling, from_input=True))
            first += [copy(b, 1 + j, me, (*chip, c), from_input=True) for j, chip in enumerate(chips)]
        for cp in first:
            cp.start()
        passed = []
        for j, chip in enumerate(chips):
            for b in range(n):
                copy(b, 1 + j, (*chip, c), me).wait_recv()
                passed.append(copy(b, 4 + j, (*chip, c), sibling))
                passed[-1].start()
        for b in range(n):
            copy(b, 0, sibling, me).wait_recv()
            for j, chip in enumerate(chips):
                copy(b, 4 + j, (*chip, 1 - c), me).wait_recv()
        for cp in first + passed:
            cp.wait_send()
        for cp in mine:
            cp.wait()

    out_types = [jax.ShapeDtypeStruct((N_DEV, *s.shape), s.dtype) for s in shards]
    return _exchange_call(body, name, shards, out_types, 7 * n, GATHER_ID if sequencer else None)


def _sibling_exchange(bufs, name, sequencer=False):
    def body(handshake, ins, outs, send_sems, recv_sems, local_sems):
        x, y, c = lax.axis_index("x"), lax.axis_index("y"), lax.axis_index("c")
        handshake([(x, y, 1 - c)])
        copies = []
        for b, (src, dst) in enumerate(zip(ins, outs)):
            for chip in range(4):
                copies.append(pltpu.make_async_remote_copy(
                    src_ref=src.at[2 * chip + (1 - c)], dst_ref=dst.at[chip],
                    send_sem=send_sems.at[4 * b + chip], recv_sem=recv_sems.at[4 * b + chip],
                    device_id=(x, y, 1 - c), device_id_type=MESH_ID))
        for cp in copies:
            cp.start()
        for cp in copies:
            cp.wait()

    out_types = [jax.ShapeDtypeStruct((4, *b.shape[1:]), b.dtype) for b in bufs]
    return _exchange_call(body, name, bufs, out_types, 4 * len(bufs), SIBLING_ID if sequencer else None)


def _chip_exchange(bufs, name, sequencer=False):
    def body(handshake, ins, outs, send_sems, recv_sems, local_sems):
        x, y, c = lax.axis_index("x"), lax.axis_index("y"), lax.axis_index("c")
        chips = [(1 - x, y), (x, 1 - y), (1 - x, 1 - y)]
        handshake([(*chip, c) for chip in chips])
        copies = []
        for b, (src, dst) in enumerate(zip(ins, outs)):
            for j, (px, py) in enumerate(chips):
                copies.append(pltpu.make_async_remote_copy(
                    src_ref=src.at[2 * px + py], dst_ref=dst.at[j],
                    send_sem=send_sems.at[3 * b + j], recv_sem=recv_sems.at[3 * b + j],
                    device_id=(px, py, c), device_id_type=MESH_ID))
        for cp in copies:
            cp.start()
        for cp in copies:
            cp.wait()

    out_types = [jax.ShapeDtypeStruct((3, *b.shape[1:]), b.dtype) for b in bufs]
    return _exchange_call(body, name, bufs, out_types, 3 * len(bufs), CHIP_ID if sequencer else None)


def _pair_add(full, got, core, out_dtype, name):
    _, rows, cols = full.shape
    tm = _divisor_tile(rows, 640, 16)

    def kern(core_ref, a_ref, b_ref, o_ref):
        o_ref[...] = (a_ref[...] + b_ref[...]).astype(o_ref.dtype)

    return pl.pallas_call(
        kern, name=name,
        grid_spec=pltpu.PrefetchScalarGridSpec(
            num_scalar_prefetch=1, grid=(4, rows // tm),
            in_specs=[pl.BlockSpec((1, tm, cols), lambda k, i, core_ref: (2 * k + core_ref[0], i, 0)),
                      pl.BlockSpec((1, tm, cols), lambda k, i, core_ref: (k, i, 0))],
            out_specs=pl.BlockSpec((1, tm, cols), lambda k, i, core_ref: (k, i, 0))),
        out_shape=jax.ShapeDtypeStruct((4, rows, cols), out_dtype), compiler_params=_params("parallel", "parallel"),
    )(core, full, got)


def _chip_add(part, got, chip, name):
    _, rows, cols = part.shape
    tm = _divisor_tile(rows, 640, 16)

    def kern(chip_ref, a_ref, b_ref, o_ref):
        o_ref[...] = ((a_ref[0].astype(F32) + b_ref[0].astype(F32)) + b_ref[1].astype(F32)) + b_ref[2].astype(F32)

    return pl.pallas_call(
        kern, name=name,
        grid_spec=pltpu.PrefetchScalarGridSpec(
            num_scalar_prefetch=1, grid=(rows // tm,),
            in_specs=[pl.BlockSpec((1, tm, cols), lambda i, chip_ref: (chip_ref[0], i, 0)),
                      pl.BlockSpec((3, tm, cols), lambda i, chip_ref: (0, i, 0))],
            out_specs=pl.BlockSpec((tm, cols), lambda i, chip_ref: (i, 0))),
        out_shape=jax.ShapeDtypeStruct((rows, cols), F32), compiler_params=_params("parallel"),
    )(chip, part, got)


def _reduce_scatter(bufs, wire_dtypes, tag, sequencer):
    core = lax.axis_index("c").astype(jnp.int32).reshape(1)
    chip = (2 * lax.axis_index("x") + lax.axis_index("y")).astype(jnp.int32).reshape(1)
    got = _sibling_exchange(bufs, f"rs_sibling_exchange_{tag}", sequencer)
    parts = [_pair_add(b, g, core, dt, f"rs_pair_add_{tag}{i}")
             for i, (b, g, dt) in enumerate(zip(bufs, got, wire_dtypes))]
    arrived = _chip_exchange(parts, f"rs_chip_exchange_{tag}", sequencer)
    return [_chip_add(p, a, chip, f"rs_chip_add_{tag}{i}") for i, (p, a) in enumerate(zip(parts, arrived))]


def _adamw(w, g, m, v, name):
    rows, cols = w.shape
    tm = _divisor_tile(rows, max(8, (512 * 1024) // cols // 8 * 8), 8)

    def kern(w_ref, g_ref, m_ref, v_ref, d_ref, nm_ref, nv_ref):
        g = g_ref[...]
        m = ADAM_B1 * m_ref[...] + (1.0 - ADAM_B1) * g
        v = ADAM_B2 * v_ref[...] + (1.0 - ADAM_B2) * (g * g)
        m_hat = m / (1.0 - ADAM_B1 ** ADAM_STEP)
        v_hat = v / (1.0 - ADAM_B2 ** ADAM_STEP)
        d_ref[...] = -ADAM_LR * (m_hat / (jnp.sqrt(v_hat) + ADAM_EPS) + ADAM_WD * w_ref[...])
        nm_ref[...] = m
        nv_ref[...] = v

    blk = pl.BlockSpec((tm, cols), lambda i: (i, 0))
    out = jax.ShapeDtypeStruct(w.shape, F32)
    return pl.pallas_call(
        kern, name=name, grid=(rows // tm,), in_specs=[blk] * 4, out_specs=[blk] * 3, out_shape=[out] * 3,
        compiler_params=_params("parallel"),
    )(w, g, m, v)


def _adamw_layers(w, g_layers, m, v, name):
    depth, rows, cols = w.shape
    tm = _divisor_tile(rows, max(8, (512 * 1024) // cols // 8 * 8), 8)

    def kern(w_ref, m_ref, v_ref, *refs):
        g_refs, (g_out, d_ref, nm_ref, nv_ref) = refs[:depth], refs[depth:]
        layer = pl.program_id(0)
        g = g_refs[0][...]
        for l in range(1, depth):
            g = jnp.where(layer == l, g_refs[l][...], g)
        m = ADAM_B1 * m_ref[0] + (1.0 - ADAM_B1) * g
        v = ADAM_B2 * v_ref[0] + (1.0 - ADAM_B2) * (g * g)
        m_hat = m / (1.0 - ADAM_B1 ** ADAM_STEP)
        v_hat = v / (1.0 - ADAM_B2 ** ADAM_STEP)
        g_out[0] = g
        d_ref[0] = -ADAM_LR * (m_hat / (jnp.sqrt(v_hat) + ADAM_EPS) + ADAM_WD * w_ref[0])
        nm_ref[0] = m
        nv_ref[0] = v

    blk = pl.BlockSpec((1, tm, cols), lambda l, i: (l, i, 0))
    out = jax.ShapeDtypeStruct(w.shape, F32)
    return pl.pallas_call(
        kern, name=name, grid=(depth, rows // tm),
        in_specs=[blk] * 3 + [pl.BlockSpec((tm, cols), lambda l, i: (i, 0))] * depth,
        out_specs=[blk] * 4, out_shape=[out] * 4, compiler_params=_params("parallel", "parallel"),
    )(w, m, v, *g_layers)


TRANSPOSED = ('ffn1_w_gate', 'ffn1_w_up', 'ffn2_w_gate', 'ffn2_w_up', 'w_in', 'mla_w_uq', 'mla_w_ukv', 'mla_w_o',
              'conv_w_out', 's5_w_out')
GROUPS = (('ffn1_w_gate', 'ffn1_w_up', 'ffn1_w_down', 'ffn2_w_gate', 'ffn2_w_up', 'ffn2_w_down', 'w_o'), ('w_in',),
          ('mla_w_o', 'conv_w_out', 's5_w_out', 's5_w_glu'), ('mla_w_uq',), ('mla_w_ukv',))
PIN_CUTS = (PIN_CQ, PIN_CKV, PIN_KR1, PIN_KR2, PIN_XBAR, PIN_BG, PIN_CG, PIN_U, PIN_GATES, PIN_GATES + D_MODEL,
            PIN_GATES + 2 * D_MODEL, PIN_END, D_IN_PAD)
PIN_PIECES = ((PIN_CQ, Q_RANK), (PIN_CKV, KV_RANK), (PIN_KR1, HALF_ROPE), (PIN_KR2, HALF_ROPE), (PIN_XBAR, 4 * MIX),
              (PIN_GATES, 3 * D_MODEL))


def _make_split(cuts):
    @jax.custom_vjp
    def split(t):
        return tuple(t[:, a:b] for a, b in zip(cuts[:-1], cuts[1:]))

    def fwd(t):
        return split(t), None

    def bwd(_, cts):
        return (jnp.concatenate(cts, axis=1),)

    split.defvjp(fwd, bwd)
    return split


def _travel_shape(name, shape):
    return (shape[2], shape[1]) if name in TRANSPOSED else (shape[1], shape[2])


def _pack_groups(tensors, layer, dtype):
    def view(n):
        t = tensors[n][layer]
        return (t.T if n in TRANSPOSED else t).astype(dtype)
    return [jnp.concatenate([view(n) for n in grp], axis=0) for grp in GROUPS]


def _unpack_groups(bufs, shard_shapes):
    out = {}
    for buf, grp in zip(bufs, GROUPS):
        at = 0
        for n in grp:
            r, _ = _travel_shape(n, shard_shapes[n])
            out[n] = buf[..., at:at + r, :]
            at += r
    return out


def _pack_rows(arrays):
    flat = jnp.concatenate([a.reshape(-1) for a in arrays])
    rows = -(-flat.shape[0] // PACK_COLS)
    rows = -(-rows // 8) * 8
    return jnp.pad(flat, (0, rows * PACK_COLS - flat.shape[0])).reshape(rows, PACK_COLS)


def _unpack_rows(buf, shapes):
    flat = buf.reshape(-1)
    out, at = [], 0
    for s in shapes:
        n = int(np.prod(s))
        out.append(flat[at:at + n].reshape(s))
        at += n
    return out


def _full_weight(t, axis):
    if axis == 1:
        return jnp.moveaxis(t, 0, 1).reshape(t.shape[1], N_DEV * t.shape[2], t.shape[3])
    return jnp.moveaxis(t, 0, 2).reshape(t.shape[1], t.shape[2], N_DEV * t.shape[3])


def _disassemble(d):
    w_in, uq, ukv = d['w_in'], d['w_uq'], d['w_ukv']
    heads = lambda t: t.reshape(HEADS, -1, t.shape[1])
    full = dict(
        ffn1_w_gate=d['ffn1_gu'][:D_FF], ffn1_w_up=d['ffn1_gu'][D_FF:], ffn1_w_down=d['ffn1_down'],
        ffn2_w_gate=d['ffn2_gu'][:D_FF], ffn2_w_up=d['ffn2_gu'][D_FF:], ffn2_w_down=d['ffn2_down'],
        w_in=jnp.concatenate([w_in[a:a + n] for a, n in PIN_PIECES], axis=0),
        mla_w_uq=jnp.concatenate([heads(uq[:HEADS * NOPE]), heads(uq[HEADS * NOPE:HEADS * NOPE + LANES]),
                                  heads(uq[HEADS * NOPE + LANES:])], axis=1).reshape(HEADS * QK_DIM, Q_RANK),
        mla_w_ukv=jnp.concatenate([heads(ukv[:HEADS * NOPE]), heads(ukv[HEADS * NOPE:])],
                                  axis=1).reshape(HEADS * (NOPE + V_DIM), KV_RANK),
        mla_w_o=d['mla_w_o'], conv_w_out=d['conv_w_out'], s5_w_glu=d['s5_w_glu'], s5_w_out=d['s5_w_out'], w_o=d['w_o'])
    return [jnp.concatenate([full[n].reshape(N_DEV, -1, full[n].shape[-1]) for n in grp], axis=1) for grp in GROUPS]


def _assemble(gathered, shard_shapes):
    full = {n: t.reshape(N_DEV * t.shape[1], t.shape[2]) for n, t in _unpack_groups(gathered, shard_shapes).items()}
    w_in = full['w_in']
    cuts = np.cumsum((0,) + IN_SPLITS)
    cq, ckv, kr, xbar, bg, cg, u, gates = [w_in[a:b] for a, b in zip(cuts[:-1], cuts[1:])]
    pad = lambda t, n: jnp.pad(t, ((0, n - t.shape[0]), (0, 0)))
    w_in_packed = jnp.concatenate(
        [cq, ckv, pad(kr[:HALF_ROPE], LANES), pad(kr[HALF_ROPE:], LANES), xbar, bg, cg, u, gates,
         jnp.zeros((D_IN_PAD - PIN_END, D_MODEL), w_in.dtype)], axis=0)
    uq = full['mla_w_uq'].reshape(HEADS, QK_DIM, Q_RANK)
    w_uq = jnp.concatenate([uq[:, :NOPE].reshape(HEADS * NOPE, Q_RANK),
                            uq[:, NOPE:NOPE + HALF_ROPE].reshape(HEADS * HALF_ROPE, Q_RANK),
                            uq[:, NOPE + HALF_ROPE:].reshape(HEADS * HALF_ROPE, Q_RANK)], axis=0)
    ukv = full['mla_w_ukv'].reshape(HEADS, NOPE + V_DIM, KV_RANK)
    w_ukv = jnp.concatenate([ukv[:, :NOPE].reshape(HEADS * NOPE, KV_RANK),
                             ukv[:, NOPE:].reshape(HEADS * V_DIM, KV_RANK)], axis=0)
    return dict(
        ffn1_gu=jnp.concatenate([full['ffn1_w_gate'], full['ffn1_w_up']], axis=0), ffn1_down=full['ffn1_w_down'],
        w_in=w_in_packed, w_uq=w_uq, w_ukv=w_ukv, mla_w_o=full['mla_w_o'], conv_w_out=full['conv_w_out'],
        s5_w_glu=full['s5_w_glu'], s5_w_out=full['s5_w_out'], w_o=full['w_o'],
        ffn2_gu=jnp.concatenate([full['ffn2_w_gate'], full['ffn2_w_up']], axis=0), ffn2_down=full['ffn2_w_down'])


def _s5_discretize(a_re, a_im, log_dt, b_re, b_im, c_re, c_im):
    dt = jnp.exp(log_dt)[:, None]
    mag = jnp.exp(dt * a_re)
    ab_re, ab_im = mag * jnp.cos(dt * a_im), mag * jnp.sin(dt * a_im)
    den = a_re * a_re + a_im * a_im
    nr, ni = ab_re - 1.0, ab_im
    coef_re = (nr * a_re + ni * a_im) / den
    coef_im = (ni * a_re - nr * a_im) / den
    bb_re = coef_re[..., None] * b_re - coef_im[..., None] * b_im
    bb_im = coef_re[..., None] * b_im + coef_im[..., None] * b_re
    eye = jnp.eye(S5_GROUPS, dtype=F32)
    spread_b = lambda bb: jnp.einsum('gnh,gk->ghkn', bb, eye).reshape(MIX, S5_CH)
    spread_c = lambda cc: jnp.einsum('ghn,gk->gnkh', cc, eye).reshape(S5_CH, MIX)
    b_map = jnp.concatenate([spread_b(bb_re), spread_b(bb_im)], axis=1)
    c_map = jnp.concatenate([spread_c(c_re), -spread_c(c_im)], axis=0)
    return ab_re.reshape(SCAN_ROWS, LANES), ab_im.reshape(SCAN_ROWS, LANES), b_map, c_map


def _rope_tables(length):
    inv_freq = ROPE_BASE ** (-jnp.arange(0, ROPE, 2, dtype=F32) / ROPE)
    ang = jnp.arange(length).astype(F32)[:, None] * inv_freq[None, :]
    return jnp.tile(jnp.cos(ang), (1, LANES // HALF_ROPE)), jnp.tile(jnp.sin(ang), (1, LANES // HALF_ROPE))


def _heads_first(t):
    return t.reshape(t.shape[0], HEADS, -1).transpose(1, 0, 2)


def _local_loss(diff, big, n_real):
    small, wz = diff['small'], diff['wz']
    h = diff['h0']
    length = h.shape[0]
    cos, sin = _rope_tables(length)
    ln_half = lambda tag: _make_rowwise(_f_ln_half, 2, 2, (F32,), tag)
    row2 = lambda v: v.reshape(1, -1)
    for l in range(DEPTH):
        w, z = big[l], wz[l]
        p = {k: small[k][l] for k in small if k != 'meta'}

        def ffn(h, gu, down, g, b, tag):
            act = _make_swiglu(f"{tag}_swiglu")(_make_mm(f"{tag}_gu", wt=True)(h, gu[0], gu[1]))
            f = _make_mm(f"{tag}_down")(act, down[0], down[1])
            return ln_half(f"{tag}_ln")(h, f, row2(g), row2(b))[0]

        h = ffn(h, (w['ffn1_gu'], z['ffn1_gu']), (w['ffn1_down'], z['ffn1_down']), p['ln1_g'], p['ln1_b'], "ffn1")
        cq, ckv, kr1, kr2, xbar, bg, cg, u, gate_a, gate_b, gate_c, _ = _make_split(PIN_CUTS)(
            _make_mm("w_in", wt=True)(h, w['w_in'], z['w_in']))
        qn, = _make_rowwise(_f_rms, 1, 1, (BF16,), "q_rms")(cq, row2(p['mla_q_norm_g']))
        kvn, = _make_rowwise(_f_rms, 1, 1, (BF16,), "kv_rms")(ckv, row2(p['mla_kv_norm_g']))
        q_nope, q1, q2 = _make_split((0, HEADS * NOPE, HEADS * NOPE + LANES, HEADS * NOPE + 2 * LANES))(
            _make_mm("w_uq", wt=True)(qn, w['w_uq'], z['w_uq']))
        k_nope, val = _make_split((0, HEADS * NOPE, HEADS * (NOPE + V_DIM)))(
            _make_mm("w_ukv", wt=True)(kvn, w['w_ukv'], z['w_ukv']))
        rope = _make_rowwise(_f_rope, 4, 0, (BF16, BF16), "rope", n_nodiff=2)
        q1, q2 = rope(q1, q2, cos, sin)
        k1, k2 = rope(kr1, kr2, cos, sin)
        hpad = jnp.zeros((HEADS, length, HEAD_PAD - QK_DIM), BF16)
        q3 = jnp.concatenate([_heads_first(q_nope.astype(BF16)), _heads_first(q1), _heads_first(q2), hpad], -1)
        shared = lambda t: jnp.broadcast_to(t[None, :, :HALF_ROPE], (HEADS, length, HALF_ROPE))
        k3 = jnp.concatenate([_heads_first(k_nope.astype(BF16)), shared(k1), shared(k2), hpad], -1)
        v3 = _heads_first(val.astype(BF16))
        o3 = _attention(q3, k3, v3)
        y_a = _make_mm("mla_w_o", wt=True)(o3.transpose(1, 0, 2).reshape(length, MIX), w['mla_w_o'], z['mla_w_o'])
        conv = _short_conv(xbar, bg, cg, p['conv_w_full'], row2(p['conv_b']))
        y_b = _make_mm("conv_w_out", wt=True)(conv, w['conv_w_out'], z['conv_w_out'])
        ar, ai, b_map, c_map = _s5_discretize(p['s5_a_re'], p['s5_a_im'], p['s5_log_dt'], p['s5_b_re'], p['s5_b_im'],
                                              p['s5_c_re'], p['s5_c_im'])
        bu = _make_mm_f32w("s5_b")(u, b_map)
        states = _s5_scan(ar, ai, bu.reshape(length, 2 * SCAN_ROWS, LANES)).reshape(length, 2 * S5_CH)
        y_ssm = _make_mm_f32w("s5_c")(states, c_map)
        zed, = _make_rowwise(_f_gelu_skip, 2, 1, (F32,), "s5_gelu")(y_ssm, u, row2(p['s5_d']))
        t = _make_mm("s5_w_glu")(zed, w['s5_w_glu'], z['s5_w_glu'])
        glu, = _make_rowwise(_f_glu, 2, 1, (BF16,), "s5_glu")(zed, t, row2(p['s5_b_glu']))
        y_c = _make_mm("s5_w_out", wt=True)(glu, w['s5_w_out'], z['s5_w_out'])
        mixed, = _make_rowwise(_f_merge, 6, 0, (BF16,), "merge")(gate_a, gate_b, gate_c, y_a, y_b, y_c)
        mix_out = _make_mm("w_o")(mixed, w['w_o'], z['w_o'])
        h, = _make_rowwise(_f_ln_full, 2, 2, (F32,), "mix_ln")(h, mix_out, row2(p['ln2_g']), row2(p['ln2_b']))
        h = ffn(h, (w['ffn2_gu'], z['ffn2_gu']), (w['ffn2_down'], z['ffn2_down']), p['ln3_g'], p['ln3_b'], "ffn2")
    return _make_loss(n_real)(h, diff['target'])


def kernel(x, meta, ffn1_w_gate, ffn1_w_up, ffn1_w_down, ln1_g, ln1_b, w_in, mla_q_norm_g, mla_w_uq, mla_kv_norm_g, mla_w_ukv, mla_w_o, conv_w, conv_b, conv_w_out, s5_a_re, s5_a_im, s5_log_dt, s5_b_re, s5_b_im, s5_c_re, s5_c_im, s5_d, s5_w_glu, s5_b_glu, s5_w_out, w_o, ln2_g, ln2_b, ffn2_w_gate, ffn2_w_up, ffn2_w_down, ln3_g, ln3_b, loss_target, m_meta, m_ffn1_w_gate, m_ffn1_w_up, m_ffn1_w_down, m_ln1_g, m_ln1_b, m_w_in, m_mla_q_norm_g, m_mla_w_uq, m_mla_kv_norm_g, m_mla_w_ukv, m_mla_w_o, m_conv_w, m_conv_b, m_conv_w_out, m_s5_a_re, m_s5_a_im, m_s5_log_dt, m_s5_b_re, m_s5_b_im, m_s5_c_re, m_s5_c_im, m_s5_d, m_s5_w_glu, m_s5_b_glu, m_s5_w_out, m_w_o, m_ln2_g, m_ln2_b, m_ffn2_w_gate, m_ffn2_w_up, m_ffn2_w_down, m_ln3_g, m_ln3_b, v_meta, v_ffn1_w_gate, v_ffn1_w_up, v_ffn1_w_down, v_ln1_g, v_ln1_b, v_w_in, v_mla_q_norm_g, v_mla_w_uq, v_mla_kv_norm_g, v_mla_w_ukv, v_mla_w_o, v_conv_w, v_conv_b, v_conv_w_out, v_s5_a_re, v_s5_a_im, v_s5_log_dt, v_s5_b_re, v_s5_b_im, v_s5_c_re, v_s5_c_im, v_s5_d, v_s5_w_glu, v_s5_b_glu, v_s5_w_out, v_w_o, v_ln2_g, v_ln2_b, v_ffn2_w_gate, v_ffn2_w_up, v_ffn2_w_down, v_ln3_g, v_ln3_b):
    args = locals()
    w = {n: args[n] for n in WEIGHT_NAMES}
    m = {n: args["m_" + n] for n in WEIGHT_NAMES}
    v = {n: args["v_" + n] for n in WEIGHT_NAMES}
    me = 4 * lax.axis_index("x") + 2 * lax.axis_index("y") + lax.axis_index("c")
    seq = x.shape[1]
    n_real = N_META + seq
    length = -(-n_real // LANES) * LANES

    shard_shapes = {n: w[n].shape for n in BIG}
    big = [_assemble(_all_gather(_pack_groups(w, l, BF16), f"gather_weights_layer{l}", sequencer=True), shard_shapes)
           for l in range(DEPTH)]
    small_shards = _all_gather([_pack_rows([w[n] for n in SMALL_SHARDED])], "gather_small")[0].reshape(N_DEV, -1)
    meta_full = _full_weight(small_shards[:, :meta.size].reshape(N_DEV, 1, *meta.shape), 2)[0]
    conv_w_full = _full_weight(small_shards[:, meta.size:meta.size + conv_w.size].reshape(N_DEV, *conv_w.shape), 2)

    small = {n: w[n] for n in SMALL_NAMES if n not in SMALL_SHARDED}
    small['conv_w_full'] = conv_w_full
    small['meta'] = meta_full
    wz = jax.tree.map(lambda t: jnp.zeros(t.shape, F32), big)
    pad_rows = length - n_real

    def loss_fn(diff):
        h0 = jnp.concatenate([diff['small']['meta'], diff['x'], jnp.zeros((pad_rows, D_MODEL), F32)], axis=0)
        target = jnp.pad(loss_target[0], ((N_META, pad_rows), (0, 0)))
        return _local_loss(dict(h0=h0, small=diff['small'], wz=diff['wz'], target=target), big, n_real)

    loss_local, grads = jax.value_and_grad(loss_fn)(dict(x=x[0], small=small, wz=wz))
    loss = lax.psum(loss_local, AXES)

    small_names = [n for n in SMALL_NAMES if n not in SMALL_SHARDED] + ['conv_w_full', 'meta']
    small_flat = _pack_rows([grads['small'][n] for n in small_names])
    rows_each = -(-small_flat.shape[0] // (8 * N_DEV)) * 8
    small_flat = jnp.pad(small_flat, ((0, rows_each * N_DEV - small_flat.shape[0]), (0, 0)))
    layer_sums = [None] * DEPTH
    for l in reversed(range(DEPTH)):
        extra = [small_flat.reshape(N_DEV, rows_each, PACK_COLS)] if l == 0 else []
        sums = _reduce_scatter(_disassemble(grads['wz'][l]) + extra, [BF16] * len(GROUPS) + [F32] * len(extra),
                               f"grads_layer{l}", sequencer=True)
        layer_sums[l] = _unpack_groups(sums[:len(GROUPS)], shard_shapes)
        if l == 0:
            small_sum = sums[-1]
    small_all, = _all_gather([small_sum], "gather_small_grads")
    small_grads = dict(zip(small_names, _unpack_rows(small_all, [grads['small'][n].shape for n in small_names])))
    g = {}
    for name in SMALL_NAMES:
        if name == 'meta':
            g[name] = lax.dynamic_slice_in_dim(small_grads['meta'], me * meta.shape[1], meta.shape[1], axis=1)
        elif name == 'conv_w':
            g[name] = lax.dynamic_slice_in_dim(small_grads['conv_w_full'], me * conv_w.shape[2], conv_w.shape[2], axis=2)
        else:
            g[name] = small_grads[name]

    delta, new_m, new_v = {}, {}, {}
    for name in BIG:
        per_layer = [s[name].T if name in TRANSPOSED else s[name] for s in layer_sums]
        g[name], delta[name], new_m[name], new_v[name] = _adamw_layers(w[name], per_layer, m[name], v[name], f"adamw_{name}")
    shapes = [w[n].shape for n in SMALL_NAMES]
    d, nm, nv = _adamw(*[_pack_rows([t[n] for n in SMALL_NAMES]) for t in (w, g, m, v)], "adamw_small")
    for out, buf in ((delta, d), (new_m, nm), (new_v, nv)):
        out.update(zip(SMALL_NAMES, _unpack_rows(buf, shapes)))

    return (loss, grads['x'][None], *[g[n] for n in WEIGHT_NAMES], *[delta[n] for n in WEIGHT_NAMES],
            *[new_m[n] for n in WEIGHT_NAMES], *[new_v[n] for n in WEIGHT_NAMES])
```

```python
import functools
import math

import jax
import jax.numpy as jnp
import numpy as np
from jax import lax
from jax.experimental import pallas as pl
from jax.experimental.pallas import tpu as pltpu
from jax.experimental.pallas import tpu_sc as plsc

F32 = jnp.float32
BF16 = jnp.bfloat16

D_MODEL = 1024
DEPTH = 2
N_META = 16
HEADS = 8
V_DIM = 64
NOPE = 64
ROPE = 32
HALF_ROPE = ROPE // 2
QK_DIM = NOPE + ROPE
Q_RANK = 384
KV_RANK = 256
MIX = 512
CONV_K = 3
S5_GROUPS = 32
S5_GROUP = 16
S5_STATE = 64
S5_CH = S5_GROUPS * S5_STATE
D_FF = 2816
ALPHA = (2.0 * DEPTH) ** 0.25
LN_EPS = 1e-5
RMS_EPS = 1e-6
ROPE_BASE = 10000.0
IN_SPLITS = (Q_RANK, KV_RANK, ROPE, MIX, MIX, MIX, MIX, 3 * D_MODEL)
D_IN = sum(IN_SPLITS)
ADAM_LR, ADAM_B1, ADAM_B2, ADAM_EPS, ADAM_WD, ADAM_STEP = 0.001, 0.9, 0.999, 1e-08, 0.01, 10

N_DEV = 8
AXES = ("x", "y", "c")
LANES = 128
PACK_COLS = 1024
HEAD_PAD = 128
VMEM_LIMIT = 48 * 1024 * 1024

PIN_CQ, PIN_CKV, PIN_KR1, PIN_KR2, PIN_XBAR, PIN_BG, PIN_CG, PIN_U, PIN_GATES, PIN_END = (
    0, 384, 640, 768, 896, 1408, 1920, 2432, 2944, 6016)
D_IN_PAD = 6144

WEIGHT_NAMES = ['meta', 'ffn1_w_gate', 'ffn1_w_up', 'ffn1_w_down', 'ln1_g', 'ln1_b', 'w_in', 'mla_q_norm_g', 'mla_w_uq',
                'mla_kv_norm_g', 'mla_w_ukv', 'mla_w_o', 'conv_w', 'conv_b', 'conv_w_out', 's5_a_re', 's5_a_im',
                's5_log_dt', 's5_b_re', 's5_b_im', 's5_c_re', 's5_c_im', 's5_d', 's5_w_glu', 's5_b_glu', 's5_w_out',
                'w_o', 'ln2_g', 'ln2_b', 'ffn2_w_gate', 'ffn2_w_up', 'ffn2_w_down', 'ln3_g', 'ln3_b']
BIG = {'ffn1_w_gate': 2, 'ffn1_w_up': 2, 'ffn1_w_down': 1, 'w_in': 2, 'mla_w_uq': 2, 'mla_w_ukv': 2, 'mla_w_o': 2,
       'conv_w_out': 2, 's5_w_glu': 1, 's5_w_out': 2, 'w_o': 1, 'ffn2_w_gate': 2, 'ffn2_w_up': 2, 'ffn2_w_down': 1}
SMALL_SHARDED = ('meta', 'conv_w')
SMALL_NAMES = [n for n in WEIGHT_NAMES if n not in BIG]


def _divisor_tile(n, limit, mult):
    best = None
    for t in range(mult, min(n, limit) + 1, mult):
        if n % t == 0:
            best = t
    return best if best is not None else n


def _params(*sem):
    return pltpu.CompilerParams(dimension_semantics=sem, vmem_limit_bytes=VMEM_LIMIT)


def _matmul(a, b, *, ta=False, tb=False, out_dtype=F32, name):
    m, k = (a.shape[1], a.shape[0]) if ta else a.shape
    n = b.shape[0] if tb else b.shape[1]
    assert (b.shape[1] if tb else b.shape[0]) == k, (a.shape, b.shape, ta, tb)
    tm = _divisor_tile(m, 1408, LANES) if ta else _divisor_tile(m, 1088, 16)
    tn = _divisor_tile(n, 512, LANES)
    tk = _divisor_tile(k, 1408, 16 if ta else LANES)
    nk = k // tk
    dims = (((0 if ta else 1,), (1 if tb else 0,)), ((), ()))

    def kern(a_ref, b_ref, o_ref, acc_ref):
        kk = pl.program_id(2)
        part = lax.dot_general(a_ref[...].astype(BF16), b_ref[...].astype(BF16), dims, preferred_element_type=F32)

        @pl.when(kk == 0)
        def _():
            acc_ref[...] = part

        @pl.when(kk > 0)
        def _():
            acc_ref[...] += part

        @pl.when(kk == nk - 1)
        def _():
            o_ref[...] = acc_ref[...].astype(o_ref.dtype)

    a_spec = pl.BlockSpec((tk, tm), lambda i, j, kk: (kk, i)) if ta else pl.BlockSpec((tm, tk), lambda i, j, kk: (i, kk))
    b_spec = pl.BlockSpec((tn, tk), lambda i, j, kk: (j, kk)) if tb else pl.BlockSpec((tk, tn), lambda i, j, kk: (kk, j))
    return pl.pallas_call(
        kern, name=name, grid=(m // tm, n // tn, nk),
        in_specs=[a_spec, b_spec], out_specs=pl.BlockSpec((tm, tn), lambda i, j, kk: (i, j)),
        out_shape=jax.ShapeDtypeStruct((m, n), out_dtype),
        scratch_shapes=[pltpu.VMEM((tm, tn), F32)],
        compiler_params=_params("parallel", "parallel", "arbitrary"),
    )(a, b)


def _make_mm(name, wt=False):
    @jax.custom_vjp
    def mm(x, w, wz):
        return _matmul(x, w, tb=wt, name=name + "_fwd")

    def fwd(x, w, wz):
        return _matmul(x, w, tb=wt, name=name + "_fwd"), (x, w)

    def bwd(res, dy):
        x, w = res
        dx = _matmul(dy, w, tb=not wt, out_dtype=x.dtype, name=name + "_dx")
        dw = _matmul(dy, x, ta=True, name=name + "_dw") if wt else _matmul(x, dy, ta=True, name=name + "_dw")
        return dx, jnp.zeros_like(w), dw

    mm.defvjp(fwd, bwd)
    return mm


def _make_mm_f32w(name):
    @jax.custom_vjp
    def mm(x, w):
        return _matmul(x, w, name=name + "_fwd")

    def fwd(x, w):
        return _matmul(x, w, name=name + "_fwd"), (x, w)

    def bwd(res, dy):
        x, w = res
        return (_matmul(dy, w, tb=True, out_dtype=x.dtype, name=name + "_dx"),
                _matmul(x, dy, ta=True, name=name + "_dw"))

    mm.defvjp(fwd, bwd)
    return mm


def _row_tile(rows, widths):
    limit = max(16, (6 * 1024 * 1024 // 4) // max(1, sum(widths)))
    return _divisor_tile(rows, limit, 16)


def _make_rowwise(f, n_rows, n_pars, out_dtypes, name, n_nodiff=0):
    n_out = len(out_dtypes)
    n_diff = n_rows - n_nodiff

    def run_fwd(rows, pars):
        length = rows[0].shape[0]
        shapes = jax.eval_shape(lambda *a: f(*a), *[jax.ShapeDtypeStruct((16, r.shape[1]), F32) for r in rows],
                                *[jax.ShapeDtypeStruct(p.shape, F32) for p in pars])
        widths = [s.shape[1] for s in shapes]
        tm = _row_tile(length, [r.shape[1] for r in rows] + widths)

        def kern(*refs):
            ins = [r[...].astype(F32) for r in refs[:n_rows + n_pars]]
            outs = f(*ins)
            for o_ref, o in zip(refs[n_rows + n_pars:], outs):
                o_ref[...] = o.astype(o_ref.dtype)

        return pl.pallas_call(
            kern, name=name + "_fwd", grid=(length // tm,),
            in_specs=[pl.BlockSpec((tm, r.shape[1]), lambda i: (i, 0)) for r in rows]
            + [pl.BlockSpec(p.shape, lambda i: (0, 0)) for p in pars],
            out_specs=[pl.BlockSpec((tm, w), lambda i: (i, 0)) for w in widths],
            out_shape=[jax.ShapeDtypeStruct((length, w), dt) for w, dt in zip(widths, out_dtypes)],
            compiler_params=_params("parallel"),
        )(*rows, *pars)

    def run_bwd(rows, pars, cts):
        length = rows[0].shape[0]
        tm = _row_tile(length, [r.shape[1] for r in rows] * 2 + [c.shape[1] for c in cts] * 2)

        def kern(*refs):
            ins = [r[...].astype(F32) for r in refs[:n_rows + n_pars]]
            ct = [r[...].astype(F32) for r in refs[n_rows + n_pars:n_rows + n_pars + n_out]]
            out_refs = refs[n_rows + n_pars + n_out:]
            nodiff = ins[n_diff:n_rows]
            _, vjp = jax.vjp(lambda *a: f(*a[:n_diff], *nodiff, *a[n_diff:]), *ins[:n_diff], *ins[n_rows:])
            grads = vjp(tuple(ct))
            for o_ref, g in zip(out_refs[:n_diff], grads[:n_diff]):
                o_ref[...] = g.astype(o_ref.dtype)
            first = pl.program_id(0) == 0
            for o_ref, g in zip(out_refs[n_diff:], grads[n_diff:]):
                @pl.when(first)
                def _(o_ref=o_ref, g=g):
                    o_ref[...] = g

                @pl.when(jnp.logical_not(first))
                def _(o_ref=o_ref, g=g):
                    o_ref[...] += g

        return pl.pallas_call(
            kern, name=name + "_bwd", grid=(length // tm,),
            in_specs=[pl.BlockSpec((tm, r.shape[1]), lambda i: (i, 0)) for r in rows]
            + [pl.BlockSpec(p.shape, lambda i: (0, 0)) for p in pars]
            + [pl.BlockSpec((tm, c.shape[1]), lambda i: (i, 0)) for c in cts],
            out_specs=[pl.BlockSpec((tm, r.shape[1]), lambda i: (i, 0)) for r in rows[:n_diff]]
            + [pl.BlockSpec(p.shape, lambda i: (0, 0)) for p in pars],
            out_shape=[jax.ShapeDtypeStruct(r.shape, r.dtype) for r in rows[:n_diff]]
            + [jax.ShapeDtypeStruct(p.shape, F32) for p in pars],
            compiler_params=_params("arbitrary"),
        )(*rows, *pars, *cts)

    @jax.custom_vjp
    def op(*args):
        return tuple(run_fwd(args[:n_rows], args[n_rows:]))

    def fwd(*args):
        return tuple(run_fwd(args[:n_rows], args[n_rows:])), args

    def bwd(args, cts):
        grads = run_bwd(args[:n_rows], args[n_rows:], cts)
        zeros = [jnp.zeros_like(r) for r in args[n_diff:n_rows]]
        return (*grads[:n_diff], *zeros, *grads[n_diff:])

    op.defvjp(fwd, bwd)
    return op


def _layer_norm(z, g, b):
    mu = jnp.mean(z, axis=-1, keepdims=True)
    d = z - mu
    var = jnp.mean(d * d, axis=-1, keepdims=True)
    return d * lax.rsqrt(var + LN_EPS) * g + b


def _f_ln_half(h, f, g, b):
    return (_layer_norm(ALPHA * h + 0.5 * f, g, b),)


def _f_ln_full(h, f, g, b):
    return (_layer_norm(ALPHA * h + f, g, b),)


def _f_rms(x, g):
    return (x * lax.rsqrt(jnp.mean(x * x, axis=-1, keepdims=True) + RMS_EPS) * g,)


def _f_rope(x1, x2, cos, sin):
    return x1 * cos - x2 * sin, x2 * cos + x1 * sin


def _f_gelu_skip(y, u, d):
    return (jax.nn.gelu(y + d * u),)


def _f_glu(z, t, b):
    return (z * jax.nn.sigmoid(t + b),)


def _f_merge(ga, gb, gc, ya, yb, yc):
    return (jax.nn.sigmoid(ga) * ya + jax.nn.sigmoid(gb) * yb + jax.nn.sigmoid(gc) * yc,)


def _make_swiglu(name):
    def run_fwd(gu):
        length = gu.shape[0]
        tm = _row_tile(length, [3 * D_FF])

        def kern(g_ref, u_ref, o_ref):
            o_ref[...] = (jax.nn.silu(g_ref[...]) * u_ref[...]).astype(o_ref.dtype)

        half = lambda j: pl.BlockSpec((tm, D_FF), lambda i: (i, j))
        return pl.pallas_call(
            kern, name=name + "_fwd", grid=(length // tm,), in_specs=[half(0), half(1)], out_specs=half(0),
            out_shape=jax.ShapeDtypeStruct((length, D_FF), BF16), compiler_params=_params("parallel"),
        )(gu, gu)

    def run_bwd(gu, dact):
        length = gu.shape[0]
        tm = _row_tile(length, [6 * D_FF])

        def kern(g_ref, u_ref, d_ref, o_ref):
            _, vjp = jax.vjp(lambda g, u: jax.nn.silu(g) * u, g_ref[...], u_ref[...])
            dg, du = vjp(d_ref[...].astype(F32))
            o_ref[:, :D_FF] = dg
            o_ref[:, D_FF:] = du

        half = lambda j: pl.BlockSpec((tm, D_FF), lambda i: (i, j))
        return pl.pallas_call(
            kern, name=name + "_bwd", grid=(length // tm,), in_specs=[half(0), half(1), half(0)],
            out_specs=pl.BlockSpec((tm, 2 * D_FF), lambda i: (i, 0)),
            out_shape=jax.ShapeDtypeStruct(gu.shape, F32), compiler_params=_params("parallel"),
        )(gu, gu, dact)

    @jax.custom_vjp
    def op(gu):
        return run_fwd(gu)

    def fwd(gu):
        return run_fwd(gu), gu

    def bwd(gu, dact):
        return (run_bwd(gu, dact),)

    op.defvjp(fwd, bwd)
    return op


def _attn_scores(q, k, q_block, tq):
    length = k.shape[0]
    s = lax.dot_general(q, k, (((1,), (1,)), ((), ())), preferred_element_type=F32) * (QK_DIM ** -0.5)
    row = q_block * tq + lax.broadcasted_iota(jnp.int32, (tq, length), 0)
    col = lax.broadcasted_iota(jnp.int32, (tq, length), 1)
    s = jnp.where(col <= row, s, -1e30)
    e = jnp.exp(s - jnp.max(s, axis=1, keepdims=True))
    return e * (1.0 / jnp.sum(e, axis=1, keepdims=True))


ATTN_SEGMENTS = 4


def _attn_tiles(length):
    seg = length // ATTN_SEGMENTS
    return seg, _divisor_tile(seg, 272, 16)


def _attn_fwd(q3, k3, v3):
    heads, length, _ = q3.shape
    seg, tq = _attn_tiles(length)
    outs = []
    for s in range(ATTN_SEGMENTS):
        kmax, base = (s + 1) * seg, s * (seg // tq)

        def kern(q_ref, k_ref, v_ref, o_ref, base=base):
            p = _attn_scores(q_ref[0], k_ref[0], base + pl.program_id(1), tq)
            o_ref[0] = jnp.dot(p.astype(BF16), v_ref[0], preferred_element_type=F32).astype(o_ref.dtype)

        outs.append(pl.pallas_call(
            kern, name=f"attn_fwd_seg{s}", grid=(heads, seg // tq),
            in_specs=[pl.BlockSpec((1, tq, HEAD_PAD), lambda h, i, base=base: (h, base + i, 0)),
                      pl.BlockSpec((1, kmax, HEAD_PAD), lambda h, i: (h, 0, 0)),
                      pl.BlockSpec((1, kmax, V_DIM), lambda h, i: (h, 0, 0))],
            out_specs=pl.BlockSpec((1, tq, V_DIM), lambda h, i: (h, i, 0)),
            out_shape=jax.ShapeDtypeStruct((heads, seg, V_DIM), F32),
            compiler_params=_params("parallel", "parallel"),
        )(q3, k3, v3))
    return jnp.concatenate(outs, axis=1)


def _attn_bwd(q3, k3, v3, do3):
    heads, length, _ = q3.shape
    seg, tq = _attn_tiles(length)
    dk = jnp.zeros((heads, length, HEAD_PAD), F32)
    dv = jnp.zeros((heads, length, V_DIM), F32)
    dqs = [None] * ATTN_SEGMENTS
    for s in reversed(range(ATTN_SEGMENTS)):
        kmax, base = (s + 1) * seg, s * (seg // tq)

        def kern(q_ref, k_ref, v_ref, do_ref, dk_in, dv_in, dq_ref, dk_ref, dv_ref, base=base):
            i = pl.program_id(1)
            q, k, v, do = q_ref[0], k_ref[0], v_ref[0], do_ref[0].astype(BF16)
            p = _attn_scores(q, k, base + i, tq)
            dp = lax.dot_general(do, v, (((1,), (1,)), ((), ())), preferred_element_type=F32)
            ds = (p * (dp - jnp.sum(p * dp, axis=1, keepdims=True)) * (QK_DIM ** -0.5)).astype(BF16)
            dq_ref[0] = jnp.dot(ds, k, preferred_element_type=F32)
            dk_part = lax.dot_general(ds, q, (((0,), (0,)), ((), ())), preferred_element_type=F32)
            dv_part = lax.dot_general(p.astype(BF16), do, (((0,), (0,)), ((), ())), preferred_element_type=F32)

            @pl.when(i == 0)
            def _():
                dk_ref[0] = dk_in[0] + dk_part
                dv_ref[0] = dv_in[0] + dv_part

            @pl.when(i > 0)
            def _():
                dk_ref[0] += dk_part
                dv_ref[0] += dv_part

        q_blk = pl.BlockSpec((1, tq, HEAD_PAD), lambda h, i, base=base: (h, base + i, 0))
        k_blk = pl.BlockSpec((1, kmax, HEAD_PAD), lambda h, i: (h, 0, 0))
        v_blk = pl.BlockSpec((1, kmax, V_DIM), lambda h, i: (h, 0, 0))
        dqs[s], dk, dv = pl.pallas_call(
            kern, name=f"attn_bwd_seg{s}", grid=(heads, seg // tq),
            in_specs=[q_blk, k_blk, v_blk, pl.BlockSpec((1, tq, V_DIM), lambda h, i, base=base: (h, base + i, 0)),
                      k_blk, v_blk],
            out_specs=[pl.BlockSpec((1, tq, HEAD_PAD), lambda h, i: (h, i, 0)), k_blk, v_blk],
            out_shape=[jax.ShapeDtypeStruct((heads, seg, HEAD_PAD), F32), jax.ShapeDtypeStruct(dk.shape, F32),
                       jax.ShapeDtypeStruct(dv.shape, F32)],
            input_output_aliases={4: 1, 5: 2}, compiler_params=_params("parallel", "arbitrary"),
        )(q3, k3, v3, do3, dk, dv)
    return jnp.concatenate(dqs, axis=1), dk, dv


@jax.custom_vjp
def _attention(q3, k3, v3):
    return _attn_fwd(q3, k3, v3)


def _attention_fwd(q3, k3, v3):
    return _attn_fwd(q3, k3, v3), (q3, k3, v3)


def _attention_bwd(res, do3):
    q3, k3, v3 = res
    dq, dk, dv = _attn_bwd(q3, k3, v3, do3)
    return dq.astype(q3.dtype), dk.astype(k3.dtype), dv.astype(v3.dtype)


_attention.defvjp(_attention_fwd, _attention_bwd)


def _conv_terms(x, c, w_ref, cb):
    u = c * x
    row = lax.broadcasted_iota(jnp.int32, u.shape, 0)
    u1 = jnp.where(row >= 1, pltpu.roll(u, 1, 0), 0.0)
    u2 = jnp.where(row >= 2, pltpu.roll(u, 2, 0), 0.0)
    y = cb + w_ref[0:1, :] * u2 + w_ref[1:2, :] * u1 + w_ref[2:3, :] * u
    return u, u1, u2, y


def _conv_specs(length):
    col = pl.BlockSpec((length, LANES), lambda j: (0, j))
    return col, pl.BlockSpec((CONV_K, LANES), lambda j: (0, j)), pl.BlockSpec((1, LANES), lambda j: (0, j))


def _conv_fwd(x, b, c, w, cb):
    length = x.shape[0]

    def kern(x_ref, b_ref, c_ref, w_ref, cb_ref, o_ref):
        _, _, _, y = _conv_terms(x_ref[...], c_ref[...], w_ref, cb_ref[...])
        o_ref[...] = b_ref[...] * y

    col, wspec, bspec = _conv_specs(length)
    return pl.pallas_call(
        kern, name="conv_fwd", grid=(MIX // LANES,), in_specs=[col, col, col, wspec, bspec], out_specs=col,
        out_shape=jax.ShapeDtypeStruct((length, MIX), F32), compiler_params=_params("parallel"),
    )(x, b, c, w, cb)


def _conv_bwd(x, b, c, w, cb, do):
    length = x.shape[0]

    def kern(x_ref, b_ref, c_ref, w_ref, cb_ref, do_ref, dx_ref, db_ref, dc_ref, dw_ref, dcb_ref):
        x, c, do = x_ref[...], c_ref[...], do_ref[...]
        u, u1, u2, y = _conv_terms(x, c, w_ref, cb_ref[...])
        db_ref[...] = do * y
        dy = do * b_ref[...]
        row = lax.broadcasted_iota(jnp.int32, dy.shape, 0)
        dy1 = jnp.where(row < length - 1, pltpu.roll(dy, length - 1, 0), 0.0)
        dy2 = jnp.where(row < length - 2, pltpu.roll(dy, length - 2, 0), 0.0)
        du = w_ref[2:3, :] * dy + w_ref[1:2, :] * dy1 + w_ref[0:1, :] * dy2
        dx_ref[...] = du * c
        dc_ref[...] = du * x
        dw_ref[0:1, :] = jnp.sum(dy * u2, axis=0, keepdims=True)
        dw_ref[1:2, :] = jnp.sum(dy * u1, axis=0, keepdims=True)
        dw_ref[2:3, :] = jnp.sum(dy * u, axis=0, keepdims=True)
        dcb_ref[...] = jnp.sum(dy, axis=0, keepdims=True)

    col, wspec, bspec = _conv_specs(length)
    big = jax.ShapeDtypeStruct((length, MIX), F32)
    return pl.pallas_call(
        kern, name="conv_bwd", grid=(MIX // LANES,), in_specs=[col, col, col, wspec, bspec, col],
        out_specs=[col, col, col, wspec, bspec],
        out_shape=[big, big, big, jax.ShapeDtypeStruct((CONV_K, MIX), F32), jax.ShapeDtypeStruct((1, MIX), F32)],
        compiler_params=_params("parallel"),
    )(x, b, c, w, cb, do)


@jax.custom_vjp
def _short_conv(x, b, c, w, cb):
    return _conv_fwd(x, b, c, w, cb)


def _short_conv_fwd(x, b, c, w, cb):
    return _conv_fwd(x, b, c, w, cb), (x, b, c, w, cb)


def _short_conv_bwd(res, do):
    return tuple(_conv_bwd(*res, do))


_short_conv.defvjp(_short_conv_fwd, _short_conv_bwd)


SCAN_ROWS = S5_CH // LANES
SCAN_TC = 136


def _scan_fwd(ar, ai, b):
    length = b.shape[0]
    tc = _divisor_tile(length, SCAN_TC, 8)

    def kern(ar_ref, ai_ref, b_ref, x_ref, sr, si):
        @pl.when(pl.program_id(0) == 0)
        def _():
            sr[...] = jnp.zeros_like(sr)
            si[...] = jnp.zeros_like(si)

        a_re, a_im = ar_ref[...], ai_ref[...]

        def body(t, carry):
            xr, xi = carry
            nr = a_re * xr - a_im * xi + b_ref[t, 0:SCAN_ROWS, :]
            ni = a_re * xi + a_im * xr + b_ref[t, SCAN_ROWS:2 * SCAN_ROWS, :]
            x_ref[t, 0:SCAN_ROWS, :] = nr
            x_ref[t, SCAN_ROWS:2 * SCAN_ROWS, :] = ni
            return nr, ni

        xr, xi = lax.fori_loop(0, tc, body, (sr[...], si[...]), unroll=4)
        sr[...] = xr
        si[...] = xi

    par = pl.BlockSpec((SCAN_ROWS, LANES), lambda i: (0, 0))
    blk = pl.BlockSpec((tc, 2 * SCAN_ROWS, LANES), lambda i: (i, 0, 0))
    return pl.pallas_call(
        kern, name="s5_scan_fwd", grid=(length // tc,), in_specs=[par, par, blk], out_specs=blk,
        out_shape=jax.ShapeDtypeStruct(b.shape, F32), scratch_shapes=[pltpu.VMEM((SCAN_ROWS, LANES), F32)] * 2,
        compiler_params=_params("arbitrary"),
    )(ar, ai, b)


def _scan_bwd(ar, ai, x, dx):
    length = x.shape[0]
    tc = _divisor_tile(length, SCAN_TC, 8)
    n_blk = length // tc
    re, im = slice(0, SCAN_ROWS), slice(SCAN_ROWS, 2 * SCAN_ROWS)

    def kern(ar_ref, ai_ref, x_ref, dx_ref, db_ref, dar_ref, dai_ref, lr_s, li_s):
        @pl.when(pl.program_id(0) == 0)
        def _():
            lr_s[...] = jnp.zeros_like(lr_s)
            li_s[...] = jnp.zeros_like(li_s)
            dar_ref[...] = jnp.zeros_like(dar_ref)
            dai_ref[...] = jnp.zeros_like(dai_ref)

        a_re, a_im = ar_ref[...], ai_ref[...]

        def body(j, carry):
            t = tc - 1 - j
            lr, li, gr, gi = carry
            x_re, x_im = x_ref[t, re, :], x_ref[t, im, :]
            gr = gr + (lr * x_re + li * x_im)
            gi = gi + (li * x_re - lr * x_im)
            nlr = dx_ref[t, re, :] + (a_re * lr + a_im * li)
            nli = dx_ref[t, im, :] + (a_re * li - a_im * lr)
            db_ref[t, re, :] = nlr
            db_ref[t, im, :] = nli
            return nlr, nli, gr, gi

        lr, li, gr, gi = lax.fori_loop(0, tc, body, (lr_s[...], li_s[...], dar_ref[...], dai_ref[...]), unroll=4)
        lr_s[...] = lr
        li_s[...] = li
        dar_ref[...] = gr
        dai_ref[...] = gi

    par = pl.BlockSpec((SCAN_ROWS, LANES), lambda i: (0, 0))
    blk = pl.BlockSpec((tc, 2 * SCAN_ROWS, LANES), lambda i: (n_blk - 1 - i, 0, 0))
    pout = jax.ShapeDtypeStruct((SCAN_ROWS, LANES), F32)
    return pl.pallas_call(
        kern, name="s5_scan_bwd", grid=(n_blk,), in_specs=[par, par, blk, blk],
        out_specs=[blk, par, par], out_shape=[jax.ShapeDtypeStruct(x.shape, F32), pout, pout],
        scratch_shapes=[pltpu.VMEM((SCAN_ROWS, LANES), F32)] * 2, compiler_params=_params("arbitrary"),
    )(ar, ai, x, dx)


@jax.custom_vjp
def _s5_scan(ar, ai, b):
    return _scan_fwd(ar, ai, b)


def _s5_scan_fwd(ar, ai, b):
    x = _scan_fwd(ar, ai, b)
    return x, (ar, ai, x)


def _s5_scan_bwd(res, dx):
    ar, ai, x = res
    db, dar, dai = _scan_bwd(ar, ai, x, dx)
    return dar, dai, db


_s5_scan.defvjp(_s5_scan_fwd, _s5_scan_bwd)


def _loss_call(y, target, n_real):
    length = y.shape[0]
    tm = _divisor_tile(length, 544, 16)

    def kern(y_ref, t_ref, loss_ref, dy_ref):
        i = pl.program_id(0)
        row = i * tm + lax.broadcasted_iota(jnp.int32, (tm, 1), 0)
        keep = jnp.logical_and(row >= N_META, row < n_real)
        err = jnp.where(keep, y_ref[...] - t_ref[...], 0.0)
        dy_ref[...] = err * (1.0 / D_MODEL)
        part = 0.5 * jnp.sum(jnp.mean(err * err, axis=-1, keepdims=True), axis=0, keepdims=True)

        @pl.when(i == 0)
        def _():
            loss_ref[...] = jnp.zeros_like(loss_ref)

        loss_ref[...] += part

    blk = pl.BlockSpec((tm, D_MODEL), lambda i: (i, 0))
    return pl.pallas_call(
        kern, name="loss_head", grid=(length // tm,), in_specs=[blk, blk],
        out_specs=[pl.BlockSpec((8, LANES), lambda i: (0, 0)), blk],
        out_shape=[jax.ShapeDtypeStruct((8, LANES), F32), jax.ShapeDtypeStruct(y.shape, F32)],
        compiler_params=_params("arbitrary"),
    )(y, target)


def _make_loss(n_real):
    @jax.custom_vjp
    def loss(y, target):
        return _loss_call(y, target, n_real)[0][0, 0]

    def fwd(y, target):
        total, dy = _loss_call(y, target, n_real)
        return total[0, 0], dy

    def bwd(dy, ct):
        return dy * ct, jnp.zeros_like(dy)

    loss.defvjp(fwd, bwd)
    return loss


HBM_SPEC = pl.BlockSpec(memory_space=pl.ANY)
MESH_ID = pl.DeviceIdType.MESH


SC_MESH = dict(axis_name="sequencer", num_cores=1)
GATHER_ID, SLOT_ID = 1, 2


def _handshake(peers):
    barrier = pltpu.get_barrier_semaphore()
    for peer in peers:
        pl.semaphore_signal(barrier, inc=1, device_id=peer, device_id_type=MESH_ID)
    pl.semaphore_wait(barrier, len(peers))


def _exchange_call(body, name, ins, out_types, n_sems, sequencer_id):
    n_in, n_out = len(ins), len(out_types)
    sems = [pltpu.SemaphoreType.DMA((n_sems,)), pltpu.SemaphoreType.DMA((n_sems,)), pltpu.SemaphoreType.DMA((n_in,))]
    if sequencer_id is None:
        def on_core(*refs):
            body(lambda peers: None, refs[:n_in], refs[n_in:n_in + n_out], *refs[n_in + n_out:])

        return pl.pallas_call(on_core, name=name, out_shape=out_types, in_specs=[HBM_SPEC] * n_in,
                              out_specs=[HBM_SPEC] * n_out, scratch_shapes=sems)(*ins)

    def on_sequencer(*refs):
        body(_handshake, refs[:n_in], refs[n_in:n_in + n_out], *refs[n_in + n_out:])

    return pl.kernel(on_sequencer, name=name, out_type=out_types, mesh=plsc.ScalarSubcoreMesh(**SC_MESH),
                     scratch_types=sems, compiler_params=pltpu.CompilerParams(collective_id=sequencer_id))(*ins)


def _all_gather(shards, name, sequencer=False):
    n = len(shards)

    def body(handshake, x_refs, out_refs, send_sems, recv_sems, local_sems):
        x, y, c = lax.axis_index("x"), lax.axis_index("y"), lax.axis_index("c")
        me, sibling = (x, y, c), (x, y, 1 - c)
        chips = [(1 - x, y), (x, 1 - y), (1 - x, 1 - y)]
        handshake([sibling] + [(*chip, c) for chip in chips])

        def copy(b, k, block, to, from_input=False):
            px, py, pc = block
            slot = out_refs[b].at[4 * px + 2 * py + pc]
            return pltpu.make_async_remote_copy(
                src_ref=x_refs[b] if from_input else slot, dst_ref=slot,
                send_sem=send_sems.at[7 * b + k], recv_sem=recv_sems.at[7 * b + k], device_id=to, device_id_type=MESH_ID)

        mine = [pltpu.make_async_copy(x_refs[b], out_refs[b].at[4 * x + 2 * y + c], local_sems.at[b]) for b in range(n)]
        for cp in mine:
            cp.start()
        first = []
        for b in range(n):
            first.append(copy(b, 0, me, sibling, from_input=True))
            first += [copy(b, 1 + j, me, (*chip, c), from_input=True) for j, chip in enumerate(chips)]
        for cp in first:
            cp.start()
        passed = []
        for j, chip in enumerate(chips):
            for b in range(n):
                copy(b, 1 + j, (*chip, c), me).wait_recv()
                passed.append(copy(b, 4 + j, (*chip, c), sibling))
                passed[-1].start()
        for b in range(n):
            copy(b, 0, sibling, me).wait_recv()
            for j, chip in enumerate(chips):
                copy(b, 4 + j, (*chip, 1 - c), me).wait_recv()
        for cp in first + passed:
            cp.wait_send()
        for cp in mine:
            cp.wait()

    out_types = [jax.ShapeDtypeStruct((N_DEV, *s.shape), s.dtype) for s in shards]
    return _exchange_call(body, name, shards, out_types, 7 * n, GATHER_ID if sequencer else None)


def _slot_exchange(bufs, name, sequencer=False):
    def body(handshake, ins, outs, send_sems, recv_sems, local_sems):
        x, y, c = lax.axis_index("x"), lax.axis_index("y"), lax.axis_index("c")
        me = 4 * x + 2 * y + c
        flips = [(dx, dy, dc) for dx in (0, 1) for dy in (0, 1) for dc in (0, 1)][1:]
        peers = [(1 - x if dx else x, 1 - y if dy else y, 1 - c if dc else c) for dx, dy, dc in flips]
        handshake(peers)
        own = [pltpu.make_async_copy(src.at[me], dst.at[me], local_sems.at[b]) for b, (src, dst) in enumerate(zip(ins, outs))]
        copies = []
        for b, (src, dst) in enumerate(zip(ins, outs)):
            for k, (px, py, pc) in enumerate(peers):
                copies.append(pltpu.make_async_remote_copy(
                    src_ref=src.at[4 * px + 2 * py + pc], dst_ref=dst.at[me],
                    send_sem=send_sems.at[7 * b + k], recv_sem=recv_sems.at[7 * b + k],
                    device_id=(px, py, pc), device_id_type=MESH_ID))
        for cp in own + copies:
            cp.start()
        for cp in copies + own:
            cp.wait()

    out_types = [jax.ShapeDtypeStruct(b.shape, b.dtype) for b in bufs]
    return _exchange_call(body, name, bufs, out_types, 7 * len(bufs), SLOT_ID if sequencer else None)


def _slot_sum(slots, name):
    _, rows, cols = slots.shape
    tm = _divisor_tile(rows, 512, 16)

    def kern(s_ref, o_ref):
        total = s_ref[0].astype(F32)
        for d in range(1, N_DEV):
            total = total + s_ref[d].astype(F32)
        o_ref[...] = total

    return pl.pallas_call(
        kern, name=name, grid=(rows // tm,), in_specs=[pl.BlockSpec((N_DEV, tm, cols), lambda i: (0, i, 0))],
        out_specs=pl.BlockSpec((tm, cols), lambda i: (i, 0)), out_shape=jax.ShapeDtypeStruct((rows, cols), F32),
        compiler_params=_params("parallel"),
    )(slots)


def _reduce_scatter(bufs, tag, sequencer):
    arrived = _slot_exchange(bufs, f"rs_exchange_{tag}", sequencer)
    return [_slot_sum(a, f"rs_sum_{tag}{i}") for i, a in enumerate(arrived)]


def _adamw(w, g, m, v, name):
    rows, cols = w.shape
    tm = _divisor_tile(rows, max(8, (512 * 1024) // cols // 8 * 8), 8)

    def kern(w_ref, g_ref, m_ref, v_ref, d_ref, nm_ref, nv_ref):
        g = g_ref[...]
        m = ADAM_B1 * m_ref[...] + (1.0 - ADAM_B1) * g
        v = ADAM_B2 * v_ref[...] + (1.0 - ADAM_B2) * (g * g)
        m_hat = m / (1.0 - ADAM_B1 ** ADAM_STEP)
        v_hat = v / (1.0 - ADAM_B2 ** ADAM_STEP)
        d_ref[...] = -ADAM_LR * (m_hat / (jnp.sqrt(v_hat) + ADAM_EPS) + ADAM_WD * w_ref[...])
        nm_ref[...] = m
        nv_ref[...] = v

    blk = pl.BlockSpec((tm, cols), lambda i: (i, 0))
    out = jax.ShapeDtypeStruct(w.shape, F32)
    return pl.pallas_call(
        kern, name=name, grid=(rows // tm,), in_specs=[blk] * 4, out_specs=[blk] * 3, out_shape=[out] * 3,
        compiler_params=_params("parallel"),
    )(w, g, m, v)


def _adamw_layers(w, g_layers, m, v, name):
    depth, rows, cols = w.shape
    tm = _divisor_tile(rows, max(8, (512 * 1024) // cols // 8 * 8), 8)

    def kern(w_ref, m_ref, v_ref, *refs):
        g_refs, (g_out, d_ref, nm_ref, nv_ref) = refs[:depth], refs[depth:]
        layer = pl.program_id(0)
        g = g_refs[0][...]
        for l in range(1, depth):
            g = jnp.where(layer == l, g_refs[l][...], g)
        m = ADAM_B1 * m_ref[0] + (1.0 - ADAM_B1) * g
        v = ADAM_B2 * v_ref[0] + (1.0 - ADAM_B2) * (g * g)
        m_hat = m / (1.0 - ADAM_B1 ** ADAM_STEP)
        v_hat = v / (1.0 - ADAM_B2 ** ADAM_STEP)
        g_out[0] = g
        d_ref[0] = -ADAM_LR * (m_hat / (jnp.sqrt(v_hat) + ADAM_EPS) + ADAM_WD * w_ref[0])
        nm_ref[0] = m
        nv_ref[0] = v

    blk = pl.BlockSpec((1, tm, cols), lambda l, i: (l, i, 0))
    out = jax.ShapeDtypeStruct(w.shape, F32)
    return pl.pallas_call(
        kern, name=name, grid=(depth, rows // tm),
        in_specs=[blk] * 3 + [pl.BlockSpec((tm, cols), lambda l, i: (i, 0))] * depth,
        out_specs=[blk] * 4, out_shape=[out] * 4, compiler_params=_params("parallel", "parallel"),
    )(w, m, v, *g_layers)


TRANSPOSED = ('ffn1_w_gate', 'ffn1_w_up', 'ffn2_w_gate', 'ffn2_w_up', 'w_in', 'mla_w_uq', 'mla_w_ukv', 'mla_w_o',
              'conv_w_out', 's5_w_out')
PIECES = ((('ffn1_w_gate', 'ffn1_w_up', 'ffn1_w_down'),),
          (('w_in', 'w_o'), ('mla_w_o', 'conv_w_out', 's5_w_out', 's5_w_glu'), ('mla_w_uq',), ('mla_w_ukv',)),
          (('ffn2_w_gate', 'ffn2_w_up', 'ffn2_w_down'),))
PIN_CUTS = (PIN_CQ, PIN_CKV, PIN_KR1, PIN_KR2, PIN_XBAR, PIN_BG, PIN_CG, PIN_U, PIN_GATES, PIN_GATES + D_MODEL,
            PIN_GATES + 2 * D_MODEL, PIN_END, D_IN_PAD)
PIN_PIECES = ((PIN_CQ, Q_RANK), (PIN_CKV, KV_RANK), (PIN_KR1, HALF_ROPE), (PIN_KR2, HALF_ROPE), (PIN_XBAR, 4 * MIX),
              (PIN_GATES, 3 * D_MODEL))


def _make_split(cuts):
    @jax.custom_vjp
    def split(t):
        return tuple(t[:, a:b] for a, b in zip(cuts[:-1], cuts[1:]))

    def fwd(t):
        return split(t), None

    def bwd(_, cts):
        return (jnp.concatenate(cts, axis=1),)

    split.defvjp(fwd, bwd)
    return split


def _travel_shape(name, shape):
    return (shape[2], shape[1]) if name in TRANSPOSED else (shape[1], shape[2])


def _pack_groups(tensors, layer, groups, dtype):
    def view(n):
        t = tensors[n][layer]
        return (t.T if n in TRANSPOSED else t).astype(dtype)
    return [jnp.concatenate([view(n) for n in grp], axis=0) for grp in groups]


def _unpack_groups(bufs, groups, shard_shapes):
    out = {}
    for buf, grp in zip(bufs, groups):
        at = 0
        for n in grp:
            r, _ = _travel_shape(n, shard_shapes[n])
            out[n] = buf[..., at:at + r, :]
            at += r
    return out


def _pack_rows(arrays):
    flat = jnp.concatenate([a.reshape(-1) for a in arrays])
    rows = -(-flat.shape[0] // PACK_COLS)
    rows = -(-rows // 8) * 8
    return jnp.pad(flat, (0, rows * PACK_COLS - flat.shape[0])).reshape(rows, PACK_COLS)


def _unpack_rows(buf, shapes):
    flat = buf.reshape(-1)
    out, at = [], 0
    for s in shapes:
        n = int(np.prod(s))
        out.append(flat[at:at + n].reshape(s))
        at += n
    return out


def _full_weight(t, axis):
    if axis == 1:
        return jnp.moveaxis(t, 0, 1).reshape(t.shape[1], N_DEV * t.shape[2], t.shape[3])
    return jnp.moveaxis(t, 0, 2).reshape(t.shape[1], t.shape[2], N_DEV * t.shape[3])


def _disassemble(d, groups, dtype):
    heads = lambda t: t.reshape(HEADS, -1, t.shape[1])
    full = {}
    for tag in ('ffn1', 'ffn2'):
        if f'{tag}_gu' in d:
            full.update({f'{tag}_w_gate': d[f'{tag}_gu'][:D_FF], f'{tag}_w_up': d[f'{tag}_gu'][D_FF:],
                         f'{tag}_w_down': d[f'{tag}_down']})
    if 'w_in' in d:
        w_in, uq, ukv = d['w_in'], d['w_uq'], d['w_ukv']
        full.update(
            w_in=jnp.concatenate([w_in[a:a + n] for a, n in PIN_PIECES], axis=0),
            mla_w_uq=jnp.concatenate([heads(uq[:HEADS * NOPE]), heads(uq[HEADS * NOPE:HEADS * NOPE + LANES]),
                                      heads(uq[HEADS * NOPE + LANES:])], axis=1).reshape(HEADS * QK_DIM, Q_RANK),
            mla_w_ukv=jnp.concatenate([heads(ukv[:HEADS * NOPE]), heads(ukv[HEADS * NOPE:])],
                                      axis=1).reshape(HEADS * (NOPE + V_DIM), KV_RANK),
            mla_w_o=d['mla_w_o'], conv_w_out=d['conv_w_out'], s5_w_glu=d['s5_w_glu'], s5_w_out=d['s5_w_out'], w_o=d['w_o'])
    return [jnp.concatenate([full[n].reshape(N_DEV, -1, full[n].shape[-1]).astype(dtype) for n in grp], axis=1)
            for grp in groups]


def _assemble(gathered, groups, shard_shapes):
    full = {n: t.reshape(N_DEV * t.shape[1], t.shape[2])
            for n, t in _unpack_groups(gathered, groups, shard_shapes).items()}
    out = {}
    for tag in ('ffn1', 'ffn2'):
        if f'{tag}_w_gate' in full:
            out[f'{tag}_gu'] = jnp.concatenate([full[f'{tag}_w_gate'], full[f'{tag}_w_up']], axis=0)
            out[f'{tag}_down'] = full[f'{tag}_w_down']
    if 'w_in' not in full:
        return out
    w_in = full['w_in']
    cuts = np.cumsum((0,) + IN_SPLITS)
    cq, ckv, kr, xbar, bg, cg, u, gates = [w_in[a:b] for a, b in zip(cuts[:-1], cuts[1:])]
    pad = lambda t, n: jnp.pad(t, ((0, n - t.shape[0]), (0, 0)))
    w_in_packed = jnp.concatenate(
        [cq, ckv, pad(kr[:HALF_ROPE], LANES), pad(kr[HALF_ROPE:], LANES), xbar, bg, cg, u, gates,
         jnp.zeros((D_IN_PAD - PIN_END, D_MODEL), w_in.dtype)], axis=0)
    uq = full['mla_w_uq'].reshape(HEADS, QK_DIM, Q_RANK)
    w_uq = jnp.concatenate([uq[:, :NOPE].reshape(HEADS * NOPE, Q_RANK),
                            uq[:, NOPE:NOPE + HALF_ROPE].reshape(HEADS * HALF_ROPE, Q_RANK),
                            uq[:, NOPE + HALF_ROPE:].reshape(HEADS * HALF_ROPE, Q_RANK)], axis=0)
    ukv = full['mla_w_ukv'].reshape(HEADS, NOPE + V_DIM, KV_RANK)
    w_ukv = jnp.concatenate([ukv[:, :NOPE].reshape(HEADS * NOPE, KV_RANK),
                             ukv[:, NOPE:].reshape(HEADS * V_DIM, KV_RANK)], axis=0)
    out.update(w_in=w_in_packed, w_uq=w_uq, w_ukv=w_ukv, mla_w_o=full['mla_w_o'], conv_w_out=full['conv_w_out'],
               s5_w_glu=full['s5_w_glu'], s5_w_out=full['s5_w_out'], w_o=full['w_o'])
    return out


def _s5_discretize(a_re, a_im, log_dt, b_re, b_im, c_re, c_im):
    dt = jnp.exp(log_dt)[:, None]
    mag = jnp.exp(dt * a_re)
    ab_re, ab_im = mag * jnp.cos(dt * a_im), mag * jnp.sin(dt * a_im)
    den = a_re * a_re + a_im * a_im
    nr, ni = ab_re - 1.0, ab_im
    coef_re = (nr * a_re + ni * a_im) / den
    coef_im = (ni * a_re - nr * a_im) / den
    bb_re = coef_re[..., None] * b_re - coef_im[..., None] * b_im
    bb_im = coef_re[..., None] * b_im + coef_im[..., None] * b_re
    same_group = jnp.eye(S5_GROUPS, dtype=F32)[:, None, :, None]
    spread = lambda t: t.transpose(0, 2, 1)[:, :, None, :] * same_group
    spread_b = lambda bb: spread(bb).reshape(MIX, S5_CH)
    spread_c = lambda cc: spread(cc).reshape(S5_CH, MIX)
    b_map = jnp.concatenate([spread_b(bb_re), spread_b(bb_im)], axis=1)
    c_map = jnp.concatenate([spread_c(c_re), -spread_c(c_im)], axis=0)
    return ab_re.reshape(SCAN_ROWS, LANES), ab_im.reshape(SCAN_ROWS, LANES), b_map, c_map


def _rope_tables(length):
    inv_freq = ROPE_BASE ** (-jnp.arange(0, ROPE, 2, dtype=F32) / ROPE)
    ang = jnp.arange(length).astype(F32)[:, None] * inv_freq[None, :]
    return jnp.tile(jnp.cos(ang), (1, LANES // HALF_ROPE)), jnp.tile(jnp.sin(ang), (1, LANES // HALF_ROPE))


def _heads_first(t):
    return t.reshape(t.shape[0], HEADS, -1).transpose(1, 0, 2)


def _local_loss(diff, big, n_real):
    small, wz = diff['small'], diff['wz']
    h = diff['h0']
    length = h.shape[0]
    cos, sin = _rope_tables(length)
    ln_half = lambda tag: _make_rowwise(_f_ln_half, 2, 2, (F32,), tag)
    row2 = lambda v: v.reshape(1, -1)
    for l in range(DEPTH):
        w, z = big[l], wz[l]
        p = {k: small[k][l] for k in small if k != 'meta'}

        def ffn(h, gu, down, g, b, tag):
            act = _make_swiglu(f"{tag}_swiglu")(_make_mm(f"{tag}_gu", wt=True)(h, gu[0], gu[1]))
            f = _make_mm(f"{tag}_down")(act, down[0], down[1])
            return ln_half(f"{tag}_ln")(h, f, row2(g), row2(b))[0]

        h = ffn(h, (w['ffn1_gu'], z['ffn1_gu']), (w['ffn1_down'], z['ffn1_down']), p['ln1_g'], p['ln1_b'], "ffn1")
        cq, ckv, kr1, kr2, xbar, bg, cg, u, gate_a, gate_b, gate_c, _ = _make_split(PIN_CUTS)(
            _make_mm("w_in", wt=True)(h, w['w_in'], z['w_in']))
        qn, = _make_rowwise(_f_rms, 1, 1, (BF16,), "q_rms")(cq, row2(p['mla_q_norm_g']))
        kvn, = _make_rowwise(_f_rms, 1, 1, (BF16,), "kv_rms")(ckv, row2(p['mla_kv_norm_g']))
        q_nope, q1, q2 = _make_split((0, HEADS * NOPE, HEADS * NOPE + LANES, HEADS * NOPE + 2 * LANES))(
            _make_mm("w_uq", wt=True)(qn, w['w_uq'], z['w_uq']))
        k_nope, val = _make_split((0, HEADS * NOPE, HEADS * (NOPE + V_DIM)))(
            _make_mm("w_ukv", wt=True)(kvn, w['w_ukv'], z['w_ukv']))
        rope = _make_rowwise(_f_rope, 4, 0, (BF16, BF16), "rope", n_nodiff=2)
        q1, q2 = rope(q1, q2, cos, sin)
        k1, k2 = rope(kr1, kr2, cos, sin)
        hpad = jnp.zeros((HEADS, length, HEAD_PAD - QK_DIM), BF16)
        q3 = jnp.concatenate([_heads_first(q_nope.astype(BF16)), _heads_first(q1), _heads_first(q2), hpad], -1)
        shared = lambda t: jnp.broadcast_to(t[None, :, :HALF_ROPE], (HEADS, length, HALF_ROPE))
        k3 = jnp.concatenate([_heads_first(k_nope.astype(BF16)), shared(k1), shared(k2), hpad], -1)
        v3 = _heads_first(val.astype(BF16))
        o3 = _attention(q3, k3, v3)
        y_a = _make_mm("mla_w_o", wt=True)(o3.transpose(1, 0, 2).reshape(length, MIX), w['mla_w_o'], z['mla_w_o'])
        conv = _short_conv(xbar, bg, cg, p['conv_w_full'], row2(p['conv_b']))
        y_b = _make_mm("conv_w_out", wt=True)(conv, w['conv_w_out'], z['conv_w_out'])
        ar, ai, b_map, c_map = _s5_discretize(p['s5_a_re'], p['s5_a_im'], p['s5_log_dt'], p['s5_b_re'], p['s5_b_im'],
                                              p['s5_c_re'], p['s5_c_im'])
        bu = _make_mm_f32w("s5_b")(u, b_map)
        states = _s5_scan(ar, ai, bu.reshape(length, 2 * SCAN_ROWS, LANES)).reshape(length, 2 * S5_CH)
        y_ssm = _make_mm_f32w("s5_c")(states, c_map)
        zed, = _make_rowwise(_f_gelu_skip, 2, 1, (F32,), "s5_gelu")(y_ssm, u, row2(p['s5_d']))
        t = _make_mm("s5_w_glu")(zed, w['s5_w_glu'], z['s5_w_glu'])
        glu, = _make_rowwise(_f_glu, 2, 1, (BF16,), "s5_glu")(zed, t, row2(p['s5_b_glu']))
        y_c = _make_mm("s5_w_out", wt=True)(glu, w['s5_w_out'], z['s5_w_out'])
        mixed, = _make_rowwise(_f_merge, 6, 0, (BF16,), "merge")(gate_a, gate_b, gate_c, y_a, y_b, y_c)
        mix_out = _make_mm("w_o")(mixed, w['w_o'], z['w_o'])
        h, = _make_rowwise(_f_ln_full, 2, 2, (F32,), "mix_ln")(h, mix_out, row2(p['ln2_g']), row2(p['ln2_b']))
        h = ffn(h, (w['ffn2_gu'], z['ffn2_gu']), (w['ffn2_down'], z['ffn2_down']), p['ln3_g'], p['ln3_b'], "ffn2")
    return _make_loss(n_real)(h, diff['target'])


def kernel(x, meta, ffn1_w_gate, ffn1_w_up, ffn1_w_down, ln1_g, ln1_b, w_in, mla_q_norm_g, mla_w_uq, mla_kv_norm_g, mla_w_ukv, mla_w_o, conv_w, conv_b, conv_w_out, s5_a_re, s5_a_im, s5_log_dt, s5_b_re, s5_b_im, s5_c_re, s5_c_im, s5_d, s5_w_glu, s5_b_glu, s5_w_out, w_o, ln2_g, ln2_b, ffn2_w_gate, ffn2_w_up, ffn2_w_down, ln3_g, ln3_b, loss_target, m_meta, m_ffn1_w_gate, m_ffn1_w_up, m_ffn1_w_down, m_ln1_g, m_ln1_b, m_w_in, m_mla_q_norm_g, m_mla_w_uq, m_mla_kv_norm_g, m_mla_w_ukv, m_mla_w_o, m_conv_w, m_conv_b, m_conv_w_out, m_s5_a_re, m_s5_a_im, m_s5_log_dt, m_s5_b_re, m_s5_b_im, m_s5_c_re, m_s5_c_im, m_s5_d, m_s5_w_glu, m_s5_b_glu, m_s5_w_out, m_w_o, m_ln2_g, m_ln2_b, m_ffn2_w_gate, m_ffn2_w_up, m_ffn2_w_down, m_ln3_g, m_ln3_b, v_meta, v_ffn1_w_gate, v_ffn1_w_up, v_ffn1_w_down, v_ln1_g, v_ln1_b, v_w_in, v_mla_q_norm_g, v_mla_w_uq, v_mla_kv_norm_g, v_mla_w_ukv, v_mla_w_o, v_conv_w, v_conv_b, v_conv_w_out, v_s5_a_re, v_s5_a_im, v_s5_log_dt, v_s5_b_re, v_s5_b_im, v_s5_c_re, v_s5_c_im, v_s5_d, v_s5_w_glu, v_s5_b_glu, v_s5_w_out, v_w_o, v_ln2_g, v_ln2_b, v_ffn2_w_gate, v_ffn2_w_up, v_ffn2_w_down, v_ln3_g, v_ln3_b):
    args = locals()
    w = {n: args[n] for n in WEIGHT_NAMES}
    m = {n: args["m_" + n] for n in WEIGHT_NAMES}
    v = {n: args["v_" + n] for n in WEIGHT_NAMES}
    me = 4 * lax.axis_index("x") + 2 * lax.axis_index("y") + lax.axis_index("c")
    seq = x.shape[1]
    n_real = N_META + seq
    length = -(-n_real // LANES) * LANES

    shard_shapes = {n: w[n].shape for n in BIG}
    big = [{} for _ in range(DEPTH)]
    for l in range(DEPTH):
        for p, groups in enumerate(PIECES):
            gathered = _all_gather(_pack_groups(w, l, groups, BF16), f"gather_weights_layer{l}_piece{p}", sequencer=True)
            big[l].update(_assemble(gathered, groups, shard_shapes))
    small_shards = _all_gather([_pack_rows([w[n] for n in SMALL_SHARDED])], "gather_small")[0].reshape(N_DEV, -1)
    meta_full = _full_weight(small_shards[:, :meta.size].reshape(N_DEV, 1, *meta.shape), 2)[0]
    conv_w_full = _full_weight(small_shards[:, meta.size:meta.size + conv_w.size].reshape(N_DEV, *conv_w.shape), 2)

    small = {n: w[n] for n in SMALL_NAMES if n not in SMALL_SHARDED}
    small['conv_w_full'] = conv_w_full
    small['meta'] = meta_full
    wz = jax.tree.map(lambda t: jnp.zeros(t.shape, F32), big)
    pad_rows = length - n_real

    def loss_fn(diff):
        h0 = jnp.concatenate([diff['small']['meta'], diff['x'], jnp.zeros((pad_rows, D_MODEL), F32)], axis=0)
        target = jnp.pad(loss_target[0], ((N_META, pad_rows), (0, 0)))
        return _local_loss(dict(h0=h0, small=diff['small'], wz=diff['wz'], target=target), big, n_real)

    loss_local, grads = jax.value_and_grad(loss_fn)(dict(x=x[0], small=small, wz=wz))
    loss = lax.psum(loss_local, AXES)

    small_names = [n for n in SMALL_NAMES if n not in SMALL_SHARDED] + ['conv_w_full', 'meta']
    small_flat = _pack_rows([grads['small'][n] for n in small_names])
    rows_each = -(-small_flat.shape[0] // (8 * N_DEV)) * 8
    small_flat = jnp.pad(small_flat, ((0, rows_each * N_DEV - small_flat.shape[0]), (0, 0)))
    layer_sums = [{} for _ in range(DEPTH)]
    for l in reversed(range(DEPTH)):
        for p, groups in reversed(list(enumerate(PIECES))):
            last = l == 0 and p == 0
            extra = [small_flat.reshape(N_DEV, rows_each, PACK_COLS)] if last else []
            sums = _reduce_scatter(_disassemble(grads['wz'][l], groups, BF16) + extra, f"grads_layer{l}_piece{p}_",
                                   sequencer=True)
            layer_sums[l].update(_unpack_groups(sums[:len(groups)], groups, shard_shapes))
            if last:
                small_sum = sums[-1]
    small_all, = _all_gather([small_sum], "gather_small_grads")
    small_grads = dict(zip(small_names, _unpack_rows(small_all, [grads['small'][n].shape for n in small_names])))
    g = {}
    for name in SMALL_NAMES:
        if name == 'meta':
            g[name] = lax.dynamic_slice_in_dim(small_grads['meta'], me * meta.shape[1], meta.shape[1], axis=1)
        elif name == 'conv_w':
            g[name] = lax.dynamic_slice_in_dim(small_grads['conv_w_full'], me * conv_w.shape[2], conv_w.shape[2], axis=2)
        else:
            g[name] = small_grads[name]

    delta, new_m, new_v = {}, {}, {}
    for name in BIG:
        per_layer = [s[name].T if name in TRANSPOSED else s[name] for s in layer_sums]
        g[name], delta[name], new_m[name], new_v[name] = _adamw_layers(w[name], per_layer, m[name], v[name], f"adamw_{name}")
    shapes = [w[n].shape for n in SMALL_NAMES]
    d, nm, nv = _adamw(*[_pack_rows([t[n] for n in SMALL_NAMES]) for t in (w, g, m, v)], "adamw_small")
    for out, buf in ((delta, d), (new_m, nm), (new_v, nv)):
        out.update(zip(SMALL_NAMES, _unpack_rows(buf, shapes)))

    return (loss, grads['x'][None], *[g[n] for n in WEIGHT_NAMES], *[delta[n] for n in WEIGHT_NAMES],
            *[new_m[n] for n in WEIGHT_NAMES], *[new_v[n] for n in WEIGHT_NAMES])
```

```python
import functools
import math

import jax
import jax.numpy as jnp
import numpy as np
from jax import lax
from jax.experimental import pallas as pl
from jax.experimental.pallas import tpu as pltpu
from jax.experimental.pallas import tpu_sc as plsc

F32 = jnp.float32
BF16 = jnp.bfloat16

D_MODEL = 1024
DEPTH = 2
N_META = 16
HEADS = 8
V_DIM = 64
NOPE = 64
ROPE = 32
HALF_ROPE = ROPE // 2
QK_DIM = NOPE + ROPE
Q_RANK = 384
KV_RANK = 256
MIX = 512
CONV_K = 3
S5_GROUPS = 32
S5_GROUP = 16
S5_STATE = 64
S5_CH = S5_GROUPS * S5_STATE
D_FF = 2816
ALPHA = (2.0 * DEPTH) ** 0.25
LN_EPS = 1e-5
RMS_EPS = 1e-6
ROPE_BASE = 10000.0
IN_SPLITS = (Q_RANK, KV_RANK, ROPE, MIX, MIX, MIX, MIX, 3 * D_MODEL)
D_IN = sum(IN_SPLITS)
ADAM_LR, ADAM_B1, ADAM_B2, ADAM_EPS, ADAM_WD, ADAM_STEP = 0.001, 0.9, 0.999, 1e-08, 0.01, 10

N_DEV = 8
AXES = ("x", "y", "c")
LANES = 128
PACK_COLS = 1024
HEAD_PAD = 128
VMEM_LIMIT = 48 * 1024 * 1024

PIN_CQ, PIN_CKV, PIN_KR1, PIN_KR2, PIN_XBAR, PIN_BG, PIN_CG, PIN_U, PIN_GATES, PIN_END = (
    0, 384, 640, 768, 896, 1408, 1920, 2432, 2944, 6016)
D_IN_PAD = 6144

WEIGHT_NAMES = ['meta', 'ffn1_w_gate', 'ffn1_w_up', 'ffn1_w_down', 'ln1_g', 'ln1_b', 'w_in', 'mla_q_norm_g', 'mla_w_uq',
                'mla_kv_norm_g', 'mla_w_ukv', 'mla_w_o', 'conv_w', 'conv_b', 'conv_w_out', 's5_a_re', 's5_a_im',
                's5_log_dt', 's5_b_re', 's5_b_im', 's5_c_re', 's5_c_im', 's5_d', 's5_w_glu', 's5_b_glu', 's5_w_out',
                'w_o', 'ln2_g', 'ln2_b', 'ffn2_w_gate', 'ffn2_w_up', 'ffn2_w_down', 'ln3_g', 'ln3_b']
BIG = {'ffn1_w_gate': 2, 'ffn1_w_up': 2, 'ffn1_w_down': 1, 'w_in': 2, 'mla_w_uq': 2, 'mla_w_ukv': 2, 'mla_w_o': 2,
       'conv_w_out': 2, 's5_w_glu': 1, 's5_w_out': 2, 'w_o': 1, 'ffn2_w_gate': 2, 'ffn2_w_up': 2, 'ffn2_w_down': 1}
SMALL_SHARDED = ('meta', 'conv_w')
SMALL_NAMES = [n for n in WEIGHT_NAMES if n not in BIG]


def _divisor_tile(n, limit, mult):
    best = None
    for t in range(mult, min(n, limit) + 1, mult):
        if n % t == 0:
            best = t
    return best if best is not None else n


def _params(*sem):
    return pltpu.CompilerParams(dimension_semantics=sem, vmem_limit_bytes=VMEM_LIMIT)


def _matmul(a, b, *, ta=False, tb=False, out_dtype=F32, add=None, name):
    m, k = (a.shape[1], a.shape[0]) if ta else a.shape
    n = b.shape[0] if tb else b.shape[1]
    assert (b.shape[1] if tb else b.shape[0]) == k, (a.shape, b.shape, ta, tb)
    tm = _divisor_tile(m, 1408, LANES) if ta else _divisor_tile(m, 2176 if a.dtype == BF16 else 1088, 16)
    tn = _divisor_tile(n, 512, LANES)
    tk = _divisor_tile(k, 1408, 16 if ta else LANES)
    nk = k // tk
    dims = (((0 if ta else 1,), (1 if tb else 0,)), ((), ()))

    in_place = jnp.dtype(out_dtype) == jnp.dtype(F32)

    def kern(a_ref, b_ref, *rest):
        add_ref = rest[0] if add is not None else None
        o_ref, *scratch = rest[1:] if add is not None else rest
        kk = pl.program_id(2)
        part = lax.dot_general(a_ref[...].astype(BF16), b_ref[...].astype(BF16), dims, preferred_element_type=F32)
        first = lambda: part if add_ref is None else part + add_ref[...].astype(F32)
        if nk == 1:
            o_ref[...] = first().astype(o_ref.dtype)
            return
        acc_ref = o_ref if in_place else scratch[0]

        @pl.when(kk == 0)
        def _():
            acc_ref[...] = first()

        @pl.when(kk > 0)
        def _():
            acc_ref[...] += part

        if not in_place:
            @pl.when(kk == nk - 1)
            def _():
                o_ref[...] = acc_ref[...].astype(o_ref.dtype)

    a_spec = pl.BlockSpec((tk, tm), lambda i, j, kk: (kk, i)) if ta else pl.BlockSpec((tm, tk), lambda i, j, kk: (i, kk))
    b_spec = pl.BlockSpec((tn, tk), lambda i, j, kk: (j, kk)) if tb else pl.BlockSpec((tk, tn), lambda i, j, kk: (kk, j))
    o_spec = pl.BlockSpec((tm, tn), lambda i, j, kk: (i, j))
    return pl.pallas_call(
        kern, name=name, grid=(m // tm, n // tn, nk),
        in_specs=[a_spec, b_spec] + ([o_spec] if add is not None else []), out_specs=o_spec,
        out_shape=jax.ShapeDtypeStruct((m, n), out_dtype),
        scratch_shapes=[] if (nk == 1 or in_place) else [pltpu.VMEM((tm, tn), F32)],
        compiler_params=_params("parallel", "parallel", "arbitrary"),
    )(a, b, *([add] if add is not None else []))


def _make_mm(name, wt=False):
    @jax.custom_vjp
    def mm(x, w, wz):
        return _matmul(x, w, tb=wt, name=name + "_fwd")

    def fwd(x, w, wz):
        return _matmul(x, w, tb=wt, name=name + "_fwd"), (x, w)

    def bwd(res, dy):
        x, w = res
        wz_dtype = BF16
        dx = _matmul(dy, w, tb=not wt, out_dtype=x.dtype, name=name + "_dx")
        dw = (_matmul(dy, x, ta=True, out_dtype=wz_dtype, name=name + "_dw") if wt
              else _matmul(x, dy, ta=True, out_dtype=wz_dtype, name=name + "_dw"))
        return dx, jnp.zeros_like(w), dw

    mm.defvjp(fwd, bwd)
    return mm


def _make_mm_f32w(name):
    @jax.custom_vjp
    def mm(x, w):
        return _matmul(x, w, name=name + "_fwd")

    def fwd(x, w):
        return _matmul(x, w, name=name + "_fwd"), (x, w)

    def bwd(res, dy):
        x, w = res
        return (_matmul(dy, w, tb=True, out_dtype=x.dtype, name=name + "_dx"),
                _matmul(x, dy, ta=True, name=name + "_dw"))

    mm.defvjp(fwd, bwd)
    return mm


def _row_tile(rows, widths):
    limit = max(16, (6 * 1024 * 1024 // 4) // max(1, sum(widths)))
    return _divisor_tile(rows, limit, 16)


def _make_rowwise(f, n_rows, n_pars, out_dtypes, name, n_nodiff=0, grad_dtypes=None):
    n_out = len(out_dtypes)
    n_diff = n_rows - n_nodiff

    def run_fwd(rows, pars):
        length = rows[0].shape[0]
        shapes = jax.eval_shape(lambda *a: f(*a), *[jax.ShapeDtypeStruct((16, r.shape[1]), F32) for r in rows],
                                *[jax.ShapeDtypeStruct(p.shape, F32) for p in pars])
        widths = [s.shape[1] for s in shapes]
        tm = _row_tile(length, [r.shape[1] for r in rows] + widths)

        def kern(*refs):
            ins = [r[...].astype(F32) for r in refs[:n_rows + n_pars]]
            outs = f(*ins)
            for o_ref, o in zip(refs[n_rows + n_pars:], outs):
                o_ref[...] = o.astype(o_ref.dtype)

        return pl.pallas_call(
            kern, name=name + "_fwd", grid=(length // tm,),
            in_specs=[pl.BlockSpec((tm, r.shape[1]), lambda i: (i, 0)) for r in rows]
            + [pl.BlockSpec(p.shape, lambda i: (0, 0)) for p in pars],
            out_specs=[pl.BlockSpec((tm, w), lambda i: (i, 0)) for w in widths],
            out_shape=[jax.ShapeDtypeStruct((length, w), dt) for w, dt in zip(widths, out_dtypes)],
            compiler_params=_params("parallel"),
        )(*rows, *pars)

    def run_bwd(rows, pars, cts):
        length = rows[0].shape[0]
        tm = _row_tile(length, [r.shape[1] for r in rows] * 2 + [c.shape[1] for c in cts] * 2)

        def kern(*refs):
            ins = [r[...].astype(F32) for r in refs[:n_rows + n_pars]]
            ct = [r[...].astype(F32) for r in refs[n_rows + n_pars:n_rows + n_pars + n_out]]
            out_refs = refs[n_rows + n_pars + n_out:]
            nodiff = ins[n_diff:n_rows]
            _, vjp = jax.vjp(lambda *a: f(*a[:n_diff], *nodiff, *a[n_diff:]), *ins[:n_diff], *ins[n_rows:])
            grads = vjp(tuple(ct))
            for o_ref, g in zip(out_refs[:n_diff], grads[:n_diff]):
                o_ref[...] = g.astype(o_ref.dtype)
            first = pl.program_id(0) == 0
            for o_ref, g in zip(out_refs[n_diff:], grads[n_diff:]):
                @pl.when(first)
                def _(o_ref=o_ref, g=g):
                    o_ref[...] = g

                @pl.when(jnp.logical_not(first))
                def _(o_ref=o_ref, g=g):
                    o_ref[...] += g

        return pl.pallas_call(
            kern, name=name + "_bwd", grid=(length // tm,),
            in_specs=[pl.BlockSpec((tm, r.shape[1]), lambda i: (i, 0)) for r in rows]
            + [pl.BlockSpec(p.shape, lambda i: (0, 0)) for p in pars]
            + [pl.BlockSpec((tm, c.shape[1]), lambda i: (i, 0)) for c in cts],
            out_specs=[pl.BlockSpec((tm, r.shape[1]), lambda i: (i, 0)) for r in rows[:n_diff]]
            + [pl.BlockSpec(p.shape, lambda i: (0, 0)) for p in pars],
            out_shape=[jax.ShapeDtypeStruct(r.shape, r.dtype if grad_dtypes is None else grad_dtypes[i])
                       for i, r in enumerate(rows[:n_diff])]
            + [jax.ShapeDtypeStruct(p.shape, F32) for p in pars],
            compiler_params=_params("arbitrary"),
        )(*rows, *pars, *cts)

    @jax.custom_vjp
    def op(*args):
        return tuple(run_fwd(args[:n_rows], args[n_rows:]))

    def fwd(*args):
        return tuple(run_fwd(args[:n_rows], args[n_rows:])), args

    def bwd(args, cts):
        grads = run_bwd(args[:n_rows], args[n_rows:], cts)
        zeros = [jnp.zeros_like(r) for r in args[n_diff:n_rows]]
        return (*grads[:n_diff], *zeros, *grads[n_diff:])

    op.defvjp(fwd, bwd)
    op.run_fwd, op.run_bwd = run_fwd, run_bwd
    return op


def _layer_norm(z, g, b):
    mu = jnp.mean(z, axis=-1, keepdims=True)
    d = z - mu
    var = jnp.mean(d * d, axis=-1, keepdims=True)
    return d * lax.rsqrt(var + LN_EPS) * g + b


def _f_ln_half(h, f, g, b):
    return (_layer_norm(ALPHA * h + 0.5 * f, g, b),)


def _f_ln_full(h, f, g, b):
    return (_layer_norm(ALPHA * h + f, g, b),)


def _f_rms(x, g):
    return (x * lax.rsqrt(jnp.mean(x * x, axis=-1, keepdims=True) + RMS_EPS) * g,)


def _f_rope(x1, x2, cos, sin):
    return x1 * cos - x2 * sin, x2 * cos + x1 * sin


def _f_gelu_skip(y, u, d):
    return (jax.nn.gelu(y + d * u),)


def _f_glu(z, t, b):
    return (z * jax.nn.sigmoid(t + b),)


def _f_merge(ga, gb, gc, ya, yb, yc):
    return (jax.nn.sigmoid(ga) * ya + jax.nn.sigmoid(gb) * yb + jax.nn.sigmoid(gc) * yc,)


def _swiglu_fwd(gu, name):
    length = gu.shape[0]
    tm = _row_tile(length, [3 * D_FF])

    def kern(g_ref, u_ref, o_ref):
        o_ref[...] = (jax.nn.silu(g_ref[...]) * u_ref[...]).astype(o_ref.dtype)

    half = lambda j: pl.BlockSpec((tm, D_FF), lambda i: (i, j))
    return pl.pallas_call(
        kern, name=name, grid=(length // tm,), in_specs=[half(0), half(1)], out_specs=half(0),
        out_shape=jax.ShapeDtypeStruct((length, D_FF), BF16), compiler_params=_params("parallel"),
    )(gu, gu)


def _swiglu_bwd(gu, dact, name):
    length = gu.shape[0]
    tm = _row_tile(length, [4 * D_FF])

    def kern(g_ref, u_ref, d_ref, o_ref):
        _, vjp = jax.vjp(lambda g, u: jax.nn.silu(g) * u, g_ref[...], u_ref[...])
        dg, du = vjp(d_ref[...].astype(F32))
        o_ref[:, :D_FF] = dg.astype(o_ref.dtype)
        o_ref[:, D_FF:] = du.astype(o_ref.dtype)

    half = lambda j: pl.BlockSpec((tm, D_FF), lambda i: (i, j))
    return pl.pallas_call(
        kern, name=name, grid=(length // tm,), in_specs=[half(0), half(1), half(0)],
        out_specs=pl.BlockSpec((tm, 2 * D_FF), lambda i: (i, 0)),
        out_shape=jax.ShapeDtypeStruct(gu.shape, BF16), compiler_params=_params("parallel"),
    )(gu, gu, dact)


def _f_cast(x):
    return (x,)


def _make_ffn(tag):
    ln = _make_rowwise(_f_ln_half, 2, 2, (F32,), f"{tag}_ln", grad_dtypes=(F32, BF16))
    cast = _make_rowwise(_f_cast, 1, 0, (BF16,), f"{tag}_cast")

    def forward(h, w_gu, w_down, g, b):
        hb, = cast.run_fwd((h,), ())
        gu = _matmul(hb, w_gu, tb=True, name=f"{tag}_gu_fwd")
        act = _swiglu_fwd(gu, f"{tag}_swiglu_fwd")
        f = _matmul(act, w_down, name=f"{tag}_down_fwd")
        y, = ln.run_fwd((h, f), (g, b))
        return y, (h, hb, gu, act, f, w_gu, w_down, g, b)

    @jax.custom_vjp
    def block(h, w_gu, w_down, z_gu, z_down, g, b):
        return forward(h, w_gu, w_down, g, b)[0]

    def fwd(h, w_gu, w_down, z_gu, z_down, g, b):
        return forward(h, w_gu, w_down, g, b)

    def bwd(res, dy):
        h, hb, gu, act, f, w_gu, w_down, g, b = res
        dh_skip, df, dg, db = ln.run_bwd((h, f), (g, b), (dy,))
        dact = _matmul(df, w_down, tb=True, out_dtype=BF16, name=f"{tag}_down_dx")
        dw_down = _matmul(act, df, ta=True, out_dtype=BF16, name=f"{tag}_down_dw")
        dgu = _swiglu_bwd(gu, dact, f"{tag}_swiglu_bwd")
        dw_gu = _matmul(dgu, hb, ta=True, out_dtype=BF16, name=f"{tag}_gu_dw")
        dh = _matmul(dgu, w_gu, add=dh_skip, name=f"{tag}_gu_dx")
        return dh, jnp.zeros_like(w_gu), jnp.zeros_like(w_down), dw_gu, dw_down, dg, db

    block.defvjp(fwd, bwd)
    return block


def _attn_scores(q, k, q_block, tq):
    length = k.shape[0]
    s = lax.dot_general(q, k, (((1,), (1,)), ((), ())), preferred_element_type=F32) * (QK_DIM ** -0.5)
    row = q_block * tq + lax.broadcasted_iota(jnp.int32, (tq, length), 0)
    col = lax.broadcasted_iota(jnp.int32, (tq, length), 1)
    s = jnp.where(col <= row, s, -1e30)
    e = jnp.exp(s - jnp.max(s, axis=1, keepdims=True))
    return e * (1.0 / jnp.sum(e, axis=1, keepdims=True))


ATTN_SEGMENTS = 4


def _attn_tiles(length):
    seg = length // ATTN_SEGMENTS
    return seg, _divisor_tile(seg, 272, 16)


def _attn_fwd(q3, k3, v3):
    heads, length, _ = q3.shape
    seg, tq = _attn_tiles(length)
    outs = []
    for s in range(ATTN_SEGMENTS):
        kmax, base = (s + 1) * seg, s * (seg // tq)

        def kern(q_ref, k_ref, v_ref, o_ref, base=base):
            p = _attn_scores(q_ref[0], k_ref[0], base + pl.program_id(1), tq)
            o_ref[0] = jnp.dot(p.astype(BF16), v_ref[0], preferred_element_type=F32).astype(o_ref.dtype)

        outs.append(pl.pallas_call(
            kern, name=f"attn_fwd_seg{s}", grid=(heads, seg // tq),
            in_specs=[pl.BlockSpec((1, tq, HEAD_PAD), lambda h, i, base=base: (h, base + i, 0)),
                      pl.BlockSpec((1, kmax, HEAD_PAD), lambda h, i: (h, 0, 0)),
                      pl.BlockSpec((1, kmax, V_DIM), lambda h, i: (h, 0, 0))],
            out_specs=pl.BlockSpec((1, tq, V_DIM), lambda h, i: (h, i, 0)),
            out_shape=jax.ShapeDtypeStruct((heads, seg, V_DIM), F32),
            compiler_params=_params("parallel", "parallel"),
        )(q3, k3, v3))
    return jnp.concatenate(outs, axis=1)


def _attn_bwd(q3, k3, v3, do3):
    heads, length, _ = q3.shape
    seg, tq = _attn_tiles(length)
    dk = jnp.zeros((heads, length, HEAD_PAD), F32)
    dv = jnp.zeros((heads, length, V_DIM), F32)
    dqs = [None] * ATTN_SEGMENTS
    for s in reversed(range(ATTN_SEGMENTS)):
        kmax, base = (s + 1) * seg, s * (seg // tq)

        def kern(q_ref, k_ref, v_ref, do_ref, dk_in, dv_in, dq_ref, dk_ref, dv_ref, base=base):
            i = pl.program_id(1)
            q, k, v, do = q_ref[0], k_ref[0], v_ref[0], do_ref[0].astype(BF16)
            p = _attn_scores(q, k, base + i, tq)
            dp = lax.dot_general(do, v, (((1,), (1,)), ((), ())), preferred_element_type=F32)
            ds = (p * (dp - jnp.sum(p * dp, axis=1, keepdims=True)) * (QK_DIM ** -0.5)).astype(BF16)
            dq_ref[0] = jnp.dot(ds, k, preferred_element_type=F32)
            dk_part = lax.dot_general(ds, q, (((0,), (0,)), ((), ())), preferred_element_type=F32)
            dv_part = lax.dot_general(p.astype(BF16), do, (((0,), (0,)), ((), ())), preferred_element_type=F32)

            @pl.when(i == 0)
            def _():
                dk_ref[0] = dk_in[0] + dk_part
                dv_ref[0] = dv_in[0] + dv_part

            @pl.when(i > 0)
            def _():
                dk_ref[0] += dk_part
                dv_ref[0] += dv_part

        q_blk = pl.BlockSpec((1, tq, HEAD_PAD), lambda h, i, base=base: (h, base + i, 0))
        k_blk = pl.BlockSpec((1, kmax, HEAD_PAD), lambda h, i: (h, 0, 0))
        v_blk = pl.BlockSpec((1, kmax, V_DIM), lambda h, i: (h, 0, 0))
        dqs[s], dk, dv = pl.pallas_call(
            kern, name=f"attn_bwd_seg{s}", grid=(heads, seg // tq),
            in_specs=[q_blk, k_blk, v_blk, pl.BlockSpec((1, tq, V_DIM), lambda h, i, base=base: (h, base + i, 0)),
                      k_blk, v_blk],
            out_specs=[pl.BlockSpec((1, tq, HEAD_PAD), lambda h, i: (h, i, 0)), k_blk, v_blk],
            out_shape=[jax.ShapeDtypeStruct((heads, seg, HEAD_PAD), F32), jax.ShapeDtypeStruct(dk.shape, F32),
                       jax.ShapeDtypeStruct(dv.shape, F32)],
            input_output_aliases={4: 1, 5: 2}, compiler_params=_params("parallel", "arbitrary"),
        )(q3, k3, v3, do3, dk, dv)
    return jnp.concatenate(dqs, axis=1), dk, dv


@jax.custom_vjp
def _attention(q3, k3, v3):
    return _attn_fwd(q3, k3, v3)


def _attention_fwd(q3, k3, v3):
    return _attn_fwd(q3, k3, v3), (q3, k3, v3)


def _attention_bwd(res, do3):
    q3, k3, v3 = res
    dq, dk, dv = _attn_bwd(q3, k3, v3, do3)
    return dq.astype(q3.dtype), dk.astype(k3.dtype), dv.astype(v3.dtype)


_attention.defvjp(_attention_fwd, _attention_bwd)


def _conv_terms(x, c, w_ref, cb):
    u = c * x
    row = lax.broadcasted_iota(jnp.int32, u.shape, 0)
    u1 = jnp.where(row >= 1, pltpu.roll(u, 1, 0), 0.0)
    u2 = jnp.where(row >= 2, pltpu.roll(u, 2, 0), 0.0)
    y = cb + w_ref[0:1, :] * u2 + w_ref[1:2, :] * u1 + w_ref[2:3, :] * u
    return u, u1, u2, y


def _conv_specs(length):
    col = pl.BlockSpec((length, LANES), lambda j: (0, j))
    return col, pl.BlockSpec((CONV_K, LANES), lambda j: (0, j)), pl.BlockSpec((1, LANES), lambda j: (0, j))


def _conv_fwd(x, b, c, w, cb):
    length = x.shape[0]

    def kern(x_ref, b_ref, c_ref, w_ref, cb_ref, o_ref):
        _, _, _, y = _conv_terms(x_ref[...], c_ref[...], w_ref, cb_ref[...])
        o_ref[...] = b_ref[...] * y

    col, wspec, bspec = _conv_specs(length)
    return pl.pallas_call(
        kern, name="conv_fwd", grid=(MIX // LANES,), in_specs=[col, col, col, wspec, bspec], out_specs=col,
        out_shape=jax.ShapeDtypeStruct((length, MIX), F32), compiler_params=_params("parallel"),
    )(x, b, c, w, cb)


def _conv_bwd(x, b, c, w, cb, do):
    length = x.shape[0]

    def kern(x_ref, b_ref, c_ref, w_ref, cb_ref, do_ref, dx_ref, db_ref, dc_ref, dw_ref, dcb_ref):
        x, c, do = x_ref[...], c_ref[...], do_ref[...]
        u, u1, u2, y = _conv_terms(x, c, w_ref, cb_ref[...])
        db_ref[...] = do * y
        dy = do * b_ref[...]
        row = lax.broadcasted_iota(jnp.int32, dy.shape, 0)
        dy1 = jnp.where(row < length - 1, pltpu.roll(dy, length - 1, 0), 0.0)
        dy2 = jnp.where(row < length - 2, pltpu.roll(dy, length - 2, 0), 0.0)
        du = w_ref[2:3, :] * dy + w_ref[1:2, :] * dy1 + w_ref[0:1, :] * dy2
        dx_ref[...] = du * c
        dc_ref[...] = du * x
        dw_ref[0:1, :] = jnp.sum(dy * u2, axis=0, keepdims=True)
        dw_ref[1:2, :] = jnp.sum(dy * u1, axis=0, keepdims=True)
        dw_ref[2:3, :] = jnp.sum(dy * u, axis=0, keepdims=True)
        dcb_ref[...] = jnp.sum(dy, axis=0, keepdims=True)

    col, wspec, bspec = _conv_specs(length)
    big = jax.ShapeDtypeStruct((length, MIX), F32)
    return pl.pallas_call(
        kern, name="conv_bwd", grid=(MIX // LANES,), in_specs=[col, col, col, wspec, bspec, col],
        out_specs=[col, col, col, wspec, bspec],
        out_shape=[big, big, big, jax.ShapeDtypeStruct((CONV_K, MIX), F32), jax.ShapeDtypeStruct((1, MIX), F32)],
        compiler_params=_params("parallel"),
    )(x, b, c, w, cb, do)


@jax.custom_vjp
def _short_conv(x, b, c, w, cb):
    return _conv_fwd(x, b, c, w, cb)


def _short_conv_fwd(x, b, c, w, cb):
    return _conv_fwd(x, b, c, w, cb), (x, b, c, w, cb)


def _short_conv_bwd(res, do):
    return tuple(_conv_bwd(*res, do))


_short_conv.defvjp(_short_conv_fwd, _short_conv_bwd)


SCAN_ROWS = S5_CH // LANES
SCAN_TC = 136


def _scan_fwd(ar, ai, b):
    length = b.shape[0]
    tc = _divisor_tile(length, SCAN_TC, 8)

    def kern(ar_ref, ai_ref, b_ref, x_ref, sr, si):
        @pl.when(pl.program_id(0) == 0)
        def _():
            sr[...] = jnp.zeros_like(sr)
            si[...] = jnp.zeros_like(si)

        a_re, a_im = ar_ref[...], ai_ref[...]

        def body(t, carry):
            xr, xi = carry
            nr = a_re * xr - a_im * xi + b_ref[t, 0:SCAN_ROWS, :]
            ni = a_re * xi + a_im * xr + b_ref[t, SCAN_ROWS:2 * SCAN_ROWS, :]
            x_ref[t, 0:SCAN_ROWS, :] = nr
            x_ref[t, SCAN_ROWS:2 * SCAN_ROWS, :] = ni
            return nr, ni

        xr, xi = lax.fori_loop(0, tc, body, (sr[...], si[...]), unroll=4)
        sr[...] = xr
        si[...] = xi

    par = pl.BlockSpec((SCAN_ROWS, LANES), lambda i: (0, 0))
    blk = pl.BlockSpec((tc, 2 * SCAN_ROWS, LANES), lambda i: (i, 0, 0))
    return pl.pallas_call(
        kern, name="s5_scan_fwd", grid=(length // tc,), in_specs=[par, par, blk], out_specs=blk,
        out_shape=jax.ShapeDtypeStruct(b.shape, F32), scratch_shapes=[pltpu.VMEM((SCAN_ROWS, LANES), F32)] * 2,
        compiler_params=_params("arbitrary"),
    )(ar, ai, b)


def _scan_bwd(ar, ai, x, dx):
    length = x.shape[0]
    tc = _divisor_tile(length, SCAN_TC, 8)
    n_blk = length // tc
    re, im = slice(0, SCAN_ROWS), slice(SCAN_ROWS, 2 * SCAN_ROWS)

    def kern(ar_ref, ai_ref, x_ref, dx_ref, db_ref, dar_ref, dai_ref, lr_s, li_s):
        @pl.when(pl.program_id(0) == 0)
        def _():
            lr_s[...] = jnp.zeros_like(lr_s)
            li_s[...] = jnp.zeros_like(li_s)
            dar_ref[...] = jnp.zeros_like(dar_ref)
            dai_ref[...] = jnp.zeros_like(dai_ref)

        a_re, a_im = ar_ref[...], ai_ref[...]

        def body(j, carry):
            t = tc - 1 - j
            lr, li, gr, gi = carry
            x_re, x_im = x_ref[t, re, :], x_ref[t, im, :]
            gr = gr + (lr * x_re + li * x_im)
            gi = gi + (li * x_re - lr * x_im)
            nlr = dx_ref[t, re, :] + (a_re * lr + a_im * li)
            nli = dx_ref[t, im, :] + (a_re * li - a_im * lr)
            db_ref[t, re, :] = nlr
            db_ref[t, im, :] = nli
            return nlr, nli, gr, gi

        lr, li, gr, gi = lax.fori_loop(0, tc, body, (lr_s[...], li_s[...], dar_ref[...], dai_ref[...]), unroll=4)
        lr_s[...] = lr
        li_s[...] = li
        dar_ref[...] = gr
        dai_ref[...] = gi

    par = pl.BlockSpec((SCAN_ROWS, LANES), lambda i: (0, 0))
    blk = pl.BlockSpec((tc, 2 * SCAN_ROWS, LANES), lambda i: (n_blk - 1 - i, 0, 0))
    pout = jax.ShapeDtypeStruct((SCAN_ROWS, LANES), F32)
    return pl.pallas_call(
        kern, name="s5_scan_bwd", grid=(n_blk,), in_specs=[par, par, blk, blk],
        out_specs=[blk, par, par], out_shape=[jax.ShapeDtypeStruct(x.shape, F32), pout, pout],
        scratch_shapes=[pltpu.VMEM((SCAN_ROWS, LANES), F32)] * 2, compiler_params=_params("arbitrary"),
    )(ar, ai, x, dx)


@jax.custom_vjp
def _s5_scan(ar, ai, b):
    return _scan_fwd(ar, ai, b)


def _s5_scan_fwd(ar, ai, b):
    x = _scan_fwd(ar, ai, b)
    return x, (ar, ai, x)


def _s5_scan_bwd(res, dx):
    ar, ai, x = res
    db, dar, dai = _scan_bwd(ar, ai, x, dx)
    return dar, dai, db


_s5_scan.defvjp(_s5_scan_fwd, _s5_scan_bwd)


def _loss_call(y, target, n_real):
    length = y.shape[0]
    tm = _divisor_tile(length, 544, 16)

    def kern(y_ref, t_ref, loss_ref, dy_ref):
        i = pl.program_id(0)
        row = i * tm + lax.broadcasted_iota(jnp.int32, (tm, 1), 0)
        keep = jnp.logical_and(row >= N_META, row < n_real)
        err = jnp.where(keep, y_ref[...] - t_ref[...], 0.0)
        dy_ref[...] = err * (1.0 / D_MODEL)
        part = 0.5 * jnp.sum(jnp.mean(err * err, axis=-1, keepdims=True), axis=0, keepdims=True)

        @pl.when(i == 0)
        def _():
            loss_ref[...] = jnp.zeros_like(loss_ref)

        loss_ref[...] += part

    blk = pl.BlockSpec((tm, D_MODEL), lambda i: (i, 0))
    return pl.pallas_call(
        kern, name="loss_head", grid=(length // tm,), in_specs=[blk, blk],
        out_specs=[pl.BlockSpec((8, LANES), lambda i: (0, 0)), blk],
        out_shape=[jax.ShapeDtypeStruct((8, LANES), F32), jax.ShapeDtypeStruct(y.shape, F32)],
        compiler_params=_params("arbitrary"),
    )(y, target)


def _make_loss(n_real):
    @jax.custom_vjp
    def loss(y, target):
        return _loss_call(y, target, n_real)[0][0, 0]

    def fwd(y, target):
        total, dy = _loss_call(y, target, n_real)
        return total[0, 0], dy

    def bwd(dy, ct):
        return dy * ct, jnp.zeros_like(dy)

    loss.defvjp(fwd, bwd)
    return loss


HBM_SPEC = pl.BlockSpec(memory_space=pl.ANY)
MESH_ID = pl.DeviceIdType.MESH


SC_MESH = dict(axis_name="sequencer", num_cores=1)
GATHER_ID, SLOT_ID = 1, 2


def _handshake(peers):
    barrier = pltpu.get_barrier_semaphore()
    for peer in peers:
        pl.semaphore_signal(barrier, inc=1, device_id=peer, device_id_type=MESH_ID)
    pl.semaphore_wait(barrier, len(peers))


def _exchange_call(body, name, ins, out_types, n_sems, sequencer_id):
    n_in, n_out = len(ins), len(out_types)
    sems = [pltpu.SemaphoreType.DMA((n_sems,)), pltpu.SemaphoreType.DMA((n_sems,)), pltpu.SemaphoreType.DMA((n_in,))]
    if sequencer_id is None:
        def on_core(*refs):
            body(lambda peers: None, refs[:n_in], refs[n_in:n_in + n_out], *refs[n_in + n_out:])

        return pl.pallas_call(on_core, name=name, out_shape=out_types, in_specs=[HBM_SPEC] * n_in,
                              out_specs=[HBM_SPEC] * n_out, scratch_shapes=sems)(*ins)

    def on_sequencer(*refs):
        body(_handshake, refs[:n_in], refs[n_in:n_in + n_out], *refs[n_in + n_out:])

    return pl.kernel(on_sequencer, name=name, out_type=out_types, mesh=plsc.ScalarSubcoreMesh(**SC_MESH),
                     scratch_types=sems, compiler_params=pltpu.CompilerParams(collective_id=sequencer_id))(*ins)


def _all_gather(shards, name, sequencer=False):
    n = len(shards)

    def body(handshake, x_refs, out_refs, send_sems, recv_sems, local_sems):
        x, y, c = lax.axis_index("x"), lax.axis_index("y"), lax.axis_index("c")
        me, sibling = (x, y, c), (x, y, 1 - c)
        chips = [(1 - x, y), (x, 1 - y), (1 - x, 1 - y)]
        handshake([sibling] + [(*chip, c) for chip in chips])

        def copy(b, k, block, to, from_input=False):
            px, py, pc = block
            slot = out_refs[b].at[4 * px + 2 * py + pc]
            return pltpu.make_async_remote_copy(
                src_ref=x_refs[b] if from_input else slot, dst_ref=slot,
                send_sem=send_sems.at[7 * b + k], recv_sem=recv_sems.at[7 * b + k], device_id=to, device_id_type=MESH_ID)

        mine = [pltpu.make_async_copy(x_refs[b], out_refs[b].at[4 * x + 2 * y + c], local_sems.at[b]) for b in range(n)]
        for cp in mine:
            cp.start()
        first = []
        for b in range(n):
            first.append(copy(b, 0, me, sibling, from_input=True))
            first += [copy(b, 1 + j, me, (*chip, c), from_input=True) for j, chip in enumerate(chips)]
        for cp in first:
            cp.start()
        passed = []
        for j, chip in enumerate(chips):
            for b in range(n):
                copy(b, 1 + j, (*chip, c), me).wait_recv()
                passed.append(copy(b, 4 + j, (*chip, c), sibling))
                passed[-1].start()
        for b in range(n):
            copy(b, 0, sibling, me).wait_recv()
            for j, chip in enumerate(chips):
                copy(b, 4 + j, (*chip, 1 - c), me).wait_recv()
        for cp in first + passed:
            cp.wait_send()
        for cp in mine:
            cp.wait()

    out_types = [jax.ShapeDtypeStruct((N_DEV, *s.shape), s.dtype) for s in shards]
    return _exchange_call(body, name, shards, out_types, 7 * n, GATHER_ID if sequencer else None)


def _slot_exchange(bufs, name, sequencer=False):
    def body(handshake, ins, outs, send_sems, recv_sems, local_sems):
        x, y, c = lax.axis_index("x"), lax.axis_index("y"), lax.axis_index("c")
        me = 4 * x + 2 * y + c
        flips = [(dx, dy, dc) for dx in (0, 1) for dy in (0, 1) for dc in (0, 1)][1:]
        peers = [(1 - x if dx else x, 1 - y if dy else y, 1 - c if dc else c) for dx, dy, dc in flips]
        handshake(peers)
        own = [pltpu.make_async_copy(src.at[me], dst.at[me], local_sems.at[b]) for b, (src, dst) in enumerate(zip(ins, outs))]
        copies = []
        for b, (src, dst) in enumerate(zip(ins, outs)):
            for k, (px, py, pc) in enumerate(peers):
                copies.append(pltpu.make_async_remote_copy(
                    src_ref=src.at[4 * px + 2 * py + pc], dst_ref=dst.at[me],
                    send_sem=send_sems.at[7 * b + k], recv_sem=recv_sems.at[7 * b + k],
                    device_id=(px, py, pc), device_id_type=MESH_ID))
        for cp in own + copies:
            cp.start()
        for cp in copies + own:
            cp.wait()

    out_types = [jax.ShapeDtypeStruct(b.shape, b.dtype) for b in bufs]
    return _exchange_call(body, name, bufs, out_types, 7 * len(bufs), SLOT_ID if sequencer else None)


def _slot_sum(slots, name):
    _, rows, cols = slots.shape
    tm = _divisor_tile(rows, 512, 16)

    def kern(s_ref, o_ref):
        total = s_ref[0].astype(F32)
        for d in range(1, N_DEV):
            total = total + s_ref[d].astype(F32)
        o_ref[...] = total

    return pl.pallas_call(
        kern, name=name, grid=(rows // tm,), in_specs=[pl.BlockSpec((N_DEV, tm, cols), lambda i: (0, i, 0))],
        out_specs=pl.BlockSpec((tm, cols), lambda i: (i, 0)), out_shape=jax.ShapeDtypeStruct((rows, cols), F32),
        compiler_params=_params("parallel"),
    )(slots)


def _reduce_scatter(bufs, tag, sequencer):
    arrived = _slot_exchange(bufs, f"rs_exchange_{tag}", sequencer)
    return [_slot_sum(a, f"rs_sum_{tag}{i}") for i, a in enumerate(arrived)]


def _adamw(w, g, m, v, name):
    rows, cols = w.shape
    tm = _divisor_tile(rows, max(8, (512 * 1024) // cols // 8 * 8), 8)

    def kern(w_ref, g_ref, m_ref, v_ref, d_ref, nm_ref, nv_ref):
        g = g_ref[...]
        m = ADAM_B1 * m_ref[...] + (1.0 - ADAM_B1) * g
        v = ADAM_B2 * v_ref[...] + (1.0 - ADAM_B2) * (g * g)
        m_hat = m / (1.0 - ADAM_B1 ** ADAM_STEP)
        v_hat = v / (1.0 - ADAM_B2 ** ADAM_STEP)
        d_ref[...] = -ADAM_LR * (m_hat / (jnp.sqrt(v_hat) + ADAM_EPS) + ADAM_WD * w_ref[...])
        nm_ref[...] = m
        nv_ref[...] = v

    blk = pl.BlockSpec((tm, cols), lambda i: (i, 0))
    out = jax.ShapeDtypeStruct(w.shape, F32)
    return pl.pallas_call(
        kern, name=name, grid=(rows // tm,), in_specs=[blk] * 4, out_specs=[blk] * 3, out_shape=[out] * 3,
        compiler_params=_params("parallel"),
    )(w, g, m, v)


def _adamw_layers(w, g_layers, m, v, name):
    depth, rows, cols = w.shape
    tm = _divisor_tile(rows, max(8, (512 * 1024) // cols // 8 * 8), 8)

    def kern(w_ref, m_ref, v_ref, *refs):
        g_refs, (g_out, d_ref, nm_ref, nv_ref) = refs[:depth], refs[depth:]
        layer = pl.program_id(0)
        g = g_refs[0][...]
        for l in range(1, depth):
            g = jnp.where(layer == l, g_refs[l][...], g)
        m = ADAM_B1 * m_ref[0] + (1.0 - ADAM_B1) * g
        v = ADAM_B2 * v_ref[0] + (1.0 - ADAM_B2) * (g * g)
        m_hat = m / (1.0 - ADAM_B1 ** ADAM_STEP)
        v_hat = v / (1.0 - ADAM_B2 ** ADAM_STEP)
        g_out[0] = g
        d_ref[0] = -ADAM_LR * (m_hat / (jnp.sqrt(v_hat) + ADAM_EPS) + ADAM_WD * w_ref[0])
        nm_ref[0] = m
        nv_ref[0] = v

    blk = pl.BlockSpec((1, tm, cols), lambda l, i: (l, i, 0))
    out = jax.ShapeDtypeStruct(w.shape, F32)
    return pl.pallas_call(
        kern, name=name, grid=(depth, rows // tm),
        in_specs=[blk] * 3 + [pl.BlockSpec((tm, cols), lambda l, i: (i, 0))] * depth,
        out_specs=[blk] * 4, out_shape=[out] * 4, compiler_params=_params("parallel", "parallel"),
    )(w, m, v, *g_layers)


TRANSPOSED = ('ffn1_w_gate', 'ffn1_w_up', 'ffn2_w_gate', 'ffn2_w_up', 'w_in', 'mla_w_uq', 'mla_w_ukv', 'mla_w_o',
              'conv_w_out', 's5_w_out')
PIECES = ((('ffn1_w_gate', 'ffn1_w_up', 'ffn1_w_down'),),
          (('w_in', 'w_o'), ('mla_w_o', 'conv_w_out', 's5_w_out', 's5_w_glu'), ('mla_w_uq',), ('mla_w_ukv',)),
          (('ffn2_w_gate', 'ffn2_w_up', 'ffn2_w_down'),))
PIN_CUTS = (PIN_CQ, PIN_CKV, PIN_KR1, PIN_KR2, PIN_XBAR, PIN_BG, PIN_CG, PIN_U, PIN_GATES, PIN_GATES + D_MODEL,
            PIN_GATES + 2 * D_MODEL, PIN_END, D_IN_PAD)
PIN_PIECES = ((PIN_CQ, Q_RANK), (PIN_CKV, KV_RANK), (PIN_KR1, HALF_ROPE), (PIN_KR2, HALF_ROPE), (PIN_XBAR, 4 * MIX),
              (PIN_GATES, 3 * D_MODEL))


def _make_split(cuts):
    @jax.custom_vjp
    def split(t):
        return tuple(t[:, a:b] for a, b in zip(cuts[:-1], cuts[1:]))

    def fwd(t):
        return split(t), None

    def bwd(_, cts):
        return (jnp.concatenate(cts, axis=1),)

    split.defvjp(fwd, bwd)
    return split


def _travel_shape(name, shape):
    return (shape[2], shape[1]) if name in TRANSPOSED else (shape[1], shape[2])


def _pack_groups(tensors, layer, groups, dtype):
    def view(n):
        t = tensors[n][layer]
        return (t.T if n in TRANSPOSED else t).astype(dtype)
    return [jnp.concatenate([view(n) for n in grp], axis=0) for grp in groups]


def _unpack_groups(bufs, groups, shard_shapes):
    out = {}
    for buf, grp in zip(bufs, groups):
        at = 0
        for n in grp:
            r, _ = _travel_shape(n, shard_shapes[n])
            out[n] = buf[..., at:at + r, :]
            at += r
    return out


def _pack_rows(arrays):
    flat = jnp.concatenate([a.reshape(-1) for a in arrays])
    rows = -(-flat.shape[0] // PACK_COLS)
    rows = -(-rows // 8) * 8
    return jnp.pad(flat, (0, rows * PACK_COLS - flat.shape[0])).reshape(rows, PACK_COLS)


def _unpack_rows(buf, shapes):
    flat = buf.reshape(-1)
    out, at = [], 0
    for s in shapes:
        n = int(np.prod(s))
        out.append(flat[at:at + n].reshape(s))
        at += n
    return out


def _full_weight(t, axis):
    if axis == 1:
        return jnp.moveaxis(t, 0, 1).reshape(t.shape[1], N_DEV * t.shape[2], t.shape[3])
    return jnp.moveaxis(t, 0, 2).reshape(t.shape[1], t.shape[2], N_DEV * t.shape[3])


def _disassemble(d, groups, dtype):
    heads = lambda t: t.reshape(HEADS, -1, t.shape[1])
    full = {}
    for tag in ('ffn1', 'ffn2'):
        if f'{tag}_gu' in d:
            full.update({f'{tag}_w_gate': d[f'{tag}_gu'][:D_FF], f'{tag}_w_up': d[f'{tag}_gu'][D_FF:],
                         f'{tag}_w_down': d[f'{tag}_down']})
    if 'w_in' in d:
        w_in, uq, ukv = d['w_in'], d['w_uq'], d['w_ukv']
        full.update(
            w_in=jnp.concatenate([w_in[a:a + n] for a, n in PIN_PIECES], axis=0),
            mla_w_uq=jnp.concatenate([heads(uq[:HEADS * NOPE]), heads(uq[HEADS * NOPE:HEADS * NOPE + LANES]),
                                      heads(uq[HEADS * NOPE + LANES:])], axis=1).reshape(HEADS * QK_DIM, Q_RANK),
            mla_w_ukv=jnp.concatenate([heads(ukv[:HEADS * NOPE]), heads(ukv[HEADS * NOPE:])],
                                      axis=1).reshape(HEADS * (NOPE + V_DIM), KV_RANK),
            mla_w_o=d['mla_w_o'], conv_w_out=d['conv_w_out'], s5_w_glu=d['s5_w_glu'], s5_w_out=d['s5_w_out'], w_o=d['w_o'])
    return [jnp.concatenate([full[n].reshape(N_DEV, -1, full[n].shape[-1]).astype(dtype) for n in grp], axis=1)
            for grp in groups]


def _assemble(gathered, groups, shard_shapes):
    full = {n: t.reshape(N_DEV * t.shape[1], t.shape[2])
            for n, t in _unpack_groups(gathered, groups, shard_shapes).items()}
    out = {}
    for tag in ('ffn1', 'ffn2'):
        if f'{tag}_w_gate' in full:
            out[f'{tag}_gu'] = jnp.concatenate([full[f'{tag}_w_gate'], full[f'{tag}_w_up']], axis=0)
            out[f'{tag}_down'] = full[f'{tag}_w_down']
    if 'w_in' not in full:
        return out
    w_in = full['w_in']
    cuts = np.cumsum((0,) + IN_SPLITS)
    cq, ckv, kr, xbar, bg, cg, u, gates = [w_in[a:b] for a, b in zip(cuts[:-1], cuts[1:])]
    pad = lambda t, n: jnp.pad(t, ((0, n - t.shape[0]), (0, 0)))
    w_in_packed = jnp.concatenate(
        [cq, ckv, pad(kr[:HALF_ROPE], LANES), pad(kr[HALF_ROPE:], LANES), xbar, bg, cg, u, gates,
         jnp.zeros((D_IN_PAD - PIN_END, D_MODEL), w_in.dtype)], axis=0)
    uq = full['mla_w_uq'].reshape(HEADS, QK_DIM, Q_RANK)
    w_uq = jnp.concatenate([uq[:, :NOPE].reshape(HEADS * NOPE, Q_RANK),
                            uq[:, NOPE:NOPE + HALF_ROPE].reshape(HEADS * HALF_ROPE, Q_RANK),
                            uq[:, NOPE + HALF_ROPE:].reshape(HEADS * HALF_ROPE, Q_RANK)], axis=0)
    ukv = full['mla_w_ukv'].reshape(HEADS, NOPE + V_DIM, KV_RANK)
    w_ukv = jnp.concatenate([ukv[:, :NOPE].reshape(HEADS * NOPE, KV_RANK),
                             ukv[:, NOPE:].reshape(HEADS * V_DIM, KV_RANK)], axis=0)
    out.update(w_in=w_in_packed, w_uq=w_uq, w_ukv=w_ukv, mla_w_o=full['mla_w_o'], conv_w_out=full['conv_w_out'],
               s5_w_glu=full['s5_w_glu'], s5_w_out=full['s5_w_out'], w_o=full['w_o'])
    return out


def _s5_discretize(a_re, a_im, log_dt, b_re, b_im, c_re, c_im):
    dt = jnp.exp(log_dt)[:, None]
    mag = jnp.exp(dt * a_re)
    ab_re, ab_im = mag * jnp.cos(dt * a_im), mag * jnp.sin(dt * a_im)
    den = a_re * a_re + a_im * a_im
    nr, ni = ab_re - 1.0, ab_im
    coef_re = (nr * a_re + ni * a_im) / den
    coef_im = (ni * a_re - nr * a_im) / den
    bb_re = coef_re[..., None] * b_re - coef_im[..., None] * b_im
    bb_im = coef_re[..., None] * b_im + coef_im[..., None] * b_re
    unit = jnp.arange(MIX)[:, None]
    chan = jnp.arange(2 * S5_CH)[None, :]
    pair = jnp.arange(2 * S5_STATE)[:, None]
    own = (unit // S5_GROUP == (chan % S5_CH) // S5_STATE).astype(F32)
    copy = jnp.logical_and(pair // S5_STATE == chan // S5_CH, pair % S5_STATE == chan % S5_STATE).astype(F32)
    flat_b = lambda bb: bb.transpose(0, 2, 1).reshape(MIX, S5_STATE)
    flat_c = lambda cc: cc.transpose(2, 0, 1).reshape(S5_STATE, MIX)
    b_small = jnp.concatenate([flat_b(bb_re), flat_b(bb_im)], axis=1)
    c_small = jnp.concatenate([flat_c(c_re), -flat_c(c_im)], axis=0)
    b_map = _make_mm_f32w("s5_spread_b")(b_small, copy) * own
    c_map = _make_mm_f32w("s5_spread_c")(copy.T, c_small) * own.T
    return ab_re.reshape(SCAN_ROWS, LANES), ab_im.reshape(SCAN_ROWS, LANES), b_map, c_map


def _rope_tables(length):
    inv_freq = ROPE_BASE ** (-jnp.arange(0, ROPE, 2, dtype=F32) / ROPE)
    ang = jnp.arange(length).astype(F32)[:, None] * inv_freq[None, :]
    return jnp.tile(jnp.cos(ang), (1, LANES // HALF_ROPE)), jnp.tile(jnp.sin(ang), (1, LANES // HALF_ROPE))


def _heads_first(t):
    return t.reshape(t.shape[0], HEADS, -1).transpose(1, 0, 2)


def _local_loss(diff, big, n_real):
    small, wz = diff['small'], diff['wz']
    h = diff['h0']
    length = h.shape[0]
    cos, sin = _rope_tables(length)
    row2 = lambda v: v.reshape(1, -1)
    for l in range(DEPTH):
        w, z = big[l], wz[l]
        p = {k: small[k][l] for k in small if k != 'meta'}
        ffn = lambda h, tag, ln: _make_ffn(tag)(h, w[f'{tag}_gu'], w[f'{tag}_down'], z[f'{tag}_gu'], z[f'{tag}_down'],
                                                 row2(p[f'{ln}_g']), row2(p[f'{ln}_b']))
        h = ffn(h, "ffn1", "ln1")
        cq, ckv, kr1, kr2, xbar, bg, cg, u, gate_a, gate_b, gate_c, _ = _make_split(PIN_CUTS)(
            _make_mm("w_in", wt=True)(h, w['w_in'], z['w_in']))
        qn, = _make_rowwise(_f_rms, 1, 1, (BF16,), "q_rms")(cq, row2(p['mla_q_norm_g']))
        kvn, = _make_rowwise(_f_rms, 1, 1, (BF16,), "kv_rms")(ckv, row2(p['mla_kv_norm_g']))
        q_nope, q1, q2 = _make_split((0, HEADS * NOPE, HEADS * NOPE + LANES, HEADS * NOPE + 2 * LANES))(
            _make_mm("w_uq", wt=True)(qn, w['w_uq'], z['w_uq']))
        k_nope, val = _make_split((0, HEADS * NOPE, HEADS * (NOPE + V_DIM)))(
            _make_mm("w_ukv", wt=True)(kvn, w['w_ukv'], z['w_ukv']))
        rope = _make_rowwise(_f_rope, 4, 0, (BF16, BF16), "rope", n_nodiff=2)
        q1, q2 = rope(q1, q2, cos, sin)
        k1, k2 = rope(kr1, kr2, cos, sin)
        hpad = jnp.zeros((HEADS, length, HEAD_PAD - QK_DIM), BF16)
        q3 = jnp.concatenate([_heads_first(q_nope.astype(BF16)), _heads_first(q1), _heads_first(q2), hpad], -1)
        shared = lambda t: jnp.broadcast_to(t[None, :, :HALF_ROPE], (HEADS, length, HALF_ROPE))
        k3 = jnp.concatenate([_heads_first(k_nope.astype(BF16)), shared(k1), shared(k2), hpad], -1)
        v3 = _heads_first(val.astype(BF16))
        o3 = _attention(q3, k3, v3)
        y_a = _make_mm("mla_w_o", wt=True)(o3.transpose(1, 0, 2).reshape(length, MIX), w['mla_w_o'], z['mla_w_o'])
        conv = _short_conv(xbar, bg, cg, p['conv_w_full'], row2(p['conv_b']))
        y_b = _make_mm("conv_w_out", wt=True)(conv, w['conv_w_out'], z['conv_w_out'])
        ar, ai, b_map, c_map = _s5_discretize(p['s5_a_re'], p['s5_a_im'], p['s5_log_dt'], p['s5_b_re'], p['s5_b_im'],
                                              p['s5_c_re'], p['s5_c_im'])
        bu = _make_mm_f32w("s5_b")(u, b_map)
        states = _s5_scan(ar, ai, bu.reshape(length, 2 * SCAN_ROWS, LANES)).reshape(length, 2 * S5_CH)
        y_ssm = _make_mm_f32w("s5_c")(states, c_map)
        zed, = _make_rowwise(_f_gelu_skip, 2, 1, (F32,), "s5_gelu")(y_ssm, u, row2(p['s5_d']))
        t = _make_mm("s5_w_glu")(zed, w['s5_w_glu'], z['s5_w_glu'])
        glu, = _make_rowwise(_f_glu, 2, 1, (BF16,), "s5_glu")(zed, t, row2(p['s5_b_glu']))
        y_c = _make_mm("s5_w_out", wt=True)(glu, w['s5_w_out'], z['s5_w_out'])
        mixed, = _make_rowwise(_f_merge, 6, 0, (BF16,), "merge")(gate_a, gate_b, gate_c, y_a, y_b, y_c)
        mix_out = _make_mm("w_o")(mixed, w['w_o'], z['w_o'])
        h, = _make_rowwise(_f_ln_full, 2, 2, (F32,), "mix_ln")(h, mix_out, row2(p['ln2_g']), row2(p['ln2_b']))
        h = ffn(h, "ffn2", "ln3")
    return _make_loss(n_real)(h, diff['target'])


def kernel(x, meta, ffn1_w_gate, ffn1_w_up, ffn1_w_down, ln1_g, ln1_b, w_in, mla_q_norm_g, mla_w_uq, mla_kv_norm_g, mla_w_ukv, mla_w_o, conv_w, conv_b, conv_w_out, s5_a_re, s5_a_im, s5_log_dt, s5_b_re, s5_b_im, s5_c_re, s5_c_im, s5_d, s5_w_glu, s5_b_glu, s5_w_out, w_o, ln2_g, ln2_b, ffn2_w_gate, ffn2_w_up, ffn2_w_down, ln3_g, ln3_b, loss_target, m_meta, m_ffn1_w_gate, m_ffn1_w_up, m_ffn1_w_down, m_ln1_g, m_ln1_b, m_w_in, m_mla_q_norm_g, m_mla_w_uq, m_mla_kv_norm_g, m_mla_w_ukv, m_mla_w_o, m_conv_w, m_conv_b, m_conv_w_out, m_s5_a_re, m_s5_a_im, m_s5_log_dt, m_s5_b_re, m_s5_b_im, m_s5_c_re, m_s5_c_im, m_s5_d, m_s5_w_glu, m_s5_b_glu, m_s5_w_out, m_w_o, m_ln2_g, m_ln2_b, m_ffn2_w_gate, m_ffn2_w_up, m_ffn2_w_down, m_ln3_g, m_ln3_b, v_meta, v_ffn1_w_gate, v_ffn1_w_up, v_ffn1_w_down, v_ln1_g, v_ln1_b, v_w_in, v_mla_q_norm_g, v_mla_w_uq, v_mla_kv_norm_g, v_mla_w_ukv, v_mla_w_o, v_conv_w, v_conv_b, v_conv_w_out, v_s5_a_re, v_s5_a_im, v_s5_log_dt, v_s5_b_re, v_s5_b_im, v_s5_c_re, v_s5_c_im, v_s5_d, v_s5_w_glu, v_s5_b_glu, v_s5_w_out, v_w_o, v_ln2_g, v_ln2_b, v_ffn2_w_gate, v_ffn2_w_up, v_ffn2_w_down, v_ln3_g, v_ln3_b):
    args = locals()
    w = {n: args[n] for n in WEIGHT_NAMES}
    m = {n: args["m_" + n] for n in WEIGHT_NAMES}
    v = {n: args["v_" + n] for n in WEIGHT_NAMES}
    me = 4 * lax.axis_index("x") + 2 * lax.axis_index("y") + lax.axis_index("c")
    seq = x.shape[1]
    n_real = N_META + seq
    length = -(-n_real // LANES) * LANES

    shard_shapes = {n: w[n].shape for n in BIG}
    big = [{} for _ in range(DEPTH)]
    for l in range(DEPTH):
        for p, groups in enumerate(PIECES):
            gathered = _all_gather(_pack_groups(w, l, groups, BF16), f"gather_weights_layer{l}_piece{p}", sequencer=True)
            big[l].update(_assemble(gathered, groups, shard_shapes))
    small_shards = _all_gather([_pack_rows([w[n] for n in SMALL_SHARDED])], "gather_small")[0].reshape(N_DEV, -1)
    meta_full = _full_weight(small_shards[:, :meta.size].reshape(N_DEV, 1, *meta.shape), 2)[0]
    conv_w_full = _full_weight(small_shards[:, meta.size:meta.size + conv_w.size].reshape(N_DEV, *conv_w.shape), 2)

    small = {n: w[n] for n in SMALL_NAMES if n not in SMALL_SHARDED}
    small['conv_w_full'] = conv_w_full
    small['meta'] = meta_full
    wz = jax.tree.map(lambda t: jnp.zeros(t.shape, BF16), big)
    pad_rows = length - n_real

    def loss_fn(diff):
        h0 = jnp.concatenate([diff['small']['meta'], diff['x'], jnp.zeros((pad_rows, D_MODEL), F32)], axis=0)
        target = jnp.pad(loss_target[0], ((N_META, pad_rows), (0, 0)))
        return _local_loss(dict(h0=h0, small=diff['small'], wz=diff['wz'], target=target), big, n_real)

    loss_local, grads = jax.value_and_grad(loss_fn)(dict(x=x[0], small=small, wz=wz))

    small_names = [n for n in SMALL_NAMES if n not in SMALL_SHARDED] + ['conv_w_full', 'meta']
    small_flat = _pack_rows([loss_local.reshape(1)] + [grads['small'][n] for n in small_names])
    rows_each = -(-small_flat.shape[0] // (8 * N_DEV)) * 8
    small_flat = jnp.pad(small_flat, ((0, rows_each * N_DEV - small_flat.shape[0]), (0, 0)))
    layer_sums = [{} for _ in range(DEPTH)]
    for l in reversed(range(DEPTH)):
        for p, groups in reversed(list(enumerate(PIECES))):
            last = l == 0 and p == 0
            extra = [small_flat.reshape(N_DEV, rows_each, PACK_COLS)] if last else []
            sums = _reduce_scatter(_disassemble(grads['wz'][l], groups, BF16) + extra, f"grads_layer{l}_piece{p}_",
                                   sequencer=True)
            layer_sums[l].update(_unpack_groups(sums[:len(groups)], groups, shard_shapes))
            if last:
                small_sum = sums[-1]
    small_all, = _all_gather([small_sum], "gather_small_grads")
    loss, *small_sums = _unpack_rows(small_all, [()] + [grads['small'][n].shape for n in small_names])
    small_grads = dict(zip(small_names, small_sums))
    g = {}
    for name in SMALL_NAMES:
        if name == 'meta':
            g[name] = lax.dynamic_slice_in_dim(small_grads['meta'], me * meta.shape[1], meta.shape[1], axis=1)
        elif name == 'conv_w':
            g[name] = lax.dynamic_slice_in_dim(small_grads['conv_w_full'], me * conv_w.shape[2], conv_w.shape[2], axis=2)
        else:
            g[name] = small_grads[name]

    delta, new_m, new_v = {}, {}, {}
    for name in BIG:
        per_layer = [s[name].T if name in TRANSPOSED else s[name] for s in layer_sums]
        g[name], delta[name], new_m[name], new_v[name] = _adamw_layers(w[name], per_layer, m[name], v[name], f"adamw_{name}")
    shapes = [w[n].shape for n in SMALL_NAMES]
    d, nm, nv = _adamw(*[_pack_rows([t[n] for n in SMALL_NAMES]) for t in (w, g, m, v)], "adamw_small")
    for out, buf in ((delta, d), (new_m, nm), (new_v, nv)):
        out.update(zip(SMALL_NAMES, _unpack_rows(buf, shapes)))

    return (loss, grads['x'][None], *[g[n] for n in WEIGHT_NAMES], *[delta[n] for n in WEIGHT_NAMES],
            *[new_m[n] for n in WEIGHT_NAMES], *[new_v[n] for n in WEIGHT_NAMES])
```

```python
import functools
import math

import jax
import jax.numpy as jnp
import numpy as np
from jax import lax
from jax.experimental import pallas as pl
from jax.experimental.pallas import tpu as pltpu
from jax.experimental.pallas import tpu_sc as plsc

F32 = jnp.float32
BF16 = jnp.bfloat16

D_MODEL = 1024
DEPTH = 2
N_META = 16
HEADS = 8
V_DIM = 64
NOPE = 64
ROPE = 32
HALF_ROPE = ROPE // 2
QK_DIM = NOPE + ROPE
Q_RANK = 384
KV_RANK = 256
MIX = 512
CONV_K = 3
S5_GROUPS = 32
S5_GROUP = 16
S5_STATE = 64
S5_CH = S5_GROUPS * S5_STATE
D_FF = 2816
ALPHA = (2.0 * DEPTH) ** 0.25
LN_EPS = 1e-5
RMS_EPS = 1e-6
ROPE_BASE = 10000.0
IN_SPLITS = (Q_RANK, KV_RANK, ROPE, MIX, MIX, MIX, MIX, 3 * D_MODEL)
D_IN = sum(IN_SPLITS)
ADAM_LR, ADAM_B1, ADAM_B2, ADAM_EPS, ADAM_WD, ADAM_STEP = 0.001, 0.9, 0.999, 1e-08, 0.01, 10

N_DEV = 8
AXES = ("x", "y", "c")
LANES = 128
PACK_COLS = 1024
HEAD_PAD = 128
VMEM_LIMIT = 48 * 1024 * 1024

PIN_CQ, PIN_CKV, PIN_KR1, PIN_KR2, PIN_XBAR, PIN_BG, PIN_CG, PIN_U, PIN_GATES, PIN_END = (
    0, 384, 640, 768, 896, 1408, 1920, 2432, 2944, 6016)
D_IN_PAD = 6144

WEIGHT_NAMES = ['meta', 'ffn1_w_gate', 'ffn1_w_up', 'ffn1_w_down', 'ln1_g', 'ln1_b', 'w_in', 'mla_q_norm_g', 'mla_w_uq',
                'mla_kv_norm_g', 'mla_w_ukv', 'mla_w_o', 'conv_w', 'conv_b', 'conv_w_out', 's5_a_re', 's5_a_im',
                's5_log_dt', 's5_b_re', 's5_b_im', 's5_c_re', 's5_c_im', 's5_d', 's5_w_glu', 's5_b_glu', 's5_w_out',
                'w_o', 'ln2_g', 'ln2_b', 'ffn2_w_gate', 'ffn2_w_up', 'ffn2_w_down', 'ln3_g', 'ln3_b']
BIG = {'ffn1_w_gate': 2, 'ffn1_w_up': 2, 'ffn1_w_down': 1, 'w_in': 2, 'mla_w_uq': 2, 'mla_w_ukv': 2, 'mla_w_o': 2,
       'conv_w_out': 2, 's5_w_glu': 1, 's5_w_out': 2, 'w_o': 1, 'ffn2_w_gate': 2, 'ffn2_w_up': 2, 'ffn2_w_down': 1}
SMALL_SHARDED = ('meta', 'conv_w')
SMALL_NAMES = [n for n in WEIGHT_NAMES if n not in BIG]


def _divisor_tile(n, limit, mult):
    best = None
    for t in range(mult, min(n, limit) + 1, mult):
        if n % t == 0:
            best = t
    return best if best is not None else n


def _params(*sem):
    return pltpu.CompilerParams(dimension_semantics=sem, vmem_limit_bytes=VMEM_LIMIT)


def _matmul(a, b, *, ta=False, tb=False, out_dtype=F32, add=None, name):
    m, k = (a.shape[1], a.shape[0]) if ta else a.shape
    n = b.shape[0] if tb else b.shape[1]
    assert (b.shape[1] if tb else b.shape[0]) == k, (a.shape, b.shape, ta, tb)
    tm = _divisor_tile(m, 1408, LANES) if ta else _divisor_tile(m, 2176 if a.dtype == BF16 else 1088, 16)
    tn = _divisor_tile(n, 512, LANES)
    tk = _divisor_tile(k, 1408, 16 if ta else LANES)
    nk = k // tk
    dims = (((0 if ta else 1,), (1 if tb else 0,)), ((), ()))

    in_place = jnp.dtype(out_dtype) == jnp.dtype(F32)

    def kern(a_ref, b_ref, *rest):
        add_ref = rest[0] if add is not None else None
        o_ref, *scratch = rest[1:] if add is not None else rest
        kk = pl.program_id(2)
        part = lax.dot_general(a_ref[...].astype(BF16), b_ref[...].astype(BF16), dims, preferred_element_type=F32)
        first = lambda: part if add_ref is None else part + add_ref[...].astype(F32)
        if nk == 1:
            o_ref[...] = first().astype(o_ref.dtype)
            return
        acc_ref = o_ref if in_place else scratch[0]

        @pl.when(kk == 0)
        def _():
            acc_ref[...] = first()

        @pl.when(kk > 0)
        def _():
            acc_ref[...] += part

        if not in_place:
            @pl.when(kk == nk - 1)
            def _():
                o_ref[...] = acc_ref[...].astype(o_ref.dtype)

    a_spec = pl.BlockSpec((tk, tm), lambda i, j, kk: (kk, i)) if ta else pl.BlockSpec((tm, tk), lambda i, j, kk: (i, kk))
    b_spec = pl.BlockSpec((tn, tk), lambda i, j, kk: (j, kk)) if tb else pl.BlockSpec((tk, tn), lambda i, j, kk: (kk, j))
    o_spec = pl.BlockSpec((tm, tn), lambda i, j, kk: (i, j))
    return pl.pallas_call(
        kern, name=name, grid=(m // tm, n // tn, nk),
        in_specs=[a_spec, b_spec] + ([o_spec] if add is not None else []), out_specs=o_spec,
        out_shape=jax.ShapeDtypeStruct((m, n), out_dtype),
        scratch_shapes=[] if (nk == 1 or in_place) else [pltpu.VMEM((tm, tn), F32)],
        compiler_params=_params("parallel", "parallel", "arbitrary"),
    )(a, b, *([add] if add is not None else []))


def _make_mm(name, wt=False):
    @jax.custom_vjp
    def mm(x, w, wz):
        return _matmul(x, w, tb=wt, name=name + "_fwd")

    def fwd(x, w, wz):
        return _matmul(x, w, tb=wt, name=name + "_fwd"), (x, w)

    def bwd(res, dy):
        x, w = res
        wz_dtype = BF16
        dx = _matmul(dy, w, tb=not wt, out_dtype=x.dtype, name=name + "_dx")
        dw = (_matmul(dy, x, ta=True, out_dtype=wz_dtype, name=name + "_dw") if wt
              else _matmul(x, dy, ta=True, out_dtype=wz_dtype, name=name + "_dw"))
        return dx, jnp.zeros_like(w), dw

    mm.defvjp(fwd, bwd)
    return mm


def _make_mm_f32w(name):
    @jax.custom_vjp
    def mm(x, w):
        return _matmul(x, w, name=name + "_fwd")

    def fwd(x, w):
        return _matmul(x, w, name=name + "_fwd"), (x, w)

    def bwd(res, dy):
        x, w = res
        return (_matmul(dy, w, tb=True, out_dtype=x.dtype, name=name + "_dx"),
                _matmul(x, dy, ta=True, name=name + "_dw"))

    mm.defvjp(fwd, bwd)
    return mm


def _row_tile(rows, widths):
    limit = max(16, (6 * 1024 * 1024 // 4) // max(1, sum(widths)))
    return _divisor_tile(rows, limit, 16)


def _make_rowwise(f, n_rows, n_pars, out_dtypes, name, n_nodiff=0, grad_dtypes=None):
    n_out = len(out_dtypes)
    n_diff = n_rows - n_nodiff

    def run_fwd(rows, pars):
        length = rows[0].shape[0]
        shapes = jax.eval_shape(lambda *a: f(*a), *[jax.ShapeDtypeStruct((16, r.shape[1]), F32) for r in rows],
                                *[jax.ShapeDtypeStruct(p.shape, F32) for p in pars])
        widths = [s.shape[1] for s in shapes]
        tm = _row_tile(length, [r.shape[1] for r in rows] + widths)

        def kern(*refs):
            ins = [r[...].astype(F32) for r in refs[:n_rows + n_pars]]
            outs = f(*ins)
            for o_ref, o in zip(refs[n_rows + n_pars:], outs):
                o_ref[...] = o.astype(o_ref.dtype)

        return pl.pallas_call(
            kern, name=name + "_fwd", grid=(length // tm,),
            in_specs=[pl.BlockSpec((tm, r.shape[1]), lambda i: (i, 0)) for r in rows]
            + [pl.BlockSpec(p.shape, lambda i: (0, 0)) for p in pars],
            out_specs=[pl.BlockSpec((tm, w), lambda i: (i, 0)) for w in widths],
            out_shape=[jax.ShapeDtypeStruct((length, w), dt) for w, dt in zip(widths, out_dtypes)],
            compiler_params=_params("parallel"),
        )(*rows, *pars)

    def run_bwd(rows, pars, cts):
        length = rows[0].shape[0]
        tm = _row_tile(length, [r.shape[1] for r in rows] * 2 + [c.shape[1] for c in cts] * 2)

        def kern(*refs):
            ins = [r[...].astype(F32) for r in refs[:n_rows + n_pars]]
            ct = [r[...].astype(F32) for r in refs[n_rows + n_pars:n_rows + n_pars + n_out]]
            out_refs = refs[n_rows + n_pars + n_out:]
            nodiff = ins[n_diff:n_rows]
            _, vjp = jax.vjp(lambda *a: f(*a[:n_diff], *nodiff, *a[n_diff:]), *ins[:n_diff], *ins[n_rows:])
            grads = vjp(tuple(ct))
            for o_ref, g in zip(out_refs[:n_diff], grads[:n_diff]):
                o_ref[...] = g.astype(o_ref.dtype)
            first = pl.program_id(0) == 0
            for o_ref, g in zip(out_refs[n_diff:], grads[n_diff:]):
                @pl.when(first)
                def _(o_ref=o_ref, g=g):
                    o_ref[...] = g

                @pl.when(jnp.logical_not(first))
                def _(o_ref=o_ref, g=g):
                    o_ref[...] += g

        return pl.pallas_call(
            kern, name=name + "_bwd", grid=(length // tm,),
            in_specs=[pl.BlockSpec((tm, r.shape[1]), lambda i: (i, 0)) for r in rows]
            + [pl.BlockSpec(p.shape, lambda i: (0, 0)) for p in pars]
            + [pl.BlockSpec((tm, c.shape[1]), lambda i: (i, 0)) for c in cts],
            out_specs=[pl.BlockSpec((tm, r.shape[1]), lambda i: (i, 0)) for r in rows[:n_diff]]
            + [pl.BlockSpec(p.shape, lambda i: (0, 0)) for p in pars],
            out_shape=[jax.ShapeDtypeStruct(r.shape, r.dtype if grad_dtypes is None else grad_dtypes[i])
                       for i, r in enumerate(rows[:n_diff])]
            + [jax.ShapeDtypeStruct(p.shape, F32) for p in pars],
            compiler_params=_params("arbitrary"),
        )(*rows, *pars, *cts)

    @jax.custom_vjp
    def op(*args):
        return tuple(run_fwd(args[:n_rows], args[n_rows:]))

    def fwd(*args):
        return tuple(run_fwd(args[:n_rows], args[n_rows:])), args

    def bwd(args, cts):
        grads = run_bwd(args[:n_rows], args[n_rows:], cts)
        zeros = [jnp.zeros_like(r) for r in args[n_diff:n_rows]]
        return (*grads[:n_diff], *zeros, *grads[n_diff:])

    op.defvjp(fwd, bwd)
    op.run_fwd, op.run_bwd = run_fwd, run_bwd
    return op


def _layer_norm(z, g, b):
    mu = jnp.mean(z, axis=-1, keepdims=True)
    d = z - mu
    var = jnp.mean(d * d, axis=-1, keepdims=True)
    return d * lax.rsqrt(var + LN_EPS) * g + b


def _f_ln_half(h, f, g, b):
    return (_layer_norm(ALPHA * h + 0.5 * f, g, b),)


def _f_ln_full(h, f, g, b):
    return (_layer_norm(ALPHA * h + f, g, b),)


def _f_rms(x, g):
    return (x * lax.rsqrt(jnp.mean(x * x, axis=-1, keepdims=True) + RMS_EPS) * g,)


def _f_rope(x1, x2, cos, sin):
    return x1 * cos - x2 * sin, x2 * cos + x1 * sin


def _f_gelu_skip(y, u, d):
    return (jax.nn.gelu(y + d * u),)


def _f_glu(z, t, b):
    return (z * jax.nn.sigmoid(t + b),)


def _f_merge(ga, gb, gc, ya, yb, yc):
    return (jax.nn.sigmoid(ga) * ya + jax.nn.sigmoid(gb) * yb + jax.nn.sigmoid(gc) * yc,)


def _f_swiglu(gate, up):
    return (jax.nn.silu(gate) * up,)


def _f_cast(x):
    return (x,)


def _make_ffn(tag):
    ln = _make_rowwise(_f_ln_half, 2, 2, (F32,), f"{tag}_ln", grad_dtypes=(F32, BF16))
    cast = _make_rowwise(_f_cast, 1, 0, (BF16,), f"{tag}_cast")
    swiglu = _make_rowwise(_f_swiglu, 2, 0, (BF16,), f"{tag}_swiglu", grad_dtypes=(BF16, BF16))

    def forward(h, w_gate, w_up, w_down, g, b):
        hb, = cast.run_fwd((h,), ())
        gate = _matmul(hb, w_gate, tb=True, name=f"{tag}_gate_fwd")
        up = _matmul(hb, w_up, tb=True, name=f"{tag}_up_fwd")
        act, = swiglu.run_fwd((gate, up), ())
        f = _matmul(act, w_down, name=f"{tag}_down_fwd")
        y, = ln.run_fwd((h, f), (g, b))
        return y, (h, hb, gate, up, act, f, w_gate, w_up, w_down, g, b)

    @jax.custom_vjp
    def block(h, w_gate, w_up, w_down, z_gate, z_up, z_down, g, b):
        return forward(h, w_gate, w_up, w_down, g, b)[0]

    def fwd(h, w_gate, w_up, w_down, z_gate, z_up, z_down, g, b):
        return forward(h, w_gate, w_up, w_down, g, b)

    def bwd(res, dy):
        h, hb, gate, up, act, f, w_gate, w_up, w_down, g, b = res
        dh, df, dg, db = ln.run_bwd((h, f), (g, b), (dy,))
        dact = _matmul(df, w_down, tb=True, out_dtype=BF16, name=f"{tag}_down_dx")
        dw_down = _matmul(act, df, ta=True, out_dtype=BF16, name=f"{tag}_down_dw")
        dgate, dup = swiglu.run_bwd((gate, up), (), (dact,))
        dw_gate = _matmul(dgate, hb, ta=True, out_dtype=BF16, name=f"{tag}_gate_dw")
        dw_up = _matmul(dup, hb, ta=True, out_dtype=BF16, name=f"{tag}_up_dw")
        dh = _matmul(dgate, w_gate, add=dh, name=f"{tag}_gate_dx")
        dh = _matmul(dup, w_up, add=dh, name=f"{tag}_up_dx")
        zero = jnp.zeros_like
        return dh, zero(w_gate), zero(w_up), zero(w_down), dw_gate, dw_up, dw_down, dg, db

    block.defvjp(fwd, bwd)
    return block


def _attn_scores(q, k, q_block, tq):
    length = k.shape[0]
    s = lax.dot_general(q, k, (((1,), (1,)), ((), ())), preferred_element_type=F32) * (QK_DIM ** -0.5)
    row = q_block * tq + lax.broadcasted_iota(jnp.int32, (tq, length), 0)
    col = lax.broadcasted_iota(jnp.int32, (tq, length), 1)
    s = jnp.where(col <= row, s, -1e30)
    e = jnp.exp(s - jnp.max(s, axis=1, keepdims=True))
    return e * (1.0 / jnp.sum(e, axis=1, keepdims=True))


ATTN_SEGMENTS = 4


def _attn_tiles(length):
    seg = length // ATTN_SEGMENTS
    return seg, _divisor_tile(seg, 272, 16)


def _attn_fwd(q3, k3, v3):
    heads, length, _ = q3.shape
    seg, tq = _attn_tiles(length)
    outs = []
    for s in range(ATTN_SEGMENTS):
        kmax, base = (s + 1) * seg, s * (seg // tq)

        def kern(q_ref, k_ref, v_ref, o_ref, base=base):
            p = _attn_scores(q_ref[0], k_ref[0], base + pl.program_id(1), tq)
            o_ref[0] = jnp.dot(p.astype(BF16), v_ref[0], preferred_element_type=F32).astype(o_ref.dtype)

        outs.append(pl.pallas_call(
            kern, name=f"attn_fwd_seg{s}", grid=(heads, seg // tq),
            in_specs=[pl.BlockSpec((1, tq, HEAD_PAD), lambda h, i, base=base: (h, base + i, 0)),
                      pl.BlockSpec((1, kmax, HEAD_PAD), lambda h, i: (h, 0, 0)),
                      pl.BlockSpec((1, kmax, V_DIM), lambda h, i: (h, 0, 0))],
            out_specs=pl.BlockSpec((1, tq, V_DIM), lambda h, i: (h, i, 0)),
            out_shape=jax.ShapeDtypeStruct((heads, seg, V_DIM), F32),
            compiler_params=_params("parallel", "parallel"),
        )(q3, k3, v3))
    return jnp.concatenate(outs, axis=1)


def _attn_bwd(q3, k3, v3, do3):
    heads, length, _ = q3.shape
    seg, tq = _attn_tiles(length)
    dk = jnp.zeros((heads, length, HEAD_PAD), F32)
    dv = jnp.zeros((heads, length, V_DIM), F32)
    dqs = [None] * ATTN_SEGMENTS
    for s in reversed(range(ATTN_SEGMENTS)):
        kmax, base = (s + 1) * seg, s * (seg // tq)

        def kern(q_ref, k_ref, v_ref, do_ref, dk_in, dv_in, dq_ref, dk_ref, dv_ref, base=base):
            i = pl.program_id(1)
            q, k, v, do = q_ref[0], k_ref[0], v_ref[0], do_ref[0].astype(BF16)
            p = _attn_scores(q, k, base + i, tq)
            dp = lax.dot_general(do, v, (((1,), (1,)), ((), ())), preferred_element_type=F32)
            ds = (p * (dp - jnp.sum(p * dp, axis=1, keepdims=True)) * (QK_DIM ** -0.5)).astype(BF16)
            dq_ref[0] = jnp.dot(ds, k, preferred_element_type=F32)
            dk_part = lax.dot_general(ds, q, (((0,), (0,)), ((), ())), preferred_element_type=F32)
            dv_part = lax.dot_general(p.astype(BF16), do, (((0,), (0,)), ((), ())), preferred_element_type=F32)

            @pl.when(i == 0)
            def _():
                dk_ref[0] = dk_in[0] + dk_part
                dv_ref[0] = dv_in[0] + dv_part

            @pl.when(i > 0)
            def _():
                dk_ref[0] += dk_part
                dv_ref[0] += dv_part

        q_blk = pl.BlockSpec((1, tq, HEAD_PAD), lambda h, i, base=base: (h, base + i, 0))
        k_blk = pl.BlockSpec((1, kmax, HEAD_PAD), lambda h, i: (h, 0, 0))
        v_blk = pl.BlockSpec((1, kmax, V_DIM), lambda h, i: (h, 0, 0))
        dqs[s], dk, dv = pl.pallas_call(
            kern, name=f"attn_bwd_seg{s}", grid=(heads, seg // tq),
            in_specs=[q_blk, k_blk, v_blk, pl.BlockSpec((1, tq, V_DIM), lambda h, i, base=base: (h, base + i, 0)),
                      k_blk, v_blk],
            out_specs=[pl.BlockSpec((1, tq, HEAD_PAD), lambda h, i: (h, i, 0)), k_blk, v_blk],
            out_shape=[jax.ShapeDtypeStruct((heads, seg, HEAD_PAD), F32), jax.ShapeDtypeStruct(dk.shape, F32),
                       jax.ShapeDtypeStruct(dv.shape, F32)],
            input_output_aliases={4: 1, 5: 2}, compiler_params=_params("parallel", "arbitrary"),
        )(q3, k3, v3, do3, dk, dv)
    return jnp.concatenate(dqs, axis=1), dk, dv


@jax.custom_vjp
def _attention(q3, k3, v3):
    return _attn_fwd(q3, k3, v3)


def _attention_fwd(q3, k3, v3):
    return _attn_fwd(q3, k3, v3), (q3, k3, v3)


def _attention_bwd(res, do3):
    q3, k3, v3 = res
    dq, dk, dv = _attn_bwd(q3, k3, v3, do3)
    return dq.astype(q3.dtype), dk.astype(k3.dtype), dv.astype(v3.dtype)


_attention.defvjp(_attention_fwd, _attention_bwd)


def _conv_terms(x, c, w_ref, cb):
    u = c * x
    row = lax.broadcasted_iota(jnp.int32, u.shape, 0)
    u1 = jnp.where(row >= 1, pltpu.roll(u, 1, 0), 0.0)
    u2 = jnp.where(row >= 2, pltpu.roll(u, 2, 0), 0.0)
    y = cb + w_ref[0:1, :] * u2 + w_ref[1:2, :] * u1 + w_ref[2:3, :] * u
    return u, u1, u2, y


def _conv_specs(length):
    col = pl.BlockSpec((length, LANES), lambda j: (0, j))
    return col, pl.BlockSpec((CONV_K, LANES), lambda j: (0, j)), pl.BlockSpec((1, LANES), lambda j: (0, j))


def _conv_fwd(x, b, c, w, cb):
    length = x.shape[0]

    def kern(x_ref, b_ref, c_ref, w_ref, cb_ref, o_ref):
        _, _, _, y = _conv_terms(x_ref[...], c_ref[...], w_ref, cb_ref[...])
        o_ref[...] = b_ref[...] * y

    col, wspec, bspec = _conv_specs(length)
    return pl.pallas_call(
        kern, name="conv_fwd", grid=(MIX // LANES,), in_specs=[col, col, col, wspec, bspec], out_specs=col,
        out_shape=jax.ShapeDtypeStruct((length, MIX), F32), compiler_params=_params("parallel"),
    )(x, b, c, w, cb)


def _conv_bwd(x, b, c, w, cb, do):
    length = x.shape[0]

    def kern(x_ref, b_ref, c_ref, w_ref, cb_ref, do_ref, dx_ref, db_ref, dc_ref, dw_ref, dcb_ref):
        x, c, do = x_ref[...], c_ref[...], do_ref[...]
        u, u1, u2, y = _conv_terms(x, c, w_ref, cb_ref[...])
        db_ref[...] = do * y
        dy = do * b_ref[...]
        row = lax.broadcasted_iota(jnp.int32, dy.shape, 0)
        dy1 = jnp.where(row < length - 1, pltpu.roll(dy, length - 1, 0), 0.0)
        dy2 = jnp.where(row < length - 2, pltpu.roll(dy, length - 2, 0), 0.0)
        du = w_ref[2:3, :] * dy + w_ref[1:2, :] * dy1 + w_ref[0:1, :] * dy2
        dx_ref[...] = du * c
        dc_ref[...] = du * x
        dw_ref[0:1, :] = jnp.sum(dy * u2, axis=0, keepdims=True)
        dw_ref[1:2, :] = jnp.sum(dy * u1, axis=0, keepdims=True)
        dw_ref[2:3, :] = jnp.sum(dy * u, axis=0, keepdims=True)
        dcb_ref[...] = jnp.sum(dy, axis=0, keepdims=True)

    col, wspec, bspec = _conv_specs(length)
    big = jax.ShapeDtypeStruct((length, MIX), F32)
    return pl.pallas_call(
        kern, name="conv_bwd", grid=(MIX // LANES,), in_specs=[col, col, col, wspec, bspec, col],
        out_specs=[col, col, col, wspec, bspec],
        out_shape=[big, big, big, jax.ShapeDtypeStruct((CONV_K, MIX), F32), jax.ShapeDtypeStruct((1, MIX), F32)],
        compiler_params=_params("parallel"),
    )(x, b, c, w, cb, do)


@jax.custom_vjp
def _short_conv(x, b, c, w, cb):
    return _conv_fwd(x, b, c, w, cb)


def _short_conv_fwd(x, b, c, w, cb):
    return _conv_fwd(x, b, c, w, cb), (x, b, c, w, cb)


def _short_conv_bwd(res, do):
    return tuple(_conv_bwd(*res, do))


_short_conv.defvjp(_short_conv_fwd, _short_conv_bwd)


SCAN_ROWS = S5_CH // LANES
SCAN_TC = 136


def _scan_fwd(ar, ai, b):
    length = b.shape[0]
    tc = _divisor_tile(length, SCAN_TC, 8)

    def kern(ar_ref, ai_ref, b_ref, x_ref, sr, si):
        @pl.when(pl.program_id(0) == 0)
        def _():
            sr[...] = jnp.zeros_like(sr)
            si[...] = jnp.zeros_like(si)

        a_re, a_im = ar_ref[...], ai_ref[...]

        def body(t, carry):
            xr, xi = carry
            nr = a_re * xr - a_im * xi + b_ref[t, 0:SCAN_ROWS, :]
            ni = a_re * xi + a_im * xr + b_ref[t, SCAN_ROWS:2 * SCAN_ROWS, :]
            x_ref[t, 0:SCAN_ROWS, :] = nr
            x_ref[t, SCAN_ROWS:2 * SCAN_ROWS, :] = ni
            return nr, ni

        xr, xi = lax.fori_loop(0, tc, body, (sr[...], si[...]), unroll=4)
        sr[...] = xr
        si[...] = xi

    par = pl.BlockSpec((SCAN_ROWS, LANES), lambda i: (0, 0))
    blk = pl.BlockSpec((tc, 2 * SCAN_ROWS, LANES), lambda i: (i, 0, 0))
    return pl.pallas_call(
        kern, name="s5_scan_fwd", grid=(length // tc,), in_specs=[par, par, blk], out_specs=blk,
        out_shape=jax.ShapeDtypeStruct(b.shape, F32), scratch_shapes=[pltpu.VMEM((SCAN_ROWS, LANES), F32)] * 2,
        compiler_params=_params("arbitrary"),
    )(ar, ai, b)


def _scan_bwd(ar, ai, x, dx):
    length = x.shape[0]
    tc = _divisor_tile(length, SCAN_TC, 8)
    n_blk = length // tc
    re, im = slice(0, SCAN_ROWS), slice(SCAN_ROWS, 2 * SCAN_ROWS)

    def kern(ar_ref, ai_ref, x_ref, dx_ref, db_ref, dar_ref, dai_ref, lr_s, li_s):
        @pl.when(pl.program_id(0) == 0)
        def _():
            lr_s[...] = jnp.zeros_like(lr_s)
            li_s[...] = jnp.zeros_like(li_s)
            dar_ref[...] = jnp.zeros_like(dar_ref)
            dai_ref[...] = jnp.zeros_like(dai_ref)

        a_re, a_im = ar_ref[...], ai_ref[...]

        def body(j, carry):
            t = tc - 1 - j
            lr, li, gr, gi = carry
            x_re, x_im = x_ref[t, re, :], x_ref[t, im, :]
            gr = gr + (lr * x_re + li * x_im)
            gi = gi + (li * x_re - lr * x_im)
            nlr = dx_ref[t, re, :] + (a_re * lr + a_im * li)
            nli = dx_ref[t, im, :] + (a_re * li - a_im * lr)
            db_ref[t, re, :] = nlr
            db_ref[t, im, :] = nli
            return nlr, nli, gr, gi

        lr, li, gr, gi = lax.fori_loop(0, tc, body, (lr_s[...], li_s[...], dar_ref[...], dai_ref[...]), unroll=4)
        lr_s[...] = lr
        li_s[...] = li
        dar_ref[...] = gr
        dai_ref[...] = gi

    par = pl.BlockSpec((SCAN_ROWS, LANES), lambda i: (0, 0))
    blk = pl.BlockSpec((tc, 2 * SCAN_ROWS, LANES), lambda i: (n_blk - 1 - i, 0, 0))
    pout = jax.ShapeDtypeStruct((SCAN_ROWS, LANES), F32)
    return pl.pallas_call(
        kern, name="s5_scan_bwd", grid=(n_blk,), in_specs=[par, par, blk, blk],
        out_specs=[blk, par, par], out_shape=[jax.ShapeDtypeStruct(x.shape, F32), pout, pout],
        scratch_shapes=[pltpu.VMEM((SCAN_ROWS, LANES), F32)] * 2, compiler_params=_params("arbitrary"),
    )(ar, ai, x, dx)


@jax.custom_vjp
def _s5_scan(ar, ai, b):
    return _scan_fwd(ar, ai, b)


def _s5_scan_fwd(ar, ai, b):
    x = _scan_fwd(ar, ai, b)
    return x, (ar, ai, x)


def _s5_scan_bwd(res, dx):
    ar, ai, x = res
    db, dar, dai = _scan_bwd(ar, ai, x, dx)
    return dar, dai, db


_s5_scan.defvjp(_s5_scan_fwd, _s5_scan_bwd)


def _loss_call(y, target, n_real):
    length = y.shape[0]
    tm = _divisor_tile(length, 544, 16)

    def kern(y_ref, t_ref, loss_ref, dy_ref):
        i = pl.program_id(0)
        row = i * tm + lax.broadcasted_iota(jnp.int32, (tm, 1), 0)
        keep = jnp.logical_and(row >= N_META, row < n_real)
        err = jnp.where(keep, y_ref[...] - t_ref[...], 0.0)
        dy_ref[...] = err * (1.0 / D_MODEL)
        part = 0.5 * jnp.sum(jnp.mean(err * err, axis=-1, keepdims=True), axis=0, keepdims=True)

        @pl.when(i == 0)
        def _():
            loss_ref[...] = jnp.zeros_like(loss_ref)

        loss_ref[...] += part

    blk = pl.BlockSpec((tm, D_MODEL), lambda i: (i, 0))
    return pl.pallas_call(
        kern, name="loss_head", grid=(length // tm,), in_specs=[blk, blk],
        out_specs=[pl.BlockSpec((8, LANES), lambda i: (0, 0)), blk],
        out_shape=[jax.ShapeDtypeStruct((8, LANES), F32), jax.ShapeDtypeStruct(y.shape, F32)],
        compiler_params=_params("arbitrary"),
    )(y, target)


def _make_loss(n_real):
    @jax.custom_vjp
    def loss(y, target):
        return _loss_call(y, target, n_real)[0][0, 0]

    def fwd(y, target):
        total, dy = _loss_call(y, target, n_real)
        return total[0, 0], dy

    def bwd(dy, ct):
        return dy * ct, jnp.zeros_like(dy)

    loss.defvjp(fwd, bwd)
    return loss


HBM_SPEC = pl.BlockSpec(memory_space=pl.ANY)
MESH_ID = pl.DeviceIdType.MESH


SC_MESH = dict(axis_name="sequencer", num_cores=1)
GATHER_ID, SLOT_ID = 1, 2


def _handshake(peers):
    barrier = pltpu.get_barrier_semaphore()
    for peer in peers:
        pl.semaphore_signal(barrier, inc=1, device_id=peer, device_id_type=MESH_ID)
    pl.semaphore_wait(barrier, len(peers))


def _exchange_call(body, name, ins, out_types, n_sems, sequencer_id):
    n_in, n_out = len(ins), len(out_types)
    sems = [pltpu.SemaphoreType.DMA((n_sems,)), pltpu.SemaphoreType.DMA((n_sems,)), pltpu.SemaphoreType.DMA((n_in,))]
    if sequencer_id is None:
        def on_core(*refs):
            body(lambda peers: None, refs[:n_in], refs[n_in:n_in + n_out], *refs[n_in + n_out:])

        return pl.pallas_call(on_core, name=name, out_shape=out_types, in_specs=[HBM_SPEC] * n_in,
                              out_specs=[HBM_SPEC] * n_out, scratch_shapes=sems)(*ins)

    def on_sequencer(*refs):
        body(_handshake, refs[:n_in], refs[n_in:n_in + n_out], *refs[n_in + n_out:])

    return pl.kernel(on_sequencer, name=name, out_type=out_types, mesh=plsc.ScalarSubcoreMesh(**SC_MESH),
                     scratch_types=sems, compiler_params=pltpu.CompilerParams(collective_id=sequencer_id))(*ins)


def _all_gather(shards, name, sequencer=False):
    n = len(shards)

    def body(handshake, x_refs, out_refs, send_sems, recv_sems, local_sems):
        x, y, c = lax.axis_index("x"), lax.axis_index("y"), lax.axis_index("c")
        me, sibling = (x, y, c), (x, y, 1 - c)
        chips = [(1 - x, y), (x, 1 - y), (1 - x, 1 - y)]
        handshake([sibling] + [(*chip, c) for chip in chips])

        def copy(b, k, block, to, from_input=False):
            px, py, pc = block
            slot = out_refs[b].at[4 * px + 2 * py + pc]
            return pltpu.make_async_remote_copy(
                src_ref=x_refs[b] if from_input else slot, dst_ref=slot,
                send_sem=send_sems.at[7 * b + k], recv_sem=recv_sems.at[7 * b + k], device_id=to, device_id_type=MESH_ID)

        mine = [pltpu.make_async_copy(x_refs[b], out_refs[b].at[4 * x + 2 * y + c], local_sems.at[b]) for b in range(n)]
        for cp in mine:
            cp.start()
        first = []
        for b in range(n):
            first.append(copy(b, 0, me, sibling, from_input=True))
            first += [copy(b, 1 + j, me, (*chip, c), from_input=True) for j, chip in enumerate(chips)]
        for cp in first:
            cp.start()
        passed = []
        for j, chip in enumerate(chips):
            for b in range(n):
                copy(b, 1 + j, (*chip, c), me).wait_recv()
                passed.append(copy(b, 4 + j, (*chip, c), sibling))
                passed[-1].start()
        for b in range(n):
            copy(b, 0, sibling, me).wait_recv()
            for j, chip in enumerate(chips):
                copy(b, 4 + j, (*chip, 1 - c), me).wait_recv()
        for cp in first + passed:
            cp.wait_send()
        for cp in mine:
            cp.wait()

    out_types = [jax.ShapeDtypeStruct((N_DEV, *s.shape), s.dtype) for s in shards]
    return _exchange_call(body, name, shards, out_types, 7 * n, GATHER_ID if sequencer else None)


def _slot_exchange(bufs, name, sequencer=False):
    def body(handshake, ins, outs, send_sems, recv_sems, local_sems):
        x, y, c = lax.axis_index("x"), lax.axis_index("y"), lax.axis_index("c")
        me = 4 * x + 2 * y + c
        flips = [(dx, dy, dc) for dx in (0, 1) for dy in (0, 1) for dc in (0, 1)][1:]
        peers = [(1 - x if dx else x, 1 - y if dy else y, 1 - c if dc else c) for dx, dy, dc in flips]
        handshake(peers)
        own = [pltpu.make_async_copy(src.at[me], dst.at[me], local_sems.at[b]) for b, (src, dst) in enumerate(zip(ins, outs))]
        copies = []
        for b, (src, dst) in enumerate(zip(ins, outs)):
            for k, (px, py, pc) in enumerate(peers):
                copies.append(pltpu.make_async_remote_copy(
                    src_ref=src.at[4 * px + 2 * py + pc], dst_ref=dst.at[me],
                    send_sem=send_sems.at[7 * b + k], recv_sem=recv_sems.at[7 * b + k],
                    device_id=(px, py, pc), device_id_type=MESH_ID))
        for cp in own + copies:
            cp.start()
        for cp in copies + own:
            cp.wait()

    out_types = [jax.ShapeDtypeStruct(b.shape, b.dtype) for b in bufs]
    return _exchange_call(body, name, bufs, out_types, 7 * len(bufs), SLOT_ID if sequencer else None)


def _slot_sum(slots, name):
    _, rows, cols = slots.shape
    tm = _divisor_tile(rows, 512, 16)

    def kern(s_ref, o_ref):
        total = s_ref[0].astype(F32)
        for d in range(1, N_DEV):
            total = total + s_ref[d].astype(F32)
        o_ref[...] = total

    return pl.pallas_call(
        kern, name=name, grid=(rows // tm,), in_specs=[pl.BlockSpec((N_DEV, tm, cols), lambda i: (0, i, 0))],
        out_specs=pl.BlockSpec((tm, cols), lambda i: (i, 0)), out_shape=jax.ShapeDtypeStruct((rows, cols), F32),
        compiler_params=_params("parallel"),
    )(slots)


def _reduce_scatter(bufs, tag, sequencer):
    arrived = _slot_exchange(bufs, f"rs_exchange_{tag}", sequencer)
    return [_slot_sum(a, f"rs_sum_{tag}{i}") for i, a in enumerate(arrived)]


def _adamw(w, g, m, v, name):
    rows, cols = w.shape
    tm = _divisor_tile(rows, max(8, (512 * 1024) // cols // 8 * 8), 8)

    def kern(w_ref, g_ref, m_ref, v_ref, d_ref, nm_ref, nv_ref):
        g = g_ref[...]
        m = ADAM_B1 * m_ref[...] + (1.0 - ADAM_B1) * g
        v = ADAM_B2 * v_ref[...] + (1.0 - ADAM_B2) * (g * g)
        m_hat = m / (1.0 - ADAM_B1 ** ADAM_STEP)
        v_hat = v / (1.0 - ADAM_B2 ** ADAM_STEP)
        d_ref[...] = -ADAM_LR * (m_hat / (jnp.sqrt(v_hat) + ADAM_EPS) + ADAM_WD * w_ref[...])
        nm_ref[...] = m
        nv_ref[...] = v

    blk = pl.BlockSpec((tm, cols), lambda i: (i, 0))
    out = jax.ShapeDtypeStruct(w.shape, F32)
    return pl.pallas_call(
        kern, name=name, grid=(rows // tm,), in_specs=[blk] * 4, out_specs=[blk] * 3, out_shape=[out] * 3,
        compiler_params=_params("parallel"),
    )(w, g, m, v)


def _adamw_layers(w, g_layers, m, v, name):
    depth, rows, cols = w.shape
    tm = _divisor_tile(rows, max(8, (512 * 1024) // cols // 8 * 8), 8)

    def kern(w_ref, m_ref, v_ref, *refs):
        g_refs, (g_out, d_ref, nm_ref, nv_ref) = refs[:depth], refs[depth:]
        layer = pl.program_id(0)
        g = g_refs[0][...]
        for l in range(1, depth):
            g = jnp.where(layer == l, g_refs[l][...], g)
        m = ADAM_B1 * m_ref[0] + (1.0 - ADAM_B1) * g
        v = ADAM_B2 * v_ref[0] + (1.0 - ADAM_B2) * (g * g)
        m_hat = m / (1.0 - ADAM_B1 ** ADAM_STEP)
        v_hat = v / (1.0 - ADAM_B2 ** ADAM_STEP)
        g_out[0] = g
        d_ref[0] = -ADAM_LR * (m_hat / (jnp.sqrt(v_hat) + ADAM_EPS) + ADAM_WD * w_ref[0])
        nm_ref[0] = m
        nv_ref[0] = v

    blk = pl.BlockSpec((1, tm, cols), lambda l, i: (l, i, 0))
    out = jax.ShapeDtypeStruct(w.shape, F32)
    return pl.pallas_call(
        kern, name=name, grid=(depth, rows // tm),
        in_specs=[blk] * 3 + [pl.BlockSpec((tm, cols), lambda l, i: (i, 0))] * depth,
        out_specs=[blk] * 4, out_shape=[out] * 4, compiler_params=_params("parallel", "parallel"),
    )(w, m, v, *g_layers)


TRANSPOSED = ('ffn1_w_gate', 'ffn1_w_up', 'ffn2_w_gate', 'ffn2_w_up', 'w_in', 'mla_w_uq', 'mla_w_ukv', 'mla_w_o',
              'conv_w_out', 's5_w_out')
PIECES = ((('ffn1_w_gate',), ('ffn1_w_up',), ('ffn1_w_down',)),
          (('w_in', 'w_o'), ('mla_w_o', 'conv_w_out', 's5_w_out', 's5_w_glu'), ('mla_w_uq',), ('mla_w_ukv',)),
          (('ffn2_w_gate',), ('ffn2_w_up',), ('ffn2_w_down',)))
PIN_CUTS = (PIN_CQ, PIN_CKV, PIN_KR1, PIN_KR2, PIN_XBAR, PIN_BG, PIN_CG, PIN_U, PIN_GATES, PIN_GATES + D_MODEL,
            PIN_GATES + 2 * D_MODEL, PIN_END, D_IN_PAD)
PIN_PIECES = ((PIN_CQ, Q_RANK), (PIN_CKV, KV_RANK), (PIN_KR1, HALF_ROPE), (PIN_KR2, HALF_ROPE), (PIN_XBAR, 4 * MIX),
              (PIN_GATES, 3 * D_MODEL))


def _make_split(cuts):
    @jax.custom_vjp
    def split(t):
        return tuple(t[:, a:b] for a, b in zip(cuts[:-1], cuts[1:]))

    def fwd(t):
        return split(t), None

    def bwd(_, cts):
        return (jnp.concatenate(cts, axis=1),)

    split.defvjp(fwd, bwd)
    return split


def _make_projection(tag, cuts):
    cast = _make_rowwise(_f_cast, 1, 0, (BF16,), f"{tag}_cast")

    def forward(h, w):
        hb, = cast.run_fwd((h,), ())
        full = _matmul(hb, w, tb=True, name=f"{tag}_fwd")
        return tuple(full[:, a:b] for a, b in zip(cuts[:-1], cuts[1:])), (hb, w)

    @jax.custom_vjp
    def proj(h, w, wz):
        return forward(h, w)[0]

    def fwd(h, w, wz):
        return forward(h, w)

    def bwd(res, cts):
        hb, w = res
        d_full = jnp.concatenate([c.astype(BF16) for c in cts], axis=1)
        dw = _matmul(d_full, hb, ta=True, out_dtype=BF16, name=f"{tag}_dw")
        return _matmul(d_full, w, name=f"{tag}_dx"), jnp.zeros_like(w), dw

    proj.defvjp(fwd, bwd)
    return proj


def _travel_shape(name, shape):
    return (shape[2], shape[1]) if name in TRANSPOSED else (shape[1], shape[2])


def _pack_groups(tensors, layer, groups, dtype):
    def view(n):
        t = tensors[n][layer]
        return (t.T if n in TRANSPOSED else t).astype(dtype)
    return [jnp.concatenate([view(n) for n in grp], axis=0) for grp in groups]


def _unpack_groups(bufs, groups, shard_shapes):
    out = {}
    for buf, grp in zip(bufs, groups):
        at = 0
        for n in grp:
            r, _ = _travel_shape(n, shard_shapes[n])
            out[n] = buf[..., at:at + r, :]
            at += r
    return out


def _pack_rows(arrays):
    flat = jnp.concatenate([a.reshape(-1) for a in arrays])
    rows = -(-flat.shape[0] // PACK_COLS)
    rows = -(-rows // 8) * 8
    return jnp.pad(flat, (0, rows * PACK_COLS - flat.shape[0])).reshape(rows, PACK_COLS)


def _unpack_rows(buf, shapes):
    flat = buf.reshape(-1)
    out, at = [], 0
    for s in shapes:
        n = int(np.prod(s))
        out.append(flat[at:at + n].reshape(s))
        at += n
    return out


def _full_weight(t, axis):
    if axis == 1:
        return jnp.moveaxis(t, 0, 1).reshape(t.shape[1], N_DEV * t.shape[2], t.shape[3])
    return jnp.moveaxis(t, 0, 2).reshape(t.shape[1], t.shape[2], N_DEV * t.shape[3])


def _disassemble(d, groups, dtype):
    heads = lambda t: t.reshape(HEADS, -1, t.shape[1])
    full = {}
    for tag in ('ffn1', 'ffn2'):
        if f'{tag}_gate' in d:
            full.update({f'{tag}_w_gate': d[f'{tag}_gate'], f'{tag}_w_up': d[f'{tag}_up'], f'{tag}_w_down': d[f'{tag}_down']})
    if 'w_in' in d:
        w_in, uq, ukv = d['w_in'], d['w_uq'], d['w_ukv']
        full.update(
            w_in=jnp.concatenate([w_in[a:a + n] for a, n in PIN_PIECES], axis=0),
            mla_w_uq=jnp.concatenate([heads(uq[:HEADS * NOPE]), heads(uq[HEADS * NOPE:HEADS * NOPE + LANES]),
                                      heads(uq[HEADS * NOPE + LANES:])], axis=1).reshape(HEADS * QK_DIM, Q_RANK),
            mla_w_ukv=jnp.concatenate([heads(ukv[:HEADS * NOPE]), heads(ukv[HEADS * NOPE:])],
                                      axis=1).reshape(HEADS * (NOPE + V_DIM), KV_RANK),
            mla_w_o=d['mla_w_o'], conv_w_out=d['conv_w_out'], s5_w_glu=d['s5_w_glu'], s5_w_out=d['s5_w_out'], w_o=d['w_o'])
    return [jnp.concatenate([full[n].reshape(N_DEV, -1, full[n].shape[-1]).astype(dtype) for n in grp], axis=1)
            for grp in groups]


def _assemble(gathered, groups, shard_shapes):
    full = {n: t.reshape(N_DEV * t.shape[1], t.shape[2])
            for n, t in _unpack_groups(gathered, groups, shard_shapes).items()}
    out = {}
    for tag in ('ffn1', 'ffn2'):
        if f'{tag}_w_gate' in full:
            out.update({f'{tag}_gate': full[f'{tag}_w_gate'], f'{tag}_up': full[f'{tag}_w_up'],
                        f'{tag}_down': full[f'{tag}_w_down']})
    if 'w_in' not in full:
        return out
    w_in = full['w_in']
    cuts = np.cumsum((0,) + IN_SPLITS)
    cq, ckv, kr, xbar, bg, cg, u, gates = [w_in[a:b] for a, b in zip(cuts[:-1], cuts[1:])]
    pad = lambda t, n: jnp.pad(t, ((0, n - t.shape[0]), (0, 0)))
    w_in_packed = jnp.concatenate(
        [cq, ckv, pad(kr[:HALF_ROPE], LANES), pad(kr[HALF_ROPE:], LANES), xbar, bg, cg, u, gates,
         jnp.zeros((D_IN_PAD - PIN_END, D_MODEL), w_in.dtype)], axis=0)
    uq = full['mla_w_uq'].reshape(HEADS, QK_DIM, Q_RANK)
    w_uq = jnp.concatenate([uq[:, :NOPE].reshape(HEADS * NOPE, Q_RANK),
                            uq[:, NOPE:NOPE + HALF_ROPE].reshape(HEADS * HALF_ROPE, Q_RANK),
                            uq[:, NOPE + HALF_ROPE:].reshape(HEADS * HALF_ROPE, Q_RANK)], axis=0)
    ukv = full['mla_w_ukv'].reshape(HEADS, NOPE + V_DIM, KV_RANK)
    w_ukv = jnp.concatenate([ukv[:, :NOPE].reshape(HEADS * NOPE, KV_RANK),
                             ukv[:, NOPE:].reshape(HEADS * V_DIM, KV_RANK)], axis=0)
    out.update(w_in=w_in_packed, w_uq=w_uq, w_ukv=w_ukv, mla_w_o=full['mla_w_o'], conv_w_out=full['conv_w_out'],
               s5_w_glu=full['s5_w_glu'], s5_w_out=full['s5_w_out'], w_o=full['w_o'])
    return out


def _s5_discretize(a_re, a_im, log_dt, b_re, b_im, c_re, c_im):
    dt = jnp.exp(log_dt)[:, None]
    mag = jnp.exp(dt * a_re)
    ab_re, ab_im = mag * jnp.cos(dt * a_im), mag * jnp.sin(dt * a_im)
    den = a_re * a_re + a_im * a_im
    nr, ni = ab_re - 1.0, ab_im
    coef_re = (nr * a_re + ni * a_im) / den
    coef_im = (ni * a_re - nr * a_im) / den
    bb_re = coef_re[..., None] * b_re - coef_im[..., None] * b_im
    bb_im = coef_re[..., None] * b_im + coef_im[..., None] * b_re
    unit = jnp.arange(MIX)[:, None]
    chan = jnp.arange(2 * S5_CH)[None, :]
    pair = jnp.arange(2 * S5_STATE)[:, None]
    own = (unit // S5_GROUP == (chan % S5_CH) // S5_STATE).astype(F32)
    copy = jnp.logical_and(pair // S5_STATE == chan // S5_CH, pair % S5_STATE == chan % S5_STATE).astype(F32)
    flat_b = lambda bb: bb.transpose(0, 2, 1).reshape(MIX, S5_STATE)
    flat_c = lambda cc: cc.transpose(2, 0, 1).reshape(S5_STATE, MIX)
    b_small = jnp.concatenate([flat_b(bb_re), flat_b(bb_im)], axis=1)
    c_small = jnp.concatenate([flat_c(c_re), -flat_c(c_im)], axis=0)
    b_map = _make_mm_f32w("s5_spread_b")(b_small, copy) * own
    c_map = _make_mm_f32w("s5_spread_c")(copy.T, c_small) * own.T
    return ab_re.reshape(SCAN_ROWS, LANES), ab_im.reshape(SCAN_ROWS, LANES), b_map, c_map


def _rope_tables(length):
    inv_freq = ROPE_BASE ** (-jnp.arange(0, ROPE, 2, dtype=F32) / ROPE)
    ang = jnp.arange(length).astype(F32)[:, None] * inv_freq[None, :]
    return jnp.tile(jnp.cos(ang), (1, LANES // HALF_ROPE)), jnp.tile(jnp.sin(ang), (1, LANES // HALF_ROPE))


def _heads_first(t):
    return t.reshape(t.shape[0], HEADS, -1).transpose(1, 0, 2)


def _local_loss(diff, big, n_real):
    small, wz = diff['small'], diff['wz']
    h = diff['h0']
    length = h.shape[0]
    cos, sin = _rope_tables(length)
    row2 = lambda v: v.reshape(1, -1)
    for l in range(DEPTH):
        w, z = big[l], wz[l]
        p = {k: small[k][l] for k in small if k != 'meta'}
        ffn = lambda h, tag, ln: _make_ffn(tag)(
            h, *[w[f'{tag}_{k}'] for k in ('gate', 'up', 'down')], *[z[f'{tag}_{k}'] for k in ('gate', 'up', 'down')],
            row2(p[f'{ln}_g']), row2(p[f'{ln}_b']))
        h = ffn(h, "ffn1", "ln1")
        cq, ckv, kr1, kr2, xbar, bg, cg, u, gate_a, gate_b, gate_c, _ = _make_projection("w_in", PIN_CUTS)(
            h, w['w_in'], z['w_in'])
        qn, = _make_rowwise(_f_rms, 1, 1, (BF16,), "q_rms")(cq, row2(p['mla_q_norm_g']))
        kvn, = _make_rowwise(_f_rms, 1, 1, (BF16,), "kv_rms")(ckv, row2(p['mla_kv_norm_g']))
        q_nope, q1, q2 = _make_split((0, HEADS * NOPE, HEADS * NOPE + LANES, HEADS * NOPE + 2 * LANES))(
            _make_mm("w_uq", wt=True)(qn, w['w_uq'], z['w_uq']))
        k_nope, val = _make_split((0, HEADS * NOPE, HEADS * (NOPE + V_DIM)))(
            _make_mm("w_ukv", wt=True)(kvn, w['w_ukv'], z['w_ukv']))
        rope = _make_rowwise(_f_rope, 4, 0, (BF16, BF16), "rope", n_nodiff=2)
        q1, q2 = rope(q1, q2, cos, sin)
        k1, k2 = rope(kr1, kr2, cos, sin)
        hpad = jnp.zeros((HEADS, length, HEAD_PAD - QK_DIM), BF16)
        q3 = jnp.concatenate([_heads_first(q_nope.astype(BF16)), _heads_first(q1), _heads_first(q2), hpad], -1)
        shared = lambda t: jnp.broadcast_to(t[None, :, :HALF_ROPE], (HEADS, length, HALF_ROPE))
        k3 = jnp.concatenate([_heads_first(k_nope.astype(BF16)), shared(k1), shared(k2), hpad], -1)
        v3 = _heads_first(val.astype(BF16))
        o3 = _attention(q3, k3, v3)
        y_a = _make_mm("mla_w_o", wt=True)(o3.transpose(1, 0, 2).reshape(length, MIX), w['mla_w_o'], z['mla_w_o'])
        conv = _short_conv(xbar, bg, cg, p['conv_w_full'], row2(p['conv_b']))
        y_b = _make_mm("conv_w_out", wt=True)(conv, w['conv_w_out'], z['conv_w_out'])
        ar, ai, b_map, c_map = _s5_discretize(p['s5_a_re'], p['s5_a_im'], p['s5_log_dt'], p['s5_b_re'], p['s5_b_im'],
                                              p['s5_c_re'], p['s5_c_im'])
        bu = _make_mm_f32w("s5_b")(u, b_map)
        states = _s5_scan(ar, ai, bu.reshape(length, 2 * SCAN_ROWS, LANES)).reshape(length, 2 * S5_CH)
        y_ssm = _make_mm_f32w("s5_c")(states, c_map)
        zed, = _make_rowwise(_f_gelu_skip, 2, 1, (F32,), "s5_gelu")(y_ssm, u, row2(p['s5_d']))
        t = _make_mm("s5_w_glu")(zed, w['s5_w_glu'], z['s5_w_glu'])
        glu, = _make_rowwise(_f_glu, 2, 1, (BF16,), "s5_glu")(zed, t, row2(p['s5_b_glu']))
        y_c = _make_mm("s5_w_out", wt=True)(glu, w['s5_w_out'], z['s5_w_out'])
        mixed, = _make_rowwise(_f_merge, 6, 0, (BF16,), "merge")(gate_a, gate_b, gate_c, y_a, y_b, y_c)
        mix_out = _make_mm("w_o")(mixed, w['w_o'], z['w_o'])
        h, = _make_rowwise(_f_ln_full, 2, 2, (F32,), "mix_ln")(h, mix_out, row2(p['ln2_g']), row2(p['ln2_b']))
        h = ffn(h, "ffn2", "ln3")
    return _make_loss(n_real)(h, diff['target'])


def kernel(x, meta, ffn1_w_gate, ffn1_w_up, ffn1_w_down, ln1_g, ln1_b, w_in, mla_q_norm_g, mla_w_uq, mla_kv_norm_g, mla_w_ukv, mla_w_o, conv_w, conv_b, conv_w_out, s5_a_re, s5_a_im, s5_log_dt, s5_b_re, s5_b_im, s5_c_re, s5_c_im, s5_d, s5_w_glu, s5_b_glu, s5_w_out, w_o, ln2_g, ln2_b, ffn2_w_gate, ffn2_w_up, ffn2_w_down, ln3_g, ln3_b, loss_target, m_meta, m_ffn1_w_gate, m_ffn1_w_up, m_ffn1_w_down, m_ln1_g, m_ln1_b, m_w_in, m_mla_q_norm_g, m_mla_w_uq, m_mla_kv_norm_g, m_mla_w_ukv, m_mla_w_o, m_conv_w, m_conv_b, m_conv_w_out, m_s5_a_re, m_s5_a_im, m_s5_log_dt, m_s5_b_re, m_s5_b_im, m_s5_c_re, m_s5_c_im, m_s5_d, m_s5_w_glu, m_s5_b_glu, m_s5_w_out, m_w_o, m_ln2_g, m_ln2_b, m_ffn2_w_gate, m_ffn2_w_up, m_ffn2_w_down, m_ln3_g, m_ln3_b, v_meta, v_ffn1_w_gate, v_ffn1_w_up, v_ffn1_w_down, v_ln1_g, v_ln1_b, v_w_in, v_mla_q_norm_g, v_mla_w_uq, v_mla_kv_norm_g, v_mla_w_ukv, v_mla_w_o, v_conv_w, v_conv_b, v_conv_w_out, v_s5_a_re, v_s5_a_im, v_s5_log_dt, v_s5_b_re, v_s5_b_im, v_s5_c_re, v_s5_c_im, v_s5_d, v_s5_w_glu, v_s5_b_glu, v_s5_w_out, v_w_o, v_ln2_g, v_ln2_b, v_ffn2_w_gate, v_ffn2_w_up, v_ffn2_w_down, v_ln3_g, v_ln3_b):
    args = locals()
    w = {n: args[n] for n in WEIGHT_NAMES}
    m = {n: args["m_" + n] for n in WEIGHT_NAMES}
    v = {n: args["v_" + n] for n in WEIGHT_NAMES}
    me = 4 * lax.axis_index("x") + 2 * lax.axis_index("y") + lax.axis_index("c")
    seq = x.shape[1]
    n_real = N_META + seq
    length = -(-n_real // LANES) * LANES

    shard_shapes = {n: w[n].shape for n in BIG}
    big = [{} for _ in range(DEPTH)]
    for l in range(DEPTH):
        for p, groups in enumerate(PIECES):
            gathered = _all_gather(_pack_groups(w, l, groups, BF16), f"gather_weights_layer{l}_piece{p}", sequencer=True)
            big[l].update(_assemble(gathered, groups, shard_shapes))
    small_shards = _all_gather([_pack_rows([w[n] for n in SMALL_SHARDED])], "gather_small")[0].reshape(N_DEV, -1)
    meta_full = _full_weight(small_shards[:, :meta.size].reshape(N_DEV, 1, *meta.shape), 2)[0]
    conv_w_full = _full_weight(small_shards[:, meta.size:meta.size + conv_w.size].reshape(N_DEV, *conv_w.shape), 2)

    small = {n: w[n] for n in SMALL_NAMES if n not in SMALL_SHARDED}
    small['conv_w_full'] = conv_w_full
    small['meta'] = meta_full
    wz = jax.tree.map(lambda t: jnp.zeros(t.shape, BF16), big)
    pad_rows = length - n_real

    def loss_fn(diff):
        h0 = jnp.concatenate([diff['small']['meta'], diff['x'], jnp.zeros((pad_rows, D_MODEL), F32)], axis=0)
        target = jnp.pad(loss_target[0], ((N_META, pad_rows), (0, 0)))
        return _local_loss(dict(h0=h0, small=diff['small'], wz=diff['wz'], target=target), big, n_real)

    loss_local, grads = jax.value_and_grad(loss_fn)(dict(x=x[0], small=small, wz=wz))

    small_names = [n for n in SMALL_NAMES if n not in SMALL_SHARDED] + ['conv_w_full', 'meta']
    small_flat = _pack_rows([loss_local.reshape(1)] + [grads['small'][n] for n in small_names])
    rows_each = -(-small_flat.shape[0] // (8 * N_DEV)) * 8
    small_flat = jnp.pad(small_flat, ((0, rows_each * N_DEV - small_flat.shape[0]), (0, 0)))
    layer_sums = [{} for _ in range(DEPTH)]
    for l in reversed(range(DEPTH)):
        for p, groups in reversed(list(enumerate(PIECES))):
            last = l == 0 and p == 0
            extra = [small_flat.reshape(N_DEV, rows_each, PACK_COLS)] if last else []
            sums = _reduce_scatter(_disassemble(grads['wz'][l], groups, BF16) + extra, f"grads_layer{l}_piece{p}_",
                                   sequencer=True)
            layer_sums[l].update(_unpack_groups(sums[:len(groups)], groups, shard_shapes))
            if last:
                small_sum = sums[-1]
    small_all, = _all_gather([small_sum], "gather_small_grads")
    loss, *small_sums = _unpack_rows(small_all, [()] + [grads['small'][n].shape for n in small_names])
    small_grads = dict(zip(small_names, small_sums))
    g = {}
    for name in SMALL_NAMES:
        if name == 'meta':
            g[name] = lax.dynamic_slice_in_dim(small_grads['meta'], me * meta.shape[1], meta.shape[1], axis=1)
        elif name == 'conv_w':
            g[name] = lax.dynamic_slice_in_dim(small_grads['conv_w_full'], me * conv_w.shape[2], conv_w.shape[2], axis=2)
        else:
            g[name] = small_grads[name]

    delta, new_m, new_v = {}, {}, {}
    for name in BIG:
        per_layer = [s[name].T if name in TRANSPOSED else s[name] for s in layer_sums]
        g[name], delta[name], new_m[name], new_v[name] = _adamw_layers(w[name], per_layer, m[name], v[name], f"adamw_{name}")
    shapes = [w[n].shape for n in SMALL_NAMES]
    d, nm, nv = _adamw(*[_pack_rows([t[n] for n in SMALL_NAMES]) for t in (w, g, m, v)], "adamw_small")
    for out, buf in ((delta, d), (new_m, nm), (new_v, nv)):
        out.update(zip(SMALL_NAMES, _unpack_rows(buf, shapes)))

    return (loss, grads['x'][None], *[g[n] for n in WEIGHT_NAMES], *[delta[n] for n in WEIGHT_NAMES],
            *[new_m[n] for n in WEIGHT_NAMES], *[new_v[n] for n in WEIGHT_NAMES])
```

```python
import functools
import math

import jax
import jax.numpy as jnp
import numpy as np
from jax import lax
from jax.experimental import pallas as pl
from jax.experimental.pallas import tpu as pltpu
from jax.experimental.pallas import tpu_sc as plsc

F32 = jnp.float32
BF16 = jnp.bfloat16

D_MODEL = 1024
DEPTH = 2
N_META = 16
HEADS = 8
V_DIM = 64
NOPE = 64
ROPE = 32
HALF_ROPE = ROPE // 2
QK_DIM = NOPE + ROPE
Q_RANK = 384
KV_RANK = 256
MIX = 512
CONV_K = 3
S5_GROUPS = 32
S5_GROUP = 16
S5_STATE = 64
S5_CH = S5_GROUPS * S5_STATE
D_FF = 2816
ALPHA = (2.0 * DEPTH) ** 0.25
LN_EPS = 1e-5
RMS_EPS = 1e-6
ROPE_BASE = 10000.0
IN_SPLITS = (Q_RANK, KV_RANK, ROPE, MIX, MIX, MIX, MIX, 3 * D_MODEL)
D_IN = sum(IN_SPLITS)
ADAM_LR, ADAM_B1, ADAM_B2, ADAM_EPS, ADAM_WD, ADAM_STEP = 0.001, 0.9, 0.999, 1e-08, 0.01, 10

N_DEV = 8
AXES = ("x", "y", "c")
LANES = 128
PACK_COLS = 1024
HEAD_PAD = 128
VMEM_LIMIT = 48 * 1024 * 1024

PIN_CQ, PIN_CKV, PIN_KR1, PIN_KR2, PIN_XBAR, PIN_BG, PIN_CG, PIN_U, PIN_GATES, PIN_END = (
    0, 384, 640, 768, 896, 1408, 1920, 2432, 2944, 6016)
D_IN_PAD = 6144

WEIGHT_NAMES = ['meta', 'ffn1_w_gate', 'ffn1_w_up', 'ffn1_w_down', 'ln1_g', 'ln1_b', 'w_in', 'mla_q_norm_g', 'mla_w_uq',
                'mla_kv_norm_g', 'mla_w_ukv', 'mla_w_o', 'conv_w', 'conv_b', 'conv_w_out', 's5_a_re', 's5_a_im',
                's5_log_dt', 's5_b_re', 's5_b_im', 's5_c_re', 's5_c_im', 's5_d', 's5_w_glu', 's5_b_glu', 's5_w_out',
                'w_o', 'ln2_g', 'ln2_b', 'ffn2_w_gate', 'ffn2_w_up', 'ffn2_w_down', 'ln3_g', 'ln3_b']
BIG = {'ffn1_w_gate': 2, 'ffn1_w_up': 2, 'ffn1_w_down': 1, 'w_in': 2, 'mla_w_uq': 2, 'mla_w_ukv': 2, 'mla_w_o': 2,
       'conv_w_out': 2, 's5_w_glu': 1, 's5_w_out': 2, 'w_o': 1, 'ffn2_w_gate': 2, 'ffn2_w_up': 2, 'ffn2_w_down': 1}
SMALL_SHARDED = ('meta', 'conv_w')
SMALL_NAMES = [n for n in WEIGHT_NAMES if n not in BIG]


def _divisor_tile(n, limit, mult):
    best = None
    for t in range(mult, min(n, limit) + 1, mult):
        if n % t == 0:
            best = t
    return best if best is not None else n


def _params(*sem):
    return pltpu.CompilerParams(dimension_semantics=sem, vmem_limit_bytes=VMEM_LIMIT)


def _matmul(a, b, *, ta=False, tb=False, out_dtype=F32, add=None, name):
    m, k = (a.shape[1], a.shape[0]) if ta else a.shape
    n = b.shape[0] if tb else b.shape[1]
    assert (b.shape[1] if tb else b.shape[0]) == k, (a.shape, b.shape, ta, tb)
    tm = _divisor_tile(m, 1408, LANES) if ta else _divisor_tile(m, 2176 if a.dtype == BF16 else 1088, 16)
    tn = _divisor_tile(n, 512, LANES)
    tk = _divisor_tile(k, 1408, 16 if ta else LANES)
    nk = k // tk
    dims = (((0 if ta else 1,), (1 if tb else 0,)), ((), ()))

    in_place = jnp.dtype(out_dtype) == jnp.dtype(F32)

    def kern(a_ref, b_ref, *rest):
        add_ref = rest[0] if add is not None else None
        o_ref, *scratch = rest[1:] if add is not None else rest
        kk = pl.program_id(2)
        part = lax.dot_general(a_ref[...].astype(BF16), b_ref[...].astype(BF16), dims, preferred_element_type=F32)
        first = lambda: part if add_ref is None else part + add_ref[...].astype(F32)
        if nk == 1:
            o_ref[...] = first().astype(o_ref.dtype)
            return
        acc_ref = o_ref if in_place else scratch[0]

        @pl.when(kk == 0)
        def _():
            acc_ref[...] = first()

        @pl.when(kk > 0)
        def _():
            acc_ref[...] += part

        if not in_place:
            @pl.when(kk == nk - 1)
            def _():
                o_ref[...] = acc_ref[...].astype(o_ref.dtype)

    a_spec = pl.BlockSpec((tk, tm), lambda i, j, kk: (kk, i)) if ta else pl.BlockSpec((tm, tk), lambda i, j, kk: (i, kk))
    b_spec = pl.BlockSpec((tn, tk), lambda i, j, kk: (j, kk)) if tb else pl.BlockSpec((tk, tn), lambda i, j, kk: (kk, j))
    o_spec = pl.BlockSpec((tm, tn), lambda i, j, kk: (i, j))
    return pl.pallas_call(
        kern, name=name, grid=(m // tm, n // tn, nk),
        in_specs=[a_spec, b_spec] + ([o_spec] if add is not None else []), out_specs=o_spec,
        out_shape=jax.ShapeDtypeStruct((m, n), out_dtype),
        scratch_shapes=[] if (nk == 1 or in_place) else [pltpu.VMEM((tm, tn), F32)],
        compiler_params=_params("parallel", "parallel", "arbitrary"),
    )(a, b, *([add] if add is not None else []))


def _make_mm(name, wt=False):
    @jax.custom_vjp
    def mm(x, w, wz):
        return _matmul(x, w, tb=wt, name=name + "_fwd")

    def fwd(x, w, wz):
        return _matmul(x, w, tb=wt, name=name + "_fwd"), (x, w)

    def bwd(res, dy):
        x, w = res
        wz_dtype = BF16
        dx = _matmul(dy, w, tb=not wt, out_dtype=x.dtype, name=name + "_dx")
        dw = (_matmul(dy, x, ta=True, out_dtype=wz_dtype, name=name + "_dw") if wt
              else _matmul(x, dy, ta=True, out_dtype=wz_dtype, name=name + "_dw"))
        return dx, jnp.zeros_like(w), dw

    mm.defvjp(fwd, bwd)
    return mm


def _make_mm_f32w(name):
    @jax.custom_vjp
    def mm(x, w):
        return _matmul(x, w, name=name + "_fwd")

    def fwd(x, w):
        return _matmul(x, w, name=name + "_fwd"), (x, w)

    def bwd(res, dy):
        x, w = res
        return (_matmul(dy, w, tb=True, out_dtype=x.dtype, name=name + "_dx"),
                _matmul(x, dy, ta=True, name=name + "_dw"))

    mm.defvjp(fwd, bwd)
    return mm


BD_BLOCKS = 4
BD_PARTS = 2


def _bd_call(a, b, out_shape, a_blk, b_blk, o_blk, a_idx, b_idx, o_idx, dims, reduce_parts, name):
    def kern(a_ref, b_ref, o_ref):
        part = lax.dot_general(a_ref[...].astype(BF16), b_ref[...].astype(BF16), dims, preferred_element_type=F32)
        if not reduce_parts:
            o_ref[...] = part.astype(o_ref.dtype)
            return

        @pl.when(pl.program_id(1) == 0)
        def _():
            o_ref[...] = part

        @pl.when(pl.program_id(1) > 0)
        def _():
            o_ref[...] += part

    return pl.pallas_call(
        kern, name=name, grid=(BD_BLOCKS, BD_PARTS),
        in_specs=[pl.BlockSpec(a_blk, a_idx), pl.BlockSpec(b_blk, b_idx)], out_specs=pl.BlockSpec(o_blk, o_idx),
        out_shape=jax.ShapeDtypeStruct(out_shape, F32),
        compiler_params=_params("parallel", "arbitrary" if reduce_parts else "parallel"),
    )(a, b)


def _make_bd_in(name):
    wide = lambda j, p: (0, BD_BLOCKS * p + j)
    thin = lambda j, p: (0, j)
    diag = lambda j, p: (j, BD_BLOCKS * p + j)
    nn, nt, tn = (((1,), (0,)), ((), ())), (((1,), (1,)), ((), ())), (((0,), (0,)), ((), ()))

    def run(x, w):
        length, cols = x.shape[0], w.shape[1] // (BD_BLOCKS * BD_PARTS)
        return _bd_call(x, w, (length, w.shape[1]), (length, LANES), (LANES, cols), (length, cols), thin, diag, wide,
                        nn, False, name + "_fwd")

    @jax.custom_vjp
    def mm(x, w):
        return run(x, w)

    def fwd(x, w):
        return run(x, w), (x, w)

    def bwd(res, dy):
        x, w = res
        length, cols = x.shape[0], w.shape[1] // (BD_BLOCKS * BD_PARTS)
        dx = _bd_call(dy, w, x.shape, (length, cols), (LANES, cols), (length, LANES), wide, diag, thin, nt, True,
                      name + "_dx")
        dw = _bd_call(x, dy, w.shape, (length, LANES), (length, cols), (LANES, cols), thin, wide, diag, tn, False,
                      name + "_dw")
        return dx, dw

    mm.defvjp(fwd, bwd)
    return mm


def _make_bd_out(name):
    wide = lambda j, p: (0, BD_BLOCKS * p + j)
    thin = lambda j, p: (0, j)
    diag = lambda j, p: (BD_BLOCKS * p + j, j)
    nn, nt, tn = (((1,), (0,)), ((), ())), (((1,), (1,)), ((), ())), (((0,), (0,)), ((), ()))

    def run(x, w):
        length, cols = x.shape[0], w.shape[0] // (BD_BLOCKS * BD_PARTS)
        return _bd_call(x, w, (length, w.shape[1]), (length, cols), (cols, LANES), (length, LANES), wide, diag, thin,
                        nn, True, name + "_fwd")

    @jax.custom_vjp
    def mm(x, w):
        return run(x, w)

    def fwd(x, w):
        return run(x, w), (x, w)

    def bwd(res, dy):
        x, w = res
        length, cols = x.shape[0], w.shape[0] // (BD_BLOCKS * BD_PARTS)
        dx = _bd_call(dy, w, x.shape, (length, LANES), (cols, LANES), (length, cols), thin, diag, wide, nt, False,
                      name + "_dx")
        dw = _bd_call(x, dy, w.shape, (length, cols), (length, LANES), (cols, LANES), wide, thin, diag, tn, False,
                      name + "_dw")
        return dx, dw

    mm.defvjp(fwd, bwd)
    return mm


def _row_tile(rows, widths):
    limit = max(16, (6 * 1024 * 1024 // 4) // max(1, sum(widths)))
    return _divisor_tile(rows, limit, 16)


def _make_rowwise(f, n_rows, n_pars, out_dtypes, name, n_nodiff=0, grad_dtypes=None):
    n_out = len(out_dtypes)
    n_diff = n_rows - n_nodiff

    def run_fwd(rows, pars):
        length = rows[0].shape[0]
        shapes = jax.eval_shape(lambda *a: f(*a), *[jax.ShapeDtypeStruct((16, r.shape[1]), F32) for r in rows],
                                *[jax.ShapeDtypeStruct(p.shape, F32) for p in pars])
        widths = [s.shape[1] for s in shapes]
        tm = _row_tile(length, [r.shape[1] for r in rows] + widths)

        def kern(*refs):
            ins = [r[...].astype(F32) for r in refs[:n_rows + n_pars]]
            outs = f(*ins)
            for o_ref, o in zip(refs[n_rows + n_pars:], outs):
                o_ref[...] = o.astype(o_ref.dtype)

        return pl.pallas_call(
            kern, name=name + "_fwd", grid=(length // tm,),
            in_specs=[pl.BlockSpec((tm, r.shape[1]), lambda i: (i, 0)) for r in rows]
            + [pl.BlockSpec(p.shape, lambda i: (0, 0)) for p in pars],
            out_specs=[pl.BlockSpec((tm, w), lambda i: (i, 0)) for w in widths],
            out_shape=[jax.ShapeDtypeStruct((length, w), dt) for w, dt in zip(widths, out_dtypes)],
            compiler_params=_params("parallel"),
        )(*rows, *pars)

    def run_bwd(rows, pars, cts):
        length = rows[0].shape[0]
        tm = _row_tile(length, [r.shape[1] for r in rows] * 2 + [c.shape[1] for c in cts] * 2)

        def kern(*refs):
            ins = [r[...].astype(F32) for r in refs[:n_rows + n_pars]]
            ct = [r[...].astype(F32) for r in refs[n_rows + n_pars:n_rows + n_pars + n_out]]
            out_refs = refs[n_rows + n_pars + n_out:]
            nodiff = ins[n_diff:n_rows]
            _, vjp = jax.vjp(lambda *a: f(*a[:n_diff], *nodiff, *a[n_diff:]), *ins[:n_diff], *ins[n_rows:])
            grads = vjp(tuple(ct))
            for o_ref, g in zip(out_refs[:n_diff], grads[:n_diff]):
                o_ref[...] = g.astype(o_ref.dtype)
            first = pl.program_id(0) == 0
            for o_ref, g in zip(out_refs[n_diff:], grads[n_diff:]):
                @pl.when(first)
                def _(o_ref=o_ref, g=g):
                    o_ref[...] = g

                @pl.when(jnp.logical_not(first))
                def _(o_ref=o_ref, g=g):
                    o_ref[...] += g

        return pl.pallas_call(
            kern, name=name + "_bwd", grid=(length // tm,),
            in_specs=[pl.BlockSpec((tm, r.shape[1]), lambda i: (i, 0)) for r in rows]
            + [pl.BlockSpec(p.shape, lambda i: (0, 0)) for p in pars]
            + [pl.BlockSpec((tm, c.shape[1]), lambda i: (i, 0)) for c in cts],
            out_specs=[pl.BlockSpec((tm, r.shape[1]), lambda i: (i, 0)) for r in rows[:n_diff]]
            + [pl.BlockSpec(p.shape, lambda i: (0, 0)) for p in pars],
            out_shape=[jax.ShapeDtypeStruct(r.shape, r.dtype if grad_dtypes is None else grad_dtypes[i])
                       for i, r in enumerate(rows[:n_diff])]
            + [jax.ShapeDtypeStruct(p.shape, F32) for p in pars],
            compiler_params=_params("arbitrary"),
        )(*rows, *pars, *cts)

    @jax.custom_vjp
    def op(*args):
        return tuple(run_fwd(args[:n_rows], args[n_rows:]))

    def fwd(*args):
        return tuple(run_fwd(args[:n_rows], args[n_rows:])), args

    def bwd(args, cts):
        grads = run_bwd(args[:n_rows], args[n_rows:], cts)
        zeros = [jnp.zeros_like(r) for r in args[n_diff:n_rows]]
        return (*grads[:n_diff], *zeros, *grads[n_diff:])

    op.defvjp(fwd, bwd)
    op.run_fwd, op.run_bwd = run_fwd, run_bwd
    return op


def _layer_norm(z, g, b):
    mu = jnp.mean(z, axis=-1, keepdims=True)
    d = z - mu
    var = jnp.mean(d * d, axis=-1, keepdims=True)
    return d * lax.rsqrt(var + LN_EPS) * g + b


def _f_ln_half(h, f, g, b):
    return (_layer_norm(ALPHA * h + 0.5 * f, g, b),)


def _f_ln_full(h, f, g, b):
    return (_layer_norm(ALPHA * h + f, g, b),)


def _f_rms(x, g):
    return (x * lax.rsqrt(jnp.mean(x * x, axis=-1, keepdims=True) + RMS_EPS) * g,)


def _f_rope(x1, x2, cos, sin):
    return x1 * cos - x2 * sin, x2 * cos + x1 * sin


def _f_gelu_skip(y, u, d):
    return (jax.nn.gelu(y + d * u),)


def _f_glu(z, t, b):
    return (z * jax.nn.sigmoid(t + b),)


def _f_merge(ga, gb, gc, ya, yb, yc):
    return (jax.nn.sigmoid(ga) * ya + jax.nn.sigmoid(gb) * yb + jax.nn.sigmoid(gc) * yc,)


def _f_swiglu(gate, up):
    return (jax.nn.silu(gate) * up,)


def _f_cast(x):
    return (x,)


def _make_ffn(tag):
    ln = _make_rowwise(_f_ln_half, 2, 2, (F32,), f"{tag}_ln", grad_dtypes=(F32, BF16))
    cast = _make_rowwise(_f_cast, 1, 0, (BF16,), f"{tag}_cast")
    swiglu = _make_rowwise(_f_swiglu, 2, 0, (BF16,), f"{tag}_swiglu", grad_dtypes=(BF16, BF16))

    def forward(h, w_gate, w_up, w_down, g, b):
        hb, = cast.run_fwd((h,), ())
        gate = _matmul(hb, w_gate, tb=True, name=f"{tag}_gate_fwd")
        up = _matmul(hb, w_up, tb=True, name=f"{tag}_up_fwd")
        act, = swiglu.run_fwd((gate, up), ())
        f = _matmul(act, w_down, name=f"{tag}_down_fwd")
        y, = ln.run_fwd((h, f), (g, b))
        return y, (h, hb, gate, up, act, f, w_gate, w_up, w_down, g, b)

    @jax.custom_vjp
    def block(h, w_gate, w_up, w_down, z_gate, z_up, z_down, g, b):
        return forward(h, w_gate, w_up, w_down, g, b)[0]

    def fwd(h, w_gate, w_up, w_down, z_gate, z_up, z_down, g, b):
        return forward(h, w_gate, w_up, w_down, g, b)

    def bwd(res, dy):
        h, hb, gate, up, act, f, w_gate, w_up, w_down, g, b = res
        dh, df, dg, db = ln.run_bwd((h, f), (g, b), (dy,))
        dact = _matmul(df, w_down, tb=True, out_dtype=BF16, name=f"{tag}_down_dx")
        dw_down = _matmul(act, df, ta=True, out_dtype=BF16, name=f"{tag}_down_dw")
        dgate, dup = swiglu.run_bwd((gate, up), (), (dact,))
        dw_gate = _matmul(dgate, hb, ta=True, out_dtype=BF16, name=f"{tag}_gate_dw")
        dw_up = _matmul(dup, hb, ta=True, out_dtype=BF16, name=f"{tag}_up_dw")
        dh = _matmul(dgate, w_gate, add=dh, name=f"{tag}_gate_dx")
        dh = _matmul(dup, w_up, add=dh, name=f"{tag}_up_dx")
        zero = jnp.zeros_like
        return dh, zero(w_gate), zero(w_up), zero(w_down), dw_gate, dw_up, dw_down, dg, db

    block.defvjp(fwd, bwd)
    return block


def _attn_scores(q, k, q_block, tq):
    length = k.shape[0]
    s = lax.dot_general(q, k, (((1,), (1,)), ((), ())), preferred_element_type=F32) * (QK_DIM ** -0.5)
    row = q_block * tq + lax.broadcasted_iota(jnp.int32, (tq, length), 0)
    col = lax.broadcasted_iota(jnp.int32, (tq, length), 1)
    s = jnp.where(col <= row, s, -1e30)
    e = jnp.exp(s - jnp.max(s, axis=1, keepdims=True))
    return e * (1.0 / jnp.sum(e, axis=1, keepdims=True))


ATTN_SEGMENTS = 4


def _attn_tiles(length):
    seg = length // ATTN_SEGMENTS
    return seg, _divisor_tile(seg, 272, 16)


def _attn_fwd(q3, k3, v3):
    heads, length, _ = q3.shape
    seg, tq = _attn_tiles(length)
    outs = []
    for s in range(ATTN_SEGMENTS):
        kmax, base = (s + 1) * seg, s * (seg // tq)

        def kern(q_ref, k_ref, v_ref, o_ref, base=base):
            p = _attn_scores(q_ref[0], k_ref[0], base + pl.program_id(1), tq)
            o_ref[0] = jnp.dot(p.astype(BF16), v_ref[0], preferred_element_type=F32).astype(o_ref.dtype)

        outs.append(pl.pallas_call(
            kern, name=f"attn_fwd_seg{s}", grid=(heads, seg // tq),
            in_specs=[pl.BlockSpec((1, tq, HEAD_PAD), lambda h, i, base=base: (h, base + i, 0)),
                      pl.BlockSpec((1, kmax, HEAD_PAD), lambda h, i: (h, 0, 0)),
                      pl.BlockSpec((1, kmax, V_DIM), lambda h, i: (h, 0, 0))],
            out_specs=pl.BlockSpec((1, tq, V_DIM), lambda h, i: (h, i, 0)),
            out_shape=jax.ShapeDtypeStruct((heads, seg, V_DIM), F32),
            compiler_params=_params("parallel", "parallel"),
        )(q3, k3, v3))
    return jnp.concatenate(outs, axis=1)


def _attn_bwd(q3, k3, v3, do3):
    heads, length, _ = q3.shape
    seg, tq = _attn_tiles(length)
    dk = jnp.zeros((heads, length, HEAD_PAD), F32)
    dv = jnp.zeros((heads, length, V_DIM), F32)
    dqs = [None] * ATTN_SEGMENTS
    for s in reversed(range(ATTN_SEGMENTS)):
        kmax, base = (s + 1) * seg, s * (seg // tq)

        def kern(q_ref, k_ref, v_ref, do_ref, dk_in, dv_in, dq_ref, dk_ref, dv_ref, base=base):
            i = pl.program_id(1)
            q, k, v, do = q_ref[0], k_ref[0], v_ref[0], do_ref[0].astype(BF16)
            p = _attn_scores(q, k, base + i, tq)
            dp = lax.dot_general(do, v, (((1,), (1,)), ((), ())), preferred_element_type=F32)
            ds = (p * (dp - jnp.sum(p * dp, axis=1, keepdims=True)) * (QK_DIM ** -0.5)).astype(BF16)
            dq_ref[0] = jnp.dot(ds, k, preferred_element_type=F32)
            dk_part = lax.dot_general(ds, q, (((0,), (0,)), ((), ())), preferred_element_type=F32)
            dv_part = lax.dot_general(p.astype(BF16), do, (((0,), (0,)), ((), ())), preferred_element_type=F32)

            @pl.when(i == 0)
            def _():
                dk_ref[0] = dk_in[0] + dk_part
                dv_ref[0] = dv_in[0] + dv_part

            @pl.when(i > 0)
            def _():
                dk_ref[0] += dk_part
                dv_ref[0] += dv_part

        q_blk = pl.BlockSpec((1, tq, HEAD_PAD), lambda h, i, base=base: (h, base + i, 0))
        k_blk = pl.BlockSpec((1, kmax, HEAD_PAD), lambda h, i: (h, 0, 0))
        v_blk = pl.BlockSpec((1, kmax, V_DIM), lambda h, i: (h, 0, 0))
        dqs[s], dk, dv = pl.pallas_call(
            kern, name=f"attn_bwd_seg{s}", grid=(heads, seg // tq),
            in_specs=[q_blk, k_blk, v_blk, pl.BlockSpec((1, tq, V_DIM), lambda h, i, base=base: (h, base + i, 0)),
                      k_blk, v_blk],
            out_specs=[pl.BlockSpec((1, tq, HEAD_PAD), lambda h, i: (h, i, 0)), k_blk, v_blk],
            out_shape=[jax.ShapeDtypeStruct((heads, seg, HEAD_PAD), F32), jax.ShapeDtypeStruct(dk.shape, F32),
                       jax.ShapeDtypeStruct(dv.shape, F32)],
            input_output_aliases={4: 1, 5: 2}, compiler_params=_params("parallel", "arbitrary"),
        )(q3, k3, v3, do3, dk, dv)
    return jnp.concatenate(dqs, axis=1), dk, dv


@jax.custom_vjp
def _attention(q3, k3, v3):
    return _attn_fwd(q3, k3, v3)


def _attention_fwd(q3, k3, v3):
    return _attn_fwd(q3, k3, v3), (q3, k3, v3)


def _attention_bwd(res, do3):
    q3, k3, v3 = res
    dq, dk, dv = _attn_bwd(q3, k3, v3, do3)
    return dq.astype(q3.dtype), dk.astype(k3.dtype), dv.astype(v3.dtype)


_attention.defvjp(_attention_fwd, _attention_bwd)


def _conv_terms(x, c, w_ref, cb):
    u = c * x
    row = lax.broadcasted_iota(jnp.int32, u.shape, 0)
    u1 = jnp.where(row >= 1, pltpu.roll(u, 1, 0), 0.0)
    u2 = jnp.where(row >= 2, pltpu.roll(u, 2, 0), 0.0)
    y = cb + w_ref[0:1, :] * u2 + w_ref[1:2, :] * u1 + w_ref[2:3, :] * u
    return u, u1, u2, y


def _conv_specs(length):
    col = pl.BlockSpec((length, LANES), lambda j: (0, j))
    return col, pl.BlockSpec((CONV_K, LANES), lambda j: (0, j)), pl.BlockSpec((1, LANES), lambda j: (0, j))


def _conv_fwd(x, b, c, w, cb):
    length = x.shape[0]

    def kern(x_ref, b_ref, c_ref, w_ref, cb_ref, o_ref):
        _, _, _, y = _conv_terms(x_ref[...], c_ref[...], w_ref, cb_ref[...])
        o_ref[...] = b_ref[...] * y

    col, wspec, bspec = _conv_specs(length)
    return pl.pallas_call(
        kern, name="conv_fwd", grid=(MIX // LANES,), in_specs=[col, col, col, wspec, bspec], out_specs=col,
        out_shape=jax.ShapeDtypeStruct((length, MIX), F32), compiler_params=_params("parallel"),
    )(x, b, c, w, cb)


def _conv_bwd(x, b, c, w, cb, do):
    length = x.shape[0]

    def kern(x_ref, b_ref, c_ref, w_ref, cb_ref, do_ref, dx_ref, db_ref, dc_ref, dw_ref, dcb_ref):
        x, c, do = x_ref[...], c_ref[...], do_ref[...]
        u, u1, u2, y = _conv_terms(x, c, w_ref, cb_ref[...])
        db_ref[...] = do * y
        dy = do * b_ref[...]
        row = lax.broadcasted_iota(jnp.int32, dy.shape, 0)
        dy1 = jnp.where(row < length - 1, pltpu.roll(dy, length - 1, 0), 0.0)
        dy2 = jnp.where(row < length - 2, pltpu.roll(dy, length - 2, 0), 0.0)
        du = w_ref[2:3, :] * dy + w_ref[1:2, :] * dy1 + w_ref[0:1, :] * dy2
        dx_ref[...] = du * c
        dc_ref[...] = du * x
        dw_ref[0:1, :] = jnp.sum(dy * u2, axis=0, keepdims=True)
        dw_ref[1:2, :] = jnp.sum(dy * u1, axis=0, keepdims=True)
        dw_ref[2:3, :] = jnp.sum(dy * u, axis=0, keepdims=True)
        dcb_ref[...] = jnp.sum(dy, axis=0, keepdims=True)

    col, wspec, bspec = _conv_specs(length)
    big = jax.ShapeDtypeStruct((length, MIX), F32)
    return pl.pallas_call(
        kern, name="conv_bwd", grid=(MIX // LANES,), in_specs=[col, col, col, wspec, bspec, col],
        out_specs=[col, col, col, wspec, bspec],
        out_shape=[big, big, big, jax.ShapeDtypeStruct((CONV_K, MIX), F32), jax.ShapeDtypeStruct((1, MIX), F32)],
        compiler_params=_params("parallel"),
    )(x, b, c, w, cb, do)


@jax.custom_vjp
def _short_conv(x, b, c, w, cb):
    return _conv_fwd(x, b, c, w, cb)


def _short_conv_fwd(x, b, c, w, cb):
    return _conv_fwd(x, b, c, w, cb), (x, b, c, w, cb)


def _short_conv_bwd(res, do):
    return tuple(_conv_bwd(*res, do))


_short_conv.defvjp(_short_conv_fwd, _short_conv_bwd)


SCAN_ROWS = S5_CH // LANES
SCAN_TC = 136


def _scan_fwd(ar, ai, b):
    length = b.shape[0]
    tc = _divisor_tile(length, SCAN_TC, 8)

    def kern(ar_ref, ai_ref, b_ref, x_ref, sr, si):
        @pl.when(pl.program_id(0) == 0)
        def _():
            sr[...] = jnp.zeros_like(sr)
            si[...] = jnp.zeros_like(si)

        a_re, a_im = ar_ref[...], ai_ref[...]

        def body(t, carry):
            xr, xi = carry
            nr = a_re * xr - a_im * xi + b_ref[t, 0:SCAN_ROWS, :]
            ni = a_re * xi + a_im * xr + b_ref[t, SCAN_ROWS:2 * SCAN_ROWS, :]
            x_ref[t, 0:SCAN_ROWS, :] = nr
            x_ref[t, SCAN_ROWS:2 * SCAN_ROWS, :] = ni
            return nr, ni

        xr, xi = lax.fori_loop(0, tc, body, (sr[...], si[...]), unroll=4)
        sr[...] = xr
        si[...] = xi

    par = pl.BlockSpec((SCAN_ROWS, LANES), lambda i: (0, 0))
    blk = pl.BlockSpec((tc, 2 * SCAN_ROWS, LANES), lambda i: (i, 0, 0))
    return pl.pallas_call(
        kern, name="s5_scan_fwd", grid=(length // tc,), in_specs=[par, par, blk], out_specs=blk,
        out_shape=jax.ShapeDtypeStruct(b.shape, F32), scratch_shapes=[pltpu.VMEM((SCAN_ROWS, LANES), F32)] * 2,
        compiler_params=_params("arbitrary"),
    )(ar, ai, b)


def _scan_bwd(ar, ai, x, dx):
    length = x.shape[0]
    tc = _divisor_tile(length, SCAN_TC, 8)
    n_blk = length // tc
    re, im = slice(0, SCAN_ROWS), slice(SCAN_ROWS, 2 * SCAN_ROWS)

    def kern(ar_ref, ai_ref, x_ref, dx_ref, db_ref, dar_ref, dai_ref, lr_s, li_s):
        @pl.when(pl.program_id(0) == 0)
        def _():
            lr_s[...] = jnp.zeros_like(lr_s)
            li_s[...] = jnp.zeros_like(li_s)
            dar_ref[...] = jnp.zeros_like(dar_ref)
            dai_ref[...] = jnp.zeros_like(dai_ref)

        a_re, a_im = ar_ref[...], ai_ref[...]

        def body(j, carry):
            t = tc - 1 - j
            lr, li, gr, gi = carry
            x_re, x_im = x_ref[t, re, :], x_ref[t, im, :]
            gr = gr + (lr * x_re + li * x_im)
            gi = gi + (li * x_re - lr * x_im)
            nlr = dx_ref[t, re, :] + (a_re * lr + a_im * li)
            nli = dx_ref[t, im, :] + (a_re * li - a_im * lr)
            db_ref[t, re, :] = nlr
            db_ref[t, im, :] = nli
            return nlr, nli, gr, gi

        lr, li, gr, gi = lax.fori_loop(0, tc, body, (lr_s[...], li_s[...], dar_ref[...], dai_ref[...]), unroll=4)
        lr_s[...] = lr
        li_s[...] = li
        dar_ref[...] = gr
        dai_ref[...] = gi

    par = pl.BlockSpec((SCAN_ROWS, LANES), lambda i: (0, 0))
    blk = pl.BlockSpec((tc, 2 * SCAN_ROWS, LANES), lambda i: (n_blk - 1 - i, 0, 0))
    pout = jax.ShapeDtypeStruct((SCAN_ROWS, LANES), F32)
    return pl.pallas_call(
        kern, name="s5_scan_bwd", grid=(n_blk,), in_specs=[par, par, blk, blk],
        out_specs=[blk, par, par], out_shape=[jax.ShapeDtypeStruct(x.shape, F32), pout, pout],
        scratch_shapes=[pltpu.VMEM((SCAN_ROWS, LANES), F32)] * 2, compiler_params=_params("arbitrary"),
    )(ar, ai, x, dx)


@jax.custom_vjp
def _s5_scan(ar, ai, b):
    return _scan_fwd(ar, ai, b)


def _s5_scan_fwd(ar, ai, b):
    x = _scan_fwd(ar, ai, b)
    return x, (ar, ai, x)


def _s5_scan_bwd(res, dx):
    ar, ai, x = res
    db, dar, dai = _scan_bwd(ar, ai, x, dx)
    return dar, dai, db


_s5_scan.defvjp(_s5_scan_fwd, _s5_scan_bwd)


def _loss_call(y, target, n_real):
    length = y.shape[0]
    tm = _divisor_tile(length, 544, 16)

    def kern(y_ref, t_ref, loss_ref, dy_ref):
        i = pl.program_id(0)
        row = i * tm + lax.broadcasted_iota(jnp.int32, (tm, 1), 0)
        keep = jnp.logical_and(row >= N_META, row < n_real)
        err = jnp.where(keep, y_ref[...] - t_ref[...], 0.0)
        dy_ref[...] = err * (1.0 / D_MODEL)
        part = 0.5 * jnp.sum(jnp.mean(err * err, axis=-1, keepdims=True), axis=0, keepdims=True)

        @pl.when(i == 0)
        def _():
            loss_ref[...] = jnp.zeros_like(loss_ref)

        loss_ref[...] += part

    blk = pl.BlockSpec((tm, D_MODEL), lambda i: (i, 0))
    return pl.pallas_call(
        kern, name="loss_head", grid=(length // tm,), in_specs=[blk, blk],
        out_specs=[pl.BlockSpec((8, LANES), lambda i: (0, 0)), blk],
        out_shape=[jax.ShapeDtypeStruct((8, LANES), F32), jax.ShapeDtypeStruct(y.shape, F32)],
        compiler_params=_params("arbitrary"),
    )(y, target)


def _make_loss(n_real):
    @jax.custom_vjp
    def loss(y, target):
        return _loss_call(y, target, n_real)[0][0, 0]

    def fwd(y, target):
        total, dy = _loss_call(y, target, n_real)
        return total[0, 0], dy

    def bwd(dy, ct):
        return dy * ct, jnp.zeros_like(dy)

    loss.defvjp(fwd, bwd)
    return loss


HBM_SPEC = pl.BlockSpec(memory_space=pl.ANY)
MESH_ID = pl.DeviceIdType.MESH


SC_MESH = dict(axis_name="sequencer", num_cores=1)
GATHER_ID, SLOT_ID = 1, 2


def _handshake(peers):
    barrier = pltpu.get_barrier_semaphore()
    for peer in peers:
        pl.semaphore_signal(barrier, inc=1, device_id=peer, device_id_type=MESH_ID)
    pl.semaphore_wait(barrier, len(peers))


def _exchange_call(body, name, ins, out_types, n_sems, sequencer_id):
    n_in, n_out = len(ins), len(out_types)
    sems = [pltpu.SemaphoreType.DMA((n_sems,)), pltpu.SemaphoreType.DMA((n_sems,)), pltpu.SemaphoreType.DMA((n_in,))]
    if sequencer_id is None:
        def on_core(*refs):
            body(lambda peers: None, refs[:n_in], refs[n_in:n_in + n_out], *refs[n_in + n_out:])

        return pl.pallas_call(on_core, name=name, out_shape=out_types, in_specs=[HBM_SPEC] * n_in,
                              out_specs=[HBM_SPEC] * n_out, scratch_shapes=sems)(*ins)

    def on_sequencer(*refs):
        body(_handshake, refs[:n_in], refs[n_in:n_in + n_out], *refs[n_in + n_out:])

    return pl.kernel(on_sequencer, name=name, out_type=out_types, mesh=plsc.ScalarSubcoreMesh(**SC_MESH),
                     scratch_types=sems, compiler_params=pltpu.CompilerParams(collective_id=sequencer_id))(*ins)


def _all_gather(shards, name, sequencer=False):
    n = len(shards)

    def body(handshake, x_refs, out_refs, send_sems, recv_sems, local_sems):
        x, y, c = lax.axis_index("x"), lax.axis_index("y"), lax.axis_index("c")
        me, sibling = (x, y, c), (x, y, 1 - c)
        chips = [(1 - x, y), (x, 1 - y), (1 - x, 1 - y)]
        handshake([sibling] + [(*chip, c) for chip in chips])

        def copy(b, k, block, to, from_input=False):
            px, py, pc = block
            slot = out_refs[b].at[4 * px + 2 * py + pc]
            return pltpu.make_async_remote_copy(
                src_ref=x_refs[b] if from_input else slot, dst_ref=slot,
                send_sem=send_sems.at[7 * b + k], recv_sem=recv_sems.at[7 * b + k], device_id=to, device_id_type=MESH_ID)

        mine = [pltpu.make_async_copy(x_refs[b], out_refs[b].at[4 * x + 2 * y + c], local_sems.at[b]) for b in range(n)]
        for cp in mine:
            cp.start()
        first = []
        for b in range(n):
            first.append(copy(b, 0, me, sibling, from_input=True))
            first += [copy(b, 1 + j, me, (*chip, c), from_input=True) for j, chip in enumerate(chips)]
        for cp in first:
            cp.start()
        passed = []
        for j, chip in enumerate(chips):
            for b in range(n):
                copy(b, 1 + j, (*chip, c), me).wait_recv()
                passed.append(copy(b, 4 + j, (*chip, c), sibling))
                passed[-1].start()
        for b in range(n):
            copy(b, 0, sibling, me).wait_recv()
            for j, chip in enumerate(chips):
                copy(b, 4 + j, (*chip, 1 - c), me).wait_recv()
        for cp in first + passed:
            cp.wait_send()
        for cp in mine:
            cp.wait()

    out_types = [jax.ShapeDtypeStruct((N_DEV, *s.shape), s.dtype) for s in shards]
    return _exchange_call(body, name, shards, out_types, 7 * n, GATHER_ID if sequencer else None)


def _slot_exchange(bufs, name, sequencer=False):
    def body(handshake, ins, outs, send_sems, recv_sems, local_sems):
        x, y, c = lax.axis_index("x"), lax.axis_index("y"), lax.axis_index("c")
        me = 4 * x + 2 * y + c
        flips = [(dx, dy, dc) for dx in (0, 1) for dy in (0, 1) for dc in (0, 1)][1:]
        peers = [(1 - x if dx else x, 1 - y if dy else y, 1 - c if dc else c) for dx, dy, dc in flips]
        handshake(peers)
        own = [pltpu.make_async_copy(src.at[me], dst.at[me], local_sems.at[b]) for b, (src, dst) in enumerate(zip(ins, outs))]
        copies = []
        for b, (src, dst) in enumerate(zip(ins, outs)):
            for k, (px, py, pc) in enumerate(peers):
                copies.append(pltpu.make_async_remote_copy(
                    src_ref=src.at[4 * px + 2 * py + pc], dst_ref=dst.at[me],
                    send_sem=send_sems.at[7 * b + k], recv_sem=recv_sems.at[7 * b + k],
                    device_id=(px, py, pc), device_id_type=MESH_ID))
        for cp in own + copies:
            cp.start()
        for cp in copies + own:
            cp.wait()

    out_types = [jax.ShapeDtypeStruct(b.shape, b.dtype) for b in bufs]
    return _exchange_call(body, name, bufs, out_types, 7 * len(bufs), SLOT_ID if sequencer else None)


def _slot_sum(slots, name):
    _, rows, cols = slots.shape
    tm = _divisor_tile(rows, 512, 16)

    def kern(s_ref, o_ref):
        total = s_ref[0].astype(F32)
        for d in range(1, N_DEV):
            total = total + s_ref[d].astype(F32)
        o_ref[...] = total

    return pl.pallas_call(
        kern, name=name, grid=(rows // tm,), in_specs=[pl.BlockSpec((N_DEV, tm, cols), lambda i: (0, i, 0))],
        out_specs=pl.BlockSpec((tm, cols), lambda i: (i, 0)), out_shape=jax.ShapeDtypeStruct((rows, cols), F32),
        compiler_params=_params("parallel"),
    )(slots)


def _reduce_scatter(bufs, tag, sequencer):
    arrived = _slot_exchange(bufs, f"rs_exchange_{tag}", sequencer)
    return [_slot_sum(a, f"rs_sum_{tag}{i}") for i, a in enumerate(arrived)]


def _adamw(w, g, m, v, name):
    rows, cols = w.shape
    tm = _divisor_tile(rows, max(8, (512 * 1024) // cols // 8 * 8), 8)

    def kern(w_ref, g_ref, m_ref, v_ref, d_ref, nm_ref, nv_ref):
        g = g_ref[...]
        m = ADAM_B1 * m_ref[...] + (1.0 - ADAM_B1) * g
        v = ADAM_B2 * v_ref[...] + (1.0 - ADAM_B2) * (g * g)
        m_hat = m / (1.0 - ADAM_B1 ** ADAM_STEP)
        v_hat = v / (1.0 - ADAM_B2 ** ADAM_STEP)
        d_ref[...] = -ADAM_LR * (m_hat / (jnp.sqrt(v_hat) + ADAM_EPS) + ADAM_WD * w_ref[...])
        nm_ref[...] = m
        nv_ref[...] = v

    blk = pl.BlockSpec((tm, cols), lambda i: (i, 0))
    out = jax.ShapeDtypeStruct(w.shape, F32)
    return pl.pallas_call(
        kern, name=name, grid=(rows // tm,), in_specs=[blk] * 4, out_specs=[blk] * 3, out_shape=[out] * 3,
        compiler_params=_params("parallel"),
    )(w, g, m, v)


def _adamw_layers(w, g_layers, m, v, name):
    depth, rows, cols = w.shape
    tm = _divisor_tile(rows, max(8, (512 * 1024) // cols // 8 * 8), 8)

    def kern(w_ref, m_ref, v_ref, *refs):
        g_refs, (g_out, d_ref, nm_ref, nv_ref) = refs[:depth], refs[depth:]
        layer = pl.program_id(0)
        g = g_refs[0][...]
        for l in range(1, depth):
            g = jnp.where(layer == l, g_refs[l][...], g)
        m = ADAM_B1 * m_ref[0] + (1.0 - ADAM_B1) * g
        v = ADAM_B2 * v_ref[0] + (1.0 - ADAM_B2) * (g * g)
        m_hat = m / (1.0 - ADAM_B1 ** ADAM_STEP)
        v_hat = v / (1.0 - ADAM_B2 ** ADAM_STEP)
        g_out[0] = g
        d_ref[0] = -ADAM_LR * (m_hat / (jnp.sqrt(v_hat) + ADAM_EPS) + ADAM_WD * w_ref[0])
        nm_ref[0] = m
        nv_ref[0] = v

    blk = pl.BlockSpec((1, tm, cols), lambda l, i: (l, i, 0))
    out = jax.ShapeDtypeStruct(w.shape, F32)
    return pl.pallas_call(
        kern, name=name, grid=(depth, rows // tm),
        in_specs=[blk] * 3 + [pl.BlockSpec((tm, cols), lambda l, i: (i, 0))] * depth,
        out_specs=[blk] * 4, out_shape=[out] * 4, compiler_params=_params("parallel", "parallel"),
    )(w, m, v, *g_layers)


TRANSPOSED = ('ffn1_w_gate', 'ffn1_w_up', 'ffn2_w_gate', 'ffn2_w_up', 'w_in', 'mla_w_uq', 'mla_w_ukv', 'mla_w_o',
              'conv_w_out', 's5_w_out')
PIECES = ((('ffn1_w_gate',), ('ffn1_w_up',), ('ffn1_w_down',)),
          (('w_in', 'w_o'), ('mla_w_o', 'conv_w_out', 's5_w_out', 's5_w_glu'), ('mla_w_uq',), ('mla_w_ukv',)),
          (('ffn2_w_gate',), ('ffn2_w_up',), ('ffn2_w_down',)))
PIN_CUTS = (PIN_CQ, PIN_CKV, PIN_KR1, PIN_KR2, PIN_XBAR, PIN_BG, PIN_CG, PIN_U, PIN_GATES, PIN_GATES + D_MODEL,
            PIN_GATES + 2 * D_MODEL, PIN_END, D_IN_PAD)
PIN_PIECES = ((PIN_CQ, Q_RANK), (PIN_CKV, KV_RANK), (PIN_KR1, HALF_ROPE), (PIN_KR2, HALF_ROPE), (PIN_XBAR, 4 * MIX),
              (PIN_GATES, 3 * D_MODEL))


def _make_split(cuts):
    @jax.custom_vjp
    def split(t):
        return tuple(t[:, a:b] for a, b in zip(cuts[:-1], cuts[1:]))

    def fwd(t):
        return split(t), None

    def bwd(_, cts):
        return (jnp.concatenate(cts, axis=1),)

    split.defvjp(fwd, bwd)
    return split


def _make_projection(tag, cuts):
    cast = _make_rowwise(_f_cast, 1, 0, (BF16,), f"{tag}_cast")

    def forward(h, w):
        hb, = cast.run_fwd((h,), ())
        full = _matmul(hb, w, tb=True, name=f"{tag}_fwd")
        return tuple(full[:, a:b] for a, b in zip(cuts[:-1], cuts[1:])), (hb, w)

    @jax.custom_vjp
    def proj(h, w, wz):
        return forward(h, w)[0]

    def fwd(h, w, wz):
        return forward(h, w)

    def bwd(res, cts):
        hb, w = res
        d_full = jnp.concatenate([c.astype(BF16) for c in cts], axis=1)
        dw = _matmul(d_full, hb, ta=True, out_dtype=BF16, name=f"{tag}_dw")
        return _matmul(d_full, w, name=f"{tag}_dx"), jnp.zeros_like(w), dw

    proj.defvjp(fwd, bwd)
    return proj


def _travel_shape(name, shape):
    return (shape[2], shape[1]) if name in TRANSPOSED else (shape[1], shape[2])


def _pack_groups(tensors, layer, groups, dtype):
    def view(n):
        t = tensors[n][layer]
        return (t.T if n in TRANSPOSED else t).astype(dtype)
    return [jnp.concatenate([view(n) for n in grp], axis=0) for grp in groups]


def _unpack_groups(bufs, groups, shard_shapes):
    out = {}
    for buf, grp in zip(bufs, groups):
        at = 0
        for n in grp:
            r, _ = _travel_shape(n, shard_shapes[n])
            out[n] = buf[..., at:at + r, :]
            at += r
    return out


def _pack_rows(arrays):
    flat = jnp.concatenate([a.reshape(-1) for a in arrays])
    rows = -(-flat.shape[0] // PACK_COLS)
    rows = -(-rows // 8) * 8
    return jnp.pad(flat, (0, rows * PACK_COLS - flat.shape[0])).reshape(rows, PACK_COLS)


def _unpack_rows(buf, shapes):
    flat = buf.reshape(-1)
    out, at = [], 0
    for s in shapes:
        n = int(np.prod(s))
        out.append(flat[at:at + n].reshape(s))
        at += n
    return out


def _full_weight(t, axis):
    if axis == 1:
        return jnp.moveaxis(t, 0, 1).reshape(t.shape[1], N_DEV * t.shape[2], t.shape[3])
    return jnp.moveaxis(t, 0, 2).reshape(t.shape[1], t.shape[2], N_DEV * t.shape[3])


def _disassemble(d, groups, dtype):
    heads = lambda t: t.reshape(HEADS, -1, t.shape[1])
    full = {}
    for tag in ('ffn1', 'ffn2'):
        if f'{tag}_gate' in d:
            full.update({f'{tag}_w_gate': d[f'{tag}_gate'], f'{tag}_w_up': d[f'{tag}_up'], f'{tag}_w_down': d[f'{tag}_down']})
    if 'w_in' in d:
        w_in, uq, ukv = d['w_in'], d['w_uq'], d['w_ukv']
        full.update(
            w_in=jnp.concatenate([w_in[a:a + n] for a, n in PIN_PIECES], axis=0),
            mla_w_uq=jnp.concatenate([heads(uq[:HEADS * NOPE]), heads(uq[HEADS * NOPE:HEADS * NOPE + LANES]),
                                      heads(uq[HEADS * NOPE + LANES:])], axis=1).reshape(HEADS * QK_DIM, Q_RANK),
            mla_w_ukv=jnp.concatenate([heads(ukv[:HEADS * NOPE]), heads(ukv[HEADS * NOPE:])],
                                      axis=1).reshape(HEADS * (NOPE + V_DIM), KV_RANK),
            mla_w_o=d['mla_w_o'], conv_w_out=d['conv_w_out'], s5_w_glu=d['s5_w_glu'], s5_w_out=d['s5_w_out'], w_o=d['w_o'])
    return [jnp.concatenate([full[n].reshape(N_DEV, -1, full[n].shape[-1]).astype(dtype) for n in grp], axis=1)
            for grp in groups]


def _assemble(gathered, groups, shard_shapes):
    full = {n: t.reshape(N_DEV * t.shape[1], t.shape[2])
            for n, t in _unpack_groups(gathered, groups, shard_shapes).items()}
    out = {}
    for tag in ('ffn1', 'ffn2'):
        if f'{tag}_w_gate' in full:
            out.update({f'{tag}_gate': full[f'{tag}_w_gate'], f'{tag}_up': full[f'{tag}_w_up'],
                        f'{tag}_down': full[f'{tag}_w_down']})
    if 'w_in' not in full:
        return out
    w_in = full['w_in']
    cuts = np.cumsum((0,) + IN_SPLITS)
    cq, ckv, kr, xbar, bg, cg, u, gates = [w_in[a:b] for a, b in zip(cuts[:-1], cuts[1:])]
    pad = lambda t, n: jnp.pad(t, ((0, n - t.shape[0]), (0, 0)))
    w_in_packed = jnp.concatenate(
        [cq, ckv, pad(kr[:HALF_ROPE], LANES), pad(kr[HALF_ROPE:], LANES), xbar, bg, cg, u, gates,
         jnp.zeros((D_IN_PAD - PIN_END, D_MODEL), w_in.dtype)], axis=0)
    uq = full['mla_w_uq'].reshape(HEADS, QK_DIM, Q_RANK)
    w_uq = jnp.concatenate([uq[:, :NOPE].reshape(HEADS * NOPE, Q_RANK),
                            uq[:, NOPE:NOPE + HALF_ROPE].reshape(HEADS * HALF_ROPE, Q_RANK),
                            uq[:, NOPE + HALF_ROPE:].reshape(HEADS * HALF_ROPE, Q_RANK)], axis=0)
    ukv = full['mla_w_ukv'].reshape(HEADS, NOPE + V_DIM, KV_RANK)
    w_ukv = jnp.concatenate([ukv[:, :NOPE].reshape(HEADS * NOPE, KV_RANK),
                             ukv[:, NOPE:].reshape(HEADS * V_DIM, KV_RANK)], axis=0)
    out.update(w_in=w_in_packed, w_uq=w_uq, w_ukv=w_ukv, mla_w_o=full['mla_w_o'], conv_w_out=full['conv_w_out'],
               s5_w_glu=full['s5_w_glu'], s5_w_out=full['s5_w_out'], w_o=full['w_o'])
    return out


def _s5_discretize(a_re, a_im, log_dt, b_re, b_im, c_re, c_im):
    dt = jnp.exp(log_dt)[:, None]
    mag = jnp.exp(dt * a_re)
    ab_re, ab_im = mag * jnp.cos(dt * a_im), mag * jnp.sin(dt * a_im)
    den = a_re * a_re + a_im * a_im
    nr, ni = ab_re - 1.0, ab_im
    coef_re = (nr * a_re + ni * a_im) / den
    coef_im = (ni * a_re - nr * a_im) / den
    bb_re = coef_re[..., None] * b_re - coef_im[..., None] * b_im
    bb_im = coef_re[..., None] * b_im + coef_im[..., None] * b_re
    unit = jnp.arange(MIX)[:, None]
    chan = jnp.arange(2 * S5_CH)[None, :]
    pair = jnp.arange(2 * S5_STATE)[:, None]
    own = unit // S5_GROUP == (chan % S5_CH) // S5_STATE
    copy = jnp.logical_and(pair // S5_STATE == chan // S5_CH, pair % S5_STATE == chan % S5_STATE).astype(F32)
    flat_b = lambda bb: bb.transpose(0, 2, 1).reshape(MIX, S5_STATE)
    flat_c = lambda cc: cc.transpose(2, 0, 1).reshape(S5_STATE, MIX)
    b_small = jnp.concatenate([flat_b(bb_re), flat_b(bb_im)], axis=1)
    c_small = jnp.concatenate([flat_c(c_re), -flat_c(c_im)], axis=0)
    b_map = jnp.where(own, _make_mm_f32w("s5_spread_b")(b_small, copy), 0.0)
    c_map = jnp.where(own.T, _make_mm_f32w("s5_spread_c")(copy.T, c_small), 0.0)
    return ab_re.reshape(SCAN_ROWS, LANES), ab_im.reshape(SCAN_ROWS, LANES), b_map, c_map


def _rope_tables(length):
    inv_freq = ROPE_BASE ** (-jnp.arange(0, ROPE, 2, dtype=F32) / ROPE)
    ang = jnp.arange(length).astype(F32)[:, None] * inv_freq[None, :]
    return jnp.tile(jnp.cos(ang), (1, LANES // HALF_ROPE)), jnp.tile(jnp.sin(ang), (1, LANES // HALF_ROPE))


def _heads_first(t):
    return t.reshape(t.shape[0], HEADS, -1).transpose(1, 0, 2)


def _local_loss(diff, big, n_real):
    small, wz = diff['small'], diff['wz']
    h = diff['h0']
    length = h.shape[0]
    cos, sin = _rope_tables(length)
    row2 = lambda v: v.reshape(1, -1)
    for l in range(DEPTH):
        w, z = big[l], wz[l]
        p = {k: small[k][l] for k in small if k != 'meta'}
        ffn = lambda h, tag, ln: _make_ffn(tag)(
            h, *[w[f'{tag}_{k}'] for k in ('gate', 'up', 'down')], *[z[f'{tag}_{k}'] for k in ('gate', 'up', 'down')],
            row2(p[f'{ln}_g']), row2(p[f'{ln}_b']))
        h = ffn(h, "ffn1", "ln1")
        cq, ckv, kr1, kr2, xbar, bg, cg, u, gate_a, gate_b, gate_c, _ = _make_projection("w_in", PIN_CUTS)(
            h, w['w_in'], z['w_in'])
        qn, = _make_rowwise(_f_rms, 1, 1, (BF16,), "q_rms")(cq, row2(p['mla_q_norm_g']))
        kvn, = _make_rowwise(_f_rms, 1, 1, (BF16,), "kv_rms")(ckv, row2(p['mla_kv_norm_g']))
        q_nope, q1, q2 = _make_split((0, HEADS * NOPE, HEADS * NOPE + LANES, HEADS * NOPE + 2 * LANES))(
            _make_mm("w_uq", wt=True)(qn, w['w_uq'], z['w_uq']))
        k_nope, val = _make_split((0, HEADS * NOPE, HEADS * (NOPE + V_DIM)))(
            _make_mm("w_ukv", wt=True)(kvn, w['w_ukv'], z['w_ukv']))
        rope = _make_rowwise(_f_rope, 4, 0, (BF16, BF16), "rope", n_nodiff=2)
        q1, q2 = rope(q1, q2, cos, sin)
        k1, k2 = rope(kr1, kr2, cos, sin)
        hpad = jnp.zeros((HEADS, length, HEAD_PAD - QK_DIM), BF16)
        q3 = jnp.concatenate([_heads_first(q_nope.astype(BF16)), _heads_first(q1), _heads_first(q2), hpad], -1)
        shared = lambda t: jnp.broadcast_to(t[None, :, :HALF_ROPE], (HEADS, length, HALF_ROPE))
        k3 = jnp.concatenate([_heads_first(k_nope.astype(BF16)), shared(k1), shared(k2), hpad], -1)
        v3 = _heads_first(val.astype(BF16))
        o3 = _attention(q3, k3, v3)
        y_a = _make_mm("mla_w_o", wt=True)(o3.transpose(1, 0, 2).reshape(length, MIX), w['mla_w_o'], z['mla_w_o'])
        conv = _short_conv(xbar, bg, cg, p['conv_w_full'], row2(p['conv_b']))
        y_b = _make_mm("conv_w_out", wt=True)(conv, w['conv_w_out'], z['conv_w_out'])
        ar, ai, b_map, c_map = _s5_discretize(p['s5_a_re'], p['s5_a_im'], p['s5_log_dt'], p['s5_b_re'], p['s5_b_im'],
                                              p['s5_c_re'], p['s5_c_im'])
        bu = _make_bd_in("s5_b")(u, b_map)
        states = _s5_scan(ar, ai, bu.reshape(length, 2 * SCAN_ROWS, LANES)).reshape(length, 2 * S5_CH)
        y_ssm = _make_bd_out("s5_c")(states, c_map)
        zed, = _make_rowwise(_f_gelu_skip, 2, 1, (F32,), "s5_gelu")(y_ssm, u, row2(p['s5_d']))
        t = _make_mm("s5_w_glu")(zed, w['s5_w_glu'], z['s5_w_glu'])
        glu, = _make_rowwise(_f_glu, 2, 1, (BF16,), "s5_glu")(zed, t, row2(p['s5_b_glu']))
        y_c = _make_mm("s5_w_out", wt=True)(glu, w['s5_w_out'], z['s5_w_out'])
        mixed, = _make_rowwise(_f_merge, 6, 0, (BF16,), "merge")(gate_a, gate_b, gate_c, y_a, y_b, y_c)
        mix_out = _make_mm("w_o")(mixed, w['w_o'], z['w_o'])
        h, = _make_rowwise(_f_ln_full, 2, 2, (F32,), "mix_ln")(h, mix_out, row2(p['ln2_g']), row2(p['ln2_b']))
        h = ffn(h, "ffn2", "ln3")
    return _make_loss(n_real)(h, diff['target'])


def kernel(x, meta, ffn1_w_gate, ffn1_w_up, ffn1_w_down, ln1_g, ln1_b, w_in, mla_q_norm_g, mla_w_uq, mla_kv_norm_g, mla_w_ukv, mla_w_o, conv_w, conv_b, conv_w_out, s5_a_re, s5_a_im, s5_log_dt, s5_b_re, s5_b_im, s5_c_re, s5_c_im, s5_d, s5_w_glu, s5_b_glu, s5_w_out, w_o, ln2_g, ln2_b, ffn2_w_gate, ffn2_w_up, ffn2_w_down, ln3_g, ln3_b, loss_target, m_meta, m_ffn1_w_gate, m_ffn1_w_up, m_ffn1_w_down, m_ln1_g, m_ln1_b, m_w_in, m_mla_q_norm_g, m_mla_w_uq, m_mla_kv_norm_g, m_mla_w_ukv, m_mla_w_o, m_conv_w, m_conv_b, m_conv_w_out, m_s5_a_re, m_s5_a_im, m_s5_log_dt, m_s5_b_re, m_s5_b_im, m_s5_c_re, m_s5_c_im, m_s5_d, m_s5_w_glu, m_s5_b_glu, m_s5_w_out, m_w_o, m_ln2_g, m_ln2_b, m_ffn2_w_gate, m_ffn2_w_up, m_ffn2_w_down, m_ln3_g, m_ln3_b, v_meta, v_ffn1_w_gate, v_ffn1_w_up, v_ffn1_w_down, v_ln1_g, v_ln1_b, v_w_in, v_mla_q_norm_g, v_mla_w_uq, v_mla_kv_norm_g, v_mla_w_ukv, v_mla_w_o, v_conv_w, v_conv_b, v_conv_w_out, v_s5_a_re, v_s5_a_im, v_s5_log_dt, v_s5_b_re, v_s5_b_im, v_s5_c_re, v_s5_c_im, v_s5_d, v_s5_w_glu, v_s5_b_glu, v_s5_w_out, v_w_o, v_ln2_g, v_ln2_b, v_ffn2_w_gate, v_ffn2_w_up, v_ffn2_w_down, v_ln3_g, v_ln3_b):
    args = locals()
    w = {n: args[n] for n in WEIGHT_NAMES}
    m = {n: args["m_" + n] for n in WEIGHT_NAMES}
    v = {n: args["v_" + n] for n in WEIGHT_NAMES}
    me = 4 * lax.axis_index("x") + 2 * lax.axis_index("y") + lax.axis_index("c")
    seq = x.shape[1]
    n_real = N_META + seq
    length = -(-n_real // LANES) * LANES

    shard_shapes = {n: w[n].shape for n in BIG}
    small_shards = _all_gather([_pack_rows([w[n] for n in SMALL_SHARDED])], "gather_small",
                               sequencer=True)[0].reshape(N_DEV, -1)
    big = [{} for _ in range(DEPTH)]
    for l in range(DEPTH):
        for p, groups in enumerate(PIECES):
            gathered = _all_gather(_pack_groups(w, l, groups, BF16), f"gather_weights_layer{l}_piece{p}", sequencer=True)
            big[l].update(_assemble(gathered, groups, shard_shapes))
    meta_full = _full_weight(small_shards[:, :meta.size].reshape(N_DEV, 1, *meta.shape), 2)[0]
    conv_w_full = _full_weight(small_shards[:, meta.size:meta.size + conv_w.size].reshape(N_DEV, *conv_w.shape), 2)

    small = {n: w[n] for n in SMALL_NAMES if n not in SMALL_SHARDED}
    small['conv_w_full'] = conv_w_full
    small['meta'] = meta_full
    wz = jax.tree.map(lambda t: jnp.zeros(t.shape, BF16), big)
    pad_rows = length - n_real

    def loss_fn(diff):
        h0 = jnp.concatenate([diff['small']['meta'], diff['x'], jnp.zeros((pad_rows, D_MODEL), F32)], axis=0)
        target = jnp.pad(loss_target[0], ((N_META, pad_rows), (0, 0)))
        return _local_loss(dict(h0=h0, small=diff['small'], wz=diff['wz'], target=target), big, n_real)

    loss_local, grads = jax.value_and_grad(loss_fn)(dict(x=x[0], small=small, wz=wz))

    small_names = [n for n in SMALL_NAMES if n not in SMALL_SHARDED] + ['conv_w_full', 'meta']
    small_flat = _pack_rows([loss_local.reshape(1)] + [grads['small'][n] for n in small_names])
    rows_each = -(-small_flat.shape[0] // (8 * N_DEV)) * 8
    small_flat = jnp.pad(small_flat, ((0, rows_each * N_DEV - small_flat.shape[0]), (0, 0)))
    layer_sums = [{} for _ in range(DEPTH)]
    for l in reversed(range(DEPTH)):
        for p, groups in reversed(list(enumerate(PIECES))):
            last = l == 0 and p == 0
            extra = [small_flat.reshape(N_DEV, rows_each, PACK_COLS)] if last else []
            sums = _reduce_scatter(_disassemble(grads['wz'][l], groups, BF16) + extra, f"grads_layer{l}_piece{p}_",
                                   sequencer=True)
            layer_sums[l].update(_unpack_groups(sums[:len(groups)], groups, shard_shapes))
            if last:
                small_sum = sums[-1]
    small_all, = _all_gather([small_sum], "gather_small_grads")
    loss, *small_sums = _unpack_rows(small_all, [()] + [grads['small'][n].shape for n in small_names])
    small_grads = dict(zip(small_names, small_sums))
    g = {}
    for name in SMALL_NAMES:
        if name == 'meta':
            g[name] = lax.dynamic_slice_in_dim(small_grads['meta'], me * meta.shape[1], meta.shape[1], axis=1)
        elif name == 'conv_w':
            g[name] = lax.dynamic_slice_in_dim(small_grads['conv_w_full'], me * conv_w.shape[2], conv_w.shape[2], axis=2)
        else:
            g[name] = small_grads[name]

    delta, new_m, new_v = {}, {}, {}
    for name in BIG:
        per_layer = [s[name].T if name in TRANSPOSED else s[name] for s in layer_sums]
        g[name], delta[name], new_m[name], new_v[name] = _adamw_layers(w[name], per_layer, m[name], v[name], f"adamw_{name}")
    shapes = [w[n].shape for n in SMALL_NAMES]
    d, nm, nv = _adamw(*[_pack_rows([t[n] for n in SMALL_NAMES]) for t in (w, g, m, v)], "adamw_small")
    for out, buf in ((delta, d), (new_m, nm), (new_v, nv)):
        out.update(zip(SMALL_NAMES, _unpack_rows(buf, shapes)))

    return (loss, grads['x'][None], *[g[n] for n in WEIGHT_NAMES], *[delta[n] for n in WEIGHT_NAMES],
            *[new_m[n] for n in WEIGHT_NAMES], *[new_v[n] for n in WEIGHT_NAMES])
```

```python
import functools
import math

import jax
import jax.numpy as jnp
import numpy as np
from jax import lax
from jax.experimental import pallas as pl
from jax.experimental.pallas import tpu as pltpu
from jax.experimental.pallas import tpu_sc as plsc

F32 = jnp.float32
BF16 = jnp.bfloat16

D_MODEL = 1024
DEPTH = 2
N_META = 16
HEADS = 8
V_DIM = 64
NOPE = 64
ROPE = 32
HALF_ROPE = ROPE // 2
QK_DIM = NOPE + ROPE
Q_RANK = 384
KV_RANK = 256
MIX = 512
CONV_K = 3
S5_GROUPS = 32
S5_GROUP = 16
S5_STATE = 64
S5_CH = S5_GROUPS * S5_STATE
D_FF = 2816
ALPHA = (2.0 * DEPTH) ** 0.25
LN_EPS = 1e-5
RMS_EPS = 1e-6
ROPE_BASE = 10000.0
IN_SPLITS = (Q_RANK, KV_RANK, ROPE, MIX, MIX, MIX, MIX, 3 * D_MODEL)
D_IN = sum(IN_SPLITS)
ADAM_LR, ADAM_B1, ADAM_B2, ADAM_EPS, ADAM_WD, ADAM_STEP = 0.001, 0.9, 0.999, 1e-08, 0.01, 10

N_DEV = 8
AXES = ("x", "y", "c")
LANES = 128
PACK_COLS = 1024
HEAD_PAD = 128
VMEM_LIMIT = 48 * 1024 * 1024

PIN_CQ, PIN_CKV, PIN_KR1, PIN_KR2, PIN_XBAR, PIN_BG, PIN_CG, PIN_U, PIN_GATES, PIN_END = (
    0, 384, 640, 768, 896, 1408, 1920, 2432, 2944, 6016)
D_IN_PAD = 6144

WEIGHT_NAMES = ['meta', 'ffn1_w_gate', 'ffn1_w_up', 'ffn1_w_down', 'ln1_g', 'ln1_b', 'w_in', 'mla_q_norm_g', 'mla_w_uq',
                'mla_kv_norm_g', 'mla_w_ukv', 'mla_w_o', 'conv_w', 'conv_b', 'conv_w_out', 's5_a_re', 's5_a_im',
                's5_log_dt', 's5_b_re', 's5_b_im', 's5_c_re', 's5_c_im', 's5_d', 's5_w_glu', 's5_b_glu', 's5_w_out',
                'w_o', 'ln2_g', 'ln2_b', 'ffn2_w_gate', 'ffn2_w_up', 'ffn2_w_down', 'ln3_g', 'ln3_b']
BIG = {'ffn1_w_gate': 2, 'ffn1_w_up': 2, 'ffn1_w_down': 1, 'w_in': 2, 'mla_w_uq': 2, 'mla_w_ukv': 2, 'mla_w_o': 2,
       'conv_w_out': 2, 's5_w_glu': 1, 's5_w_out': 2, 'w_o': 1, 'ffn2_w_gate': 2, 'ffn2_w_up': 2, 'ffn2_w_down': 1}
SMALL_SHARDED = ('meta', 'conv_w')
SMALL_NAMES = [n for n in WEIGHT_NAMES if n not in BIG]


def _divisor_tile(n, limit, mult):
    best = None
    for t in range(mult, min(n, limit) + 1, mult):
        if n % t == 0:
            best = t
    return best if best is not None else n


def _params(*sem):
    return pltpu.CompilerParams(dimension_semantics=sem, vmem_limit_bytes=VMEM_LIMIT)


def _matmul(a, b, *, ta=False, tb=False, out_dtype=F32, add=None, name):
    m, k = (a.shape[1], a.shape[0]) if ta else a.shape
    n = b.shape[0] if tb else b.shape[1]
    assert (b.shape[1] if tb else b.shape[0]) == k, (a.shape, b.shape, ta, tb)
    tm = _divisor_tile(m, 1408, LANES) if ta else _divisor_tile(m, 2176 if a.dtype == BF16 else 1088, 16)
    tn = _divisor_tile(n, 512, LANES)
    tk = _divisor_tile(k, 1408, 16 if ta else LANES)
    nk = k // tk
    dims = (((0 if ta else 1,), (1 if tb else 0,)), ((), ()))

    in_place = jnp.dtype(out_dtype) == jnp.dtype(F32)

    def kern(a_ref, b_ref, *rest):
        add_ref = rest[0] if add is not None else None
        o_ref, *scratch = rest[1:] if add is not None else rest
        kk = pl.program_id(2)
        part = lax.dot_general(a_ref[...].astype(BF16), b_ref[...].astype(BF16), dims, preferred_element_type=F32)
        first = lambda: part if add_ref is None else part + add_ref[...].astype(F32)
        if nk == 1:
            o_ref[...] = first().astype(o_ref.dtype)
            return
        acc_ref = o_ref if in_place else scratch[0]

        @pl.when(kk == 0)
        def _():
            acc_ref[...] = first()

        @pl.when(kk > 0)
        def _():
            acc_ref[...] += part

        if not in_place:
            @pl.when(kk == nk - 1)
            def _():
                o_ref[...] = acc_ref[...].astype(o_ref.dtype)

    a_spec = pl.BlockSpec((tk, tm), lambda i, j, kk: (kk, i)) if ta else pl.BlockSpec((tm, tk), lambda i, j, kk: (i, kk))
    b_spec = pl.BlockSpec((tn, tk), lambda i, j, kk: (j, kk)) if tb else pl.BlockSpec((tk, tn), lambda i, j, kk: (kk, j))
    o_spec = pl.BlockSpec((tm, tn), lambda i, j, kk: (i, j))
    return pl.pallas_call(
        kern, name=name, grid=(m // tm, n // tn, nk),
        in_specs=[a_spec, b_spec] + ([o_spec] if add is not None else []), out_specs=o_spec,
        out_shape=jax.ShapeDtypeStruct((m, n), out_dtype),
        scratch_shapes=[] if (nk == 1 or in_place) else [pltpu.VMEM((tm, tn), F32)],
        compiler_params=_params("parallel", "parallel", "arbitrary"),
    )(a, b, *([add] if add is not None else []))


def _make_mm(name, wt=False):
    @jax.custom_vjp
    def mm(x, w, wz):
        return _matmul(x, w, tb=wt, name=name + "_fwd")

    def fwd(x, w, wz):
        return _matmul(x, w, tb=wt, name=name + "_fwd"), (x, w)

    def bwd(res, dy):
        x, w = res
        wz_dtype = BF16
        dx = _matmul(dy, w, tb=not wt, out_dtype=x.dtype, name=name + "_dx")
        dw = (_matmul(dy, x, ta=True, out_dtype=wz_dtype, name=name + "_dw") if wt
              else _matmul(x, dy, ta=True, out_dtype=wz_dtype, name=name + "_dw"))
        return dx, jnp.zeros_like(w), dw

    mm.defvjp(fwd, bwd)
    return mm


def _make_mm_f32w(name):
    @jax.custom_vjp
    def mm(x, w):
        return _matmul(x, w, name=name + "_fwd")

    def fwd(x, w):
        return _matmul(x, w, name=name + "_fwd"), (x, w)

    def bwd(res, dy):
        x, w = res
        return (_matmul(dy, w, tb=True, out_dtype=x.dtype, name=name + "_dx"),
                _matmul(x, dy, ta=True, name=name + "_dw"))

    mm.defvjp(fwd, bwd)
    return mm


BD_BLOCKS = 4
BD_PARTS = 2


def _bd_call(a, b, out_shape, a_blk, b_blk, o_blk, a_idx, b_idx, o_idx, dims, reduce_parts, name):
    def kern(a_ref, b_ref, o_ref):
        part = lax.dot_general(a_ref[...].astype(BF16), b_ref[...].astype(BF16), dims, preferred_element_type=F32)
        if not reduce_parts:
            o_ref[...] = part.astype(o_ref.dtype)
            return

        @pl.when(pl.program_id(1) == 0)
        def _():
            o_ref[...] = part

        @pl.when(pl.program_id(1) > 0)
        def _():
            o_ref[...] += part

    return pl.pallas_call(
        kern, name=name, grid=(BD_BLOCKS, BD_PARTS),
        in_specs=[pl.BlockSpec(a_blk, a_idx), pl.BlockSpec(b_blk, b_idx)], out_specs=pl.BlockSpec(o_blk, o_idx),
        out_shape=jax.ShapeDtypeStruct(out_shape, F32),
        compiler_params=_params("parallel", "arbitrary" if reduce_parts else "parallel"),
    )(a, b)


def _make_bd_in(name):
    wide = lambda j, p: (0, BD_BLOCKS * p + j)
    thin = lambda j, p: (0, j)
    diag = lambda j, p: (j, BD_BLOCKS * p + j)
    nn, nt, tn = (((1,), (0,)), ((), ())), (((1,), (1,)), ((), ())), (((0,), (0,)), ((), ()))

    def run(x, w):
        length, cols = x.shape[0], w.shape[1] // (BD_BLOCKS * BD_PARTS)
        return _bd_call(x, w, (length, w.shape[1]), (length, LANES), (LANES, cols), (length, cols), thin, diag, wide,
                        nn, False, name + "_fwd")

    @jax.custom_vjp
    def mm(x, w):
        return run(x, w)

    def fwd(x, w):
        return run(x, w), (x, w)

    def bwd(res, dy):
        x, w = res
        length, cols = x.shape[0], w.shape[1] // (BD_BLOCKS * BD_PARTS)
        dx = _bd_call(dy, w, x.shape, (length, cols), (LANES, cols), (length, LANES), wide, diag, thin, nt, True,
                      name + "_dx")
        dw = _bd_call(x, dy, w.shape, (length, LANES), (length, cols), (LANES, cols), thin, wide, diag, tn, False,
                      name + "_dw")
        return dx, dw

    mm.defvjp(fwd, bwd)
    return mm


def _make_bd_out(name):
    wide = lambda j, p: (0, BD_BLOCKS * p + j)
    thin = lambda j, p: (0, j)
    diag = lambda j, p: (BD_BLOCKS * p + j, j)
    nn, nt, tn = (((1,), (0,)), ((), ())), (((1,), (1,)), ((), ())), (((0,), (0,)), ((), ()))

    def run(x, w):
        length, cols = x.shape[0], w.shape[0] // (BD_BLOCKS * BD_PARTS)
        return _bd_call(x, w, (length, w.shape[1]), (length, cols), (cols, LANES), (length, LANES), wide, diag, thin,
                        nn, True, name + "_fwd")

    @jax.custom_vjp
    def mm(x, w):
        return run(x, w)

    def fwd(x, w):
        return run(x, w), (x, w)

    def bwd(res, dy):
        x, w = res
        length, cols = x.shape[0], w.shape[0] // (BD_BLOCKS * BD_PARTS)
        dx = _bd_call(dy, w, x.shape, (length, LANES), (cols, LANES), (length, cols), thin, diag, wide, nt, False,
                      name + "_dx")
        dw = _bd_call(x, dy, w.shape, (length, cols), (length, LANES), (cols, LANES), wide, thin, diag, tn, False,
                      name + "_dw")
        return dx, dw

    mm.defvjp(fwd, bwd)
    return mm


def _row_tile(rows, widths):
    limit = max(16, (6 * 1024 * 1024 // 4) // max(1, sum(widths)))
    return _divisor_tile(rows, limit, 16)


def _make_rowwise(f, n_rows, n_pars, out_dtypes, name, n_nodiff=0, grad_dtypes=None):
    n_out = len(out_dtypes)
    n_diff = n_rows - n_nodiff

    def run_fwd(rows, pars):
        length = rows[0].shape[0]
        shapes = jax.eval_shape(lambda *a: f(*a), *[jax.ShapeDtypeStruct((16, r.shape[1]), F32) for r in rows],
                                *[jax.ShapeDtypeStruct(p.shape, F32) for p in pars])
        widths = [s.shape[1] for s in shapes]
        tm = _row_tile(length, [r.shape[1] for r in rows] + widths)

        def kern(*refs):
            ins = [r[...].astype(F32) for r in refs[:n_rows + n_pars]]
            outs = f(*ins)
            for o_ref, o in zip(refs[n_rows + n_pars:], outs):
                o_ref[...] = o.astype(o_ref.dtype)

        return pl.pallas_call(
            kern, name=name + "_fwd", grid=(length // tm,),
            in_specs=[pl.BlockSpec((tm, r.shape[1]), lambda i: (i, 0)) for r in rows]
            + [pl.BlockSpec(p.shape, lambda i: (0, 0)) for p in pars],
            out_specs=[pl.BlockSpec((tm, w), lambda i: (i, 0)) for w in widths],
            out_shape=[jax.ShapeDtypeStruct((length, w), dt) for w, dt in zip(widths, out_dtypes)],
            compiler_params=_params("parallel"),
        )(*rows, *pars)

    def run_bwd(rows, pars, cts):
        length = rows[0].shape[0]
        tm = _row_tile(length, [r.shape[1] for r in rows] * 2 + [c.shape[1] for c in cts] * 2)

        def kern(*refs):
            ins = [r[...].astype(F32) for r in refs[:n_rows + n_pars]]
            ct = [r[...].astype(F32) for r in refs[n_rows + n_pars:n_rows + n_pars + n_out]]
            out_refs = refs[n_rows + n_pars + n_out:]
            nodiff = ins[n_diff:n_rows]
            _, vjp = jax.vjp(lambda *a: f(*a[:n_diff], *nodiff, *a[n_diff:]), *ins[:n_diff], *ins[n_rows:])
            grads = vjp(tuple(ct))
            for o_ref, g in zip(out_refs[:n_diff], grads[:n_diff]):
                o_ref[...] = g.astype(o_ref.dtype)
            first = pl.program_id(0) == 0
            for o_ref, g in zip(out_refs[n_diff:], grads[n_diff:]):
                @pl.when(first)
                def _(o_ref=o_ref, g=g):
                    o_ref[...] = g

                @pl.when(jnp.logical_not(first))
                def _(o_ref=o_ref, g=g):
                    o_ref[...] += g

        return pl.pallas_call(
            kern, name=name + "_bwd", grid=(length // tm,),
            in_specs=[pl.BlockSpec((tm, r.shape[1]), lambda i: (i, 0)) for r in rows]
            + [pl.BlockSpec(p.shape, lambda i: (0, 0)) for p in pars]
            + [pl.BlockSpec((tm, c.shape[1]), lambda i: (i, 0)) for c in cts],
            out_specs=[pl.BlockSpec((tm, r.shape[1]), lambda i: (i, 0)) for r in rows[:n_diff]]
            + [pl.BlockSpec(p.shape, lambda i: (0, 0)) for p in pars],
            out_shape=[jax.ShapeDtypeStruct(r.shape, r.dtype if grad_dtypes is None else grad_dtypes[i])
                       for i, r in enumerate(rows[:n_diff])]
            + [jax.ShapeDtypeStruct(p.shape, F32) for p in pars],
            compiler_params=_params("arbitrary"),
        )(*rows, *pars, *cts)

    @jax.custom_vjp
    def op(*args):
        return tuple(run_fwd(args[:n_rows], args[n_rows:]))

    def fwd(*args):
        return tuple(run_fwd(args[:n_rows], args[n_rows:])), args

    def bwd(args, cts):
        grads = run_bwd(args[:n_rows], args[n_rows:], cts)
        zeros = [jnp.zeros_like(r) for r in args[n_diff:n_rows]]
        return (*grads[:n_diff], *zeros, *grads[n_diff:])

    op.defvjp(fwd, bwd)
    op.run_fwd, op.run_bwd = run_fwd, run_bwd
    return op


def _layer_norm(z, g, b):
    mu = jnp.mean(z, axis=-1, keepdims=True)
    d = z - mu
    var = jnp.mean(d * d, axis=-1, keepdims=True)
    return d * lax.rsqrt(var + LN_EPS) * g + b


def _f_ln_half(h, f, g, b):
    return (_layer_norm(ALPHA * h + 0.5 * f, g, b),)


def _f_ln_full(h, f, g, b):
    return (_layer_norm(ALPHA * h + f, g, b),)


def _f_rms(x, g):
    return (x * lax.rsqrt(jnp.mean(x * x, axis=-1, keepdims=True) + RMS_EPS) * g,)


def _f_rope(x1, x2, cos, sin):
    return x1 * cos - x2 * sin, x2 * cos + x1 * sin


def _f_gelu_skip(y, u, d):
    return (jax.nn.gelu(y + d * u),)


def _f_glu(z, t, b):
    return (z * jax.nn.sigmoid(t + b),)


def _f_merge(ga, gb, gc, ya, yb, yc):
    return (jax.nn.sigmoid(ga) * ya + jax.nn.sigmoid(gb) * yb + jax.nn.sigmoid(gc) * yc,)


def _f_swiglu(gate, up):
    return (jax.nn.silu(gate) * up,)


def _f_cast(x):
    return (x,)


def _make_ffn(tag):
    ln = _make_rowwise(_f_ln_half, 2, 2, (F32,), f"{tag}_ln", grad_dtypes=(F32, BF16))
    cast = _make_rowwise(_f_cast, 1, 0, (BF16,), f"{tag}_cast")
    swiglu = _make_rowwise(_f_swiglu, 2, 0, (BF16,), f"{tag}_swiglu", grad_dtypes=(BF16, BF16))

    def forward(h, w_gate, w_up, w_down, g, b):
        hb, = cast.run_fwd((h,), ())
        gate = _matmul(hb, w_gate, tb=True, name=f"{tag}_gate_fwd")
        up = _matmul(hb, w_up, tb=True, name=f"{tag}_up_fwd")
        act, = swiglu.run_fwd((gate, up), ())
        f = _matmul(act, w_down, name=f"{tag}_down_fwd")
        y, = ln.run_fwd((h, f), (g, b))
        return y, (h, hb, gate, up, act, f, w_gate, w_up, w_down, g, b)

    @jax.custom_vjp
    def block(h, w_gate, w_up, w_down, z_gate, z_up, z_down, g, b):
        return forward(h, w_gate, w_up, w_down, g, b)[0]

    def fwd(h, w_gate, w_up, w_down, z_gate, z_up, z_down, g, b):
        return forward(h, w_gate, w_up, w_down, g, b)

    def bwd(res, dy):
        h, hb, gate, up, act, f, w_gate, w_up, w_down, g, b = res
        dh, df, dg, db = ln.run_bwd((h, f), (g, b), (dy,))
        dact = _matmul(df, w_down, tb=True, out_dtype=BF16, name=f"{tag}_down_dx")
        dw_down = _matmul(act, df, ta=True, out_dtype=BF16, name=f"{tag}_down_dw")
        dgate, dup = swiglu.run_bwd((gate, up), (), (dact,))
        dw_gate = _matmul(dgate, hb, ta=True, out_dtype=BF16, name=f"{tag}_gate_dw")
        dw_up = _matmul(dup, hb, ta=True, out_dtype=BF16, name=f"{tag}_up_dw")
        dh = _matmul(dgate, w_gate, add=dh, name=f"{tag}_gate_dx")
        dh = _matmul(dup, w_up, add=dh, name=f"{tag}_up_dx")
        zero = jnp.zeros_like
        return dh, zero(w_gate), zero(w_up), zero(w_down), dw_gate, dw_up, dw_down, dg, db

    block.defvjp(fwd, bwd)
    return block


def _attn_scores(q, k, q_block, tq):
    length = k.shape[0]
    s = lax.dot_general(q, k, (((1,), (1,)), ((), ())), preferred_element_type=F32) * (QK_DIM ** -0.5)
    row = q_block * tq + lax.broadcasted_iota(jnp.int32, (tq, length), 0)
    col = lax.broadcasted_iota(jnp.int32, (tq, length), 1)
    s = jnp.where(col <= row, s, -1e30)
    e = jnp.exp(s - jnp.max(s, axis=1, keepdims=True))
    return e * (1.0 / jnp.sum(e, axis=1, keepdims=True))


ATTN_SEGMENTS = 4


def _attn_tiles(length):
    seg = length // ATTN_SEGMENTS
    return seg, _divisor_tile(seg, 272, 16)


def _attn_fwd(q3, k3, v3):
    heads, length, _ = q3.shape
    seg, tq = _attn_tiles(length)
    outs = []
    for s in range(ATTN_SEGMENTS):
        kmax, base = (s + 1) * seg, s * (seg // tq)

        def kern(q_ref, k_ref, v_ref, o_ref, base=base):
            p = _attn_scores(q_ref[0], k_ref[0], base + pl.program_id(1), tq)
            o_ref[0] = jnp.dot(p.astype(BF16), v_ref[0], preferred_element_type=F32).astype(o_ref.dtype)

        outs.append(pl.pallas_call(
            kern, name=f"attn_fwd_seg{s}", grid=(heads, seg // tq),
            in_specs=[pl.BlockSpec((1, tq, HEAD_PAD), lambda h, i, base=base: (h, base + i, 0)),
                      pl.BlockSpec((1, kmax, HEAD_PAD), lambda h, i: (h, 0, 0)),
                      pl.BlockSpec((1, kmax, V_DIM), lambda h, i: (h, 0, 0))],
            out_specs=pl.BlockSpec((1, tq, V_DIM), lambda h, i: (h, i, 0)),
            out_shape=jax.ShapeDtypeStruct((heads, seg, V_DIM), F32),
            compiler_params=_params("parallel", "parallel"),
        )(q3, k3, v3))
    return jnp.concatenate(outs, axis=1)


def _attn_bwd(q3, k3, v3, do3):
    heads, length, _ = q3.shape
    seg, tq = _attn_tiles(length)
    dk = jnp.zeros((heads, length, HEAD_PAD), F32)
    dv = jnp.zeros((heads, length, V_DIM), F32)
    dqs = [None] * ATTN_SEGMENTS
    for s in reversed(range(ATTN_SEGMENTS)):
        kmax, base = (s + 1) * seg, s * (seg // tq)

        def kern(q_ref, k_ref, v_ref, do_ref, dk_in, dv_in, dq_ref, dk_ref, dv_ref, base=base):
            i = pl.program_id(1)
            q, k, v, do = q_ref[0], k_ref[0], v_ref[0], do_ref[0].astype(BF16)
            p = _attn_scores(q, k, base + i, tq)
            dp = lax.dot_general(do, v, (((1,), (1,)), ((), ())), preferred_element_type=F32)
            ds = (p * (dp - jnp.sum(p * dp, axis=1, keepdims=True)) * (QK_DIM ** -0.5)).astype(BF16)
            dq_ref[0] = jnp.dot(ds, k, preferred_element_type=F32)
            dk_part = lax.dot_general(ds, q, (((0,), (0,)), ((), ())), preferred_element_type=F32)
            dv_part = lax.dot_general(p.astype(BF16), do, (((0,), (0,)), ((), ())), preferred_element_type=F32)

            @pl.when(i == 0)
            def _():
                dk_ref[0] = dk_in[0] + dk_part
                dv_ref[0] = dv_in[0] + dv_part

            @pl.when(i > 0)
            def _():
                dk_ref[0] += dk_part
                dv_ref[0] += dv_part

        q_blk = pl.BlockSpec((1, tq, HEAD_PAD), lambda h, i, base=base: (h, base + i, 0))
        k_blk = pl.BlockSpec((1, kmax, HEAD_PAD), lambda h, i: (h, 0, 0))
        v_blk = pl.BlockSpec((1, kmax, V_DIM), lambda h, i: (h, 0, 0))
        dqs[s], dk, dv = pl.pallas_call(
            kern, name=f"attn_bwd_seg{s}", grid=(heads, seg // tq),
            in_specs=[q_blk, k_blk, v_blk, pl.BlockSpec((1, tq, V_DIM), lambda h, i, base=base: (h, base + i, 0)),
                      k_blk, v_blk],
            out_specs=[pl.BlockSpec((1, tq, HEAD_PAD), lambda h, i: (h, i, 0)), k_blk, v_blk],
            out_shape=[jax.ShapeDtypeStruct((heads, seg, HEAD_PAD), F32), jax.ShapeDtypeStruct(dk.shape, F32),
                       jax.ShapeDtypeStruct(dv.shape, F32)],
            input_output_aliases={4: 1, 5: 2}, compiler_params=_params("parallel", "arbitrary"),
        )(q3, k3, v3, do3, dk, dv)
    return jnp.concatenate(dqs, axis=1), dk, dv


@jax.custom_vjp
def _attention(q3, k3, v3):
    return _attn_fwd(q3, k3, v3)


def _attention_fwd(q3, k3, v3):
    return _attn_fwd(q3, k3, v3), (q3, k3, v3)


def _attention_bwd(res, do3):
    q3, k3, v3 = res
    dq, dk, dv = _attn_bwd(q3, k3, v3, do3)
    return dq.astype(q3.dtype), dk.astype(k3.dtype), dv.astype(v3.dtype)


_attention.defvjp(_attention_fwd, _attention_bwd)


def _conv_terms(x, c, w_ref, cb):
    u = c * x
    row = lax.broadcasted_iota(jnp.int32, u.shape, 0)
    u1 = jnp.where(row >= 1, pltpu.roll(u, 1, 0), 0.0)
    u2 = jnp.where(row >= 2, pltpu.roll(u, 2, 0), 0.0)
    y = cb + w_ref[0:1, :] * u2 + w_ref[1:2, :] * u1 + w_ref[2:3, :] * u
    return u, u1, u2, y


def _conv_specs(length):
    col = pl.BlockSpec((length, LANES), lambda j: (0, j))
    return col, pl.BlockSpec((CONV_K, LANES), lambda j: (0, j)), pl.BlockSpec((1, LANES), lambda j: (0, j))


def _conv_fwd(x, b, c, w, cb):
    length = x.shape[0]

    def kern(x_ref, b_ref, c_ref, w_ref, cb_ref, o_ref):
        _, _, _, y = _conv_terms(x_ref[...], c_ref[...], w_ref, cb_ref[...])
        o_ref[...] = b_ref[...] * y

    col, wspec, bspec = _conv_specs(length)
    return pl.pallas_call(
        kern, name="conv_fwd", grid=(MIX // LANES,), in_specs=[col, col, col, wspec, bspec], out_specs=col,
        out_shape=jax.ShapeDtypeStruct((length, MIX), F32), compiler_params=_params("parallel"),
    )(x, b, c, w, cb)


def _conv_bwd(x, b, c, w, cb, do):
    length = x.shape[0]

    def kern(x_ref, b_ref, c_ref, w_ref, cb_ref, do_ref, dx_ref, db_ref, dc_ref, dw_ref, dcb_ref):
        x, c, do = x_ref[...], c_ref[...], do_ref[...]
        u, u1, u2, y = _conv_terms(x, c, w_ref, cb_ref[...])
        db_ref[...] = do * y
        dy = do * b_ref[...]
        row = lax.broadcasted_iota(jnp.int32, dy.shape, 0)
        dy1 = jnp.where(row < length - 1, pltpu.roll(dy, length - 1, 0), 0.0)
        dy2 = jnp.where(row < length - 2, pltpu.roll(dy, length - 2, 0), 0.0)
        du = w_ref[2:3, :] * dy + w_ref[1:2, :] * dy1 + w_ref[0:1, :] * dy2
        dx_ref[...] = du * c
        dc_ref[...] = du * x
        dw_ref[0:1, :] = jnp.sum(dy * u2, axis=0, keepdims=True)
        dw_ref[1:2, :] = jnp.sum(dy * u1, axis=0, keepdims=True)
        dw_ref[2:3, :] = jnp.sum(dy * u, axis=0, keepdims=True)
        dcb_ref[...] = jnp.sum(dy, axis=0, keepdims=True)

    col, wspec, bspec = _conv_specs(length)
    big = jax.ShapeDtypeStruct((length, MIX), F32)
    return pl.pallas_call(
        kern, name="conv_bwd", grid=(MIX // LANES,), in_specs=[col, col, col, wspec, bspec, col],
        out_specs=[col, col, col, wspec, bspec],
        out_shape=[big, big, big, jax.ShapeDtypeStruct((CONV_K, MIX), F32), jax.ShapeDtypeStruct((1, MIX), F32)],
        compiler_params=_params("parallel"),
    )(x, b, c, w, cb, do)


@jax.custom_vjp
def _short_conv(x, b, c, w, cb):
    return _conv_fwd(x, b, c, w, cb)


def _short_conv_fwd(x, b, c, w, cb):
    return _conv_fwd(x, b, c, w, cb), (x, b, c, w, cb)


def _short_conv_bwd(res, do):
    return tuple(_conv_bwd(*res, do))


_short_conv.defvjp(_short_conv_fwd, _short_conv_bwd)


SCAN_ROWS = S5_CH // LANES
SCAN_TC = 136


def _scan_fwd(ar, ai, b):
    length = b.shape[0]
    tc = _divisor_tile(length, SCAN_TC, 8)

    def kern(ar_ref, ai_ref, b_ref, x_ref, sr, si):
        @pl.when(pl.program_id(0) == 0)
        def _():
            sr[...] = jnp.zeros_like(sr)
            si[...] = jnp.zeros_like(si)

        a_re, a_im = ar_ref[...], ai_ref[...]

        def body(t, carry):
            xr, xi = carry
            nr = a_re * xr - a_im * xi + b_ref[t, 0:SCAN_ROWS, :]
            ni = a_re * xi + a_im * xr + b_ref[t, SCAN_ROWS:2 * SCAN_ROWS, :]
            x_ref[t, 0:SCAN_ROWS, :] = nr
            x_ref[t, SCAN_ROWS:2 * SCAN_ROWS, :] = ni
            return nr, ni

        xr, xi = lax.fori_loop(0, tc, body, (sr[...], si[...]), unroll=4)
        sr[...] = xr
        si[...] = xi

    par = pl.BlockSpec((SCAN_ROWS, LANES), lambda i: (0, 0))
    blk = pl.BlockSpec((tc, 2 * SCAN_ROWS, LANES), lambda i: (i, 0, 0))
    return pl.pallas_call(
        kern, name="s5_scan_fwd", grid=(length // tc,), in_specs=[par, par, blk], out_specs=blk,
        out_shape=jax.ShapeDtypeStruct(b.shape, F32), scratch_shapes=[pltpu.VMEM((SCAN_ROWS, LANES), F32)] * 2,
        compiler_params=_params("arbitrary"),
    )(ar, ai, b)


def _scan_bwd(ar, ai, x, dx):
    length = x.shape[0]
    tc = _divisor_tile(length, SCAN_TC, 8)
    n_blk = length // tc
    re, im = slice(0, SCAN_ROWS), slice(SCAN_ROWS, 2 * SCAN_ROWS)

    def kern(ar_ref, ai_ref, x_ref, dx_ref, db_ref, dar_ref, dai_ref, lr_s, li_s):
        @pl.when(pl.program_id(0) == 0)
        def _():
            lr_s[...] = jnp.zeros_like(lr_s)
            li_s[...] = jnp.zeros_like(li_s)
            dar_ref[...] = jnp.zeros_like(dar_ref)
            dai_ref[...] = jnp.zeros_like(dai_ref)

        a_re, a_im = ar_ref[...], ai_ref[...]

        def body(j, carry):
            t = tc - 1 - j
            lr, li, gr, gi = carry
            x_re, x_im = x_ref[t, re, :], x_ref[t, im, :]
            gr = gr + (lr * x_re + li * x_im)
            gi = gi + (li * x_re - lr * x_im)
            nlr = dx_ref[t, re, :] + (a_re * lr + a_im * li)
            nli = dx_ref[t, im, :] + (a_re * li - a_im * lr)
            db_ref[t, re, :] = nlr
            db_ref[t, im, :] = nli
            return nlr, nli, gr, gi

        lr, li, gr, gi = lax.fori_loop(0, tc, body, (lr_s[...], li_s[...], dar_ref[...], dai_ref[...]), unroll=4)
        lr_s[...] = lr
        li_s[...] = li
        dar_ref[...] = gr
        dai_ref[...] = gi

    par = pl.BlockSpec((SCAN_ROWS, LANES), lambda i: (0, 0))
    blk = pl.BlockSpec((tc, 2 * SCAN_ROWS, LANES), lambda i: (n_blk - 1 - i, 0, 0))
    pout = jax.ShapeDtypeStruct((SCAN_ROWS, LANES), F32)
    return pl.pallas_call(
        kern, name="s5_scan_bwd", grid=(n_blk,), in_specs=[par, par, blk, blk],
        out_specs=[blk, par, par], out_shape=[jax.ShapeDtypeStruct(x.shape, F32), pout, pout],
        scratch_shapes=[pltpu.VMEM((SCAN_ROWS, LANES), F32)] * 2, compiler_params=_params("arbitrary"),
    )(ar, ai, x, dx)


@jax.custom_vjp
def _s5_scan(ar, ai, b):
    return _scan_fwd(ar, ai, b)


def _s5_scan_fwd(ar, ai, b):
    x = _scan_fwd(ar, ai, b)
    return x, (ar, ai, x)


def _s5_scan_bwd(res, dx):
    ar, ai, x = res
    db, dar, dai = _scan_bwd(ar, ai, x, dx)
    return dar, dai, db


_s5_scan.defvjp(_s5_scan_fwd, _s5_scan_bwd)


def _loss_call(y, target, n_real):
    length = y.shape[0]
    tm = _divisor_tile(length, 544, 16)

    def kern(y_ref, t_ref, loss_ref, dy_ref):
        i = pl.program_id(0)
        row = i * tm + lax.broadcasted_iota(jnp.int32, (tm, 1), 0)
        keep = jnp.logical_and(row >= N_META, row < n_real)
        err = jnp.where(keep, y_ref[...] - t_ref[...], 0.0)
        dy_ref[...] = err * (1.0 / D_MODEL)
        part = 0.5 * jnp.sum(jnp.mean(err * err, axis=-1, keepdims=True), axis=0, keepdims=True)

        @pl.when(i == 0)
        def _():
            loss_ref[...] = jnp.zeros_like(loss_ref)

        loss_ref[...] += part

    blk = pl.BlockSpec((tm, D_MODEL), lambda i: (i, 0))
    return pl.pallas_call(
        kern, name="loss_head", grid=(length // tm,), in_specs=[blk, blk],
        out_specs=[pl.BlockSpec((8, LANES), lambda i: (0, 0)), blk],
        out_shape=[jax.ShapeDtypeStruct((8, LANES), F32), jax.ShapeDtypeStruct(y.shape, F32)],
        compiler_params=_params("arbitrary"),
    )(y, target)


def _make_loss(n_real):
    @jax.custom_vjp
    def loss(y, target):
        return _loss_call(y, target, n_real)[0][0, 0]

    def fwd(y, target):
        total, dy = _loss_call(y, target, n_real)
        return total[0, 0], dy

    def bwd(dy, ct):
        return dy * ct, jnp.zeros_like(dy)

    loss.defvjp(fwd, bwd)
    return loss


HBM_SPEC = pl.BlockSpec(memory_space=pl.ANY)
MESH_ID = pl.DeviceIdType.MESH


SC_MESH = dict(axis_name="sequencer", num_cores=1)
GATHER_ID, SLOT_ID = 1, 2


def _handshake(peers):
    barrier = pltpu.get_barrier_semaphore()
    for peer in peers:
        pl.semaphore_signal(barrier, inc=1, device_id=peer, device_id_type=MESH_ID)
    pl.semaphore_wait(barrier, len(peers))


def _exchange_call(body, name, ins, out_types, n_sems, sequencer_id):
    n_in, n_out = len(ins), len(out_types)
    sems = [pltpu.SemaphoreType.DMA((n_sems,)), pltpu.SemaphoreType.DMA((n_sems,)), pltpu.SemaphoreType.DMA((n_in,))]
    if sequencer_id is None:
        def on_core(*refs):
            body(lambda peers: None, refs[:n_in], refs[n_in:n_in + n_out], *refs[n_in + n_out:])

        return pl.pallas_call(on_core, name=name, out_shape=out_types, in_specs=[HBM_SPEC] * n_in,
                              out_specs=[HBM_SPEC] * n_out, scratch_shapes=sems)(*ins)

    def on_sequencer(*refs):
        body(_handshake, refs[:n_in], refs[n_in:n_in + n_out], *refs[n_in + n_out:])

    return pl.kernel(on_sequencer, name=name, out_type=out_types, mesh=plsc.ScalarSubcoreMesh(**SC_MESH),
                     scratch_types=sems, compiler_params=pltpu.CompilerParams(collective_id=sequencer_id))(*ins)


def _all_gather(shards, name, sequencer=False):
    n = len(shards)

    def body(handshake, x_refs, out_refs, send_sems, recv_sems, local_sems):
        x, y, c = lax.axis_index("x"), lax.axis_index("y"), lax.axis_index("c")
        me, sibling = (x, y, c), (x, y, 1 - c)
        chips = [(1 - x, y), (x, 1 - y), (1 - x, 1 - y)]
        handshake([sibling] + [(*chip, c) for chip in chips])

        def copy(b, k, block, to, from_input=False):
            px, py, pc = block
            slot = out_refs[b].at[4 * px + 2 * py + pc]
            return pltpu.make_async_remote_copy(
                src_ref=x_refs[b] if from_input else slot, dst_ref=slot,
                send_sem=send_sems.at[7 * b + k], recv_sem=recv_sems.at[7 * b + k], device_id=to, device_id_type=MESH_ID)

        mine = [pltpu.make_async_copy(x_refs[b], out_refs[b].at[4 * x + 2 * y + c], local_sems.at[b]) for b in range(n)]
        for cp in mine:
            cp.start()
        first = []
        for b in range(n):
            first.append(copy(b, 0, me, sibling, from_input=True))
            first += [copy(b, 1 + j, me, (*chip, c), from_input=True) for j, chip in enumerate(chips)]
        for cp in first:
            cp.start()
        passed = []
        for j, chip in enumerate(chips):
            for b in range(n):
                copy(b, 1 + j, (*chip, c), me).wait_recv()
                passed.append(copy(b, 4 + j, (*chip, c), sibling))
                passed[-1].start()
        for b in range(n):
            copy(b, 0, sibling, me).wait_recv()
            for j, chip in enumerate(chips):
                copy(b, 4 + j, (*chip, 1 - c), me).wait_recv()
        for cp in first + passed:
            cp.wait_send()
        for cp in mine:
            cp.wait()

    out_types = [jax.ShapeDtypeStruct((N_DEV, *s.shape), s.dtype) for s in shards]
    return _exchange_call(body, name, shards, out_types, 7 * n, GATHER_ID if sequencer else None)


def _slot_exchange(bufs, name, sequencer=False):
    def body(handshake, ins, outs, send_sems, recv_sems, local_sems):
        x, y, c = lax.axis_index("x"), lax.axis_index("y"), lax.axis_index("c")
        me = 4 * x + 2 * y + c
        flips = [(dx, dy, dc) for dx in (0, 1) for dy in (0, 1) for dc in (0, 1)][1:]
        peers = [(1 - x if dx else x, 1 - y if dy else y, 1 - c if dc else c) for dx, dy, dc in flips]
        handshake(peers)
        own = [pltpu.make_async_copy(src.at[me], dst.at[me], local_sems.at[b]) for b, (src, dst) in enumerate(zip(ins, outs))]
        copies = []
        for b, (src, dst) in enumerate(zip(ins, outs)):
            for k, (px, py, pc) in enumerate(peers):
                copies.append(pltpu.make_async_remote_copy(
                    src_ref=src.at[4 * px + 2 * py + pc], dst_ref=dst.at[me],
                    send_sem=send_sems.at[7 * b + k], recv_sem=recv_sems.at[7 * b + k],
                    device_id=(px, py, pc), device_id_type=MESH_ID))
        for cp in own + copies:
            cp.start()
        for cp in copies + own:
            cp.wait()

    out_types = [jax.ShapeDtypeStruct(b.shape, b.dtype) for b in bufs]
    return _exchange_call(body, name, bufs, out_types, 7 * len(bufs), SLOT_ID if sequencer else None)


def _slot_sum(slots, name):
    _, rows, cols = slots.shape
    tm = _divisor_tile(rows, 512, 16)

    def kern(s_ref, o_ref):
        total = s_ref[0].astype(F32)
        for d in range(1, N_DEV):
            total = total + s_ref[d].astype(F32)
        o_ref[...] = total

    return pl.pallas_call(
        kern, name=name, grid=(rows // tm,), in_specs=[pl.BlockSpec((N_DEV, tm, cols), lambda i: (0, i, 0))],
        out_specs=pl.BlockSpec((tm, cols), lambda i: (i, 0)), out_shape=jax.ShapeDtypeStruct((rows, cols), F32),
        compiler_params=_params("parallel"),
    )(slots)


def _reduce_scatter(bufs, tag, sequencer):
    arrived = _slot_exchange(bufs, f"rs_exchange_{tag}", sequencer)
    return [_slot_sum(a, f"rs_sum_{tag}{i}") for i, a in enumerate(arrived)]


def _adamw(w, g, m, v, name):
    rows, cols = w.shape
    tm = _divisor_tile(rows, max(8, (512 * 1024) // cols // 8 * 8), 8)

    def kern(w_ref, g_ref, m_ref, v_ref, d_ref, nm_ref, nv_ref):
        g = g_ref[...]
        m = ADAM_B1 * m_ref[...] + (1.0 - ADAM_B1) * g
        v = ADAM_B2 * v_ref[...] + (1.0 - ADAM_B2) * (g * g)
        m_hat = m / (1.0 - ADAM_B1 ** ADAM_STEP)
        v_hat = v / (1.0 - ADAM_B2 ** ADAM_STEP)
        d_ref[...] = -ADAM_LR * (m_hat / (jnp.sqrt(v_hat) + ADAM_EPS) + ADAM_WD * w_ref[...])
        nm_ref[...] = m
        nv_ref[...] = v

    blk = pl.BlockSpec((tm, cols), lambda i: (i, 0))
    out = jax.ShapeDtypeStruct(w.shape, F32)
    return pl.pallas_call(
        kern, name=name, grid=(rows // tm,), in_specs=[blk] * 4, out_specs=[blk] * 3, out_shape=[out] * 3,
        compiler_params=_params("parallel"),
    )(w, g, m, v)


def _adamw_layers(w, g_layers, m, v, name):
    depth, rows, cols = w.shape
    tm = _divisor_tile(rows, max(8, (512 * 1024) // cols // 8 * 8), 8)

    def kern(w_ref, m_ref, v_ref, *refs):
        g_refs, (g_out, d_ref, nm_ref, nv_ref) = refs[:depth], refs[depth:]
        layer = pl.program_id(0)
        g = g_refs[0][...]
        for l in range(1, depth):
            g = jnp.where(layer == l, g_refs[l][...], g)
        m = ADAM_B1 * m_ref[0] + (1.0 - ADAM_B1) * g
        v = ADAM_B2 * v_ref[0] + (1.0 - ADAM_B2) * (g * g)
        m_hat = m / (1.0 - ADAM_B1 ** ADAM_STEP)
        v_hat = v / (1.0 - ADAM_B2 ** ADAM_STEP)
        g_out[0] = g
        d_ref[0] = -ADAM_LR * (m_hat / (jnp.sqrt(v_hat) + ADAM_EPS) + ADAM_WD * w_ref[0])
        nm_ref[0] = m
        nv_ref[0] = v

    blk = pl.BlockSpec((1, tm, cols), lambda l, i: (l, i, 0))
    out = jax.ShapeDtypeStruct(w.shape, F32)
    return pl.pallas_call(
        kern, name=name, grid=(depth, rows // tm),
        in_specs=[blk] * 3 + [pl.BlockSpec((tm, cols), lambda l, i: (i, 0))] * depth,
        out_specs=[blk] * 4, out_shape=[out] * 4, compiler_params=_params("parallel", "parallel"),
    )(w, m, v, *g_layers)


TRANSPOSED = ('ffn1_w_gate', 'ffn1_w_up', 'ffn2_w_gate', 'ffn2_w_up', 'w_in', 'mla_w_uq', 'mla_w_ukv', 'mla_w_o',
              'conv_w_out', 's5_w_out')
PIECES = ((('ffn1_w_gate',), ('ffn1_w_up',), ('ffn1_w_down',)),
          (('w_in', 'w_o'), ('mla_w_o', 'conv_w_out', 's5_w_out', 's5_w_glu'), ('mla_w_uq',), ('mla_w_ukv',)),
          (('ffn2_w_gate',), ('ffn2_w_up',), ('ffn2_w_down',)))
PIN_CUTS = (PIN_CQ, PIN_CKV, PIN_KR1, PIN_KR2, PIN_XBAR, PIN_BG, PIN_CG, PIN_U, PIN_GATES, PIN_GATES + D_MODEL,
            PIN_GATES + 2 * D_MODEL, PIN_END, D_IN_PAD)
PIN_PIECES = ((PIN_CQ, Q_RANK), (PIN_CKV, KV_RANK), (PIN_KR1, HALF_ROPE), (PIN_KR2, HALF_ROPE), (PIN_XBAR, 4 * MIX),
              (PIN_GATES, 3 * D_MODEL))


def _make_split(cuts):
    @jax.custom_vjp
    def split(t):
        return tuple(t[:, a:b] for a, b in zip(cuts[:-1], cuts[1:]))

    def fwd(t):
        return split(t), None

    def bwd(_, cts):
        return (jnp.concatenate(cts, axis=1),)

    split.defvjp(fwd, bwd)
    return split


def _make_projection(tag, cuts):
    cast = _make_rowwise(_f_cast, 1, 0, (BF16,), f"{tag}_cast")

    def forward(h, w):
        hb, = cast.run_fwd((h,), ())
        full = _matmul(hb, w, tb=True, name=f"{tag}_fwd")
        return tuple(full[:, a:b] for a, b in zip(cuts[:-1], cuts[1:])), (hb, w)

    @jax.custom_vjp
    def proj(h, w, wz):
        return forward(h, w)[0]

    def fwd(h, w, wz):
        return forward(h, w)

    def bwd(res, cts):
        hb, w = res
        d_full = jnp.concatenate([c.astype(BF16) for c in cts], axis=1)
        dw = _matmul(d_full, hb, ta=True, out_dtype=BF16, name=f"{tag}_dw")
        return _matmul(d_full, w, name=f"{tag}_dx"), jnp.zeros_like(w), dw

    proj.defvjp(fwd, bwd)
    return proj


def _travel_shape(name, shape):
    return (shape[2], shape[1]) if name in TRANSPOSED else (shape[1], shape[2])


def _pack_groups(tensors, layer, groups, dtype):
    def view(n):
        t = tensors[n][layer]
        return (t.T if n in TRANSPOSED else t).astype(dtype)
    return [jnp.concatenate([view(n) for n in grp], axis=0) for grp in groups]


def _unpack_groups(bufs, groups, shard_shapes):
    out = {}
    for buf, grp in zip(bufs, groups):
        at = 0
        for n in grp:
            r, _ = _travel_shape(n, shard_shapes[n])
            out[n] = buf[..., at:at + r, :]
            at += r
    return out


def _pack_rows(arrays):
    flat = jnp.concatenate([a.reshape(-1) for a in arrays])
    rows = -(-flat.shape[0] // PACK_COLS)
    rows = -(-rows // 8) * 8
    return jnp.pad(flat, (0, rows * PACK_COLS - flat.shape[0])).reshape(rows, PACK_COLS)


def _unpack_rows(buf, shapes):
    flat = buf.reshape(-1)
    out, at = [], 0
    for s in shapes:
        n = int(np.prod(s))
        out.append(flat[at:at + n].reshape(s))
        at += n
    return out


def _full_weight(t, axis):
    if axis == 1:
        return jnp.moveaxis(t, 0, 1).reshape(t.shape[1], N_DEV * t.shape[2], t.shape[3])
    return jnp.moveaxis(t, 0, 2).reshape(t.shape[1], t.shape[2], N_DEV * t.shape[3])


def _disassemble(d, groups, dtype):
    heads = lambda t: t.reshape(HEADS, -1, t.shape[1])
    full = {}
    for tag in ('ffn1', 'ffn2'):
        if f'{tag}_gate' in d:
            full.update({f'{tag}_w_gate': d[f'{tag}_gate'], f'{tag}_w_up': d[f'{tag}_up'], f'{tag}_w_down': d[f'{tag}_down']})
    if 'w_in' in d:
        w_in, uq, ukv = d['w_in'], d['w_uq'], d['w_ukv']
        full.update(
            w_in=jnp.concatenate([w_in[a:a + n] for a, n in PIN_PIECES], axis=0),
            mla_w_uq=jnp.concatenate([heads(uq[:HEADS * NOPE]), heads(uq[HEADS * NOPE:HEADS * NOPE + LANES]),
                                      heads(uq[HEADS * NOPE + LANES:])], axis=1).reshape(HEADS * QK_DIM, Q_RANK),
            mla_w_ukv=jnp.concatenate([heads(ukv[:HEADS * NOPE]), heads(ukv[HEADS * NOPE:])],
                                      axis=1).reshape(HEADS * (NOPE + V_DIM), KV_RANK),
            mla_w_o=d['mla_w_o'], conv_w_out=d['conv_w_out'], s5_w_glu=d['s5_w_glu'], s5_w_out=d['s5_w_out'], w_o=d['w_o'])
    return [jnp.concatenate([full[n].reshape(N_DEV, -1, full[n].shape[-1]).astype(dtype) for n in grp], axis=1)
            for grp in groups]


def _assemble(gathered, groups, shard_shapes):
    full = {n: t.reshape(N_DEV * t.shape[1], t.shape[2])
            for n, t in _unpack_groups(gathered, groups, shard_shapes).items()}
    out = {}
    for tag in ('ffn1', 'ffn2'):
        if f'{tag}_w_gate' in full:
            out.update({f'{tag}_gate': full[f'{tag}_w_gate'], f'{tag}_up': full[f'{tag}_w_up'],
                        f'{tag}_down': full[f'{tag}_w_down']})
    if 'w_in' not in full:
        return out
    w_in = full['w_in']
    cuts = np.cumsum((0,) + IN_SPLITS)
    cq, ckv, kr, xbar, bg, cg, u, gates = [w_in[a:b] for a, b in zip(cuts[:-1], cuts[1:])]
    pad = lambda t, n: jnp.pad(t, ((0, n - t.shape[0]), (0, 0)))
    w_in_packed = jnp.concatenate(
        [cq, ckv, pad(kr[:HALF_ROPE], LANES), pad(kr[HALF_ROPE:], LANES), xbar, bg, cg, u, gates,
         jnp.zeros((D_IN_PAD - PIN_END, D_MODEL), w_in.dtype)], axis=0)
    uq = full['mla_w_uq'].reshape(HEADS, QK_DIM, Q_RANK)
    w_uq = jnp.concatenate([uq[:, :NOPE].reshape(HEADS * NOPE, Q_RANK),
                            uq[:, NOPE:NOPE + HALF_ROPE].reshape(HEADS * HALF_ROPE, Q_RANK),
                            uq[:, NOPE + HALF_ROPE:].reshape(HEADS * HALF_ROPE, Q_RANK)], axis=0)
    ukv = full['mla_w_ukv'].reshape(HEADS, NOPE + V_DIM, KV_RANK)
    w_ukv = jnp.concatenate([ukv[:, :NOPE].reshape(HEADS * NOPE, KV_RANK),
                             ukv[:, NOPE:].reshape(HEADS * V_DIM, KV_RANK)], axis=0)
    out.update(w_in=w_in_packed, w_uq=w_uq, w_ukv=w_ukv, mla_w_o=full['mla_w_o'], conv_w_out=full['conv_w_out'],
               s5_w_glu=full['s5_w_glu'], s5_w_out=full['s5_w_out'], w_o=full['w_o'])
    return out


def _s5_discretize(a_re, a_im, log_dt, b_re, b_im, c_re, c_im):
    dt = jnp.exp(log_dt)[:, None]
    mag = jnp.exp(dt * a_re)
    ab_re, ab_im = mag * jnp.cos(dt * a_im), mag * jnp.sin(dt * a_im)
    den = a_re * a_re + a_im * a_im
    nr, ni = ab_re - 1.0, ab_im
    coef_re = (nr * a_re + ni * a_im) / den
    coef_im = (ni * a_re - nr * a_im) / den
    bb_re = coef_re[..., None] * b_re - coef_im[..., None] * b_im
    bb_im = coef_re[..., None] * b_im + coef_im[..., None] * b_re
    unit = jnp.arange(MIX)[:, None]
    chan = jnp.arange(2 * S5_CH)[None, :]
    pair = jnp.arange(2 * S5_STATE)[:, None]
    own = unit // S5_GROUP == (chan % S5_CH) // S5_STATE
    copy = jnp.logical_and(pair // S5_STATE == chan // S5_CH, pair % S5_STATE == chan % S5_STATE).astype(F32)
    flat_b = lambda bb: bb.transpose(0, 2, 1).reshape(MIX, S5_STATE)
    flat_c = lambda cc: cc.transpose(2, 0, 1).reshape(S5_STATE, MIX)
    b_small = jnp.concatenate([flat_b(bb_re), flat_b(bb_im)], axis=1)
    c_small = jnp.concatenate([flat_c(c_re), -flat_c(c_im)], axis=0)
    b_map = jnp.where(own, _make_mm_f32w("s5_spread_b")(b_small, copy), 0.0)
    c_map = jnp.where(own.T, _make_mm_f32w("s5_spread_c")(copy.T, c_small), 0.0)
    return ab_re.reshape(SCAN_ROWS, LANES), ab_im.reshape(SCAN_ROWS, LANES), b_map, c_map


def _rope_tables(length):
    inv_freq = ROPE_BASE ** (-jnp.arange(0, ROPE, 2, dtype=F32) / ROPE)
    ang = jnp.arange(length).astype(F32)[:, None] * inv_freq[None, :]
    return jnp.tile(jnp.cos(ang), (1, LANES // HALF_ROPE)), jnp.tile(jnp.sin(ang), (1, LANES // HALF_ROPE))


def _heads_first(t):
    return t.reshape(t.shape[0], HEADS, -1).transpose(1, 0, 2)


def _local_loss(diff, big, n_real):
    small, wz = diff['small'], diff['wz']
    h = diff['h0']
    length = h.shape[0]
    cos, sin = _rope_tables(length)
    row2 = lambda v: v.reshape(1, -1)
    for l in range(DEPTH):
        w, z = big[l], wz[l]
        p = {k: small[k][l] for k in small if k != 'meta'}
        ffn = lambda h, tag, ln: _make_ffn(tag)(
            h, *[w[f'{tag}_{k}'] for k in ('gate', 'up', 'down')], *[z[f'{tag}_{k}'] for k in ('gate', 'up', 'down')],
            row2(p[f'{ln}_g']), row2(p[f'{ln}_b']))
        h = ffn(h, "ffn1", "ln1")
        cq, ckv, kr1, kr2, xbar, bg, cg, u, gate_a, gate_b, gate_c, _ = _make_projection("w_in", PIN_CUTS)(
            h, w['w_in'], z['w_in'])
        qn, = _make_rowwise(_f_rms, 1, 1, (BF16,), "q_rms")(cq, row2(p['mla_q_norm_g']))
        kvn, = _make_rowwise(_f_rms, 1, 1, (BF16,), "kv_rms")(ckv, row2(p['mla_kv_norm_g']))
        q_nope, q1, q2 = _make_split((0, HEADS * NOPE, HEADS * NOPE + LANES, HEADS * NOPE + 2 * LANES))(
            _make_mm("w_uq", wt=True)(qn, w['w_uq'], z['w_uq']))
        k_nope, val = _make_split((0, HEADS * NOPE, HEADS * (NOPE + V_DIM)))(
            _make_mm("w_ukv", wt=True)(kvn, w['w_ukv'], z['w_ukv']))
        rope = _make_rowwise(_f_rope, 4, 0, (BF16, BF16), "rope", n_nodiff=2)
        q1, q2 = rope(q1, q2, cos, sin)
        k1, k2 = rope(kr1, kr2, cos, sin)
        hpad = jnp.zeros((HEADS, length, HEAD_PAD - QK_DIM), BF16)
        q3 = jnp.concatenate([_heads_first(q_nope.astype(BF16)), _heads_first(q1), _heads_first(q2), hpad], -1)
        shared = lambda t: jnp.broadcast_to(t[None, :, :HALF_ROPE], (HEADS, length, HALF_ROPE))
        k3 = jnp.concatenate([_heads_first(k_nope.astype(BF16)), shared(k1), shared(k2), hpad], -1)
        v3 = _heads_first(val.astype(BF16))
        o3 = _attention(q3, k3, v3)
        y_a = _make_mm("mla_w_o", wt=True)(o3.transpose(1, 0, 2).reshape(length, MIX), w['mla_w_o'], z['mla_w_o'])
        conv = _short_conv(xbar, bg, cg, p['conv_w_full'], row2(p['conv_b']))
        y_b = _make_mm("conv_w_out", wt=True)(conv, w['conv_w_out'], z['conv_w_out'])
        ar, ai, b_map, c_map = _s5_discretize(p['s5_a_re'], p['s5_a_im'], p['s5_log_dt'], p['s5_b_re'], p['s5_b_im'],
                                              p['s5_c_re'], p['s5_c_im'])
        bu = _make_bd_in("s5_b")(u, b_map)
        states = _s5_scan(ar, ai, bu.reshape(length, 2 * SCAN_ROWS, LANES)).reshape(length, 2 * S5_CH)
        y_ssm = _make_bd_out("s5_c")(states, c_map)
        zed, = _make_rowwise(_f_gelu_skip, 2, 1, (F32,), "s5_gelu")(y_ssm, u, row2(p['s5_d']))
        t = _make_mm("s5_w_glu")(zed, w['s5_w_glu'], z['s5_w_glu'])
        glu, = _make_rowwise(_f_glu, 2, 1, (BF16,), "s5_glu")(zed, t, row2(p['s5_b_glu']))
        y_c = _make_mm("s5_w_out", wt=True)(glu, w['s5_w_out'], z['s5_w_out'])
        mixed, = _make_rowwise(_f_merge, 6, 0, (BF16,), "merge")(gate_a, gate_b, gate_c, y_a, y_b, y_c)
        mix_out = _make_mm("w_o")(mixed, w['w_o'], z['w_o'])
        h, = _make_rowwise(_f_ln_full, 2, 2, (F32,), "mix_ln")(h, mix_out, row2(p['ln2_g']), row2(p['ln2_b']))
        h = ffn(h, "ffn2", "ln3")
    return _make_loss(n_real)(h, diff['target'])


def kernel(x, meta, ffn1_w_gate, ffn1_w_up, ffn1_w_down, ln1_g, ln1_b, w_in, mla_q_norm_g, mla_w_uq, mla_kv_norm_g, mla_w_ukv, mla_w_o, conv_w, conv_b, conv_w_out, s5_a_re, s5_a_im, s5_log_dt, s5_b_re, s5_b_im, s5_c_re, s5_c_im, s5_d, s5_w_glu, s5_b_glu, s5_w_out, w_o, ln2_g, ln2_b, ffn2_w_gate, ffn2_w_up, ffn2_w_down, ln3_g, ln3_b, loss_target, m_meta, m_ffn1_w_gate, m_ffn1_w_up, m_ffn1_w_down, m_ln1_g, m_ln1_b, m_w_in, m_mla_q_norm_g, m_mla_w_uq, m_mla_kv_norm_g, m_mla_w_ukv, m_mla_w_o, m_conv_w, m_conv_b, m_conv_w_out, m_s5_a_re, m_s5_a_im, m_s5_log_dt, m_s5_b_re, m_s5_b_im, m_s5_c_re, m_s5_c_im, m_s5_d, m_s5_w_glu, m_s5_b_glu, m_s5_w_out, m_w_o, m_ln2_g, m_ln2_b, m_ffn2_w_gate, m_ffn2_w_up, m_ffn2_w_down, m_ln3_g, m_ln3_b, v_meta, v_ffn1_w_gate, v_ffn1_w_up, v_ffn1_w_down, v_ln1_g, v_ln1_b, v_w_in, v_mla_q_norm_g, v_mla_w_uq, v_mla_kv_norm_g, v_mla_w_ukv, v_mla_w_o, v_conv_w, v_conv_b, v_conv_w_out, v_s5_a_re, v_s5_a_im, v_s5_log_dt, v_s5_b_re, v_s5_b_im, v_s5_c_re, v_s5_c_im, v_s5_d, v_s5_w_glu, v_s5_b_glu, v_s5_w_out, v_w_o, v_ln2_g, v_ln2_b, v_ffn2_w_gate, v_ffn2_w_up, v_ffn2_w_down, v_ln3_g, v_ln3_b):
    args = locals()
    w = {n: args[n] for n in WEIGHT_NAMES}
    m = {n: args["m_" + n] for n in WEIGHT_NAMES}
    v = {n: args["v_" + n] for n in WEIGHT_NAMES}
    me = 4 * lax.axis_index("x") + 2 * lax.axis_index("y") + lax.axis_index("c")
    seq = x.shape[1]
    n_real = N_META + seq
    length = -(-n_real // LANES) * LANES

    shard_shapes = {n: w[n].shape for n in BIG}
    small_shards = _all_gather([_pack_rows([w[n] for n in SMALL_SHARDED])], "gather_small",
                               sequencer=True)[0].reshape(N_DEV, -1)
    big = [{} for _ in range(DEPTH)]
    for l in range(DEPTH):
        for p, groups in enumerate(PIECES):
            packed = _pack_groups(w, l, groups, BF16)
            if (l, p) == (0, 1):
                packed = lax.optimization_barrier((gathered, packed))[1]
            gathered = _all_gather(packed, f"gather_weights_layer{l}_piece{p}", sequencer=True)
            big[l].update(_assemble(gathered, groups, shard_shapes))
    meta_full = _full_weight(small_shards[:, :meta.size].reshape(N_DEV, 1, *meta.shape), 2)[0]
    conv_w_full = _full_weight(small_shards[:, meta.size:meta.size + conv_w.size].reshape(N_DEV, *conv_w.shape), 2)

    small = {n: w[n] for n in SMALL_NAMES if n not in SMALL_SHARDED}
    small['conv_w_full'] = conv_w_full
    small['meta'] = meta_full
    wz = jax.tree.map(lambda t: jnp.zeros(t.shape, BF16), big)
    pad_rows = length - n_real

    def loss_fn(diff):
        h0 = jnp.concatenate([diff['small']['meta'], diff['x'], jnp.zeros((pad_rows, D_MODEL), F32)], axis=0)
        target = jnp.pad(loss_target[0], ((N_META, pad_rows), (0, 0)))
        return _local_loss(dict(h0=h0, small=diff['small'], wz=diff['wz'], target=target), big, n_real)

    loss_local, grads = jax.value_and_grad(loss_fn)(dict(x=x[0], small=small, wz=wz))

    small_names = [n for n in SMALL_NAMES if n not in SMALL_SHARDED] + ['conv_w_full', 'meta']
    small_flat = _pack_rows([loss_local.reshape(1)] + [grads['small'][n] for n in small_names])
    rows_each = -(-small_flat.shape[0] // (8 * N_DEV)) * 8
    small_flat = jnp.pad(small_flat, ((0, rows_each * N_DEV - small_flat.shape[0]), (0, 0)))
    layer_sums = [{} for _ in range(DEPTH)]
    for l in reversed(range(DEPTH)):
        for p, groups in reversed(list(enumerate(PIECES))):
            last = l == 0 and p == 0
            extra = [small_flat.reshape(N_DEV, rows_each, PACK_COLS)] if last else []
            sums = _reduce_scatter(_disassemble(grads['wz'][l], groups, BF16) + extra, f"grads_layer{l}_piece{p}_",
                                   sequencer=True)
            layer_sums[l].update(_unpack_groups(sums[:len(groups)], groups, shard_shapes))
            if last:
                small_sum = sums[-1]
    small_all, = _all_gather([small_sum], "gather_small_grads")
    loss, *small_sums = _unpack_rows(small_all, [()] + [grads['small'][n].shape for n in small_names])
    small_grads = dict(zip(small_names, small_sums))
    g = {}
    for name in SMALL_NAMES:
        if name == 'meta':
            g[name] = lax.dynamic_slice_in_dim(small_grads['meta'], me * meta.shape[1], meta.shape[1], axis=1)
        elif name == 'conv_w':
            g[name] = lax.dynamic_slice_in_dim(small_grads['conv_w_full'], me * conv_w.shape[2], conv_w.shape[2], axis=2)
        else:
            g[name] = small_grads[name]

    delta, new_m, new_v = {}, {}, {}
    for name in BIG:
        per_layer = [s[name].T if name in TRANSPOSED else s[name] for s in layer_sums]
        g[name], delta[name], new_m[name], new_v[name] = _adamw_layers(w[name], per_layer, m[name], v[name], f"adamw_{name}")
    shapes = [w[n].shape for n in SMALL_NAMES]
    d, nm, nv = _adamw(*[_pack_rows([t[n] for n in SMALL_NAMES]) for t in (w, g, m, v)], "adamw_small")
    for out, buf in ((delta, d), (new_m, nm), (new_v, nv)):
        out.update(zip(SMALL_NAMES, _unpack_rows(buf, shapes)))

    return (loss, grads['x'][None], *[g[n] for n in WEIGHT_NAMES], *[delta[n] for n in WEIGHT_NAMES],
            *[new_m[n] for n in WEIGHT_NAMES], *[new_v[n] for n in WEIGHT_NAMES])
```

```python
import jax
import jax.numpy as jnp
import numpy as np
from jax import lax
from jax.experimental import pallas as pl
from jax.experimental.pallas import tpu as pltpu
from jax.experimental.pallas import tpu_sc as plsc

F32 = jnp.float32
BF16 = jnp.bfloat16

D_MODEL = 1024
DEPTH = 2
N_META = 16
HEADS = 8
V_DIM = 64
NOPE = 64
ROPE = 32
HALF_ROPE = ROPE // 2
QK_DIM = NOPE + ROPE
Q_RANK = 384
KV_RANK = 256
MIX = 512
CONV_K = 3
S5_GROUPS = 32
S5_GROUP = 16
S5_STATE = 64
S5_CH = S5_GROUPS * S5_STATE
D_FF = 2816
ALPHA = (2.0 * DEPTH) ** 0.25
LN_EPS = 1e-5
RMS_EPS = 1e-6
ROPE_BASE = 10000.0
IN_SPLITS = (Q_RANK, KV_RANK, ROPE, MIX, MIX, MIX, MIX, 3 * D_MODEL)
D_IN = sum(IN_SPLITS)
ADAM_LR, ADAM_B1, ADAM_B2, ADAM_EPS, ADAM_WD, ADAM_STEP = 0.001, 0.9, 0.999, 1e-08, 0.01, 10

N_DEV = 8
AXES = ("x", "y", "c")
LANES = 128
PACK_COLS = 1024
HEAD_PAD = 128
VMEM_LIMIT = 48 * 1024 * 1024

PIN_CQ, PIN_CKV, PIN_KR1, PIN_KR2, PIN_XBAR, PIN_BG, PIN_CG, PIN_U, PIN_GATES, PIN_END = (
    0, 384, 640, 768, 896, 1408, 1920, 2432, 2944, 6016)
D_IN_PAD = 6144

WEIGHT_NAMES = ['meta', 'ffn1_w_gate', 'ffn1_w_up', 'ffn1_w_down', 'ln1_g', 'ln1_b', 'w_in', 'mla_q_norm_g', 'mla_w_uq',
                'mla_kv_norm_g', 'mla_w_ukv', 'mla_w_o', 'conv_w', 'conv_b', 'conv_w_out', 's5_a_re', 's5_a_im',
                's5_log_dt', 's5_b_re', 's5_b_im', 's5_c_re', 's5_c_im', 's5_d', 's5_w_glu', 's5_b_glu', 's5_w_out',
                'w_o', 'ln2_g', 'ln2_b', 'ffn2_w_gate', 'ffn2_w_up', 'ffn2_w_down', 'ln3_g', 'ln3_b']
BIG = {'ffn1_w_gate': 2, 'ffn1_w_up': 2, 'ffn1_w_down': 1, 'w_in': 2, 'mla_w_uq': 2, 'mla_w_ukv': 2, 'mla_w_o': 2,
       'conv_w_out': 2, 's5_w_glu': 1, 's5_w_out': 2, 'w_o': 1, 'ffn2_w_gate': 2, 'ffn2_w_up': 2, 'ffn2_w_down': 1}
SMALL_SHARDED = ('meta', 'conv_w')
SMALL_NAMES = [n for n in WEIGHT_NAMES if n not in BIG]


def _divisor_tile(n, limit, mult):
    best = None
    for t in range(mult, min(n, limit) + 1, mult):
        if n % t == 0:
            best = t
    return best if best is not None else n


def _params(*sem):
    return pltpu.CompilerParams(dimension_semantics=sem, vmem_limit_bytes=VMEM_LIMIT)


def _matmul(a, b, *, ta=False, tb=False, out_dtype=F32, add=None, name):
    m, k = (a.shape[1], a.shape[0]) if ta else a.shape
    n = b.shape[0] if tb else b.shape[1]
    assert (b.shape[1] if tb else b.shape[0]) == k, (a.shape, b.shape, ta, tb)
    tm = _divisor_tile(m, 1408, LANES) if ta else _divisor_tile(m, 2176 if a.dtype == BF16 else 1088, 16)
    tn = _divisor_tile(n, 512, LANES)
    whole_k = ta and a.dtype == BF16 and b.dtype == BF16
    tk = _divisor_tile(k, 2176 if whole_k else 1408, 16 if ta else LANES)
    nk = k // tk
    dims = (((0 if ta else 1,), (1 if tb else 0,)), ((), ()))

    in_place = jnp.dtype(out_dtype) == jnp.dtype(F32)

    def kern(a_ref, b_ref, *rest):
        add_ref = rest[0] if add is not None else None
        o_ref, *scratch = rest[1:] if add is not None else rest
        kk = pl.program_id(2)
        part = lax.dot_general(a_ref[...].astype(BF16), b_ref[...].astype(BF16), dims, preferred_element_type=F32)
        first = lambda: part if add_ref is None else part + add_ref[...].astype(F32)
        if nk == 1:
            o_ref[...] = first().astype(o_ref.dtype)
            return
        acc_ref = o_ref if in_place else scratch[0]

        @pl.when(kk == 0)
        def _():
            acc_ref[...] = first()

        @pl.when(kk > 0)
        def _():
            acc_ref[...] += part

        if not in_place:
            @pl.when(kk == nk - 1)
            def _():
                o_ref[...] = acc_ref[...].astype(o_ref.dtype)

    a_spec = pl.BlockSpec((tk, tm), lambda i, j, kk: (kk, i)) if ta else pl.BlockSpec((tm, tk), lambda i, j, kk: (i, kk))
    b_spec = pl.BlockSpec((tn, tk), lambda i, j, kk: (j, kk)) if tb else pl.BlockSpec((tk, tn), lambda i, j, kk: (kk, j))
    o_spec = pl.BlockSpec((tm, tn), lambda i, j, kk: (i, j))
    return pl.pallas_call(
        kern, name=name, grid=(m // tm, n // tn, nk),
        in_specs=[a_spec, b_spec] + ([o_spec] if add is not None else []), out_specs=o_spec,
        out_shape=jax.ShapeDtypeStruct((m, n), out_dtype),
        scratch_shapes=[] if (nk == 1 or in_place) else [pltpu.VMEM((tm, tn), F32)],
        compiler_params=_params("parallel", "parallel", "arbitrary"),
    )(a, b, *([add] if add is not None else []))


def _make_mm(name, wt=False):
    @jax.custom_vjp
    def mm(x, w, wz):
        return _matmul(x, w, tb=wt, name=name + "_fwd")

    def fwd(x, w, wz):
        return _matmul(x, w, tb=wt, name=name + "_fwd"), (x, w)

    def bwd(res, dy):
        x, w = res
        wz_dtype = BF16
        dx = _matmul(dy, w, tb=not wt, out_dtype=x.dtype, name=name + "_dx")
        dw = (_matmul(dy, x, ta=True, out_dtype=wz_dtype, name=name + "_dw") if wt
              else _matmul(x, dy, ta=True, out_dtype=wz_dtype, name=name + "_dw"))
        return dx, jnp.zeros_like(w), dw

    mm.defvjp(fwd, bwd)
    return mm


def _make_mm_f32w(name):
    @jax.custom_vjp
    def mm(x, w):
        return _matmul(x, w, name=name + "_fwd")

    def fwd(x, w):
        return _matmul(x, w, name=name + "_fwd"), (x, w)

    def bwd(res, dy):
        x, w = res
        return (_matmul(dy, w, tb=True, out_dtype=x.dtype, name=name + "_dx"),
                _matmul(x, dy, ta=True, name=name + "_dw"))

    mm.defvjp(fwd, bwd)
    return mm


BD_BLOCKS = 4
BD_PARTS = 2


def _bd_call(a, b, out_shape, a_blk, b_blk, o_blk, a_idx, b_idx, o_idx, dims, reduce_parts, name):
    def kern(a_ref, b_ref, o_ref):
        part = lax.dot_general(a_ref[...].astype(BF16), b_ref[...].astype(BF16), dims, preferred_element_type=F32)
        if not reduce_parts:
            o_ref[...] = part.astype(o_ref.dtype)
            return

        @pl.when(pl.program_id(1) == 0)
        def _():
            o_ref[...] = part

        @pl.when(pl.program_id(1) > 0)
        def _():
            o_ref[...] += part

    return pl.pallas_call(
        kern, name=name, grid=(BD_BLOCKS, BD_PARTS),
        in_specs=[pl.BlockSpec(a_blk, a_idx), pl.BlockSpec(b_blk, b_idx)], out_specs=pl.BlockSpec(o_blk, o_idx),
        out_shape=jax.ShapeDtypeStruct(out_shape, F32),
        compiler_params=_params("parallel", "arbitrary" if reduce_parts else "parallel"),
    )(a, b)


def _make_bd_in(name):
    wide = lambda j, p: (0, BD_BLOCKS * p + j)
    thin = lambda j, p: (0, j)
    diag = lambda j, p: (j, BD_BLOCKS * p + j)
    nn, nt, tn = (((1,), (0,)), ((), ())), (((1,), (1,)), ((), ())), (((0,), (0,)), ((), ()))

    def run(x, w):
        length, cols = x.shape[0], w.shape[1] // (BD_BLOCKS * BD_PARTS)
        return _bd_call(x, w, (length, w.shape[1]), (length, LANES), (LANES, cols), (length, cols), thin, diag, wide,
                        nn, False, name + "_fwd")

    @jax.custom_vjp
    def mm(x, w):
        return run(x, w)

    def fwd(x, w):
        return run(x, w), (x, w)

    def bwd(res, dy):
        x, w = res
        length, cols = x.shape[0], w.shape[1] // (BD_BLOCKS * BD_PARTS)
        dx = _bd_call(dy, w, x.shape, (length, cols), (LANES, cols), (length, LANES), wide, diag, thin, nt, True,
                      name + "_dx")
        dw = _bd_call(x, dy, w.shape, (length, LANES), (length, cols), (LANES, cols), thin, wide, diag, tn, False,
                      name + "_dw")
        return dx, dw

    mm.defvjp(fwd, bwd)
    return mm


def _make_bd_out(name):
    wide = lambda j, p: (0, BD_BLOCKS * p + j)
    thin = lambda j, p: (0, j)
    diag = lambda j, p: (BD_BLOCKS * p + j, j)
    nn, nt, tn = (((1,), (0,)), ((), ())), (((1,), (1,)), ((), ())), (((0,), (0,)), ((), ()))

    def run(x, w):
        length, cols = x.shape[0], w.shape[0] // (BD_BLOCKS * BD_PARTS)
        return _bd_call(x, w, (length, w.shape[1]), (length, cols), (cols, LANES), (length, LANES), wide, diag, thin,
                        nn, True, name + "_fwd")

    @jax.custom_vjp
    def mm(x, w):
        return run(x, w)

    def fwd(x, w):
        return run(x, w), (x, w)

    def bwd(res, dy):
        x, w = res
        length, cols = x.shape[0], w.shape[0] // (BD_BLOCKS * BD_PARTS)
        dx = _bd_call(dy, w, x.shape, (length, LANES), (cols, LANES), (length, cols), thin, diag, wide, nt, False,
                      name + "_dx")
        dw = _bd_call(x, dy, w.shape, (length, cols), (length, LANES), (cols, LANES), wide, thin, diag, tn, False,
                      name + "_dw")
        return dx, dw

    mm.defvjp(fwd, bwd)
    return mm


def _row_tile(rows, widths):
    limit = max(16, (6 * 1024 * 1024 // 4) // max(1, sum(widths)))
    return _divisor_tile(rows, limit, 16)


def _make_rowwise(f, n_rows, n_pars, out_dtypes, name, n_nodiff=0, grad_dtypes=None):
    n_out = len(out_dtypes)
    n_diff = n_rows - n_nodiff

    def run_fwd(rows, pars):
        length = rows[0].shape[0]
        shapes = jax.eval_shape(lambda *a: f(*a), *[jax.ShapeDtypeStruct((16, r.shape[1]), F32) for r in rows],
                                *[jax.ShapeDtypeStruct(p.shape, F32) for p in pars])
        widths = [s.shape[1] for s in shapes]
        tm = _row_tile(length, [r.shape[1] for r in rows] + widths)

        def kern(*refs):
            ins = [r[...].astype(F32) for r in refs[:n_rows + n_pars]]
            outs = f(*ins)
            for o_ref, o in zip(refs[n_rows + n_pars:], outs):
                o_ref[...] = o.astype(o_ref.dtype)

        return pl.pallas_call(
            kern, name=name + "_fwd", grid=(length // tm,),
            in_specs=[pl.BlockSpec((tm, r.shape[1]), lambda i: (i, 0)) for r in rows]
            + [pl.BlockSpec(p.shape, lambda i: (0, 0)) for p in pars],
            out_specs=[pl.BlockSpec((tm, w), lambda i: (i, 0)) for w in widths],
            out_shape=[jax.ShapeDtypeStruct((length, w), dt) for w, dt in zip(widths, out_dtypes)],
            compiler_params=_params("parallel"),
        )(*rows, *pars)

    def run_bwd(rows, pars, cts):
        length = rows[0].shape[0]
        tm = _row_tile(length, [r.shape[1] for r in rows] * 2 + [c.shape[1] for c in cts] * 2)

        def kern(*refs):
            ins = [r[...].astype(F32) for r in refs[:n_rows + n_pars]]
            ct = [r[...].astype(F32) for r in refs[n_rows + n_pars:n_rows + n_pars + n_out]]
            out_refs = refs[n_rows + n_pars + n_out:]
            nodiff = ins[n_diff:n_rows]
            _, vjp = jax.vjp(lambda *a: f(*a[:n_diff], *nodiff, *a[n_diff:]), *ins[:n_diff], *ins[n_rows:])
            grads = vjp(tuple(ct))
            for o_ref, g in zip(out_refs[:n_diff], grads[:n_diff]):
                o_ref[...] = g.astype(o_ref.dtype)
            first = pl.program_id(0) == 0
            for o_ref, g in zip(out_refs[n_diff:], grads[n_diff:]):
                @pl.when(first)
                def _(o_ref=o_ref, g=g):
                    o_ref[...] = g

                @pl.when(jnp.logical_not(first))
                def _(o_ref=o_ref, g=g):
                    o_ref[...] += g

        return pl.pallas_call(
            kern, name=name + "_bwd", grid=(length // tm,),
            in_specs=[pl.BlockSpec((tm, r.shape[1]), lambda i: (i, 0)) for r in rows]
            + [pl.BlockSpec(p.shape, lambda i: (0, 0)) for p in pars]
            + [pl.BlockSpec((tm, c.shape[1]), lambda i: (i, 0)) for c in cts],
            out_specs=[pl.BlockSpec((tm, r.shape[1]), lambda i: (i, 0)) for r in rows[:n_diff]]
            + [pl.BlockSpec(p.shape, lambda i: (0, 0)) for p in pars],
            out_shape=[jax.ShapeDtypeStruct(r.shape, r.dtype if grad_dtypes is None else grad_dtypes[i])
                       for i, r in enumerate(rows[:n_diff])]
            + [jax.ShapeDtypeStruct(p.shape, F32) for p in pars],
            compiler_params=_params("arbitrary"),
        )(*rows, *pars, *cts)

    @jax.custom_vjp
    def op(*args):
        return tuple(run_fwd(args[:n_rows], args[n_rows:]))

    def fwd(*args):
        return tuple(run_fwd(args[:n_rows], args[n_rows:])), args

    def bwd(args, cts):
        grads = run_bwd(args[:n_rows], args[n_rows:], cts)
        zeros = [jnp.zeros_like(r) for r in args[n_diff:n_rows]]
        return (*grads[:n_diff], *zeros, *grads[n_diff:])

    op.defvjp(fwd, bwd)
    op.run_fwd, op.run_bwd = run_fwd, run_bwd
    return op


def _layer_norm(z, g, b):
    mu = jnp.mean(z, axis=-1, keepdims=True)
    d = z - mu
    var = jnp.mean(d * d, axis=-1, keepdims=True)
    return d * lax.rsqrt(var + LN_EPS) * g + b


def _f_ln_half(h, f, g, b):
    return (_layer_norm(ALPHA * h + 0.5 * f, g, b),)


def _f_ln_full(h, f, g, b):
    return (_layer_norm(ALPHA * h + f, g, b),)


def _f_rms(x, g):
    return (x * lax.rsqrt(jnp.mean(x * x, axis=-1, keepdims=True) + RMS_EPS) * g,)


def _f_rope(x1, x2, cos, sin):
    return x1 * cos - x2 * sin, x2 * cos + x1 * sin


def _f_gelu_skip(y, u, d):
    return (jax.nn.gelu(y + d * u),)


def _f_glu(z, t, b):
    return (z * jax.nn.sigmoid(t + b),)


def _f_merge(ga, gb, gc, ya, yb, yc):
    return (jax.nn.sigmoid(ga) * ya + jax.nn.sigmoid(gb) * yb + jax.nn.sigmoid(gc) * yc,)


def _f_swiglu(gate, up):
    return (jax.nn.silu(gate) * up,)


def _f_cast(x):
    return (x,)


def _make_ffn(tag):
    ln = _make_rowwise(_f_ln_half, 2, 2, (F32,), f"{tag}_ln", grad_dtypes=(F32, BF16))
    cast = _make_rowwise(_f_cast, 1, 0, (BF16,), f"{tag}_cast")
    swiglu = _make_rowwise(_f_swiglu, 2, 0, (BF16,), f"{tag}_swiglu", grad_dtypes=(BF16, BF16))

    def forward(h, w_gate, w_up, w_down, g, b):
        hb, = cast.run_fwd((h,), ())
        gate = _matmul(hb, w_gate, tb=True, name=f"{tag}_gate_fwd")
        up = _matmul(hb, w_up, tb=True, name=f"{tag}_up_fwd")
        act, = swiglu.run_fwd((gate, up), ())
        f = _matmul(act, w_down, name=f"{tag}_down_fwd")
        y, = ln.run_fwd((h, f), (g, b))
        return y, (h, hb, gate, up, act, f, w_gate, w_up, w_down, g, b)

    @jax.custom_vjp
    def block(h, w_gate, w_up, w_down, z_gate, z_up, z_down, g, b):
        return forward(h, w_gate, w_up, w_down, g, b)[0]

    def fwd(h, w_gate, w_up, w_down, z_gate, z_up, z_down, g, b):
        return forward(h, w_gate, w_up, w_down, g, b)

    def bwd(res, dy):
        h, hb, gate, up, act, f, w_gate, w_up, w_down, g, b = res
        dh, df, dg, db = ln.run_bwd((h, f), (g, b), (dy,))
        dact = _matmul(df, w_down, tb=True, out_dtype=BF16, name=f"{tag}_down_dx")
        dw_down = _matmul(act, df, ta=True, out_dtype=BF16, name=f"{tag}_down_dw")
        dgate, dup = swiglu.run_bwd((gate, up), (), (dact,))
        dw_gate = _matmul(dgate, hb, ta=True, out_dtype=BF16, name=f"{tag}_gate_dw")
        dw_up = _matmul(dup, hb, ta=True, out_dtype=BF16, name=f"{tag}_up_dw")
        dh = _matmul(dgate, w_gate, add=dh, name=f"{tag}_gate_dx")
        dh = _matmul(dup, w_up, add=dh, name=f"{tag}_up_dx")
        zero = jnp.zeros_like
        return dh, zero(w_gate), zero(w_up), zero(w_down), dw_gate, dw_up, dw_down, dg, db

    block.defvjp(fwd, bwd)
    return block


def _attn_scores(q, k, q_block, tq):
    length = k.shape[0]
    s = lax.dot_general(q, k, (((1,), (1,)), ((), ())), preferred_element_type=F32) * (QK_DIM ** -0.5)
    row = q_block * tq + lax.broadcasted_iota(jnp.int32, (tq, length), 0)
    col = lax.broadcasted_iota(jnp.int32, (tq, length), 1)
    s = jnp.where(col <= row, s, -1e30)
    e = jnp.exp(s - jnp.max(s, axis=1, keepdims=True))
    return e * (1.0 / jnp.sum(e, axis=1, keepdims=True))


ATTN_SEGMENTS = 4


def _attn_tiles(length):
    seg = length // ATTN_SEGMENTS
    return seg, _divisor_tile(seg, 272, 16)


def _attn_fwd(q3, k3, v3):
    heads, length, _ = q3.shape
    seg, tq = _attn_tiles(length)
    outs = []
    for s in range(ATTN_SEGMENTS):
        kmax, base = (s + 1) * seg, s * (seg // tq)

        def kern(q_ref, k_ref, v_ref, o_ref, base=base):
            p = _attn_scores(q_ref[0], k_ref[0], base + pl.program_id(1), tq)
            o_ref[0] = jnp.dot(p.astype(BF16), v_ref[0], preferred_element_type=F32).astype(o_ref.dtype)

        outs.append(pl.pallas_call(
            kern, name=f"attn_fwd_seg{s}", grid=(heads, seg // tq),
            in_specs=[pl.BlockSpec((1, tq, HEAD_PAD), lambda h, i, base=base: (h, base + i, 0)),
                      pl.BlockSpec((1, kmax, HEAD_PAD), lambda h, i: (h, 0, 0)),
                      pl.BlockSpec((1, kmax, V_DIM), lambda h, i: (h, 0, 0))],
            out_specs=pl.BlockSpec((1, tq, V_DIM), lambda h, i: (h, i, 0)),
            out_shape=jax.ShapeDtypeStruct((heads, seg, V_DIM), F32),
            compiler_params=_params("parallel", "parallel"),
        )(q3, k3, v3))
    return jnp.concatenate(outs, axis=1)


def _attn_bwd(q3, k3, v3, do3):
    heads, length, _ = q3.shape
    seg, tq = _attn_tiles(length)
    dk = jnp.zeros((heads, length, HEAD_PAD), F32)
    dv = jnp.zeros((heads, length, V_DIM), F32)
    dqs = [None] * ATTN_SEGMENTS
    for s in reversed(range(ATTN_SEGMENTS)):
        kmax, base = (s + 1) * seg, s * (seg // tq)

        def kern(q_ref, k_ref, v_ref, do_ref, dk_in, dv_in, dq_ref, dk_ref, dv_ref, base=base):
            i = pl.program_id(1)
            q, k, v, do = q_ref[0], k_ref[0], v_ref[0], do_ref[0].astype(BF16)
            p = _attn_scores(q, k, base + i, tq)
            dp = lax.dot_general(do, v, (((1,), (1,)), ((), ())), preferred_element_type=F32)
            ds = (p * (dp - jnp.sum(p * dp, axis=1, keepdims=True)) * (QK_DIM ** -0.5)).astype(BF16)
            dq_ref[0] = jnp.dot(ds, k, preferred_element_type=F32)
            dk_part = lax.dot_general(ds, q, (((0,), (0,)), ((), ())), preferred_element_type=F32)
            dv_part = lax.dot_general(p.astype(BF16), do, (((0,), (0,)), ((), ())), preferred_element_type=F32)

            @pl.when(i == 0)
            def _():
                dk_ref[0] = dk_in[0] + dk_part
                dv_ref[0] = dv_in[0] + dv_part

            @pl.when(i > 0)
            def _():
                dk_ref[0] += dk_part
                dv_ref[0] += dv_part

        q_blk = pl.BlockSpec((1, tq, HEAD_PAD), lambda h, i, base=base: (h, base + i, 0))
        k_blk = pl.BlockSpec((1, kmax, HEAD_PAD), lambda h, i: (h, 0, 0))
        v_blk = pl.BlockSpec((1, kmax, V_DIM), lambda h, i: (h, 0, 0))
        dqs[s], dk, dv = pl.pallas_call(
            kern, name=f"attn_bwd_seg{s}", grid=(heads, seg // tq),
            in_specs=[q_blk, k_blk, v_blk, pl.BlockSpec((1, tq, V_DIM), lambda h, i, base=base: (h, base + i, 0)),
                      k_blk, v_blk],
            out_specs=[pl.BlockSpec((1, tq, HEAD_PAD), lambda h, i: (h, i, 0)), k_blk, v_blk],
            out_shape=[jax.ShapeDtypeStruct((heads, seg, HEAD_PAD), F32), jax.ShapeDtypeStruct(dk.shape, F32),
                       jax.ShapeDtypeStruct(dv.shape, F32)],
            input_output_aliases={4: 1, 5: 2}, compiler_params=_params("parallel", "arbitrary"),
        )(q3, k3, v3, do3, dk, dv)
    return jnp.concatenate(dqs, axis=1), dk, dv


@jax.custom_vjp
def _attention(q3, k3, v3):
    return _attn_fwd(q3, k3, v3)


def _attention_fwd(q3, k3, v3):
    return _attn_fwd(q3, k3, v3), (q3, k3, v3)


def _attention_bwd(res, do3):
    q3, k3, v3 = res
    dq, dk, dv = _attn_bwd(q3, k3, v3, do3)
    return dq.astype(q3.dtype), dk.astype(k3.dtype), dv.astype(v3.dtype)


_attention.defvjp(_attention_fwd, _attention_bwd)


def _conv_terms(x, c, w_ref, cb):
    u = c * x
    row = lax.broadcasted_iota(jnp.int32, u.shape, 0)
    u1 = jnp.where(row >= 1, pltpu.roll(u, 1, 0), 0.0)
    u2 = jnp.where(row >= 2, pltpu.roll(u, 2, 0), 0.0)
    y = cb + w_ref[0:1, :] * u2 + w_ref[1:2, :] * u1 + w_ref[2:3, :] * u
    return u, u1, u2, y


def _conv_specs(length):
    col = pl.BlockSpec((length, LANES), lambda j: (0, j))
    return col, pl.BlockSpec((CONV_K, LANES), lambda j: (0, j)), pl.BlockSpec((1, LANES), lambda j: (0, j))


def _conv_fwd(x, b, c, w, cb):
    length = x.shape[0]

    def kern(x_ref, b_ref, c_ref, w_ref, cb_ref, o_ref):
        _, _, _, y = _conv_terms(x_ref[...], c_ref[...], w_ref, cb_ref[...])
        o_ref[...] = b_ref[...] * y

    col, wspec, bspec = _conv_specs(length)
    return pl.pallas_call(
        kern, name="conv_fwd", grid=(MIX // LANES,), in_specs=[col, col, col, wspec, bspec], out_specs=col,
        out_shape=jax.ShapeDtypeStruct((length, MIX), F32), compiler_params=_params("parallel"),
    )(x, b, c, w, cb)


def _conv_bwd(x, b, c, w, cb, do):
    length = x.shape[0]

    def kern(x_ref, b_ref, c_ref, w_ref, cb_ref, do_ref, dx_ref, db_ref, dc_ref, dw_ref, dcb_ref):
        x, c, do = x_ref[...], c_ref[...], do_ref[...]
        u, u1, u2, y = _conv_terms(x, c, w_ref, cb_ref[...])
        db_ref[...] = do * y
        dy = do * b_ref[...]
        row = lax.broadcasted_iota(jnp.int32, dy.shape, 0)
        dy1 = jnp.where(row < length - 1, pltpu.roll(dy, length - 1, 0), 0.0)
        dy2 = jnp.where(row < length - 2, pltpu.roll(dy, length - 2, 0), 0.0)
        du = w_ref[2:3, :] * dy + w_ref[1:2, :] * dy1 + w_ref[0:1, :] * dy2
        dx_ref[...] = du * c
        dc_ref[...] = du * x
        dw_ref[0:1, :] = jnp.sum(dy * u2, axis=0, keepdims=True)
        dw_ref[1:2, :] = jnp.sum(dy * u1, axis=0, keepdims=True)
        dw_ref[2:3, :] = jnp.sum(dy * u, axis=0, keepdims=True)
        dcb_ref[...] = jnp.sum(dy, axis=0, keepdims=True)

    col, wspec, bspec = _conv_specs(length)
    big = jax.ShapeDtypeStruct((length, MIX), F32)
    return pl.pallas_call(
        kern, name="conv_bwd", grid=(MIX // LANES,), in_specs=[col, col, col, wspec, bspec, col],
        out_specs=[col, col, col, wspec, bspec],
        out_shape=[big, big, big, jax.ShapeDtypeStruct((CONV_K, MIX), F32), jax.ShapeDtypeStruct((1, MIX), F32)],
        compiler_params=_params("parallel"),
    )(x, b, c, w, cb, do)


@jax.custom_vjp
def _short_conv(x, b, c, w, cb):
    return _conv_fwd(x, b, c, w, cb)


def _short_conv_fwd(x, b, c, w, cb):
    return _conv_fwd(x, b, c, w, cb), (x, b, c, w, cb)


def _short_conv_bwd(res, do):
    return tuple(_conv_bwd(*res, do))


_short_conv.defvjp(_short_conv_fwd, _short_conv_bwd)


SCAN_ROWS = S5_CH // LANES
SCAN_TC = 136


def _scan_fwd(ar, ai, b):
    length = b.shape[0]
    tc = _divisor_tile(length, SCAN_TC, 8)

    def kern(ar_ref, ai_ref, b_ref, x_ref, sr, si):
        @pl.when(pl.program_id(0) == 0)
        def _():
            sr[...] = jnp.zeros_like(sr)
            si[...] = jnp.zeros_like(si)

        a_re, a_im = ar_ref[...], ai_ref[...]

        def body(t, carry):
            xr, xi = carry
            nr = a_re * xr - a_im * xi + b_ref[t, 0:SCAN_ROWS, :]
            ni = a_re * xi + a_im * xr + b_ref[t, SCAN_ROWS:2 * SCAN_ROWS, :]
            x_ref[t, 0:SCAN_ROWS, :] = nr
            x_ref[t, SCAN_ROWS:2 * SCAN_ROWS, :] = ni
            return nr, ni

        xr, xi = lax.fori_loop(0, tc, body, (sr[...], si[...]), unroll=4)
        sr[...] = xr
        si[...] = xi

    par = pl.BlockSpec((SCAN_ROWS, LANES), lambda i: (0, 0))
    blk = pl.BlockSpec((tc, 2 * SCAN_ROWS, LANES), lambda i: (i, 0, 0))
    return pl.pallas_call(
        kern, name="s5_scan_fwd", grid=(length // tc,), in_specs=[par, par, blk], out_specs=blk,
        out_shape=jax.ShapeDtypeStruct(b.shape, F32), scratch_shapes=[pltpu.VMEM((SCAN_ROWS, LANES), F32)] * 2,
        compiler_params=_params("arbitrary"),
    )(ar, ai, b)


def _scan_bwd(ar, ai, x, dx):
    length = x.shape[0]
    tc = _divisor_tile(length, SCAN_TC, 8)
    n_blk = length // tc
    re, im = slice(0, SCAN_ROWS), slice(SCAN_ROWS, 2 * SCAN_ROWS)

    def kern(ar_ref, ai_ref, x_ref, dx_ref, db_ref, dar_ref, dai_ref, lr_s, li_s):
        @pl.when(pl.program_id(0) == 0)
        def _():
            lr_s[...] = jnp.zeros_like(lr_s)
            li_s[...] = jnp.zeros_like(li_s)
            dar_ref[...] = jnp.zeros_like(dar_ref)
            dai_ref[...] = jnp.zeros_like(dai_ref)

        a_re, a_im = ar_ref[...], ai_ref[...]

        def body(j, carry):
            t = tc - 1 - j
            lr, li, gr, gi = carry
            x_re, x_im = x_ref[t, re, :], x_ref[t, im, :]
            gr = gr + (lr * x_re + li * x_im)
            gi = gi + (li * x_re - lr * x_im)
            nlr = dx_ref[t, re, :] + (a_re * lr + a_im * li)
            nli = dx_ref[t, im, :] + (a_re * li - a_im * lr)
            db_ref[t, re, :] = nlr
            db_ref[t, im, :] = nli
            return nlr, nli, gr, gi

        lr, li, gr, gi = lax.fori_loop(0, tc, body, (lr_s[...], li_s[...], dar_ref[...], dai_ref[...]), unroll=4)
        lr_s[...] = lr
        li_s[...] = li
        dar_ref[...] = gr
        dai_ref[...] = gi

    par = pl.BlockSpec((SCAN_ROWS, LANES), lambda i: (0, 0))
    blk = pl.BlockSpec((tc, 2 * SCAN_ROWS, LANES), lambda i: (n_blk - 1 - i, 0, 0))
    pout = jax.ShapeDtypeStruct((SCAN_ROWS, LANES), F32)
    return pl.pallas_call(
        kern, name="s5_scan_bwd", grid=(n_blk,), in_specs=[par, par, blk, blk],
        out_specs=[blk, par, par], out_shape=[jax.ShapeDtypeStruct(x.shape, F32), pout, pout],
        scratch_shapes=[pltpu.VMEM((SCAN_ROWS, LANES), F32)] * 2, compiler_params=_params("arbitrary"),
    )(ar, ai, x, dx)


@jax.custom_vjp
def _s5_scan(ar, ai, b):
    return _scan_fwd(ar, ai, b)


def _s5_scan_fwd(ar, ai, b):
    x = _scan_fwd(ar, ai, b)
    return x, (ar, ai, x)


def _s5_scan_bwd(res, dx):
    ar, ai, x = res
    db, dar, dai = _scan_bwd(ar, ai, x, dx)
    return dar, dai, db


_s5_scan.defvjp(_s5_scan_fwd, _s5_scan_bwd)


def _loss_call(y, target, n_real):
    length = y.shape[0]
    tm = _divisor_tile(length, 544, 16)

    def kern(y_ref, t_ref, loss_ref, dy_ref):
        i = pl.program_id(0)
        row = i * tm + lax.broadcasted_iota(jnp.int32, (tm, 1), 0)
        keep = jnp.logical_and(row >= N_META, row < n_real)
        err = jnp.where(keep, y_ref[...] - t_ref[...], 0.0)
        dy_ref[...] = err * (1.0 / D_MODEL)
        part = 0.5 * jnp.sum(jnp.mean(err * err, axis=-1, keepdims=True), axis=0, keepdims=True)

        @pl.when(i == 0)
        def _():
            loss_ref[...] = jnp.zeros_like(loss_ref)

        loss_ref[...] += part

    blk = pl.BlockSpec((tm, D_MODEL), lambda i: (i, 0))
    return pl.pallas_call(
        kern, name="loss_head", grid=(length // tm,), in_specs=[blk, blk],
        out_specs=[pl.BlockSpec((8, LANES), lambda i: (0, 0)), blk],
        out_shape=[jax.ShapeDtypeStruct((8, LANES), F32), jax.ShapeDtypeStruct(y.shape, F32)],
        compiler_params=_params("arbitrary"),
    )(y, target)


def _make_loss(n_real):
    @jax.custom_vjp
    def loss(y, target):
        return _loss_call(y, target, n_real)[0][0, 0]

    def fwd(y, target):
        total, dy = _loss_call(y, target, n_real)
        return total[0, 0], dy

    def bwd(dy, ct):
        return dy * ct, jnp.zeros_like(dy)

    loss.defvjp(fwd, bwd)
    return loss


HBM_SPEC = pl.BlockSpec(memory_space=pl.ANY)
MESH_ID = pl.DeviceIdType.MESH


SC_MESH = dict(axis_name="sequencer", num_cores=1)
GATHER_ID, SLOT_ID = 1, 2


def _handshake(peers):
    barrier = pltpu.get_barrier_semaphore()
    for peer in peers:
        pl.semaphore_signal(barrier, inc=1, device_id=peer, device_id_type=MESH_ID)
    pl.semaphore_wait(barrier, len(peers))


def _exchange_call(body, name, ins, out_types, n_sems, sequencer_id):
    n_in, n_out = len(ins), len(out_types)
    sems = [pltpu.SemaphoreType.DMA((n_sems,)), pltpu.SemaphoreType.DMA((n_sems,)), pltpu.SemaphoreType.DMA((n_in,))]
    if sequencer_id is None:
        def on_core(*refs):
            body(lambda peers: None, refs[:n_in], refs[n_in:n_in + n_out], *refs[n_in + n_out:])

        return pl.pallas_call(on_core, name=name, out_shape=out_types, in_specs=[HBM_SPEC] * n_in,
                              out_specs=[HBM_SPEC] * n_out, scratch_shapes=sems)(*ins)

    def on_sequencer(*refs):
        body(_handshake, refs[:n_in], refs[n_in:n_in + n_out], *refs[n_in + n_out:])

    return pl.kernel(on_sequencer, name=name, out_type=out_types, mesh=plsc.ScalarSubcoreMesh(**SC_MESH),
                     scratch_types=sems, compiler_params=pltpu.CompilerParams(collective_id=sequencer_id))(*ins)


def _all_gather(shards, name, sequencer=False):
    n = len(shards)

    def body(handshake, x_refs, out_refs, send_sems, recv_sems, local_sems):
        x, y, c = lax.axis_index("x"), lax.axis_index("y"), lax.axis_index("c")
        me, sibling = (x, y, c), (x, y, 1 - c)
        chips = [(1 - x, y), (x, 1 - y), (1 - x, 1 - y)]
        handshake([sibling] + [(*chip, c) for chip in chips])

        def copy(b, k, block, to, from_input=False):
            px, py, pc = block
            slot = out_refs[b].at[4 * px + 2 * py + pc]
            return pltpu.make_async_remote_copy(
                src_ref=x_refs[b] if from_input else slot, dst_ref=slot,
                send_sem=send_sems.at[7 * b + k], recv_sem=recv_sems.at[7 * b + k], device_id=to, device_id_type=MESH_ID)

        mine = [pltpu.make_async_copy(x_refs[b], out_refs[b].at[4 * x + 2 * y + c], local_sems.at[b]) for b in range(n)]
        for cp in mine:
            cp.start()
        first = []
        for b in range(n):
            first.append(copy(b, 0, me, sibling, from_input=True))
            first += [copy(b, 1 + j, me, (*chip, c), from_input=True) for j, chip in enumerate(chips)]
        for cp in first:
            cp.start()
        passed = []
        for j, chip in enumerate(chips):
            for b in range(n):
                copy(b, 1 + j, (*chip, c), me).wait_recv()
                passed.append(copy(b, 4 + j, (*chip, c), sibling))
                passed[-1].start()
        for b in range(n):
            copy(b, 0, sibling, me).wait_recv()
            for j, chip in enumerate(chips):
                copy(b, 4 + j, (*chip, 1 - c), me).wait_recv()
        for cp in first + passed:
            cp.wait_send()
        for cp in mine:
            cp.wait()

    out_types = [jax.ShapeDtypeStruct((N_DEV, *s.shape), s.dtype) for s in shards]
    return _exchange_call(body, name, shards, out_types, 7 * n, GATHER_ID if sequencer else None)


def _slot_exchange(bufs, name, sequencer=False):
    def body(handshake, ins, outs, send_sems, recv_sems, local_sems):
        x, y, c = lax.axis_index("x"), lax.axis_index("y"), lax.axis_index("c")
        me = 4 * x + 2 * y + c
        flips = [(dx, dy, dc) for dx in (0, 1) for dy in (0, 1) for dc in (0, 1)][1:]
        peers = [(1 - x if dx else x, 1 - y if dy else y, 1 - c if dc else c) for dx, dy, dc in flips]
        handshake(peers)
        own = [pltpu.make_async_copy(src.at[me], dst.at[me], local_sems.at[b]) for b, (src, dst) in enumerate(zip(ins, outs))]
        copies = []
        for b, (src, dst) in enumerate(zip(ins, outs)):
            for k, (px, py, pc) in enumerate(peers):
                copies.append(pltpu.make_async_remote_copy(
                    src_ref=src.at[4 * px + 2 * py + pc], dst_ref=dst.at[me],
                    send_sem=send_sems.at[7 * b + k], recv_sem=recv_sems.at[7 * b + k],
                    device_id=(px, py, pc), device_id_type=MESH_ID))
        for cp in own + copies:
            cp.start()
        for cp in copies + own:
            cp.wait()

    out_types = [jax.ShapeDtypeStruct(b.shape, b.dtype) for b in bufs]
    return _exchange_call(body, name, bufs, out_types, 7 * len(bufs), SLOT_ID if sequencer else None)


def _slot_sum(slots, name):
    _, rows, cols = slots.shape
    tm = _divisor_tile(rows, 512, 16)

    def kern(s_ref, o_ref):
        total = s_ref[0].astype(F32)
        for d in range(1, N_DEV):
            total = total + s_ref[d].astype(F32)
        o_ref[...] = total

    return pl.pallas_call(
        kern, name=name, grid=(rows // tm,), in_specs=[pl.BlockSpec((N_DEV, tm, cols), lambda i: (0, i, 0))],
        out_specs=pl.BlockSpec((tm, cols), lambda i: (i, 0)), out_shape=jax.ShapeDtypeStruct((rows, cols), F32),
        compiler_params=_params("parallel"),
    )(slots)


def _reduce_scatter(bufs, tag, sequencer):
    arrived = _slot_exchange(bufs, f"rs_exchange_{tag}", sequencer)
    return [_slot_sum(a, f"rs_sum_{tag}{i}") for i, a in enumerate(arrived)]


def _adamw(w, g, m, v, name):
    rows, cols = w.shape
    tm = _divisor_tile(rows, max(8, (512 * 1024) // cols // 8 * 8), 8)

    def kern(w_ref, g_ref, m_ref, v_ref, d_ref, nm_ref, nv_ref):
        g = g_ref[...]
        m = ADAM_B1 * m_ref[...] + (1.0 - ADAM_B1) * g
        v = ADAM_B2 * v_ref[...] + (1.0 - ADAM_B2) * (g * g)
        m_hat = m / (1.0 - ADAM_B1 ** ADAM_STEP)
        v_hat = v / (1.0 - ADAM_B2 ** ADAM_STEP)
        d_ref[...] = -ADAM_LR * (m_hat / (jnp.sqrt(v_hat) + ADAM_EPS) + ADAM_WD * w_ref[...])
        nm_ref[...] = m
        nv_ref[...] = v

    blk = pl.BlockSpec((tm, cols), lambda i: (i, 0))
    out = jax.ShapeDtypeStruct(w.shape, F32)
    return pl.pallas_call(
        kern, name=name, grid=(rows // tm,), in_specs=[blk] * 4, out_specs=[blk] * 3, out_shape=[out] * 3,
        compiler_params=_params("parallel"),
    )(w, g, m, v)


def _adamw_layers(w, g_layers, m, v, name):
    depth, rows, cols = w.shape
    tm = _divisor_tile(rows, max(8, (512 * 1024) // cols // 8 * 8), 8)

    def kern(w_ref, m_ref, v_ref, *refs):
        g_refs, (g_out, d_ref, nm_ref, nv_ref) = refs[:depth], refs[depth:]
        layer = pl.program_id(0)
        g = g_refs[0][...]
        for l in range(1, depth):
            g = jnp.where(layer == l, g_refs[l][...], g)
        m = ADAM_B1 * m_ref[0] + (1.0 - ADAM_B1) * g
        v = ADAM_B2 * v_ref[0] + (1.0 - ADAM_B2) * (g * g)
        m_hat = m / (1.0 - ADAM_B1 ** ADAM_STEP)
        v_hat = v / (1.0 - ADAM_B2 ** ADAM_STEP)
        g_out[0] = g
        d_ref[0] = -ADAM_LR * (m_hat / (jnp.sqrt(v_hat) + ADAM_EPS) + ADAM_WD * w_ref[0])
        nm_ref[0] = m
        nv_ref[0] = v

    blk = pl.BlockSpec((1, tm, cols), lambda l, i: (l, i, 0))
    out = jax.ShapeDtypeStruct(w.shape, F32)
    return pl.pallas_call(
        kern, name=name, grid=(depth, rows // tm),
        in_specs=[blk] * 3 + [pl.BlockSpec((tm, cols), lambda l, i: (i, 0))] * depth,
        out_specs=[blk] * 4, out_shape=[out] * 4, compiler_params=_params("parallel", "parallel"),
    )(w, m, v, *g_layers)


TRANSPOSED = ('ffn1_w_gate', 'ffn1_w_up', 'ffn2_w_gate', 'ffn2_w_up', 'w_in', 'mla_w_uq', 'mla_w_ukv', 'mla_w_o',
              'conv_w_out', 's5_w_out')
PIECES = ((('ffn1_w_gate',), ('ffn1_w_up',), ('ffn1_w_down',)),
          (('w_in', 'w_o'), ('mla_w_o', 'conv_w_out', 's5_w_out', 's5_w_glu'), ('mla_w_uq',), ('mla_w_ukv',)),
          (('ffn2_w_gate',), ('ffn2_w_up',), ('ffn2_w_down',)))
PIN_CUTS = (PIN_CQ, PIN_CKV, PIN_KR1, PIN_KR2, PIN_XBAR, PIN_BG, PIN_CG, PIN_U, PIN_GATES, PIN_GATES + D_MODEL,
            PIN_GATES + 2 * D_MODEL, PIN_END, D_IN_PAD)
PIN_PIECES = ((PIN_CQ, Q_RANK), (PIN_CKV, KV_RANK), (PIN_KR1, HALF_ROPE), (PIN_KR2, HALF_ROPE), (PIN_XBAR, 4 * MIX),
              (PIN_GATES, 3 * D_MODEL))


def _make_split(cuts):
    @jax.custom_vjp
    def split(t):
        return tuple(t[:, a:b] for a, b in zip(cuts[:-1], cuts[1:]))

    def fwd(t):
        return split(t), None

    def bwd(_, cts):
        return (jnp.concatenate(cts, axis=1),)

    split.defvjp(fwd, bwd)
    return split


def _make_projection(tag, cuts):
    cast = _make_rowwise(_f_cast, 1, 0, (BF16,), f"{tag}_cast")

    def forward(h, w):
        hb, = cast.run_fwd((h,), ())
        full = _matmul(hb, w, tb=True, name=f"{tag}_fwd")
        return tuple(full[:, a:b] for a, b in zip(cuts[:-1], cuts[1:])), (hb, w)

    @jax.custom_vjp
    def proj(h, w, wz):
        return forward(h, w)[0]

    def fwd(h, w, wz):
        return forward(h, w)

    def bwd(res, cts):
        hb, w = res
        d_full = jnp.concatenate([c.astype(BF16) for c in cts], axis=1)
        dw = _matmul(d_full, hb, ta=True, out_dtype=BF16, name=f"{tag}_dw")
        return _matmul(d_full, w, name=f"{tag}_dx"), jnp.zeros_like(w), dw

    proj.defvjp(fwd, bwd)
    return proj


def _travel_shape(name, shape):
    return (shape[2], shape[1]) if name in TRANSPOSED else (shape[1], shape[2])


def _pack_groups(tensors, layer, groups, dtype):
    def view(n):
        t = tensors[n][layer]
        return (t.T if n in TRANSPOSED else t).astype(dtype)
    return [jnp.concatenate([view(n) for n in grp], axis=0) for grp in groups]


def _unpack_groups(bufs, groups, shard_shapes):
    out = {}
    for buf, grp in zip(bufs, groups):
        at = 0
        for n in grp:
            r, _ = _travel_shape(n, shard_shapes[n])
            out[n] = buf[..., at:at + r, :]
            at += r
    return out


def _pack_rows(arrays):
    flat = jnp.concatenate([a.reshape(-1) for a in arrays])
    rows = -(-flat.shape[0] // PACK_COLS)
    rows = -(-rows // 8) * 8
    return jnp.pad(flat, (0, rows * PACK_COLS - flat.shape[0])).reshape(rows, PACK_COLS)


def _unpack_rows(buf, shapes):
    flat = buf.reshape(-1)
    out, at = [], 0
    for s in shapes:
        n = int(np.prod(s))
        out.append(flat[at:at + n].reshape(s))
        at += n
    return out


def _full_weight(t, axis):
    if axis == 1:
        return jnp.moveaxis(t, 0, 1).reshape(t.shape[1], N_DEV * t.shape[2], t.shape[3])
    return jnp.moveaxis(t, 0, 2).reshape(t.shape[1], t.shape[2], N_DEV * t.shape[3])


def _disassemble(d, groups, dtype):
    heads = lambda t: t.reshape(HEADS, -1, t.shape[1])
    full = {}
    for tag in ('ffn1', 'ffn2'):
        if f'{tag}_gate' in d:
            full.update({f'{tag}_w_gate': d[f'{tag}_gate'], f'{tag}_w_up': d[f'{tag}_up'], f'{tag}_w_down': d[f'{tag}_down']})
    if 'w_in' in d:
        w_in, uq, ukv = d['w_in'], d['w_uq'], d['w_ukv']
        full.update(
            w_in=jnp.concatenate([w_in[a:a + n] for a, n in PIN_PIECES], axis=0),
            mla_w_uq=jnp.concatenate([heads(uq[:HEADS * NOPE]), heads(uq[HEADS * NOPE:HEADS * NOPE + LANES]),
                                      heads(uq[HEADS * NOPE + LANES:])], axis=1).reshape(HEADS * QK_DIM, Q_RANK),
            mla_w_ukv=jnp.concatenate([heads(ukv[:HEADS * NOPE]), heads(ukv[HEADS * NOPE:])],
                                      axis=1).reshape(HEADS * (NOPE + V_DIM), KV_RANK),
            mla_w_o=d['mla_w_o'], conv_w_out=d['conv_w_out'], s5_w_glu=d['s5_w_glu'], s5_w_out=d['s5_w_out'], w_o=d['w_o'])
    return [jnp.concatenate([full[n].reshape(N_DEV, -1, full[n].shape[-1]).astype(dtype) for n in grp], axis=1)
            for grp in groups]


def _assemble(gathered, groups, shard_shapes):
    full = {n: t.reshape(N_DEV * t.shape[1], t.shape[2])
            for n, t in _unpack_groups(gathered, groups, shard_shapes).items()}
    out = {}
    for tag in ('ffn1', 'ffn2'):
        if f'{tag}_w_gate' in full:
            out.update({f'{tag}_gate': full[f'{tag}_w_gate'], f'{tag}_up': full[f'{tag}_w_up'],
                        f'{tag}_down': full[f'{tag}_w_down']})
    if 'w_in' not in full:
        return out
    w_in = full['w_in']
    cuts = np.cumsum((0,) + IN_SPLITS)
    cq, ckv, kr, xbar, bg, cg, u, gates = [w_in[a:b] for a, b in zip(cuts[:-1], cuts[1:])]
    pad = lambda t, n: jnp.pad(t, ((0, n - t.shape[0]), (0, 0)))
    w_in_packed = jnp.concatenate(
        [cq, ckv, pad(kr[:HALF_ROPE], LANES), pad(kr[HALF_ROPE:], LANES), xbar, bg, cg, u, gates,
         jnp.zeros((D_IN_PAD - PIN_END, D_MODEL), w_in.dtype)], axis=0)
    uq = full['mla_w_uq'].reshape(HEADS, QK_DIM, Q_RANK)
    w_uq = jnp.concatenate([uq[:, :NOPE].reshape(HEADS * NOPE, Q_RANK),
                            uq[:, NOPE:NOPE + HALF_ROPE].reshape(HEADS * HALF_ROPE, Q_RANK),
                            uq[:, NOPE + HALF_ROPE:].reshape(HEADS * HALF_ROPE, Q_RANK)], axis=0)
    ukv = full['mla_w_ukv'].reshape(HEADS, NOPE + V_DIM, KV_RANK)
    w_ukv = jnp.concatenate([ukv[:, :NOPE].reshape(HEADS * NOPE, KV_RANK),
                             ukv[:, NOPE:].reshape(HEADS * V_DIM, KV_RANK)], axis=0)
    out.update(w_in=w_in_packed, w_uq=w_uq, w_ukv=w_ukv, mla_w_o=full['mla_w_o'], conv_w_out=full['conv_w_out'],
               s5_w_glu=full['s5_w_glu'], s5_w_out=full['s5_w_out'], w_o=full['w_o'])
    return out


def _s5_discretize(a_re, a_im, log_dt, b_re, b_im, c_re, c_im):
    dt = jnp.exp(log_dt)[:, None]
    mag = jnp.exp(dt * a_re)
    ab_re, ab_im = mag * jnp.cos(dt * a_im), mag * jnp.sin(dt * a_im)
    den = a_re * a_re + a_im * a_im
    nr, ni = ab_re - 1.0, ab_im
    coef_re = (nr * a_re + ni * a_im) / den
    coef_im = (ni * a_re - nr * a_im) / den
    bb_re = coef_re[..., None] * b_re - coef_im[..., None] * b_im
    bb_im = coef_re[..., None] * b_im + coef_im[..., None] * b_re
    unit = jnp.arange(MIX)[:, None]
    chan = jnp.arange(2 * S5_CH)[None, :]
    pair = jnp.arange(2 * S5_STATE)[:, None]
    own = unit // S5_GROUP == (chan % S5_CH) // S5_STATE
    copy = jnp.logical_and(pair // S5_STATE == chan // S5_CH, pair % S5_STATE == chan % S5_STATE).astype(F32)
    flat_b = lambda bb: bb.transpose(0, 2, 1).reshape(MIX, S5_STATE)
    flat_c = lambda cc: cc.transpose(2, 0, 1).reshape(S5_STATE, MIX)
    b_small = jnp.concatenate([flat_b(bb_re), flat_b(bb_im)], axis=1)
    c_small = jnp.concatenate([flat_c(c_re), -flat_c(c_im)], axis=0)
    b_map = jnp.where(own, _make_mm_f32w("s5_spread_b")(b_small, copy), 0.0)
    c_map = jnp.where(own.T, _make_mm_f32w("s5_spread_c")(copy.T, c_small), 0.0)
    return ab_re.reshape(SCAN_ROWS, LANES), ab_im.reshape(SCAN_ROWS, LANES), b_map, c_map


def _rope_tables(length):
    inv_freq = ROPE_BASE ** (-jnp.arange(0, ROPE, 2, dtype=F32) / ROPE)
    ang = jnp.arange(length).astype(F32)[:, None] * inv_freq[None, :]
    return jnp.tile(jnp.cos(ang), (1, LANES // HALF_ROPE)), jnp.tile(jnp.sin(ang), (1, LANES // HALF_ROPE))


def _heads_first(t):
    return t.reshape(t.shape[0], HEADS, -1).transpose(1, 0, 2)


def _local_loss(diff, big, n_real):
    small, wz = diff['small'], diff['wz']
    h = diff['h0']
    length = h.shape[0]
    cos, sin = _rope_tables(length)
    row2 = lambda v: v.reshape(1, -1)
    for l in range(DEPTH):
        w, z = big[l], wz[l]
        p = {k: small[k][l] for k in small if k != 'meta'}
        ffn = lambda h, tag, ln: _make_ffn(tag)(
            h, *[w[f'{tag}_{k}'] for k in ('gate', 'up', 'down')], *[z[f'{tag}_{k}'] for k in ('gate', 'up', 'down')],
            row2(p[f'{ln}_g']), row2(p[f'{ln}_b']))
        h = ffn(h, "ffn1", "ln1")
        cq, ckv, kr1, kr2, xbar, bg, cg, u, gate_a, gate_b, gate_c, _ = _make_projection("w_in", PIN_CUTS)(
            h, w['w_in'], z['w_in'])
        qn, = _make_rowwise(_f_rms, 1, 1, (BF16,), "q_rms")(cq, row2(p['mla_q_norm_g']))
        kvn, = _make_rowwise(_f_rms, 1, 1, (BF16,), "kv_rms")(ckv, row2(p['mla_kv_norm_g']))
        q_nope, q1, q2 = _make_split((0, HEADS * NOPE, HEADS * NOPE + LANES, HEADS * NOPE + 2 * LANES))(
            _make_mm("w_uq", wt=True)(qn, w['w_uq'], z['w_uq']))
        k_nope, val = _make_split((0, HEADS * NOPE, HEADS * (NOPE + V_DIM)))(
            _make_mm("w_ukv", wt=True)(kvn, w['w_ukv'], z['w_ukv']))
        rope = _make_rowwise(_f_rope, 4, 0, (BF16, BF16), "rope", n_nodiff=2)
        q1, q2 = rope(q1, q2, cos, sin)
        k1, k2 = rope(kr1, kr2, cos, sin)
        hpad = jnp.zeros((HEADS, length, HEAD_PAD - QK_DIM), BF16)
        q3 = jnp.concatenate([_heads_first(q_nope.astype(BF16)), _heads_first(q1), _heads_first(q2), hpad], -1)
        shared = lambda t: jnp.broadcast_to(t[None, :, :HALF_ROPE], (HEADS, length, HALF_ROPE))
        k3 = jnp.concatenate([_heads_first(k_nope.astype(BF16)), shared(k1), shared(k2), hpad], -1)
        v3 = _heads_first(val.astype(BF16))
        o3 = _attention(q3, k3, v3)
        y_a = _make_mm("mla_w_o", wt=True)(o3.transpose(1, 0, 2).reshape(length, MIX), w['mla_w_o'], z['mla_w_o'])
        conv = _short_conv(xbar, bg, cg, p['conv_w_full'], row2(p['conv_b']))
        y_b = _make_mm("conv_w_out", wt=True)(conv, w['conv_w_out'], z['conv_w_out'])
        ar, ai, b_map, c_map = _s5_discretize(p['s5_a_re'], p['s5_a_im'], p['s5_log_dt'], p['s5_b_re'], p['s5_b_im'],
                                              p['s5_c_re'], p['s5_c_im'])
        bu = _make_bd_in("s5_b")(u, b_map)
        states = _s5_scan(ar, ai, bu.reshape(length, 2 * SCAN_ROWS, LANES)).reshape(length, 2 * S5_CH)
        y_ssm = _make_bd_out("s5_c")(states, c_map)
        zed, = _make_rowwise(_f_gelu_skip, 2, 1, (F32,), "s5_gelu")(y_ssm, u, row2(p['s5_d']))
        t = _make_mm("s5_w_glu")(zed, w['s5_w_glu'], z['s5_w_glu'])
        glu, = _make_rowwise(_f_glu, 2, 1, (BF16,), "s5_glu")(zed, t, row2(p['s5_b_glu']))
        y_c = _make_mm("s5_w_out", wt=True)(glu, w['s5_w_out'], z['s5_w_out'])
        mixed, = _make_rowwise(_f_merge, 6, 0, (BF16,), "merge")(gate_a, gate_b, gate_c, y_a, y_b, y_c)
        mix_out = _make_mm("w_o")(mixed, w['w_o'], z['w_o'])
        h, = _make_rowwise(_f_ln_full, 2, 2, (F32,), "mix_ln")(h, mix_out, row2(p['ln2_g']), row2(p['ln2_b']))
        h = ffn(h, "ffn2", "ln3")
    return _make_loss(n_real)(h, diff['target'])


def kernel(x, meta, ffn1_w_gate, ffn1_w_up, ffn1_w_down, ln1_g, ln1_b, w_in, mla_q_norm_g, mla_w_uq, mla_kv_norm_g, mla_w_ukv, mla_w_o, conv_w, conv_b, conv_w_out, s5_a_re, s5_a_im, s5_log_dt, s5_b_re, s5_b_im, s5_c_re, s5_c_im, s5_d, s5_w_glu, s5_b_glu, s5_w_out, w_o, ln2_g, ln2_b, ffn2_w_gate, ffn2_w_up, ffn2_w_down, ln3_g, ln3_b, loss_target, m_meta, m_ffn1_w_gate, m_ffn1_w_up, m_ffn1_w_down, m_ln1_g, m_ln1_b, m_w_in, m_mla_q_norm_g, m_mla_w_uq, m_mla_kv_norm_g, m_mla_w_ukv, m_mla_w_o, m_conv_w, m_conv_b, m_conv_w_out, m_s5_a_re, m_s5_a_im, m_s5_log_dt, m_s5_b_re, m_s5_b_im, m_s5_c_re, m_s5_c_im, m_s5_d, m_s5_w_glu, m_s5_b_glu, m_s5_w_out, m_w_o, m_ln2_g, m_ln2_b, m_ffn2_w_gate, m_ffn2_w_up, m_ffn2_w_down, m_ln3_g, m_ln3_b, v_meta, v_ffn1_w_gate, v_ffn1_w_up, v_ffn1_w_down, v_ln1_g, v_ln1_b, v_w_in, v_mla_q_norm_g, v_mla_w_uq, v_mla_kv_norm_g, v_mla_w_ukv, v_mla_w_o, v_conv_w, v_conv_b, v_conv_w_out, v_s5_a_re, v_s5_a_im, v_s5_log_dt, v_s5_b_re, v_s5_b_im, v_s5_c_re, v_s5_c_im, v_s5_d, v_s5_w_glu, v_s5_b_glu, v_s5_w_out, v_w_o, v_ln2_g, v_ln2_b, v_ffn2_w_gate, v_ffn2_w_up, v_ffn2_w_down, v_ln3_g, v_ln3_b):
    args = locals()
    w = {n: args[n] for n in WEIGHT_NAMES}
    m = {n: args["m_" + n] for n in WEIGHT_NAMES}
    v = {n: args["v_" + n] for n in WEIGHT_NAMES}
    me = 4 * lax.axis_index("x") + 2 * lax.axis_index("y") + lax.axis_index("c")
    seq = x.shape[1]
    n_real = N_META + seq
    length = -(-n_real // LANES) * LANES

    shard_shapes = {n: w[n].shape for n in BIG}
    small_shards = _all_gather([_pack_rows([w[n] for n in SMALL_SHARDED])], "gather_small",
                               sequencer=True)[0].reshape(N_DEV, -1)
    big = [{} for _ in range(DEPTH)]
    for l in range(DEPTH):
        for p, groups in enumerate(PIECES):
            packed = _pack_groups(w, l, groups, BF16)
            if (l, p) == (0, 1):
                packed = lax.optimization_barrier((gathered, packed))[1]
            gathered = _all_gather(packed, f"gather_weights_layer{l}_piece{p}", sequencer=True)
            big[l].update(_assemble(gathered, groups, shard_shapes))
    meta_full = _full_weight(small_shards[:, :meta.size].reshape(N_DEV, 1, *meta.shape), 2)[0]
    conv_w_full = _full_weight(small_shards[:, meta.size:meta.size + conv_w.size].reshape(N_DEV, *conv_w.shape), 2)

    small = {n: w[n] for n in SMALL_NAMES if n not in SMALL_SHARDED}
    small['conv_w_full'] = conv_w_full
    small['meta'] = meta_full
    wz = jax.tree.map(lambda t: jnp.zeros(t.shape, BF16), big)
    pad_rows = length - n_real

    def loss_fn(diff):
        h0 = jnp.concatenate([diff['small']['meta'], diff['x'], jnp.zeros((pad_rows, D_MODEL), F32)], axis=0)
        target = jnp.pad(loss_target[0], ((N_META, pad_rows), (0, 0)))
        return _local_loss(dict(h0=h0, small=diff['small'], wz=diff['wz'], target=target), big, n_real)

    loss_local, grads = jax.value_and_grad(loss_fn)(dict(x=x[0], small=small, wz=wz))

    small_names = [n for n in SMALL_NAMES if n not in SMALL_SHARDED] + ['conv_w_full', 'meta']
    small_flat = _pack_rows([loss_local.reshape(1)] + [grads['small'][n] for n in small_names])
    rows_each = -(-small_flat.shape[0] // (8 * N_DEV)) * 8
    small_flat = jnp.pad(small_flat, ((0, rows_each * N_DEV - small_flat.shape[0]), (0, 0)))
    layer_sums = [{} for _ in range(DEPTH)]
    for l in reversed(range(DEPTH)):
        for p, groups in reversed(list(enumerate(PIECES))):
            last = l == 0 and p == 0
            extra = [small_flat.reshape(N_DEV, rows_each, PACK_COLS)] if last else []
            sums = _reduce_scatter(_disassemble(grads['wz'][l], groups, BF16) + extra, f"grads_layer{l}_piece{p}_",
                                   sequencer=True)
            layer_sums[l].update(_unpack_groups(sums[:len(groups)], groups, shard_shapes))
            if last:
                small_sum = sums[-1]
    small_all, = _all_gather([small_sum], "gather_small_grads")
    loss, *small_sums = _unpack_rows(small_all, [()] + [grads['small'][n].shape for n in small_names])
    small_grads = dict(zip(small_names, small_sums))
    g = {}
    for name in SMALL_NAMES:
        if name == 'meta':
            g[name] = lax.dynamic_slice_in_dim(small_grads['meta'], me * meta.shape[1], meta.shape[1], axis=1)
        elif name == 'conv_w':
            g[name] = lax.dynamic_slice_in_dim(small_grads['conv_w_full'], me * conv_w.shape[2], conv_w.shape[2], axis=2)
        else:
            g[name] = small_grads[name]

    delta, new_m, new_v = {}, {}, {}
    for name in BIG:
        per_layer = [s[name].T if name in TRANSPOSED else s[name] for s in layer_sums]
        g[name], delta[name], new_m[name], new_v[name] = _adamw_layers(w[name], per_layer, m[name], v[name], f"adamw_{name}")
    shapes = [w[n].shape for n in SMALL_NAMES]
    d, nm, nv = _adamw(*[_pack_rows([t[n] for n in SMALL_NAMES]) for t in (w, g, m, v)], "adamw_small")
    for out, buf in ((delta, d), (new_m, nm), (new_v, nv)):
        out.update(zip(SMALL_NAMES, _unpack_rows(buf, shapes)))

    return (loss, grads['x'][None], *[g[n] for n in WEIGHT_NAMES], *[delta[n] for n in WEIGHT_NAMES],
            *[new_m[n] for n in WEIGHT_NAMES], *[new_v[n] for n in WEIGHT_NAMES])
```

```python
import jax
import jax.numpy as jnp
import numpy as np
from jax import lax
from jax.experimental import pallas as pl
from jax.experimental.pallas import tpu as pltpu
from jax.experimental.pallas import tpu_sc as plsc

F32 = jnp.float32
BF16 = jnp.bfloat16

D_MODEL = 1024
DEPTH = 2
N_META = 16
HEADS = 8
V_DIM = 64
NOPE = 64
ROPE = 32
HALF_ROPE = ROPE // 2
QK_DIM = NOPE + ROPE
Q_RANK = 384
KV_RANK = 256
MIX = 512
CONV_K = 3
S5_GROUPS = 32
S5_GROUP = 16
S5_STATE = 64
S5_CH = S5_GROUPS * S5_STATE
D_FF = 2816
ALPHA = (2.0 * DEPTH) ** 0.25
LN_EPS = 1e-5
RMS_EPS = 1e-6
ROPE_BASE = 10000.0
IN_SPLITS = (Q_RANK, KV_RANK, ROPE, MIX, MIX, MIX, MIX, 3 * D_MODEL)
D_IN = sum(IN_SPLITS)
ADAM_LR, ADAM_B1, ADAM_B2, ADAM_EPS, ADAM_WD, ADAM_STEP = 0.001, 0.9, 0.999, 1e-08, 0.01, 10

N_DEV = 8
AXES = ("x", "y", "c")
LANES = 128
PACK_COLS = 1024
HEAD_PAD = 128
VMEM_LIMIT = 48 * 1024 * 1024

PIN_CQ, PIN_CKV, PIN_KR1, PIN_KR2, PIN_XBAR, PIN_BG, PIN_CG, PIN_U, PIN_GATES, PIN_END = (
    0, 384, 640, 768, 896, 1408, 1920, 2432, 2944, 6016)
D_IN_PAD = 6144

WEIGHT_NAMES = ['meta', 'ffn1_w_gate', 'ffn1_w_up', 'ffn1_w_down', 'ln1_g', 'ln1_b', 'w_in', 'mla_q_norm_g', 'mla_w_uq',
                'mla_kv_norm_g', 'mla_w_ukv', 'mla_w_o', 'conv_w', 'conv_b', 'conv_w_out', 's5_a_re', 's5_a_im',
                's5_log_dt', 's5_b_re', 's5_b_im', 's5_c_re', 's5_c_im', 's5_d', 's5_w_glu', 's5_b_glu', 's5_w_out',
                'w_o', 'ln2_g', 'ln2_b', 'ffn2_w_gate', 'ffn2_w_up', 'ffn2_w_down', 'ln3_g', 'ln3_b']
BIG = {'ffn1_w_gate': 2, 'ffn1_w_up': 2, 'ffn1_w_down': 1, 'w_in': 2, 'mla_w_uq': 2, 'mla_w_ukv': 2, 'mla_w_o': 2,
       'conv_w_out': 2, 's5_w_glu': 1, 's5_w_out': 2, 'w_o': 1, 'ffn2_w_gate': 2, 'ffn2_w_up': 2, 'ffn2_w_down': 1}
SMALL_SHARDED = ('meta', 'conv_w')
SMALL_NAMES = [n for n in WEIGHT_NAMES if n not in BIG]


def _divisor_tile(n, limit, mult):
    best = None
    for t in range(mult, min(n, limit) + 1, mult):
        if n % t == 0:
            best = t
    return best if best is not None else n


def _params(*sem):
    return pltpu.CompilerParams(dimension_semantics=sem, vmem_limit_bytes=VMEM_LIMIT)


def _matmul(a, b, *, ta=False, tb=False, out_dtype=F32, add=None, name):
    m, k = (a.shape[1], a.shape[0]) if ta else a.shape
    n = b.shape[0] if tb else b.shape[1]
    assert (b.shape[1] if tb else b.shape[0]) == k, (a.shape, b.shape, ta, tb)
    both_bf16 = a.dtype == BF16 and b.dtype == BF16
    deep = both_bf16 and not ta and 1408 < k <= D_FF
    tm = (_divisor_tile(m, 1408, LANES) if ta else
          _divisor_tile(m, 1088 if (deep or a.dtype != BF16) else 2176, 16))
    tn = _divisor_tile(n, 512, LANES)
    tk = k if deep else _divisor_tile(k, 2176 if (ta and both_bf16) else 1408, 16 if ta else LANES)
    nk = k // tk
    dims = (((0 if ta else 1,), (1 if tb else 0,)), ((), ()))

    in_place = jnp.dtype(out_dtype) == jnp.dtype(F32)

    def kern(a_ref, b_ref, *rest):
        add_ref = rest[0] if add is not None else None
        o_ref, *scratch = rest[1:] if add is not None else rest
        kk = pl.program_id(2)
        part = lax.dot_general(a_ref[...].astype(BF16), b_ref[...].astype(BF16), dims, preferred_element_type=F32)
        first = lambda: part if add_ref is None else part + add_ref[...].astype(F32)
        if nk == 1:
            o_ref[...] = first().astype(o_ref.dtype)
            return
        acc_ref = o_ref if in_place else scratch[0]

        @pl.when(kk == 0)
        def _():
            acc_ref[...] = first()

        @pl.when(kk > 0)
        def _():
            acc_ref[...] += part

        if not in_place:
            @pl.when(kk == nk - 1)
            def _():
                o_ref[...] = acc_ref[...].astype(o_ref.dtype)

    a_spec = pl.BlockSpec((tk, tm), lambda i, j, kk: (kk, i)) if ta else pl.BlockSpec((tm, tk), lambda i, j, kk: (i, kk))
    b_spec = pl.BlockSpec((tn, tk), lambda i, j, kk: (j, kk)) if tb else pl.BlockSpec((tk, tn), lambda i, j, kk: (kk, j))
    o_spec = pl.BlockSpec((tm, tn), lambda i, j, kk: (i, j))
    return pl.pallas_call(
        kern, name=name, grid=(m // tm, n // tn, nk),
        in_specs=[a_spec, b_spec] + ([o_spec] if add is not None else []), out_specs=o_spec,
        out_shape=jax.ShapeDtypeStruct((m, n), out_dtype),
        scratch_shapes=[] if (nk == 1 or in_place) else [pltpu.VMEM((tm, tn), F32)],
        compiler_params=_params("parallel", "parallel", "arbitrary"),
    )(a, b, *([add] if add is not None else []))


def _make_mm(name, wt=False):
    @jax.custom_vjp
    def mm(x, w, wz):
        return _matmul(x, w, tb=wt, name=name + "_fwd")

    def fwd(x, w, wz):
        return _matmul(x, w, tb=wt, name=name + "_fwd"), (x, w)

    def bwd(res, dy):
        x, w = res
        wz_dtype = BF16
        dx = _matmul(dy, w, tb=not wt, out_dtype=x.dtype, name=name + "_dx")
        dw = (_matmul(dy, x, ta=True, out_dtype=wz_dtype, name=name + "_dw") if wt
              else _matmul(x, dy, ta=True, out_dtype=wz_dtype, name=name + "_dw"))
        return dx, jnp.zeros_like(w), dw

    mm.defvjp(fwd, bwd)
    return mm


def _make_mm_f32w(name):
    @jax.custom_vjp
    def mm(x, w):
        return _matmul(x, w, name=name + "_fwd")

    def fwd(x, w):
        return _matmul(x, w, name=name + "_fwd"), (x, w)

    def bwd(res, dy):
        x, w = res
        return (_matmul(dy, w, tb=True, out_dtype=x.dtype, name=name + "_dx"),
                _matmul(x, dy, ta=True, name=name + "_dw"))

    mm.defvjp(fwd, bwd)
    return mm


BD_BLOCKS = 4
BD_PARTS = 2


def _bd_call(a, b, out_shape, a_blk, b_blk, o_blk, a_idx, b_idx, o_idx, dims, reduce_parts, name):
    def kern(a_ref, b_ref, o_ref):
        part = lax.dot_general(a_ref[...].astype(BF16), b_ref[...].astype(BF16), dims, preferred_element_type=F32)
        if not reduce_parts:
            o_ref[...] = part.astype(o_ref.dtype)
            return

        @pl.when(pl.program_id(1) == 0)
        def _():
            o_ref[...] = part

        @pl.when(pl.program_id(1) > 0)
        def _():
            o_ref[...] += part

    return pl.pallas_call(
        kern, name=name, grid=(BD_BLOCKS, BD_PARTS),
        in_specs=[pl.BlockSpec(a_blk, a_idx), pl.BlockSpec(b_blk, b_idx)], out_specs=pl.BlockSpec(o_blk, o_idx),
        out_shape=jax.ShapeDtypeStruct(out_shape, F32),
        compiler_params=_params("parallel", "arbitrary" if reduce_parts else "parallel"),
    )(a, b)


def _make_bd_in(name):
    wide = lambda j, p: (0, BD_BLOCKS * p + j)
    thin = lambda j, p: (0, j)
    diag = lambda j, p: (j, BD_BLOCKS * p + j)
    nn, nt, tn = (((1,), (0,)), ((), ())), (((1,), (1,)), ((), ())), (((0,), (0,)), ((), ()))

    def run(x, w):
        length, cols = x.shape[0], w.shape[1] // (BD_BLOCKS * BD_PARTS)
        return _bd_call(x, w, (length, w.shape[1]), (length, LANES), (LANES, cols), (length, cols), thin, diag, wide,
                        nn, False, name + "_fwd")

    @jax.custom_vjp
    def mm(x, w):
        return run(x, w)

    def fwd(x, w):
        return run(x, w), (x, w)

    def bwd(res, dy):
        x, w = res
        length, cols = x.shape[0], w.shape[1] // (BD_BLOCKS * BD_PARTS)
        dx = _bd_call(dy, w, x.shape, (length, cols), (LANES, cols), (length, LANES), wide, diag, thin, nt, True,
                      name + "_dx")
        dw = _bd_call(x, dy, w.shape, (length, LANES), (length, cols), (LANES, cols), thin, wide, diag, tn, False,
                      name + "_dw")
        return dx, dw

    mm.defvjp(fwd, bwd)
    return mm


def _make_bd_out(name):
    wide = lambda j, p: (0, BD_BLOCKS * p + j)
    thin = lambda j, p: (0, j)
    diag = lambda j, p: (BD_BLOCKS * p + j, j)
    nn, nt, tn = (((1,), (0,)), ((), ())), (((1,), (1,)), ((), ())), (((0,), (0,)), ((), ()))

    def run(x, w):
        length, cols = x.shape[0], w.shape[0] // (BD_BLOCKS * BD_PARTS)
        return _bd_call(x, w, (length, w.shape[1]), (length, cols), (cols, LANES), (length, LANES), wide, diag, thin,
                        nn, True, name + "_fwd")

    @jax.custom_vjp
    def mm(x, w):
        return run(x, w)

    def fwd(x, w):
        return run(x, w), (x, w)

    def bwd(res, dy):
        x, w = res
        length, cols = x.shape[0], w.shape[0] // (BD_BLOCKS * BD_PARTS)
        dx = _bd_call(dy, w, x.shape, (length, LANES), (cols, LANES), (length, cols), thin, diag, wide, nt, False,
                      name + "_dx")
        dw = _bd_call(x, dy, w.shape, (length, cols), (length, LANES), (cols, LANES), wide, thin, diag, tn, False,
                      name + "_dw")
        return dx, dw

    mm.defvjp(fwd, bwd)
    return mm


def _row_tile(rows, widths):
    limit = max(16, (6 * 1024 * 1024 // 4) // max(1, sum(widths)))
    return _divisor_tile(rows, limit, 16)


def _make_rowwise(f, n_rows, n_pars, out_dtypes, name, n_nodiff=0, grad_dtypes=None):
    n_out = len(out_dtypes)
    n_diff = n_rows - n_nodiff

    def run_fwd(rows, pars):
        length = rows[0].shape[0]
        shapes = jax.eval_shape(lambda *a: f(*a), *[jax.ShapeDtypeStruct((16, r.shape[1]), F32) for r in rows],
                                *[jax.ShapeDtypeStruct(p.shape, F32) for p in pars])
        widths = [s.shape[1] for s in shapes]
        tm = _row_tile(length, [r.shape[1] for r in rows] + widths)

        def kern(*refs):
            ins = [r[...].astype(F32) for r in refs[:n_rows + n_pars]]
            outs = f(*ins)
            for o_ref, o in zip(refs[n_rows + n_pars:], outs):
                o_ref[...] = o.astype(o_ref.dtype)

        return pl.pallas_call(
            kern, name=name + "_fwd", grid=(length // tm,),
            in_specs=[pl.BlockSpec((tm, r.shape[1]), lambda i: (i, 0)) for r in rows]
            + [pl.BlockSpec(p.shape, lambda i: (0, 0)) for p in pars],
            out_specs=[pl.BlockSpec((tm, w), lambda i: (i, 0)) for w in widths],
            out_shape=[jax.ShapeDtypeStruct((length, w), dt) for w, dt in zip(widths, out_dtypes)],
            compiler_params=_params("parallel"),
        )(*rows, *pars)

    def run_bwd(rows, pars, cts):
        length = rows[0].shape[0]
        tm = _row_tile(length, [r.shape[1] for r in rows] * 2 + [c.shape[1] for c in cts] * 2)

        def kern(*refs):
            ins = [r[...].astype(F32) for r in refs[:n_rows + n_pars]]
            ct = [r[...].astype(F32) for r in refs[n_rows + n_pars:n_rows + n_pars + n_out]]
            out_refs = refs[n_rows + n_pars + n_out:]
            nodiff = ins[n_diff:n_rows]
            _, vjp = jax.vjp(lambda *a: f(*a[:n_diff], *nodiff, *a[n_diff:]), *ins[:n_diff], *ins[n_rows:])
            grads = vjp(tuple(ct))
            for o_ref, g in zip(out_refs[:n_diff], grads[:n_diff]):
                o_ref[...] = g.astype(o_ref.dtype)
            first = pl.program_id(0) == 0
            for o_ref, g in zip(out_refs[n_diff:], grads[n_diff:]):
                @pl.when(first)
                def _(o_ref=o_ref, g=g):
                    o_ref[...] = g

                @pl.when(jnp.logical_not(first))
                def _(o_ref=o_ref, g=g):
                    o_ref[...] += g

        return pl.pallas_call(
            kern, name=name + "_bwd", grid=(length // tm,),
            in_specs=[pl.BlockSpec((tm, r.shape[1]), lambda i: (i, 0)) for r in rows]
            + [pl.BlockSpec(p.shape, lambda i: (0, 0)) for p in pars]
            + [pl.BlockSpec((tm, c.shape[1]), lambda i: (i, 0)) for c in cts],
            out_specs=[pl.BlockSpec((tm, r.shape[1]), lambda i: (i, 0)) for r in rows[:n_diff]]
            + [pl.BlockSpec(p.shape, lambda i: (0, 0)) for p in pars],
            out_shape=[jax.ShapeDtypeStruct(r.shape, r.dtype if grad_dtypes is None else grad_dtypes[i])
                       for i, r in enumerate(rows[:n_diff])]
            + [jax.ShapeDtypeStruct(p.shape, F32) for p in pars],
            compiler_params=_params("arbitrary"),
        )(*rows, *pars, *cts)

    @jax.custom_vjp
    def op(*args):
        return tuple(run_fwd(args[:n_rows], args[n_rows:]))

    def fwd(*args):
        return tuple(run_fwd(args[:n_rows], args[n_rows:])), args

    def bwd(args, cts):
        grads = run_bwd(args[:n_rows], args[n_rows:], cts)
        zeros = [jnp.zeros_like(r) for r in args[n_diff:n_rows]]
        return (*grads[:n_diff], *zeros, *grads[n_diff:])

    op.defvjp(fwd, bwd)
    op.run_fwd, op.run_bwd = run_fwd, run_bwd
    return op


def _layer_norm(z, g, b):
    mu = jnp.mean(z, axis=-1, keepdims=True)
    d = z - mu
    var = jnp.mean(d * d, axis=-1, keepdims=True)
    return d * lax.rsqrt(var + LN_EPS) * g + b


def _f_ln_half(h, f, g, b):
    return (_layer_norm(ALPHA * h + 0.5 * f, g, b),)


def _f_ln_full(h, f, g, b):
    return (_layer_norm(ALPHA * h + f, g, b),)


def _f_rms(x, g):
    return (x * lax.rsqrt(jnp.mean(x * x, axis=-1, keepdims=True) + RMS_EPS) * g,)


def _f_rope(x1, x2, cos, sin):
    return x1 * cos - x2 * sin, x2 * cos + x1 * sin


def _f_gelu_skip(y, u, d):
    return (jax.nn.gelu(y + d * u),)


def _f_glu(z, t, b):
    return (z * jax.nn.sigmoid(t + b),)


def _f_merge(ga, gb, gc, ya, yb, yc):
    return (jax.nn.sigmoid(ga) * ya + jax.nn.sigmoid(gb) * yb + jax.nn.sigmoid(gc) * yc,)


def _f_swiglu(gate, up):
    return (jax.nn.silu(gate) * up,)


def _f_cast(x):
    return (x,)


def _make_ffn(tag):
    ln = _make_rowwise(_f_ln_half, 2, 2, (F32,), f"{tag}_ln", grad_dtypes=(F32, BF16))
    cast = _make_rowwise(_f_cast, 1, 0, (BF16,), f"{tag}_cast")
    swiglu = _make_rowwise(_f_swiglu, 2, 0, (BF16,), f"{tag}_swiglu", grad_dtypes=(BF16, BF16))

    def forward(h, w_gate, w_up, w_down, g, b):
        hb, = cast.run_fwd((h,), ())
        gate = _matmul(hb, w_gate, tb=True, name=f"{tag}_gate_fwd")
        up = _matmul(hb, w_up, tb=True, name=f"{tag}_up_fwd")
        act, = swiglu.run_fwd((gate, up), ())
        f = _matmul(act, w_down, name=f"{tag}_down_fwd")
        y, = ln.run_fwd((h, f), (g, b))
        return y, (h, hb, gate, up, act, f, w_gate, w_up, w_down, g, b)

    @jax.custom_vjp
    def block(h, w_gate, w_up, w_down, z_gate, z_up, z_down, g, b):
        return forward(h, w_gate, w_up, w_down, g, b)[0]

    def fwd(h, w_gate, w_up, w_down, z_gate, z_up, z_down, g, b):
        return forward(h, w_gate, w_up, w_down, g, b)

    def bwd(res, dy):
        h, hb, gate, up, act, f, w_gate, w_up, w_down, g, b = res
        dh, df, dg, db = ln.run_bwd((h, f), (g, b), (dy,))
        dact = _matmul(df, w_down, tb=True, out_dtype=BF16, name=f"{tag}_down_dx")
        dw_down = _matmul(act, df, ta=True, out_dtype=BF16, name=f"{tag}_down_dw")
        dgate, dup = swiglu.run_bwd((gate, up), (), (dact,))
        dw_gate = _matmul(dgate, hb, ta=True, out_dtype=BF16, name=f"{tag}_gate_dw")
        dw_up = _matmul(dup, hb, ta=True, out_dtype=BF16, name=f"{tag}_up_dw")
        dh = _matmul(dgate, w_gate, add=dh, name=f"{tag}_gate_dx")
        dh = _matmul(dup, w_up, add=dh, name=f"{tag}_up_dx")
        zero = jnp.zeros_like
        return dh, zero(w_gate), zero(w_up), zero(w_down), dw_gate, dw_up, dw_down, dg, db

    block.defvjp(fwd, bwd)
    return block


def _attn_scores(q, k, q_block, tq):
    length = k.shape[0]
    s = lax.dot_general(q, k, (((1,), (1,)), ((), ())), preferred_element_type=F32) * (QK_DIM ** -0.5)
    row = q_block * tq + lax.broadcasted_iota(jnp.int32, (tq, length), 0)
    col = lax.broadcasted_iota(jnp.int32, (tq, length), 1)
    s = jnp.where(col <= row, s, -1e30)
    e = jnp.exp(s - jnp.max(s, axis=1, keepdims=True))
    return e * (1.0 / jnp.sum(e, axis=1, keepdims=True))


ATTN_SEGMENTS = 4


def _attn_tiles(length):
    seg = length // ATTN_SEGMENTS
    return seg, _divisor_tile(seg, 272, 16)


def _attn_fwd(q3, k3, v3):
    heads, length, _ = q3.shape
    seg, tq = _attn_tiles(length)
    outs = []
    for s in range(ATTN_SEGMENTS):
        kmax, base = (s + 1) * seg, s * (seg // tq)

        def kern(q_ref, k_ref, v_ref, o_ref, base=base):
            p = _attn_scores(q_ref[0], k_ref[0], base + pl.program_id(1), tq)
            o_ref[0] = jnp.dot(p.astype(BF16), v_ref[0], preferred_element_type=F32).astype(o_ref.dtype)

        outs.append(pl.pallas_call(
            kern, name=f"attn_fwd_seg{s}", grid=(heads, seg // tq),
            in_specs=[pl.BlockSpec((1, tq, HEAD_PAD), lambda h, i, base=base: (h, base + i, 0)),
                      pl.BlockSpec((1, kmax, HEAD_PAD), lambda h, i: (h, 0, 0)),
                      pl.BlockSpec((1, kmax, V_DIM), lambda h, i: (h, 0, 0))],
            out_specs=pl.BlockSpec((1, tq, V_DIM), lambda h, i: (h, i, 0)),
            out_shape=jax.ShapeDtypeStruct((heads, seg, V_DIM), F32),
            compiler_params=_params("parallel", "parallel"),
        )(q3, k3, v3))
    return jnp.concatenate(outs, axis=1)


def _attn_bwd(q3, k3, v3, do3):
    heads, length, _ = q3.shape
    seg, tq = _attn_tiles(length)
    dk = jnp.zeros((heads, length, HEAD_PAD), F32)
    dv = jnp.zeros((heads, length, V_DIM), F32)
    dqs = [None] * ATTN_SEGMENTS
    for s in reversed(range(ATTN_SEGMENTS)):
        kmax, base = (s + 1) * seg, s * (seg // tq)

        def kern(q_ref, k_ref, v_ref, do_ref, dk_in, dv_in, dq_ref, dk_ref, dv_ref, base=base):
            i = pl.program_id(1)
            q, k, v, do = q_ref[0], k_ref[0], v_ref[0], do_ref[0].astype(BF16)
            p = _attn_scores(q, k, base + i, tq)
            dp = lax.dot_general(do, v, (((1,), (1,)), ((), ())), preferred_element_type=F32)
            ds = (p * (dp - jnp.sum(p * dp, axis=1, keepdims=True)) * (QK_DIM ** -0.5)).astype(BF16)
            dq_ref[0] = jnp.dot(ds, k, preferred_element_type=F32)
            dk_part = lax.dot_general(ds, q, (((0,), (0,)), ((), ())), preferred_element_type=F32)
            dv_part = lax.dot_general(p.astype(BF16), do, (((0,), (0,)), ((), ())), preferred_element_type=F32)

            @pl.when(i == 0)
            def _():
                dk_ref[0] = dk_in[0] + dk_part
                dv_ref[0] = dv_in[0] + dv_part

            @pl.when(i > 0)
            def _():
                dk_ref[0] += dk_part
                dv_ref[0] += dv_part

        q_blk = pl.BlockSpec((1, tq, HEAD_PAD), lambda h, i, base=base: (h, base + i, 0))
        k_blk = pl.BlockSpec((1, kmax, HEAD_PAD), lambda h, i: (h, 0, 0))
        v_blk = pl.BlockSpec((1, kmax, V_DIM), lambda h, i: (h, 0, 0))
        dqs[s], dk, dv = pl.pallas_call(
            kern, name=f"attn_bwd_seg{s}", grid=(heads, seg // tq),
            in_specs=[q_blk, k_blk, v_blk, pl.BlockSpec((1, tq, V_DIM), lambda h, i, base=base: (h, base + i, 0)),
                      k_blk, v_blk],
            out_specs=[pl.BlockSpec((1, tq, HEAD_PAD), lambda h, i: (h, i, 0)), k_blk, v_blk],
            out_shape=[jax.ShapeDtypeStruct((heads, seg, HEAD_PAD), F32), jax.ShapeDtypeStruct(dk.shape, F32),
                       jax.ShapeDtypeStruct(dv.shape, F32)],
            input_output_aliases={4: 1, 5: 2}, compiler_params=_params("parallel", "arbitrary"),
        )(q3, k3, v3, do3, dk, dv)
    return jnp.concatenate(dqs, axis=1), dk, dv


@jax.custom_vjp
def _attention(q3, k3, v3):
    return _attn_fwd(q3, k3, v3)


def _attention_fwd(q3, k3, v3):
    return _attn_fwd(q3, k3, v3), (q3, k3, v3)


def _attention_bwd(res, do3):
    q3, k3, v3 = res
    dq, dk, dv = _attn_bwd(q3, k3, v3, do3)
    return dq.astype(q3.dtype), dk.astype(k3.dtype), dv.astype(v3.dtype)


_attention.defvjp(_attention_fwd, _attention_bwd)


def _conv_terms(x, c, w_ref, cb):
    u = c * x
    row = lax.broadcasted_iota(jnp.int32, u.shape, 0)
    u1 = jnp.where(row >= 1, pltpu.roll(u, 1, 0), 0.0)
    u2 = jnp.where(row >= 2, pltpu.roll(u, 2, 0), 0.0)
    y = cb + w_ref[0:1, :] * u2 + w_ref[1:2, :] * u1 + w_ref[2:3, :] * u
    return u, u1, u2, y


def _conv_specs(length):
    col = pl.BlockSpec((length, LANES), lambda j: (0, j))
    return col, pl.BlockSpec((CONV_K, LANES), lambda j: (0, j)), pl.BlockSpec((1, LANES), lambda j: (0, j))


def _conv_fwd(x, b, c, w, cb):
    length = x.shape[0]

    def kern(x_ref, b_ref, c_ref, w_ref, cb_ref, o_ref):
        _, _, _, y = _conv_terms(x_ref[...], c_ref[...], w_ref, cb_ref[...])
        o_ref[...] = b_ref[...] * y

    col, wspec, bspec = _conv_specs(length)
    return pl.pallas_call(
        kern, name="conv_fwd", grid=(MIX // LANES,), in_specs=[col, col, col, wspec, bspec], out_specs=col,
        out_shape=jax.ShapeDtypeStruct((length, MIX), F32), compiler_params=_params("parallel"),
    )(x, b, c, w, cb)


def _conv_bwd(x, b, c, w, cb, do):
    length = x.shape[0]

    def kern(x_ref, b_ref, c_ref, w_ref, cb_ref, do_ref, dx_ref, db_ref, dc_ref, dw_ref, dcb_ref):
        x, c, do = x_ref[...], c_ref[...], do_ref[...]
        u, u1, u2, y = _conv_terms(x, c, w_ref, cb_ref[...])
        db_ref[...] = do * y
        dy = do * b_ref[...]
        row = lax.broadcasted_iota(jnp.int32, dy.shape, 0)
        dy1 = jnp.where(row < length - 1, pltpu.roll(dy, length - 1, 0), 0.0)
        dy2 = jnp.where(row < length - 2, pltpu.roll(dy, length - 2, 0), 0.0)
        du = w_ref[2:3, :] * dy + w_ref[1:2, :] * dy1 + w_ref[0:1, :] * dy2
        dx_ref[...] = du * c
        dc_ref[...] = du * x
        dw_ref[0:1, :] = jnp.sum(dy * u2, axis=0, keepdims=True)
        dw_ref[1:2, :] = jnp.sum(dy * u1, axis=0, keepdims=True)
        dw_ref[2:3, :] = jnp.sum(dy * u, axis=0, keepdims=True)
        dcb_ref[...] = jnp.sum(dy, axis=0, keepdims=True)

    col, wspec, bspec = _conv_specs(length)
    big = jax.ShapeDtypeStruct((length, MIX), F32)
    return pl.pallas_call(
        kern, name="conv_bwd", grid=(MIX // LANES,), in_specs=[col, col, col, wspec, bspec, col],
        out_specs=[col, col, col, wspec, bspec],
        out_shape=[big, big, big, jax.ShapeDtypeStruct((CONV_K, MIX), F32), jax.ShapeDtypeStruct((1, MIX), F32)],
        compiler_params=_params("parallel"),
    )(x, b, c, w, cb, do)


@jax.custom_vjp
def _short_conv(x, b, c, w, cb):
    return _conv_fwd(x, b, c, w, cb)


def _short_conv_fwd(x, b, c, w, cb):
    return _conv_fwd(x, b, c, w, cb), (x, b, c, w, cb)


def _short_conv_bwd(res, do):
    return tuple(_conv_bwd(*res, do))


_short_conv.defvjp(_short_conv_fwd, _short_conv_bwd)


SCAN_ROWS = S5_CH // LANES
SCAN_TC = 136


def _scan_fwd(ar, ai, b):
    length = b.shape[0]
    tc = _divisor_tile(length, SCAN_TC, 8)

    def kern(ar_ref, ai_ref, b_ref, x_ref, sr, si):
        @pl.when(pl.program_id(0) == 0)
        def _():
            sr[...] = jnp.zeros_like(sr)
            si[...] = jnp.zeros_like(si)

        a_re, a_im = ar_ref[...], ai_ref[...]

        def body(t, carry):
            xr, xi = carry
            nr = a_re * xr - a_im * xi + b_ref[t, 0:SCAN_ROWS, :]
            ni = a_re * xi + a_im * xr + b_ref[t, SCAN_ROWS:2 * SCAN_ROWS, :]
            x_ref[t, 0:SCAN_ROWS, :] = nr
            x_ref[t, SCAN_ROWS:2 * SCAN_ROWS, :] = ni
            return nr, ni

        xr, xi = lax.fori_loop(0, tc, body, (sr[...], si[...]), unroll=4)
        sr[...] = xr
        si[...] = xi

    par = pl.BlockSpec((SCAN_ROWS, LANES), lambda i: (0, 0))
    blk = pl.BlockSpec((tc, 2 * SCAN_ROWS, LANES), lambda i: (i, 0, 0))
    return pl.pallas_call(
        kern, name="s5_scan_fwd", grid=(length // tc,), in_specs=[par, par, blk], out_specs=blk,
        out_shape=jax.ShapeDtypeStruct(b.shape, F32), scratch_shapes=[pltpu.VMEM((SCAN_ROWS, LANES), F32)] * 2,
        compiler_params=_params("arbitrary"),
    )(ar, ai, b)


def _scan_bwd(ar, ai, x, dx):
    length = x.shape[0]
    tc = _divisor_tile(length, SCAN_TC, 8)
    n_blk = length // tc
    re, im = slice(0, SCAN_ROWS), slice(SCAN_ROWS, 2 * SCAN_ROWS)

    def kern(ar_ref, ai_ref, x_ref, dx_ref, db_ref, dar_ref, dai_ref, lr_s, li_s):
        @pl.when(pl.program_id(0) == 0)
        def _():
            lr_s[...] = jnp.zeros_like(lr_s)
            li_s[...] = jnp.zeros_like(li_s)
            dar_ref[...] = jnp.zeros_like(dar_ref)
            dai_ref[...] = jnp.zeros_like(dai_ref)

        a_re, a_im = ar_ref[...], ai_ref[...]

        def body(j, carry):
            t = tc - 1 - j
            lr, li, gr, gi = carry
            x_re, x_im = x_ref[t, re, :], x_ref[t, im, :]
            gr = gr + (lr * x_re + li * x_im)
            gi = gi + (li * x_re - lr * x_im)
            nlr = dx_ref[t, re, :] + (a_re * lr + a_im * li)
            nli = dx_ref[t, im, :] + (a_re * li - a_im * lr)
            db_ref[t, re, :] = nlr
            db_ref[t, im, :] = nli
            return nlr, nli, gr, gi

        lr, li, gr, gi = lax.fori_loop(0, tc, body, (lr_s[...], li_s[...], dar_ref[...], dai_ref[...]), unroll=4)
        lr_s[...] = lr
        li_s[...] = li
        dar_ref[...] = gr
        dai_ref[...] = gi

    par = pl.BlockSpec((SCAN_ROWS, LANES), lambda i: (0, 0))
    blk = pl.BlockSpec((tc, 2 * SCAN_ROWS, LANES), lambda i: (n_blk - 1 - i, 0, 0))
    pout = jax.ShapeDtypeStruct((SCAN_ROWS, LANES), F32)
    return pl.pallas_call(
        kern, name="s5_scan_bwd", grid=(n_blk,), in_specs=[par, par, blk, blk],
        out_specs=[blk, par, par], out_shape=[jax.ShapeDtypeStruct(x.shape, F32), pout, pout],
        scratch_shapes=[pltpu.VMEM((SCAN_ROWS, LANES), F32)] * 2, compiler_params=_params("arbitrary"),
    )(ar, ai, x, dx)


@jax.custom_vjp
def _s5_scan(ar, ai, b):
    return _scan_fwd(ar, ai, b)


def _s5_scan_fwd(ar, ai, b):
    x = _scan_fwd(ar, ai, b)
    return x, (ar, ai, x)


def _s5_scan_bwd(res, dx):
    ar, ai, x = res
    db, dar, dai = _scan_bwd(ar, ai, x, dx)
    return dar, dai, db


_s5_scan.defvjp(_s5_scan_fwd, _s5_scan_bwd)


def _loss_call(y, target, n_real):
    length = y.shape[0]
    tm = _divisor_tile(length, 544, 16)

    def kern(y_ref, t_ref, loss_ref, dy_ref):
        i = pl.program_id(0)
        row = i * tm + lax.broadcasted_iota(jnp.int32, (tm, 1), 0)
        keep = jnp.logical_and(row >= N_META, row < n_real)
        err = jnp.where(keep, y_ref[...] - t_ref[...], 0.0)
        dy_ref[...] = err * (1.0 / D_MODEL)
        part = 0.5 * jnp.sum(jnp.mean(err * err, axis=-1, keepdims=True), axis=0, keepdims=True)

        @pl.when(i == 0)
        def _():
            loss_ref[...] = jnp.zeros_like(loss_ref)

        loss_ref[...] += part

    blk = pl.BlockSpec((tm, D_MODEL), lambda i: (i, 0))
    return pl.pallas_call(
        kern, name="loss_head", grid=(length // tm,), in_specs=[blk, blk],
        out_specs=[pl.BlockSpec((8, LANES), lambda i: (0, 0)), blk],
        out_shape=[jax.ShapeDtypeStruct((8, LANES), F32), jax.ShapeDtypeStruct(y.shape, F32)],
        compiler_params=_params("arbitrary"),
    )(y, target)


def _make_loss(n_real):
    @jax.custom_vjp
    def loss(y, target):
        return _loss_call(y, target, n_real)[0][0, 0]

    def fwd(y, target):
        total, dy = _loss_call(y, target, n_real)
        return total[0, 0], dy

    def bwd(dy, ct):
        return dy * ct, jnp.zeros_like(dy)

    loss.defvjp(fwd, bwd)
    return loss


HBM_SPEC = pl.BlockSpec(memory_space=pl.ANY)
MESH_ID = pl.DeviceIdType.MESH


SC_MESH = dict(axis_name="sequencer", num_cores=1)
GATHER_ID, SLOT_ID = 1, 2


def _handshake(peers):
    barrier = pltpu.get_barrier_semaphore()
    for peer in peers:
        pl.semaphore_signal(barrier, inc=1, device_id=peer, device_id_type=MESH_ID)
    pl.semaphore_wait(barrier, len(peers))


def _exchange_call(body, name, ins, out_types, n_sems, sequencer_id):
    n_in, n_out = len(ins), len(out_types)
    sems = [pltpu.SemaphoreType.DMA((n_sems,)), pltpu.SemaphoreType.DMA((n_sems,)), pltpu.SemaphoreType.DMA((n_in,))]
    if sequencer_id is None:
        def on_core(*refs):
            body(lambda peers: None, refs[:n_in], refs[n_in:n_in + n_out], *refs[n_in + n_out:])

        return pl.pallas_call(on_core, name=name, out_shape=out_types, in_specs=[HBM_SPEC] * n_in,
                              out_specs=[HBM_SPEC] * n_out, scratch_shapes=sems)(*ins)

    def on_sequencer(*refs):
        body(_handshake, refs[:n_in], refs[n_in:n_in + n_out], *refs[n_in + n_out:])

    return pl.kernel(on_sequencer, name=name, out_type=out_types, mesh=plsc.ScalarSubcoreMesh(**SC_MESH),
                     scratch_types=sems, compiler_params=pltpu.CompilerParams(collective_id=sequencer_id))(*ins)


def _all_gather(shards, name, sequencer=False):
    n = len(shards)

    def body(handshake, x_refs, out_refs, send_sems, recv_sems, local_sems):
        x, y, c = lax.axis_index("x"), lax.axis_index("y"), lax.axis_index("c")
        me, sibling = (x, y, c), (x, y, 1 - c)
        chips = [(1 - x, y), (x, 1 - y), (1 - x, 1 - y)]
        handshake([sibling] + [(*chip, c) for chip in chips])

        def copy(b, k, block, to, from_input=False):
            px, py, pc = block
            slot = out_refs[b].at[4 * px + 2 * py + pc]
            return pltpu.make_async_remote_copy(
                src_ref=x_refs[b] if from_input else slot, dst_ref=slot,
                send_sem=send_sems.at[7 * b + k], recv_sem=recv_sems.at[7 * b + k], device_id=to, device_id_type=MESH_ID)

        mine = [pltpu.make_async_copy(x_refs[b], out_refs[b].at[4 * x + 2 * y + c], local_sems.at[b]) for b in range(n)]
        for cp in mine:
            cp.start()
        first = []
        for b in range(n):
            first.append(copy(b, 0, me, sibling, from_input=True))
            first += [copy(b, 1 + j, me, (*chip, c), from_input=True) for j, chip in enumerate(chips)]
        for cp in first:
            cp.start()
        passed = []
        for j, chip in enumerate(chips):
            for b in range(n):
                copy(b, 1 + j, (*chip, c), me).wait_recv()
                passed.append(copy(b, 4 + j, (*chip, c), sibling))
                passed[-1].start()
        for b in range(n):
            copy(b, 0, sibling, me).wait_recv()
            for j, chip in enumerate(chips):
                copy(b, 4 + j, (*chip, 1 - c), me).wait_recv()
        for cp in first + passed:
            cp.wait_send()
        for cp in mine:
            cp.wait()

    out_types = [jax.ShapeDtypeStruct((N_DEV, *s.shape), s.dtype) for s in shards]
    return _exchange_call(body, name, shards, out_types, 7 * n, GATHER_ID if sequencer else None)


def _slot_exchange(bufs, name, sequencer=False):
    def body(handshake, ins, outs, send_sems, recv_sems, local_sems):
        x, y, c = lax.axis_index("x"), lax.axis_index("y"), lax.axis_index("c")
        me = 4 * x + 2 * y + c
        flips = [(dx, dy, dc) for dx in (0, 1) for dy in (0, 1) for dc in (0, 1)][1:]
        peers = [(1 - x if dx else x, 1 - y if dy else y, 1 - c if dc else c) for dx, dy, dc in flips]
        handshake(peers)
        own = [pltpu.make_async_copy(src.at[me], dst.at[me], local_sems.at[b]) for b, (src, dst) in enumerate(zip(ins, outs))]
        copies = []
        for b, (src, dst) in enumerate(zip(ins, outs)):
            for k, (px, py, pc) in enumerate(peers):
                copies.append(pltpu.make_async_remote_copy(
                    src_ref=src.at[4 * px + 2 * py + pc], dst_ref=dst.at[me],
                    send_sem=send_sems.at[7 * b + k], recv_sem=recv_sems.at[7 * b + k],
                    device_id=(px, py, pc), device_id_type=MESH_ID))
        for cp in own + copies:
            cp.start()
        for cp in copies + own:
            cp.wait()

    out_types = [jax.ShapeDtypeStruct(b.shape, b.dtype) for b in bufs]
    return _exchange_call(body, name, bufs, out_types, 7 * len(bufs), SLOT_ID if sequencer else None)


def _slot_sum(slots, name):
    _, rows, cols = slots.shape
    tm = _divisor_tile(rows, 512, 16)

    def kern(s_ref, o_ref):
        total = s_ref[0].astype(F32)
        for d in range(1, N_DEV):
            total = total + s_ref[d].astype(F32)
        o_ref[...] = total

    return pl.pallas_call(
        kern, name=name, grid=(rows // tm,), in_specs=[pl.BlockSpec((N_DEV, tm, cols), lambda i: (0, i, 0))],
        out_specs=pl.BlockSpec((tm, cols), lambda i: (i, 0)), out_shape=jax.ShapeDtypeStruct((rows, cols), F32),
        compiler_params=_params("parallel"),
    )(slots)


def _reduce_scatter(bufs, tag, sequencer):
    arrived = _slot_exchange(bufs, f"rs_exchange_{tag}", sequencer)
    return [_slot_sum(a, f"rs_sum_{tag}{i}") for i, a in enumerate(arrived)]


def _adamw(w, g, m, v, name):
    rows, cols = w.shape
    tm = _divisor_tile(rows, max(8, (512 * 1024) // cols // 8 * 8), 8)

    def kern(w_ref, g_ref, m_ref, v_ref, d_ref, nm_ref, nv_ref):
        g = g_ref[...]
        m = ADAM_B1 * m_ref[...] + (1.0 - ADAM_B1) * g
        v = ADAM_B2 * v_ref[...] + (1.0 - ADAM_B2) * (g * g)
        m_hat = m / (1.0 - ADAM_B1 ** ADAM_STEP)
        v_hat = v / (1.0 - ADAM_B2 ** ADAM_STEP)
        d_ref[...] = -ADAM_LR * (m_hat / (jnp.sqrt(v_hat) + ADAM_EPS) + ADAM_WD * w_ref[...])
        nm_ref[...] = m
        nv_ref[...] = v

    blk = pl.BlockSpec((tm, cols), lambda i: (i, 0))
    out = jax.ShapeDtypeStruct(w.shape, F32)
    return pl.pallas_call(
        kern, name=name, grid=(rows // tm,), in_specs=[blk] * 4, out_specs=[blk] * 3, out_shape=[out] * 3,
        compiler_params=_params("parallel"),
    )(w, g, m, v)


def _adamw_layers(w, g_layers, m, v, name):
    depth, rows, cols = w.shape
    tm = _divisor_tile(rows, max(8, (512 * 1024) // cols // 8 * 8), 8)

    def kern(w_ref, m_ref, v_ref, *refs):
        g_refs, (g_out, d_ref, nm_ref, nv_ref) = refs[:depth], refs[depth:]
        layer = pl.program_id(0)
        g = g_refs[0][...]
        for l in range(1, depth):
            g = jnp.where(layer == l, g_refs[l][...], g)
        m = ADAM_B1 * m_ref[0] + (1.0 - ADAM_B1) * g
        v = ADAM_B2 * v_ref[0] + (1.0 - ADAM_B2) * (g * g)
        m_hat = m / (1.0 - ADAM_B1 ** ADAM_STEP)
        v_hat = v / (1.0 - ADAM_B2 ** ADAM_STEP)
        g_out[0] = g
        d_ref[0] = -ADAM_LR * (m_hat / (jnp.sqrt(v_hat) + ADAM_EPS) + ADAM_WD * w_ref[0])
        nm_ref[0] = m
        nv_ref[0] = v

    blk = pl.BlockSpec((1, tm, cols), lambda l, i: (l, i, 0))
    out = jax.ShapeDtypeStruct(w.shape, F32)
    return pl.pallas_call(
        kern, name=name, grid=(depth, rows // tm),
        in_specs=[blk] * 3 + [pl.BlockSpec((tm, cols), lambda l, i: (i, 0))] * depth,
        out_specs=[blk] * 4, out_shape=[out] * 4, compiler_params=_params("parallel", "parallel"),
    )(w, m, v, *g_layers)


TRANSPOSED = ('ffn1_w_gate', 'ffn1_w_up', 'ffn2_w_gate', 'ffn2_w_up', 'w_in', 'mla_w_uq', 'mla_w_ukv', 'mla_w_o',
              'conv_w_out', 's5_w_out')
PIECES = ((('ffn1_w_gate',), ('ffn1_w_up',), ('ffn1_w_down',)),
          (('w_in', 'w_o'), ('mla_w_o', 'conv_w_out', 's5_w_out', 's5_w_glu'), ('mla_w_uq',), ('mla_w_ukv',)),
          (('ffn2_w_gate',), ('ffn2_w_up',), ('ffn2_w_down',)))
PIN_CUTS = (PIN_CQ, PIN_CKV, PIN_KR1, PIN_KR2, PIN_XBAR, PIN_BG, PIN_CG, PIN_U, PIN_GATES, PIN_GATES + D_MODEL,
            PIN_GATES + 2 * D_MODEL, PIN_END, D_IN_PAD)
PIN_PIECES = ((PIN_CQ, Q_RANK), (PIN_CKV, KV_RANK), (PIN_KR1, HALF_ROPE), (PIN_KR2, HALF_ROPE), (PIN_XBAR, 4 * MIX),
              (PIN_GATES, 3 * D_MODEL))


def _make_split(cuts):
    @jax.custom_vjp
    def split(t):
        return tuple(t[:, a:b] for a, b in zip(cuts[:-1], cuts[1:]))

    def fwd(t):
        return split(t), None

    def bwd(_, cts):
        return (jnp.concatenate(cts, axis=1),)

    split.defvjp(fwd, bwd)
    return split


def _make_projection(tag, cuts):
    cast = _make_rowwise(_f_cast, 1, 0, (BF16,), f"{tag}_cast")

    def forward(h, w):
        hb, = cast.run_fwd((h,), ())
        full = _matmul(hb, w, tb=True, name=f"{tag}_fwd")
        return tuple(full[:, a:b] for a, b in zip(cuts[:-1], cuts[1:])), (hb, w)

    @jax.custom_vjp
    def proj(h, w, wz):
        return forward(h, w)[0]

    def fwd(h, w, wz):
        return forward(h, w)

    def bwd(res, cts):
        hb, w = res
        d_full = jnp.concatenate([c.astype(BF16) for c in cts], axis=1)
        dw = _matmul(d_full, hb, ta=True, out_dtype=BF16, name=f"{tag}_dw")
        return _matmul(d_full, w, name=f"{tag}_dx"), jnp.zeros_like(w), dw

    proj.defvjp(fwd, bwd)
    return proj


def _travel_shape(name, shape):
    return (shape[2], shape[1]) if name in TRANSPOSED else (shape[1], shape[2])


def _pack_groups(tensors, layer, groups, dtype):
    def view(n):
        t = tensors[n][layer]
        return (t.T if n in TRANSPOSED else t).astype(dtype)
    return [jnp.concatenate([view(n) for n in grp], axis=0) for grp in groups]


def _unpack_groups(bufs, groups, shard_shapes):
    out = {}
    for buf, grp in zip(bufs, groups):
        at = 0
        for n in grp:
            r, _ = _travel_shape(n, shard_shapes[n])
            out[n] = buf[..., at:at + r, :]
            at += r
    return out


def _pack_rows(arrays):
    flat = jnp.concatenate([a.reshape(-1) for a in arrays])
    rows = -(-flat.shape[0] // PACK_COLS)
    rows = -(-rows // 8) * 8
    return jnp.pad(flat, (0, rows * PACK_COLS - flat.shape[0])).reshape(rows, PACK_COLS)


def _unpack_rows(buf, shapes):
    flat = buf.reshape(-1)
    out, at = [], 0
    for s in shapes:
        n = int(np.prod(s))
        out.append(flat[at:at + n].reshape(s))
        at += n
    return out


def _full_weight(t, axis):
    if axis == 1:
        return jnp.moveaxis(t, 0, 1).reshape(t.shape[1], N_DEV * t.shape[2], t.shape[3])
    return jnp.moveaxis(t, 0, 2).reshape(t.shape[1], t.shape[2], N_DEV * t.shape[3])


def _disassemble(d, groups, dtype):
    heads = lambda t: t.reshape(HEADS, -1, t.shape[1])
    full = {}
    for tag in ('ffn1', 'ffn2'):
        if f'{tag}_gate' in d:
            full.update({f'{tag}_w_gate': d[f'{tag}_gate'], f'{tag}_w_up': d[f'{tag}_up'], f'{tag}_w_down': d[f'{tag}_down']})
    if 'w_in' in d:
        w_in, uq, ukv = d['w_in'], d['w_uq'], d['w_ukv']
        full.update(
            w_in=jnp.concatenate([w_in[a:a + n] for a, n in PIN_PIECES], axis=0),
            mla_w_uq=jnp.concatenate([heads(uq[:HEADS * NOPE]), heads(uq[HEADS * NOPE:HEADS * NOPE + LANES]),
                                      heads(uq[HEADS * NOPE + LANES:])], axis=1).reshape(HEADS * QK_DIM, Q_RANK),
            mla_w_ukv=jnp.concatenate([heads(ukv[:HEADS * NOPE]), heads(ukv[HEADS * NOPE:])],
                                      axis=1).reshape(HEADS * (NOPE + V_DIM), KV_RANK),
            mla_w_o=d['mla_w_o'], conv_w_out=d['conv_w_out'], s5_w_glu=d['s5_w_glu'], s5_w_out=d['s5_w_out'], w_o=d['w_o'])
    return [jnp.concatenate([full[n].reshape(N_DEV, -1, full[n].shape[-1]).astype(dtype) for n in grp], axis=1)
            for grp in groups]


def _assemble(gathered, groups, shard_shapes):
    full = {n: t.reshape(N_DEV * t.shape[1], t.shape[2])
            for n, t in _unpack_groups(gathered, groups, shard_shapes).items()}
    out = {}
    for tag in ('ffn1', 'ffn2'):
        if f'{tag}_w_gate' in full:
            out.update({f'{tag}_gate': full[f'{tag}_w_gate'], f'{tag}_up': full[f'{tag}_w_up'],
                        f'{tag}_down': full[f'{tag}_w_down']})
    if 'w_in' not in full:
        return out
    w_in = full['w_in']
    cuts = np.cumsum((0,) + IN_SPLITS)
    cq, ckv, kr, xbar, bg, cg, u, gates = [w_in[a:b] for a, b in zip(cuts[:-1], cuts[1:])]
    pad = lambda t, n: jnp.pad(t, ((0, n - t.shape[0]), (0, 0)))
    w_in_packed = jnp.concatenate(
        [cq, ckv, pad(kr[:HALF_ROPE], LANES), pad(kr[HALF_ROPE:], LANES), xbar, bg, cg, u, gates,
         jnp.zeros((D_IN_PAD - PIN_END, D_MODEL), w_in.dtype)], axis=0)
    uq = full['mla_w_uq'].reshape(HEADS, QK_DIM, Q_RANK)
    w_uq = jnp.concatenate([uq[:, :NOPE].reshape(HEADS * NOPE, Q_RANK),
                            uq[:, NOPE:NOPE + HALF_ROPE].reshape(HEADS * HALF_ROPE, Q_RANK),
                            uq[:, NOPE + HALF_ROPE:].reshape(HEADS * HALF_ROPE, Q_RANK)], axis=0)
    ukv = full['mla_w_ukv'].reshape(HEADS, NOPE + V_DIM, KV_RANK)
    w_ukv = jnp.concatenate([ukv[:, :NOPE].reshape(HEADS * NOPE, KV_RANK),
                             ukv[:, NOPE:].reshape(HEADS * V_DIM, KV_RANK)], axis=0)
    out.update(w_in=w_in_packed, w_uq=w_uq, w_ukv=w_ukv, mla_w_o=full['mla_w_o'], conv_w_out=full['conv_w_out'],
               s5_w_glu=full['s5_w_glu'], s5_w_out=full['s5_w_out'], w_o=full['w_o'])
    return out


def _s5_discretize(a_re, a_im, log_dt, b_re, b_im, c_re, c_im):
    dt = jnp.exp(log_dt)[:, None]
    mag = jnp.exp(dt * a_re)
    ab_re, ab_im = mag * jnp.cos(dt * a_im), mag * jnp.sin(dt * a_im)
    den = a_re * a_re + a_im * a_im
    nr, ni = ab_re - 1.0, ab_im
    coef_re = (nr * a_re + ni * a_im) / den
    coef_im = (ni * a_re - nr * a_im) / den
    bb_re = coef_re[..., None] * b_re - coef_im[..., None] * b_im
    bb_im = coef_re[..., None] * b_im + coef_im[..., None] * b_re
    unit = jnp.arange(MIX)[:, None]
    chan = jnp.arange(2 * S5_CH)[None, :]
    pair = jnp.arange(2 * S5_STATE)[:, None]
    own = unit // S5_GROUP == (chan % S5_CH) // S5_STATE
    copy = jnp.logical_and(pair // S5_STATE == chan // S5_CH, pair % S5_STATE == chan % S5_STATE).astype(F32)
    flat_b = lambda bb: bb.transpose(0, 2, 1).reshape(MIX, S5_STATE)
    flat_c = lambda cc: cc.transpose(2, 0, 1).reshape(S5_STATE, MIX)
    b_small = jnp.concatenate([flat_b(bb_re), flat_b(bb_im)], axis=1)
    c_small = jnp.concatenate([flat_c(c_re), -flat_c(c_im)], axis=0)
    b_map = jnp.where(own, _make_mm_f32w("s5_spread_b")(b_small, copy), 0.0)
    c_map = jnp.where(own.T, _make_mm_f32w("s5_spread_c")(copy.T, c_small), 0.0)
    return ab_re.reshape(SCAN_ROWS, LANES), ab_im.reshape(SCAN_ROWS, LANES), b_map, c_map


def _rope_tables(length):
    inv_freq = ROPE_BASE ** (-jnp.arange(0, ROPE, 2, dtype=F32) / ROPE)
    ang = jnp.arange(length).astype(F32)[:, None] * inv_freq[None, :]
    return jnp.tile(jnp.cos(ang), (1, LANES // HALF_ROPE)), jnp.tile(jnp.sin(ang), (1, LANES // HALF_ROPE))


def _heads_first(t):
    return t.reshape(t.shape[0], HEADS, -1).transpose(1, 0, 2)


def _local_loss(diff, big, n_real):
    small, wz = diff['small'], diff['wz']
    h = diff['h0']
    length = h.shape[0]
    cos, sin = _rope_tables(length)
    row2 = lambda v: v.reshape(1, -1)
    for l in range(DEPTH):
        w, z = big[l], wz[l]
        p = {k: small[k][l] for k in small if k != 'meta'}
        ffn = lambda h, tag, ln: _make_ffn(tag)(
            h, *[w[f'{tag}_{k}'] for k in ('gate', 'up', 'down')], *[z[f'{tag}_{k}'] for k in ('gate', 'up', 'down')],
            row2(p[f'{ln}_g']), row2(p[f'{ln}_b']))
        h = ffn(h, "ffn1", "ln1")
        cq, ckv, kr1, kr2, xbar, bg, cg, u, gate_a, gate_b, gate_c, _ = _make_projection("w_in", PIN_CUTS)(
            h, w['w_in'], z['w_in'])
        qn, = _make_rowwise(_f_rms, 1, 1, (BF16,), "q_rms")(cq, row2(p['mla_q_norm_g']))
        kvn, = _make_rowwise(_f_rms, 1, 1, (BF16,), "kv_rms")(ckv, row2(p['mla_kv_norm_g']))
        q_nope, q1, q2 = _make_split((0, HEADS * NOPE, HEADS * NOPE + LANES, HEADS * NOPE + 2 * LANES))(
            _make_mm("w_uq", wt=True)(qn, w['w_uq'], z['w_uq']))
        k_nope, val = _make_split((0, HEADS * NOPE, HEADS * (NOPE + V_DIM)))(
            _make_mm("w_ukv", wt=True)(kvn, w['w_ukv'], z['w_ukv']))
        rope = _make_rowwise(_f_rope, 4, 0, (BF16, BF16), "rope", n_nodiff=2)
        q1, q2 = rope(q1, q2, cos, sin)
        k1, k2 = rope(kr1, kr2, cos, sin)
        hpad = jnp.zeros((HEADS, length, HEAD_PAD - QK_DIM), BF16)
        q3 = jnp.concatenate([_heads_first(q_nope.astype(BF16)), _heads_first(q1), _heads_first(q2), hpad], -1)
        shared = lambda t: jnp.broadcast_to(t[None, :, :HALF_ROPE], (HEADS, length, HALF_ROPE))
        k3 = jnp.concatenate([_heads_first(k_nope.astype(BF16)), shared(k1), shared(k2), hpad], -1)
        v3 = _heads_first(val.astype(BF16))
        o3 = _attention(q3, k3, v3)
        y_a = _make_mm("mla_w_o", wt=True)(o3.transpose(1, 0, 2).reshape(length, MIX), w['mla_w_o'], z['mla_w_o'])
        conv = _short_conv(xbar, bg, cg, p['conv_w_full'], row2(p['conv_b']))
        y_b = _make_mm("conv_w_out", wt=True)(conv, w['conv_w_out'], z['conv_w_out'])
        ar, ai, b_map, c_map = _s5_discretize(p['s5_a_re'], p['s5_a_im'], p['s5_log_dt'], p['s5_b_re'], p['s5_b_im'],
                                              p['s5_c_re'], p['s5_c_im'])
        bu = _make_bd_in("s5_b")(u, b_map)
        states = _s5_scan(ar, ai, bu.reshape(length, 2 * SCAN_ROWS, LANES)).reshape(length, 2 * S5_CH)
        y_ssm = _make_bd_out("s5_c")(states, c_map)
        zed, = _make_rowwise(_f_gelu_skip, 2, 1, (F32,), "s5_gelu")(y_ssm, u, row2(p['s5_d']))
        t = _make_mm("s5_w_glu")(zed, w['s5_w_glu'], z['s5_w_glu'])
        glu, = _make_rowwise(_f_glu, 2, 1, (BF16,), "s5_glu")(zed, t, row2(p['s5_b_glu']))
        y_c = _make_mm("s5_w_out", wt=True)(glu, w['s5_w_out'], z['s5_w_out'])
        mixed, = _make_rowwise(_f_merge, 6, 0, (BF16,), "merge")(gate_a, gate_b, gate_c, y_a, y_b, y_c)
        mix_out = _make_mm("w_o")(mixed, w['w_o'], z['w_o'])
        h, = _make_rowwise(_f_ln_full, 2, 2, (F32,), "mix_ln")(h, mix_out, row2(p['ln2_g']), row2(p['ln2_b']))
        h = ffn(h, "ffn2", "ln3")
    return _make_loss(n_real)(h, diff['target'])


def kernel(x, meta, ffn1_w_gate, ffn1_w_up, ffn1_w_down, ln1_g, ln1_b, w_in, mla_q_norm_g, mla_w_uq, mla_kv_norm_g, mla_w_ukv, mla_w_o, conv_w, conv_b, conv_w_out, s5_a_re, s5_a_im, s5_log_dt, s5_b_re, s5_b_im, s5_c_re, s5_c_im, s5_d, s5_w_glu, s5_b_glu, s5_w_out, w_o, ln2_g, ln2_b, ffn2_w_gate, ffn2_w_up, ffn2_w_down, ln3_g, ln3_b, loss_target, m_meta, m_ffn1_w_gate, m_ffn1_w_up, m_ffn1_w_down, m_ln1_g, m_ln1_b, m_w_in, m_mla_q_norm_g, m_mla_w_uq, m_mla_kv_norm_g, m_mla_w_ukv, m_mla_w_o, m_conv_w, m_conv_b, m_conv_w_out, m_s5_a_re, m_s5_a_im, m_s5_log_dt, m_s5_b_re, m_s5_b_im, m_s5_c_re, m_s5_c_im, m_s5_d, m_s5_w_glu, m_s5_b_glu, m_s5_w_out, m_w_o, m_ln2_g, m_ln2_b, m_ffn2_w_gate, m_ffn2_w_up, m_ffn2_w_down, m_ln3_g, m_ln3_b, v_meta, v_ffn1_w_gate, v_ffn1_w_up, v_ffn1_w_down, v_ln1_g, v_ln1_b, v_w_in, v_mla_q_norm_g, v_mla_w_uq, v_mla_kv_norm_g, v_mla_w_ukv, v_mla_w_o, v_conv_w, v_conv_b, v_conv_w_out, v_s5_a_re, v_s5_a_im, v_s5_log_dt, v_s5_b_re, v_s5_b_im, v_s5_c_re, v_s5_c_im, v_s5_d, v_s5_w_glu, v_s5_b_glu, v_s5_w_out, v_w_o, v_ln2_g, v_ln2_b, v_ffn2_w_gate, v_ffn2_w_up, v_ffn2_w_down, v_ln3_g, v_ln3_b):
    args = locals()
    w = {n: args[n] for n in WEIGHT_NAMES}
    m = {n: args["m_" + n] for n in WEIGHT_NAMES}
    v = {n: args["v_" + n] for n in WEIGHT_NAMES}
    me = 4 * lax.axis_index("x") + 2 * lax.axis_index("y") + lax.axis_index("c")
    seq = x.shape[1]
    n_real = N_META + seq
    length = -(-n_real // LANES) * LANES

    shard_shapes = {n: w[n].shape for n in BIG}
    small_shards = _all_gather([_pack_rows([w[n] for n in SMALL_SHARDED])], "gather_small",
                               sequencer=True)[0].reshape(N_DEV, -1)
    big = [{} for _ in range(DEPTH)]
    for l in range(DEPTH):
        for p, groups in enumerate(PIECES):
            packed = _pack_groups(w, l, groups, BF16)
            if (l, p) == (0, 1):
                packed = lax.optimization_barrier((gathered, packed))[1]
            gathered = _all_gather(packed, f"gather_weights_layer{l}_piece{p}", sequencer=True)
            big[l].update(_assemble(gathered, groups, shard_shapes))
    meta_full = _full_weight(small_shards[:, :meta.size].reshape(N_DEV, 1, *meta.shape), 2)[0]
    conv_w_full = _full_weight(small_shards[:, meta.size:meta.size + conv_w.size].reshape(N_DEV, *conv_w.shape), 2)

    small = {n: w[n] for n in SMALL_NAMES if n not in SMALL_SHARDED}
    small['conv_w_full'] = conv_w_full
    small['meta'] = meta_full
    wz = jax.tree.map(lambda t: jnp.zeros(t.shape, BF16), big)
    pad_rows = length - n_real

    def loss_fn(diff):
        h0 = jnp.concatenate([diff['small']['meta'], diff['x'], jnp.zeros((pad_rows, D_MODEL), F32)], axis=0)
        target = jnp.pad(loss_target[0], ((N_META, pad_rows), (0, 0)))
        return _local_loss(dict(h0=h0, small=diff['small'], wz=diff['wz'], target=target), big, n_real)

    loss_local, grads = jax.value_and_grad(loss_fn)(dict(x=x[0], small=small, wz=wz))

    small_names = [n for n in SMALL_NAMES if n not in SMALL_SHARDED] + ['conv_w_full', 'meta']
    small_flat = _pack_rows([loss_local.reshape(1)] + [grads['small'][n] for n in small_names])
    rows_each = -(-small_flat.shape[0] // (8 * N_DEV)) * 8
    small_flat = jnp.pad(small_flat, ((0, rows_each * N_DEV - small_flat.shape[0]), (0, 0)))
    layer_sums = [{} for _ in range(DEPTH)]
    for l in reversed(range(DEPTH)):
        for p, groups in reversed(list(enumerate(PIECES))):
            last = l == 0 and p == 0
            extra = [small_flat.reshape(N_DEV, rows_each, PACK_COLS)] if last else []
            sums = _reduce_scatter(_disassemble(grads['wz'][l], groups, BF16) + extra, f"grads_layer{l}_piece{p}_",
                                   sequencer=True)
            layer_sums[l].update(_unpack_groups(sums[:len(groups)], groups, shard_shapes))
            if last:
                small_sum = sums[-1]
    small_all, = _all_gather([small_sum], "gather_small_grads")
    loss, *small_sums = _unpack_rows(small_all, [()] + [grads['small'][n].shape for n in small_names])
    small_grads = dict(zip(small_names, small_sums))
    g = {}
    for name in SMALL_NAMES:
        if name == 'meta':
            g[name] = lax.dynamic_slice_in_dim(small_grads['meta'], me * meta.shape[1], meta.shape[1], axis=1)
        elif name == 'conv_w':
            g[name] = lax.dynamic_slice_in_dim(small_grads['conv_w_full'], me * conv_w.shape[2], conv_w.shape[2], axis=2)
        else:
            g[name] = small_grads[name]

    delta, new_m, new_v = {}, {}, {}
    for name in BIG:
        per_layer = [s[name].T if name in TRANSPOSED else s[name] for s in layer_sums]
        g[name], delta[name], new_m[name], new_v[name] = _adamw_layers(w[name], per_layer, m[name], v[name], f"adamw_{name}")
    shapes = [w[n].shape for n in SMALL_NAMES]
    d, nm, nv = _adamw(*[_pack_rows([t[n] for n in SMALL_NAMES]) for t in (w, g, m, v)], "adamw_small")
    for out, buf in ((delta, d), (new_m, nm), (new_v, nv)):
        out.update(zip(SMALL_NAMES, _unpack_rows(buf, shapes)))

    return (loss, grads['x'][None], *[g[n] for n in WEIGHT_NAMES], *[delta[n] for n in WEIGHT_NAMES],
            *[new_m[n] for n in WEIGHT_NAMES], *[new_v[n] for n in WEIGHT_NAMES])
```

```python
import jax
import jax.numpy as jnp
import numpy as np
from jax import lax
from jax.experimental import pallas as pl
from jax.experimental.pallas import tpu as pltpu
from jax.experimental.pallas import tpu_sc as plsc

F32 = jnp.float32
BF16 = jnp.bfloat16

D_MODEL = 1024
DEPTH = 2
N_META = 16
HEADS = 8
V_DIM = 64
NOPE = 64
ROPE = 32
HALF_ROPE = ROPE // 2
QK_DIM = NOPE + ROPE
Q_RANK = 384
KV_RANK = 256
MIX = 512
CONV_K = 3
S5_GROUPS = 32
S5_GROUP = 16
S5_STATE = 64
S5_CH = S5_GROUPS * S5_STATE
D_FF = 2816
ALPHA = (2.0 * DEPTH) ** 0.25
LN_EPS = 1e-5
RMS_EPS = 1e-6
ROPE_BASE = 10000.0
IN_SPLITS = (Q_RANK, KV_RANK, ROPE, MIX, MIX, MIX, MIX, 3 * D_MODEL)
D_IN = sum(IN_SPLITS)
ADAM_LR, ADAM_B1, ADAM_B2, ADAM_EPS, ADAM_WD, ADAM_STEP = 0.001, 0.9, 0.999, 1e-08, 0.01, 10

N_DEV = 8
AXES = ("x", "y", "c")
LANES = 128
PACK_COLS = 1024
HEAD_PAD = 128
VMEM_LIMIT = 48 * 1024 * 1024

PIN_CQ, PIN_CKV, PIN_KR1, PIN_KR2, PIN_XBAR, PIN_BG, PIN_CG, PIN_U, PIN_GATES, PIN_END = (
    0, 384, 640, 768, 896, 1408, 1920, 2432, 2944, 6016)
D_IN_PAD = 6144

WEIGHT_NAMES = ['meta', 'ffn1_w_gate', 'ffn1_w_up', 'ffn1_w_down', 'ln1_g', 'ln1_b', 'w_in', 'mla_q_norm_g', 'mla_w_uq',
                'mla_kv_norm_g', 'mla_w_ukv', 'mla_w_o', 'conv_w', 'conv_b', 'conv_w_out', 's5_a_re', 's5_a_im',
                's5_log_dt', 's5_b_re', 's5_b_im', 's5_c_re', 's5_c_im', 's5_d', 's5_w_glu', 's5_b_glu', 's5_w_out',
                'w_o', 'ln2_g', 'ln2_b', 'ffn2_w_gate', 'ffn2_w_up', 'ffn2_w_down', 'ln3_g', 'ln3_b']
BIG = {'ffn1_w_gate': 2, 'ffn1_w_up': 2, 'ffn1_w_down': 1, 'w_in': 2, 'mla_w_uq': 2, 'mla_w_ukv': 2, 'mla_w_o': 2,
       'conv_w_out': 2, 's5_w_glu': 1, 's5_w_out': 2, 'w_o': 1, 'ffn2_w_gate': 2, 'ffn2_w_up': 2, 'ffn2_w_down': 1}
SMALL_SHARDED = ('meta', 'conv_w')
SMALL_NAMES = [n for n in WEIGHT_NAMES if n not in BIG]


def _divisor_tile(n, limit, mult):
    best = None
    for t in range(mult, min(n, limit) + 1, mult):
        if n % t == 0:
            best = t
    return best if best is not None else n


def _params(*sem):
    return pltpu.CompilerParams(dimension_semantics=sem, vmem_limit_bytes=VMEM_LIMIT)


def _matmul(a, b, *, ta=False, tb=False, out_dtype=F32, add=None, name):
    m, k = (a.shape[1], a.shape[0]) if ta else a.shape
    n = b.shape[0] if tb else b.shape[1]
    assert (b.shape[1] if tb else b.shape[0]) == k, (a.shape, b.shape, ta, tb)
    both_bf16 = a.dtype == BF16 and b.dtype == BF16
    deep = both_bf16 and not ta and 1408 < k <= D_FF
    tm = (_divisor_tile(m, 1408, LANES) if ta else
          _divisor_tile(m, 1088 if (deep or a.dtype != BF16) else 2176, 16))
    tn = _divisor_tile(n, 512, LANES)
    tk = k if deep else _divisor_tile(k, 2176 if (ta and both_bf16) else 1408, 16 if ta else LANES)
    nk = k // tk
    dims = (((0 if ta else 1,), (1 if tb else 0,)), ((), ()))

    in_place = jnp.dtype(out_dtype) == jnp.dtype(F32)

    def kern(a_ref, b_ref, *rest):
        add_ref = rest[0] if add is not None else None
        o_ref, *scratch = rest[1:] if add is not None else rest
        kk = pl.program_id(2)
        part = lax.dot_general(a_ref[...].astype(BF16), b_ref[...].astype(BF16), dims, preferred_element_type=F32)
        first = lambda: part if add_ref is None else part + add_ref[...].astype(F32)
        if nk == 1:
            o_ref[...] = first().astype(o_ref.dtype)
            return
        acc_ref = o_ref if in_place else scratch[0]

        @pl.when(kk == 0)
        def _():
            acc_ref[...] = first()

        @pl.when(kk > 0)
        def _():
            acc_ref[...] += part

        if not in_place:
            @pl.when(kk == nk - 1)
            def _():
                o_ref[...] = acc_ref[...].astype(o_ref.dtype)

    a_spec = pl.BlockSpec((tk, tm), lambda i, j, kk: (kk, i)) if ta else pl.BlockSpec((tm, tk), lambda i, j, kk: (i, kk))
    b_spec = pl.BlockSpec((tn, tk), lambda i, j, kk: (j, kk)) if tb else pl.BlockSpec((tk, tn), lambda i, j, kk: (kk, j))
    o_spec = pl.BlockSpec((tm, tn), lambda i, j, kk: (i, j))
    return pl.pallas_call(
        kern, name=name, grid=(m // tm, n // tn, nk),
        in_specs=[a_spec, b_spec] + ([o_spec] if add is not None else []), out_specs=o_spec,
        out_shape=jax.ShapeDtypeStruct((m, n), out_dtype),
        scratch_shapes=[] if (nk == 1 or in_place) else [pltpu.VMEM((tm, tn), F32)],
        compiler_params=_params("parallel", "parallel", "arbitrary"),
    )(a, b, *([add] if add is not None else []))


def _make_mm(name, wt=False):
    @jax.custom_vjp
    def mm(x, w, wz):
        return _matmul(x, w, tb=wt, name=name + "_fwd")

    def fwd(x, w, wz):
        return _matmul(x, w, tb=wt, name=name + "_fwd"), (x, w)

    def bwd(res, dy):
        x, w = res
        wz_dtype = BF16
        dx = _matmul(dy, w, tb=not wt, out_dtype=x.dtype, name=name + "_dx")
        dw = (_matmul(dy, x, ta=True, out_dtype=wz_dtype, name=name + "_dw") if wt
              else _matmul(x, dy, ta=True, out_dtype=wz_dtype, name=name + "_dw"))
        return dx, jnp.zeros_like(w), dw

    mm.defvjp(fwd, bwd)
    return mm


def _make_mm_f32w(name):
    @jax.custom_vjp
    def mm(x, w):
        return _matmul(x, w, name=name + "_fwd")

    def fwd(x, w):
        return _matmul(x, w, name=name + "_fwd"), (x, w)

    def bwd(res, dy):
        x, w = res
        return (_matmul(dy, w, tb=True, out_dtype=x.dtype, name=name + "_dx"),
                _matmul(x, dy, ta=True, name=name + "_dw"))

    mm.defvjp(fwd, bwd)
    return mm


BD_BLOCKS = 4
BD_PARTS = 2


def _bd_call(a, b, out_shape, a_blk, b_blk, o_blk, a_idx, b_idx, o_idx, dims, reduce_parts, name):
    def kern(a_ref, b_ref, o_ref):
        part = lax.dot_general(a_ref[...].astype(BF16), b_ref[...].astype(BF16), dims, preferred_element_type=F32)
        if not reduce_parts:
            o_ref[...] = part.astype(o_ref.dtype)
            return

        @pl.when(pl.program_id(1) == 0)
        def _():
            o_ref[...] = part

        @pl.when(pl.program_id(1) > 0)
        def _():
            o_ref[...] += part

    return pl.pallas_call(
        kern, name=name, grid=(BD_BLOCKS, BD_PARTS),
        in_specs=[pl.BlockSpec(a_blk, a_idx), pl.BlockSpec(b_blk, b_idx)], out_specs=pl.BlockSpec(o_blk, o_idx),
        out_shape=jax.ShapeDtypeStruct(out_shape, F32),
        compiler_params=_params("parallel", "arbitrary" if reduce_parts else "parallel"),
    )(a, b)


def _make_bd_in(name):
    wide = lambda j, p: (0, BD_BLOCKS * p + j)
    thin = lambda j, p: (0, j)
    diag = lambda j, p: (j, BD_BLOCKS * p + j)
    nn, nt, tn = (((1,), (0,)), ((), ())), (((1,), (1,)), ((), ())), (((0,), (0,)), ((), ()))

    def run(x, w):
        length, cols = x.shape[0], w.shape[1] // (BD_BLOCKS * BD_PARTS)
        return _bd_call(x, w, (length, w.shape[1]), (length, LANES), (LANES, cols), (length, cols), thin, diag, wide,
                        nn, False, name + "_fwd")

    @jax.custom_vjp
    def mm(x, w):
        return run(x, w)

    def fwd(x, w):
        return run(x, w), (x, w)

    def bwd(res, dy):
        x, w = res
        length, cols = x.shape[0], w.shape[1] // (BD_BLOCKS * BD_PARTS)
        dx = _bd_call(dy, w, x.shape, (length, cols), (LANES, cols), (length, LANES), wide, diag, thin, nt, True,
                      name + "_dx")
        dw = _bd_call(x, dy, w.shape, (length, LANES), (length, cols), (LANES, cols), thin, wide, diag, tn, False,
                      name + "_dw")
        return dx, dw

    mm.defvjp(fwd, bwd)
    return mm


def _make_bd_out(name):
    wide = lambda j, p: (0, BD_BLOCKS * p + j)
    thin = lambda j, p: (0, j)
    diag = lambda j, p: (BD_BLOCKS * p + j, j)
    nn, nt, tn = (((1,), (0,)), ((), ())), (((1,), (1,)), ((), ())), (((0,), (0,)), ((), ()))

    def run(x, w):
        length, cols = x.shape[0], w.shape[0] // (BD_BLOCKS * BD_PARTS)
        return _bd_call(x, w, (length, w.shape[1]), (length, cols), (cols, LANES), (length, LANES), wide, diag, thin,
                        nn, True, name + "_fwd")

    @jax.custom_vjp
    def mm(x, w):
        return run(x, w)

    def fwd(x, w):
        return run(x, w), (x, w)

    def bwd(res, dy):
        x, w = res
        length, cols = x.shape[0], w.shape[0] // (BD_BLOCKS * BD_PARTS)
        dx = _bd_call(dy, w, x.shape, (length, LANES), (cols, LANES), (length, cols), thin, diag, wide, nt, False,
                      name + "_dx")
        dw = _bd_call(x, dy, w.shape, (length, cols), (length, LANES), (cols, LANES), wide, thin, diag, tn, False,
                      name + "_dw")
        return dx, dw

    mm.defvjp(fwd, bwd)
    return mm


def _row_tile(rows, widths):
    limit = max(16, (6 * 1024 * 1024 // 4) // max(1, sum(widths)))
    return _divisor_tile(rows, limit, 16)


def _make_rowwise(f, n_rows, n_pars, out_dtypes, name, n_nodiff=0, grad_dtypes=None):
    n_out = len(out_dtypes)
    n_diff = n_rows - n_nodiff

    def run_fwd(rows, pars):
        length = rows[0].shape[0]
        shapes = jax.eval_shape(lambda *a: f(*a), *[jax.ShapeDtypeStruct((16, r.shape[1]), F32) for r in rows],
                                *[jax.ShapeDtypeStruct(p.shape, F32) for p in pars])
        widths = [s.shape[1] for s in shapes]
        tm = _row_tile(length, [r.shape[1] for r in rows] + widths)

        def kern(*refs):
            ins = [r[...].astype(F32) for r in refs[:n_rows + n_pars]]
            outs = f(*ins)
            for o_ref, o in zip(refs[n_rows + n_pars:], outs):
                o_ref[...] = o.astype(o_ref.dtype)

        return pl.pallas_call(
            kern, name=name + "_fwd", grid=(length // tm,),
            in_specs=[pl.BlockSpec((tm, r.shape[1]), lambda i: (i, 0)) for r in rows]
            + [pl.BlockSpec(p.shape, lambda i: (0, 0)) for p in pars],
            out_specs=[pl.BlockSpec((tm, w), lambda i: (i, 0)) for w in widths],
            out_shape=[jax.ShapeDtypeStruct((length, w), dt) for w, dt in zip(widths, out_dtypes)],
            compiler_params=_params("parallel"),
        )(*rows, *pars)

    def run_bwd(rows, pars, cts):
        length = rows[0].shape[0]
        tm = _row_tile(length, [r.shape[1] for r in rows] * 2 + [c.shape[1] for c in cts] * 2)

        def kern(*refs):
            ins = [r[...].astype(F32) for r in refs[:n_rows + n_pars]]
            ct = [r[...].astype(F32) for r in refs[n_rows + n_pars:n_rows + n_pars + n_out]]
            out_refs = refs[n_rows + n_pars + n_out:]
            nodiff = ins[n_diff:n_rows]
            _, vjp = jax.vjp(lambda *a: f(*a[:n_diff], *nodiff, *a[n_diff:]), *ins[:n_diff], *ins[n_rows:])
            grads = vjp(tuple(ct))
            for o_ref, g in zip(out_refs[:n_diff], grads[:n_diff]):
                o_ref[...] = g.astype(o_ref.dtype)
            first = pl.program_id(0) == 0
            for o_ref, g in zip(out_refs[n_diff:], grads[n_diff:]):
                @pl.when(first)
                def _(o_ref=o_ref, g=g):
                    o_ref[...] = g

                @pl.when(jnp.logical_not(first))
                def _(o_ref=o_ref, g=g):
                    o_ref[...] += g

        return pl.pallas_call(
            kern, name=name + "_bwd", grid=(length // tm,),
            in_specs=[pl.BlockSpec((tm, r.shape[1]), lambda i: (i, 0)) for r in rows]
            + [pl.BlockSpec(p.shape, lambda i: (0, 0)) for p in pars]
            + [pl.BlockSpec((tm, c.shape[1]), lambda i: (i, 0)) for c in cts],
            out_specs=[pl.BlockSpec((tm, r.shape[1]), lambda i: (i, 0)) for r in rows[:n_diff]]
            + [pl.BlockSpec(p.shape, lambda i: (0, 0)) for p in pars],
            out_shape=[jax.ShapeDtypeStruct(r.shape, r.dtype if grad_dtypes is None else grad_dtypes[i])
                       for i, r in enumerate(rows[:n_diff])]
            + [jax.ShapeDtypeStruct(p.shape, F32) for p in pars],
            compiler_params=_params("arbitrary"),
        )(*rows, *pars, *cts)

    @jax.custom_vjp
    def op(*args):
        return tuple(run_fwd(args[:n_rows], args[n_rows:]))

    def fwd(*args):
        return tuple(run_fwd(args[:n_rows], args[n_rows:])), args

    def bwd(args, cts):
        grads = run_bwd(args[:n_rows], args[n_rows:], cts)
        zeros = [jnp.zeros_like(r) for r in args[n_diff:n_rows]]
        return (*grads[:n_diff], *zeros, *grads[n_diff:])

    op.defvjp(fwd, bwd)
    op.run_fwd, op.run_bwd = run_fwd, run_bwd
    return op


def _layer_norm(z, g, b):
    mu = jnp.mean(z, axis=-1, keepdims=True)
    d = z - mu
    var = jnp.mean(d * d, axis=-1, keepdims=True)
    return d * lax.rsqrt(var + LN_EPS) * g + b


def _f_ln_half(h, f, g, b):
    return (_layer_norm(ALPHA * h + 0.5 * f, g, b),)


def _f_ln_full(h, f, g, b):
    return (_layer_norm(ALPHA * h + f, g, b),)


def _f_rms(x, g):
    return (x * lax.rsqrt(jnp.mean(x * x, axis=-1, keepdims=True) + RMS_EPS) * g,)


def _f_rope(x1, x2, cos, sin):
    return x1 * cos - x2 * sin, x2 * cos + x1 * sin


def _f_gelu_skip(y, u, d):
    return (jax.nn.gelu(y + d * u),)


def _f_glu(z, t, b):
    return (z * jax.nn.sigmoid(t + b),)


def _f_merge(ga, gb, gc, ya, yb, yc):
    return (jax.nn.sigmoid(ga) * ya + jax.nn.sigmoid(gb) * yb + jax.nn.sigmoid(gc) * yc,)


def _f_swiglu(gate, up):
    return (jax.nn.silu(gate) * up,)


def _swiglu_bwd(gate, up, dact, name):
    length, width = gate.shape
    tm = _divisor_tile(length, 128, 16)

    def kern(g_ref, u_ref, d_ref, dg_ref, du_ref):
        g, d = g_ref[...], d_ref[...].astype(F32)
        s = jax.nn.sigmoid(g)
        dg_ref[...] = (d * u_ref[...] * (s * (1.0 + g * (1.0 - s)))).astype(dg_ref.dtype)
        du_ref[...] = (d * (g * s)).astype(du_ref.dtype)

    blk = pl.BlockSpec((tm, width), lambda i: (i, 0))
    out = jax.ShapeDtypeStruct(gate.shape, BF16)
    return pl.pallas_call(kern, name=name, grid=(length // tm,), in_specs=[blk] * 3, out_specs=[blk] * 2,
                          out_shape=[out, out], compiler_params=_params("parallel"))(gate, up, dact)


def _f_cast(x):
    return (x,)


def _make_ffn(tag):
    ln = _make_rowwise(_f_ln_half, 2, 2, (F32,), f"{tag}_ln", grad_dtypes=(F32, BF16))
    cast = _make_rowwise(_f_cast, 1, 0, (BF16,), f"{tag}_cast")
    swiglu = _make_rowwise(_f_swiglu, 2, 0, (BF16,), f"{tag}_swiglu", grad_dtypes=(BF16, BF16))

    def forward(h, w_gate, w_up, w_down, g, b):
        hb, = cast.run_fwd((h,), ())
        gate = _matmul(hb, w_gate, tb=True, name=f"{tag}_gate_fwd")
        up = _matmul(hb, w_up, tb=True, name=f"{tag}_up_fwd")
        act, = swiglu.run_fwd((gate, up), ())
        f = _matmul(act, w_down, name=f"{tag}_down_fwd")
        y, = ln.run_fwd((h, f), (g, b))
        return y, (h, hb, gate, up, act, f, w_gate, w_up, w_down, g, b)

    @jax.custom_vjp
    def block(h, w_gate, w_up, w_down, z_gate, z_up, z_down, g, b):
        return forward(h, w_gate, w_up, w_down, g, b)[0]

    def fwd(h, w_gate, w_up, w_down, z_gate, z_up, z_down, g, b):
        return forward(h, w_gate, w_up, w_down, g, b)

    def bwd(res, dy):
        h, hb, gate, up, act, f, w_gate, w_up, w_down, g, b = res
        dh, df, dg, db = ln.run_bwd((h, f), (g, b), (dy,))
        dact = _matmul(df, w_down, tb=True, out_dtype=BF16, name=f"{tag}_down_dx")
        dw_down = _matmul(act, df, ta=True, out_dtype=BF16, name=f"{tag}_down_dw")
        dgate, dup = _swiglu_bwd(gate, up, dact, f"{tag}_swiglu_bwd")
        dw_gate = _matmul(dgate, hb, ta=True, out_dtype=BF16, name=f"{tag}_gate_dw")
        dw_up = _matmul(dup, hb, ta=True, out_dtype=BF16, name=f"{tag}_up_dw")
        dh = _matmul(dgate, w_gate, add=dh, name=f"{tag}_gate_dx")
        dh = _matmul(dup, w_up, add=dh, name=f"{tag}_up_dx")
        zero = jnp.zeros_like
        return dh, zero(w_gate), zero(w_up), zero(w_down), dw_gate, dw_up, dw_down, dg, db

    block.defvjp(fwd, bwd)
    return block


def _attn_scores(q, k, q_block, tq):
    length = k.shape[0]
    s = lax.dot_general(q, k, (((1,), (1,)), ((), ())), preferred_element_type=F32) * (QK_DIM ** -0.5)
    row = q_block * tq + lax.broadcasted_iota(jnp.int32, (tq, length), 0)
    col = lax.broadcasted_iota(jnp.int32, (tq, length), 1)
    s = jnp.where(col <= row, s, -1e30)
    e = jnp.exp(s - jnp.max(s, axis=1, keepdims=True))
    return e * (1.0 / jnp.sum(e, axis=1, keepdims=True))


ATTN_SEGMENTS = 4


def _attn_tiles(length):
    seg = length // ATTN_SEGMENTS
    return seg, _divisor_tile(seg, 272, 16)


def _attn_fwd(q3, k3, v3):
    heads, length, _ = q3.shape
    seg, tq = _attn_tiles(length)
    outs = []
    for s in range(ATTN_SEGMENTS):
        kmax, base = (s + 1) * seg, s * (seg // tq)

        def kern(q_ref, k_ref, v_ref, o_ref, base=base):
            p = _attn_scores(q_ref[0], k_ref[0], base + pl.program_id(1), tq)
            o_ref[0] = jnp.dot(p.astype(BF16), v_ref[0], preferred_element_type=F32).astype(o_ref.dtype)

        outs.append(pl.pallas_call(
            kern, name=f"attn_fwd_seg{s}", grid=(heads, seg // tq),
            in_specs=[pl.BlockSpec((1, tq, HEAD_PAD), lambda h, i, base=base: (h, base + i, 0)),
                      pl.BlockSpec((1, kmax, HEAD_PAD), lambda h, i: (h, 0, 0)),
                      pl.BlockSpec((1, kmax, V_DIM), lambda h, i: (h, 0, 0))],
            out_specs=pl.BlockSpec((1, tq, V_DIM), lambda h, i: (h, i, 0)),
            out_shape=jax.ShapeDtypeStruct((heads, seg, V_DIM), F32),
            compiler_params=_params("parallel", "parallel"),
        )(q3, k3, v3))
    return jnp.concatenate(outs, axis=1)


def _attn_bwd(q3, k3, v3, do3):
    heads, length, _ = q3.shape
    seg, tq = _attn_tiles(length)
    dk = jnp.zeros((heads, length, HEAD_PAD), F32)
    dv = jnp.zeros((heads, length, V_DIM), F32)
    dqs = [None] * ATTN_SEGMENTS
    for s in reversed(range(ATTN_SEGMENTS)):
        kmax, base = (s + 1) * seg, s * (seg // tq)

        def kern(q_ref, k_ref, v_ref, do_ref, dk_in, dv_in, dq_ref, dk_ref, dv_ref, base=base):
            i = pl.program_id(1)
            q, k, v, do = q_ref[0], k_ref[0], v_ref[0], do_ref[0].astype(BF16)
            p = _attn_scores(q, k, base + i, tq)
            dp = lax.dot_general(do, v, (((1,), (1,)), ((), ())), preferred_element_type=F32)
            ds = (p * (dp - jnp.sum(p * dp, axis=1, keepdims=True)) * (QK_DIM ** -0.5)).astype(BF16)
            dq_ref[0] = jnp.dot(ds, k, preferred_element_type=F32)
            dk_part = lax.dot_general(ds, q, (((0,), (0,)), ((), ())), preferred_element_type=F32)
            dv_part = lax.dot_general(p.astype(BF16), do, (((0,), (0,)), ((), ())), preferred_element_type=F32)

            @pl.when(i == 0)
            def _():
                dk_ref[0] = dk_in[0] + dk_part
                dv_ref[0] = dv_in[0] + dv_part

            @pl.when(i > 0)
            def _():
                dk_ref[0] += dk_part
                dv_ref[0] += dv_part

        q_blk = pl.BlockSpec((1, tq, HEAD_PAD), lambda h, i, base=base: (h, base + i, 0))
        k_blk = pl.BlockSpec((1, kmax, HEAD_PAD), lambda h, i: (h, 0, 0))
        v_blk = pl.BlockSpec((1, kmax, V_DIM), lambda h, i: (h, 0, 0))
        dqs[s], dk, dv = pl.pallas_call(
            kern, name=f"attn_bwd_seg{s}", grid=(heads, seg // tq),
            in_specs=[q_blk, k_blk, v_blk, pl.BlockSpec((1, tq, V_DIM), lambda h, i, base=base: (h, base + i, 0)),
                      k_blk, v_blk],
            out_specs=[pl.BlockSpec((1, tq, HEAD_PAD), lambda h, i: (h, i, 0)), k_blk, v_blk],
            out_shape=[jax.ShapeDtypeStruct((heads, seg, HEAD_PAD), F32), jax.ShapeDtypeStruct(dk.shape, F32),
                       jax.ShapeDtypeStruct(dv.shape, F32)],
            input_output_aliases={4: 1, 5: 2}, compiler_params=_params("parallel", "arbitrary"),
        )(q3, k3, v3, do3, dk, dv)
    return jnp.concatenate(dqs, axis=1), dk, dv


@jax.custom_vjp
def _attention(q3, k3, v3):
    return _attn_fwd(q3, k3, v3)


def _attention_fwd(q3, k3, v3):
    return _attn_fwd(q3, k3, v3), (q3, k3, v3)


def _attention_bwd(res, do3):
    q3, k3, v3 = res
    dq, dk, dv = _attn_bwd(q3, k3, v3, do3)
    return dq.astype(q3.dtype), dk.astype(k3.dtype), dv.astype(v3.dtype)


_attention.defvjp(_attention_fwd, _attention_bwd)


def _conv_terms(x, c, w_ref, cb):
    u = c * x
    row = lax.broadcasted_iota(jnp.int32, u.shape, 0)
    u1 = jnp.where(row >= 1, pltpu.roll(u, 1, 0), 0.0)
    u2 = jnp.where(row >= 2, pltpu.roll(u, 2, 0), 0.0)
    y = cb + w_ref[0:1, :] * u2 + w_ref[1:2, :] * u1 + w_ref[2:3, :] * u
    return u, u1, u2, y


def _conv_specs(length):
    col = pl.BlockSpec((length, LANES), lambda j: (0, j))
    return col, pl.BlockSpec((CONV_K, LANES), lambda j: (0, j)), pl.BlockSpec((1, LANES), lambda j: (0, j))


def _conv_fwd(x, b, c, w, cb):
    length = x.shape[0]

    def kern(x_ref, b_ref, c_ref, w_ref, cb_ref, o_ref):
        _, _, _, y = _conv_terms(x_ref[...], c_ref[...], w_ref, cb_ref[...])
        o_ref[...] = b_ref[...] * y

    col, wspec, bspec = _conv_specs(length)
    return pl.pallas_call(
        kern, name="conv_fwd", grid=(MIX // LANES,), in_specs=[col, col, col, wspec, bspec], out_specs=col,
        out_shape=jax.ShapeDtypeStruct((length, MIX), F32), compiler_params=_params("parallel"),
    )(x, b, c, w, cb)


def _conv_bwd(x, b, c, w, cb, do):
    length = x.shape[0]

    def kern(x_ref, b_ref, c_ref, w_ref, cb_ref, do_ref, dx_ref, db_ref, dc_ref, dw_ref, dcb_ref):
        x, c, do = x_ref[...], c_ref[...], do_ref[...]
        u, u1, u2, y = _conv_terms(x, c, w_ref, cb_ref[...])
        db_ref[...] = do * y
        dy = do * b_ref[...]
        row = lax.broadcasted_iota(jnp.int32, dy.shape, 0)
        dy1 = jnp.where(row < length - 1, pltpu.roll(dy, length - 1, 0), 0.0)
        dy2 = jnp.where(row < length - 2, pltpu.roll(dy, length - 2, 0), 0.0)
        du = w_ref[2:3, :] * dy + w_ref[1:2, :] * dy1 + w_ref[0:1, :] * dy2
        dx_ref[...] = du * c
        dc_ref[...] = du * x
        dw_ref[0:1, :] = jnp.sum(dy * u2, axis=0, keepdims=True)
        dw_ref[1:2, :] = jnp.sum(dy * u1, axis=0, keepdims=True)
        dw_ref[2:3, :] = jnp.sum(dy * u, axis=0, keepdims=True)
        dcb_ref[...] = jnp.sum(dy, axis=0, keepdims=True)

    col, wspec, bspec = _conv_specs(length)
    big = jax.ShapeDtypeStruct((length, MIX), F32)
    return pl.pallas_call(
        kern, name="conv_bwd", grid=(MIX // LANES,), in_specs=[col, col, col, wspec, bspec, col],
        out_specs=[col, col, col, wspec, bspec],
        out_shape=[big, big, big, jax.ShapeDtypeStruct((CONV_K, MIX), F32), jax.ShapeDtypeStruct((1, MIX), F32)],
        compiler_params=_params("parallel"),
    )(x, b, c, w, cb, do)


@jax.custom_vjp
def _short_conv(x, b, c, w, cb):
    return _conv_fwd(x, b, c, w, cb)


def _short_conv_fwd(x, b, c, w, cb):
    return _conv_fwd(x, b, c, w, cb), (x, b, c, w, cb)


def _short_conv_bwd(res, do):
    return tuple(_conv_bwd(*res, do))


_short_conv.defvjp(_short_conv_fwd, _short_conv_bwd)


SCAN_ROWS = S5_CH // LANES
SCAN_TC = 136


def _scan_fwd(ar, ai, b):
    length = b.shape[0]
    tc = _divisor_tile(length, SCAN_TC, 8)

    def kern(ar_ref, ai_ref, b_ref, x_ref, sr, si):
        @pl.when(pl.program_id(0) == 0)
        def _():
            sr[...] = jnp.zeros_like(sr)
            si[...] = jnp.zeros_like(si)

        a_re, a_im = ar_ref[...], ai_ref[...]

        def body(t, carry):
            xr, xi = carry
            nr = a_re * xr - a_im * xi + b_ref[t, 0:SCAN_ROWS, :]
            ni = a_re * xi + a_im * xr + b_ref[t, SCAN_ROWS:2 * SCAN_ROWS, :]
            x_ref[t, 0:SCAN_ROWS, :] = nr
            x_ref[t, SCAN_ROWS:2 * SCAN_ROWS, :] = ni
            return nr, ni

        xr, xi = lax.fori_loop(0, tc, body, (sr[...], si[...]), unroll=4)
        sr[...] = xr
        si[...] = xi

    par = pl.BlockSpec((SCAN_ROWS, LANES), lambda i: (0, 0))
    blk = pl.BlockSpec((tc, 2 * SCAN_ROWS, LANES), lambda i: (i, 0, 0))
    return pl.pallas_call(
        kern, name="s5_scan_fwd", grid=(length // tc,), in_specs=[par, par, blk], out_specs=blk,
        out_shape=jax.ShapeDtypeStruct(b.shape, F32), scratch_shapes=[pltpu.VMEM((SCAN_ROWS, LANES), F32)] * 2,
        compiler_params=_params("arbitrary"),
    )(ar, ai, b)


def _scan_bwd(ar, ai, x, dx):
    length = x.shape[0]
    tc = _divisor_tile(length, SCAN_TC, 8)
    n_blk = length // tc
    re, im = slice(0, SCAN_ROWS), slice(SCAN_ROWS, 2 * SCAN_ROWS)

    def kern(ar_ref, ai_ref, x_ref, dx_ref, db_ref, dar_ref, dai_ref, lr_s, li_s):
        @pl.when(pl.program_id(0) == 0)
        def _():
            lr_s[...] = jnp.zeros_like(lr_s)
            li_s[...] = jnp.zeros_like(li_s)
            dar_ref[...] = jnp.zeros_like(dar_ref)
            dai_ref[...] = jnp.zeros_like(dai_ref)

        a_re, a_im = ar_ref[...], ai_ref[...]

        def body(j, carry):
            t = tc - 1 - j
            lr, li, gr, gi = carry
            x_re, x_im = x_ref[t, re, :], x_ref[t, im, :]
            gr = gr + (lr * x_re + li * x_im)
            gi = gi + (li * x_re - lr * x_im)
            nlr = dx_ref[t, re, :] + (a_re * lr + a_im * li)
            nli = dx_ref[t, im, :] + (a_re * li - a_im * lr)
            db_ref[t, re, :] = nlr
            db_ref[t, im, :] = nli
            return nlr, nli, gr, gi

        lr, li, gr, gi = lax.fori_loop(0, tc, body, (lr_s[...], li_s[...], dar_ref[...], dai_ref[...]), unroll=4)
        lr_s[...] = lr
        li_s[...] = li
        dar_ref[...] = gr
        dai_ref[...] = gi

    par = pl.BlockSpec((SCAN_ROWS, LANES), lambda i: (0, 0))
    blk = pl.BlockSpec((tc, 2 * SCAN_ROWS, LANES), lambda i: (n_blk - 1 - i, 0, 0))
    pout = jax.ShapeDtypeStruct((SCAN_ROWS, LANES), F32)
    return pl.pallas_call(
        kern, name="s5_scan_bwd", grid=(n_blk,), in_specs=[par, par, blk, blk],
        out_specs=[blk, par, par], out_shape=[jax.ShapeDtypeStruct(x.shape, F32), pout, pout],
        scratch_shapes=[pltpu.VMEM((SCAN_ROWS, LANES), F32)] * 2, compiler_params=_params("arbitrary"),
    )(ar, ai, x, dx)


@jax.custom_vjp
def _s5_scan(ar, ai, b):
    return _scan_fwd(ar, ai, b)


def _s5_scan_fwd(ar, ai, b):
    x = _scan_fwd(ar, ai, b)
    return x, (ar, ai, x)


def _s5_scan_bwd(res, dx):
    ar, ai, x = res
    db, dar, dai = _scan_bwd(ar, ai, x, dx)
    return dar, dai, db


_s5_scan.defvjp(_s5_scan_fwd, _s5_scan_bwd)


def _loss_call(y, target, n_real):
    length = y.shape[0]
    tm = _divisor_tile(length, 544, 16)

    def kern(y_ref, t_ref, loss_ref, dy_ref):
        i = pl.program_id(0)
        row = i * tm + lax.broadcasted_iota(jnp.int32, (tm, 1), 0)
        keep = jnp.logical_and(row >= N_META, row < n_real)
        err = jnp.where(keep, y_ref[...] - t_ref[...], 0.0)
        dy_ref[...] = err * (1.0 / D_MODEL)
        part = 0.5 * jnp.sum(jnp.mean(err * err, axis=-1, keepdims=True), axis=0, keepdims=True)

        @pl.when(i == 0)
        def _():
            loss_ref[...] = jnp.zeros_like(loss_ref)

        loss_ref[...] += part

    blk = pl.BlockSpec((tm, D_MODEL), lambda i: (i, 0))
    return pl.pallas_call(
        kern, name="loss_head", grid=(length // tm,), in_specs=[blk, blk],
        out_specs=[pl.BlockSpec((8, LANES), lambda i: (0, 0)), blk],
        out_shape=[jax.ShapeDtypeStruct((8, LANES), F32), jax.ShapeDtypeStruct(y.shape, F32)],
        compiler_params=_params("arbitrary"),
    )(y, target)


def _make_loss(n_real):
    @jax.custom_vjp
    def loss(y, target):
        return _loss_call(y, target, n_real)[0][0, 0]

    def fwd(y, target):
        total, dy = _loss_call(y, target, n_real)
        return total[0, 0], dy

    def bwd(dy, ct):
        return dy * ct, jnp.zeros_like(dy)

    loss.defvjp(fwd, bwd)
    return loss


HBM_SPEC = pl.BlockSpec(memory_space=pl.ANY)
MESH_ID = pl.DeviceIdType.MESH


SC_MESH = dict(axis_name="sequencer", num_cores=1)
GATHER_ID, SLOT_ID = 1, 2


def _handshake(peers):
    barrier = pltpu.get_barrier_semaphore()
    for peer in peers:
        pl.semaphore_signal(barrier, inc=1, device_id=peer, device_id_type=MESH_ID)
    pl.semaphore_wait(barrier, len(peers))


def _exchange_call(body, name, ins, out_types, n_sems, sequencer_id):
    n_in, n_out = len(ins), len(out_types)
    sems = [pltpu.SemaphoreType.DMA((n_sems,)), pltpu.SemaphoreType.DMA((n_sems,)), pltpu.SemaphoreType.DMA((n_in,))]
    if sequencer_id is None:
        def on_core(*refs):
            body(lambda peers: None, refs[:n_in], refs[n_in:n_in + n_out], *refs[n_in + n_out:])

        return pl.pallas_call(on_core, name=name, out_shape=out_types, in_specs=[HBM_SPEC] * n_in,
                              out_specs=[HBM_SPEC] * n_out, scratch_shapes=sems)(*ins)

    def on_sequencer(*refs):
        body(_handshake, refs[:n_in], refs[n_in:n_in + n_out], *refs[n_in + n_out:])

    return pl.kernel(on_sequencer, name=name, out_type=out_types, mesh=plsc.ScalarSubcoreMesh(**SC_MESH),
                     scratch_types=sems, compiler_params=pltpu.CompilerParams(collective_id=sequencer_id))(*ins)


def _all_gather(shards, name, sequencer=False):
    n = len(shards)

    def body(handshake, x_refs, out_refs, send_sems, recv_sems, local_sems):
        x, y, c = lax.axis_index("x"), lax.axis_index("y"), lax.axis_index("c")
        me, sibling = (x, y, c), (x, y, 1 - c)
        chips = [(1 - x, y), (x, 1 - y), (1 - x, 1 - y)]
        handshake([sibling] + [(*chip, c) for chip in chips])

        def copy(b, k, block, to, from_input=False):
            px, py, pc = block
            slot = out_refs[b].at[4 * px + 2 * py + pc]
            return pltpu.make_async_remote_copy(
                src_ref=x_refs[b] if from_input else slot, dst_ref=slot,
                send_sem=send_sems.at[7 * b + k], recv_sem=recv_sems.at[7 * b + k], device_id=to, device_id_type=MESH_ID)

        mine = [pltpu.make_async_copy(x_refs[b], out_refs[b].at[4 * x + 2 * y + c], local_sems.at[b]) for b in range(n)]
        for cp in mine:
            cp.start()
        first = []
        for b in range(n):
            first.append(copy(b, 0, me, sibling, from_input=True))
            first += [copy(b, 1 + j, me, (*chip, c), from_input=True) for j, chip in enumerate(chips)]
        for cp in first:
            cp.start()
        passed = []
        for j, chip in enumerate(chips):
            for b in range(n):
                copy(b, 1 + j, (*chip, c), me).wait_recv()
                passed.append(copy(b, 4 + j, (*chip, c), sibling))
                passed[-1].start()
        for b in range(n):
            copy(b, 0, sibling, me).wait_recv()
            for j, chip in enumerate(chips):
                copy(b, 4 + j, (*chip, 1 - c), me).wait_recv()
        for cp in first + passed:
            cp.wait_send()
        for cp in mine:
            cp.wait()

    out_types = [jax.ShapeDtypeStruct((N_DEV, *s.shape), s.dtype) for s in shards]
    return _exchange_call(body, name, shards, out_types, 7 * n, GATHER_ID if sequencer else None)


def _slot_exchange(bufs, name, sequencer=False):
    def body(handshake, ins, outs, send_sems, recv_sems, local_sems):
        x, y, c = lax.axis_index("x"), lax.axis_index("y"), lax.axis_index("c")
        me = 4 * x + 2 * y + c
        flips = [(dx, dy, dc) for dx in (0, 1) for dy in (0, 1) for dc in (0, 1)][1:]
        peers = [(1 - x if dx else x, 1 - y if dy else y, 1 - c if dc else c) for dx, dy, dc in flips]
        handshake(peers)
        own = [pltpu.make_async_copy(src.at[me], dst.at[me], local_sems.at[b]) for b, (src, dst) in enumerate(zip(ins, outs))]
        copies = []
        for b, (src, dst) in enumerate(zip(ins, outs)):
            for k, (px, py, pc) in enumerate(peers):
                copies.append(pltpu.make_async_remote_copy(
                    src_ref=src.at[4 * px + 2 * py + pc], dst_ref=dst.at[me],
                    send_sem=send_sems.at[7 * b + k], recv_sem=recv_sems.at[7 * b + k],
                    device_id=(px, py, pc), device_id_type=MESH_ID))
        for cp in own + copies:
            cp.start()
        for cp in copies + own:
            cp.wait()

    out_types = [jax.ShapeDtypeStruct(b.shape, b.dtype) for b in bufs]
    return _exchange_call(body, name, bufs, out_types, 7 * len(bufs), SLOT_ID if sequencer else None)


def _slot_sum(slots, name):
    _, rows, cols = slots.shape
    tm = _divisor_tile(rows, 512, 16)

    def kern(s_ref, o_ref):
        total = s_ref[0].astype(F32)
        for d in range(1, N_DEV):
            total = total + s_ref[d].astype(F32)
        o_ref[...] = total

    return pl.pallas_call(
        kern, name=name, grid=(rows // tm,), in_specs=[pl.BlockSpec((N_DEV, tm, cols), lambda i: (0, i, 0))],
        out_specs=pl.BlockSpec((tm, cols), lambda i: (i, 0)), out_shape=jax.ShapeDtypeStruct((rows, cols), F32),
        compiler_params=_params("parallel"),
    )(slots)


def _reduce_scatter(bufs, tag, sequencer):
    arrived = _slot_exchange(bufs, f"rs_exchange_{tag}", sequencer)
    return [_slot_sum(a, f"rs_sum_{tag}{i}") for i, a in enumerate(arrived)]


def _adamw(w, g, m, v, name):
    rows, cols = w.shape
    tm = _divisor_tile(rows, max(8, (512 * 1024) // cols // 8 * 8), 8)

    def kern(w_ref, g_ref, m_ref, v_ref, d_ref, nm_ref, nv_ref):
        g = g_ref[...]
        m = ADAM_B1 * m_ref[...] + (1.0 - ADAM_B1) * g
        v = ADAM_B2 * v_ref[...] + (1.0 - ADAM_B2) * (g * g)
        m_hat = m / (1.0 - ADAM_B1 ** ADAM_STEP)
        v_hat = v / (1.0 - ADAM_B2 ** ADAM_STEP)
        d_ref[...] = -ADAM_LR * (m_hat / (jnp.sqrt(v_hat) + ADAM_EPS) + ADAM_WD * w_ref[...])
        nm_ref[...] = m
        nv_ref[...] = v

    blk = pl.BlockSpec((tm, cols), lambda i: (i, 0))
    out = jax.ShapeDtypeStruct(w.shape, F32)
    return pl.pallas_call(
        kern, name=name, grid=(rows // tm,), in_specs=[blk] * 4, out_specs=[blk] * 3, out_shape=[out] * 3,
        compiler_params=_params("parallel"),
    )(w, g, m, v)


def _adamw_layers(w, g_layers, m, v, name):
    depth, rows, cols = w.shape
    tm = _divisor_tile(rows, max(8, (512 * 1024) // cols // 8 * 8), 8)

    def kern(w_ref, m_ref, v_ref, *refs):
        g_refs, (g_out, d_ref, nm_ref, nv_ref) = refs[:depth], refs[depth:]
        layer = pl.program_id(0)
        g = g_refs[0][...]
        for l in range(1, depth):
            g = jnp.where(layer == l, g_refs[l][...], g)
        m = ADAM_B1 * m_ref[0] + (1.0 - ADAM_B1) * g
        v = ADAM_B2 * v_ref[0] + (1.0 - ADAM_B2) * (g * g)
        m_hat = m / (1.0 - ADAM_B1 ** ADAM_STEP)
        v_hat = v / (1.0 - ADAM_B2 ** ADAM_STEP)
        g_out[0] = g
        d_ref[0] = -ADAM_LR * (m_hat / (jnp.sqrt(v_hat) + ADAM_EPS) + ADAM_WD * w_ref[0])
        nm_ref[0] = m
        nv_ref[0] = v

    blk = pl.BlockSpec((1, tm, cols), lambda l, i: (l, i, 0))
    out = jax.ShapeDtypeStruct(w.shape, F32)
    return pl.pallas_call(
        kern, name=name, grid=(depth, rows // tm),
        in_specs=[blk] * 3 + [pl.BlockSpec((tm, cols), lambda l, i: (i, 0))] * depth,
        out_specs=[blk] * 4, out_shape=[out] * 4, compiler_params=_params("parallel", "parallel"),
    )(w, m, v, *g_layers)


TRANSPOSED = ('ffn1_w_gate', 'ffn1_w_up', 'ffn2_w_gate', 'ffn2_w_up', 'w_in', 'mla_w_uq', 'mla_w_ukv', 'mla_w_o',
              'conv_w_out', 's5_w_out')
PIECES = ((('ffn1_w_gate',), ('ffn1_w_up',), ('ffn1_w_down',)),
          (('w_in', 'w_o'), ('mla_w_o', 'conv_w_out', 's5_w_out', 's5_w_glu'), ('mla_w_uq',), ('mla_w_ukv',)),
          (('ffn2_w_gate',), ('ffn2_w_up',), ('ffn2_w_down',)))
PIN_CUTS = (PIN_CQ, PIN_CKV, PIN_KR1, PIN_KR2, PIN_XBAR, PIN_BG, PIN_CG, PIN_U, PIN_GATES, PIN_GATES + D_MODEL,
            PIN_GATES + 2 * D_MODEL, PIN_END, D_IN_PAD)
PIN_PIECES = ((PIN_CQ, Q_RANK), (PIN_CKV, KV_RANK), (PIN_KR1, HALF_ROPE), (PIN_KR2, HALF_ROPE), (PIN_XBAR, 4 * MIX),
              (PIN_GATES, 3 * D_MODEL))


def _make_split(cuts):
    @jax.custom_vjp
    def split(t):
        return tuple(t[:, a:b] for a, b in zip(cuts[:-1], cuts[1:]))

    def fwd(t):
        return split(t), None

    def bwd(_, cts):
        return (jnp.concatenate(cts, axis=1),)

    split.defvjp(fwd, bwd)
    return split


def _make_projection(tag, cuts):
    cast = _make_rowwise(_f_cast, 1, 0, (BF16,), f"{tag}_cast")

    def forward(h, w):
        hb, = cast.run_fwd((h,), ())
        full = _matmul(hb, w, tb=True, name=f"{tag}_fwd")
        return tuple(full[:, a:b] for a, b in zip(cuts[:-1], cuts[1:])), (hb, w)

    @jax.custom_vjp
    def proj(h, w, wz):
        return forward(h, w)[0]

    def fwd(h, w, wz):
        return forward(h, w)

    def bwd(res, cts):
        hb, w = res
        d_full = jnp.concatenate([c.astype(BF16) for c in cts], axis=1)
        dw = _matmul(d_full, hb, ta=True, out_dtype=BF16, name=f"{tag}_dw")
        return _matmul(d_full, w, name=f"{tag}_dx"), jnp.zeros_like(w), dw

    proj.defvjp(fwd, bwd)
    return proj


def _travel_shape(name, shape):
    return (shape[2], shape[1]) if name in TRANSPOSED else (shape[1], shape[2])


def _pack_groups(tensors, layer, groups, dtype):
    def view(n):
        t = tensors[n][layer]
        return (t.T if n in TRANSPOSED else t).astype(dtype)
    return [jnp.concatenate([view(n) for n in grp], axis=0) for grp in groups]


def _unpack_groups(bufs, groups, shard_shapes):
    out = {}
    for buf, grp in zip(bufs, groups):
        at = 0
        for n in grp:
            r, _ = _travel_shape(n, shard_shapes[n])
            out[n] = buf[..., at:at + r, :]
            at += r
    return out


def _pack_rows(arrays):
    flat = jnp.concatenate([a.reshape(-1) for a in arrays])
    rows = -(-flat.shape[0] // PACK_COLS)
    rows = -(-rows // 8) * 8
    return jnp.pad(flat, (0, rows * PACK_COLS - flat.shape[0])).reshape(rows, PACK_COLS)


def _unpack_rows(buf, shapes):
    flat = buf.reshape(-1)
    out, at = [], 0
    for s in shapes:
        n = int(np.prod(s))
        out.append(flat[at:at + n].reshape(s))
        at += n
    return out


def _full_weight(t, axis):
    if axis == 1:
        return jnp.moveaxis(t, 0, 1).reshape(t.shape[1], N_DEV * t.shape[2], t.shape[3])
    return jnp.moveaxis(t, 0, 2).reshape(t.shape[1], t.shape[2], N_DEV * t.shape[3])


def _disassemble(d, groups, dtype):
    heads = lambda t: t.reshape(HEADS, -1, t.shape[1])
    full = {}
    for tag in ('ffn1', 'ffn2'):
        if f'{tag}_gate' in d:
            full.update({f'{tag}_w_gate': d[f'{tag}_gate'], f'{tag}_w_up': d[f'{tag}_up'], f'{tag}_w_down': d[f'{tag}_down']})
    if 'w_in' in d:
        w_in, uq, ukv = d['w_in'], d['w_uq'], d['w_ukv']
        full.update(
            w_in=jnp.concatenate([w_in[a:a + n] for a, n in PIN_PIECES], axis=0),
            mla_w_uq=jnp.concatenate([heads(uq[:HEADS * NOPE]), heads(uq[HEADS * NOPE:HEADS * NOPE + LANES]),
                                      heads(uq[HEADS * NOPE + LANES:])], axis=1).reshape(HEADS * QK_DIM, Q_RANK),
            mla_w_ukv=jnp.concatenate([heads(ukv[:HEADS * NOPE]), heads(ukv[HEADS * NOPE:])],
                                      axis=1).reshape(HEADS * (NOPE + V_DIM), KV_RANK),
            mla_w_o=d['mla_w_o'], conv_w_out=d['conv_w_out'], s5_w_glu=d['s5_w_glu'], s5_w_out=d['s5_w_out'], w_o=d['w_o'])
    return [jnp.concatenate([full[n].reshape(N_DEV, -1, full[n].shape[-1]).astype(dtype) for n in grp], axis=1)
            for grp in groups]


def _assemble(gathered, groups, shard_shapes):
    full = {n: t.reshape(N_DEV * t.shape[1], t.shape[2])
            for n, t in _unpack_groups(gathered, groups, shard_shapes).items()}
    out = {}
    for tag in ('ffn1', 'ffn2'):
        if f'{tag}_w_gate' in full:
            out.update({f'{tag}_gate': full[f'{tag}_w_gate'], f'{tag}_up': full[f'{tag}_w_up'],
                        f'{tag}_down': full[f'{tag}_w_down']})
    if 'w_in' not in full:
        return out
    w_in = full['w_in']
    cuts = np.cumsum((0,) + IN_SPLITS)
    cq, ckv, kr, xbar, bg, cg, u, gates = [w_in[a:b] for a, b in zip(cuts[:-1], cuts[1:])]
    pad = lambda t, n: jnp.pad(t, ((0, n - t.shape[0]), (0, 0)))
    w_in_packed = jnp.concatenate(
        [cq, ckv, pad(kr[:HALF_ROPE], LANES), pad(kr[HALF_ROPE:], LANES), xbar, bg, cg, u, gates,
         jnp.zeros((D_IN_PAD - PIN_END, D_MODEL), w_in.dtype)], axis=0)
    uq = full['mla_w_uq'].reshape(HEADS, QK_DIM, Q_RANK)
    w_uq = jnp.concatenate([uq[:, :NOPE].reshape(HEADS * NOPE, Q_RANK),
                            uq[:, NOPE:NOPE + HALF_ROPE].reshape(HEADS * HALF_ROPE, Q_RANK),
                            uq[:, NOPE + HALF_ROPE:].reshape(HEADS * HALF_ROPE, Q_RANK)], axis=0)
    ukv = full['mla_w_ukv'].reshape(HEADS, NOPE + V_DIM, KV_RANK)
    w_ukv = jnp.concatenate([ukv[:, :NOPE].reshape(HEADS * NOPE, KV_RANK),
                             ukv[:, NOPE:].reshape(HEADS * V_DIM, KV_RANK)], axis=0)
    out.update(w_in=w_in_packed, w_uq=w_uq, w_ukv=w_ukv, mla_w_o=full['mla_w_o'], conv_w_out=full['conv_w_out'],
               s5_w_glu=full['s5_w_glu'], s5_w_out=full['s5_w_out'], w_o=full['w_o'])
    return out


def _s5_discretize(a_re, a_im, log_dt, b_re, b_im, c_re, c_im):
    dt = jnp.exp(log_dt)[:, None]
    mag = jnp.exp(dt * a_re)
    ab_re, ab_im = mag * jnp.cos(dt * a_im), mag * jnp.sin(dt * a_im)
    den = a_re * a_re + a_im * a_im
    nr, ni = ab_re - 1.0, ab_im
    coef_re = (nr * a_re + ni * a_im) / den
    coef_im = (ni * a_re - nr * a_im) / den
    bb_re = coef_re[..., None] * b_re - coef_im[..., None] * b_im
    bb_im = coef_re[..., None] * b_im + coef_im[..., None] * b_re
    unit = jnp.arange(MIX)[:, None]
    chan = jnp.arange(2 * S5_CH)[None, :]
    pair = jnp.arange(2 * S5_STATE)[:, None]
    own = unit // S5_GROUP == (chan % S5_CH) // S5_STATE
    copy = jnp.logical_and(pair // S5_STATE == chan // S5_CH, pair % S5_STATE == chan % S5_STATE).astype(F32)
    flat_b = lambda bb: bb.transpose(0, 2, 1).reshape(MIX, S5_STATE)
    flat_c = lambda cc: cc.transpose(2, 0, 1).reshape(S5_STATE, MIX)
    b_small = jnp.concatenate([flat_b(bb_re), flat_b(bb_im)], axis=1)
    c_small = jnp.concatenate([flat_c(c_re), -flat_c(c_im)], axis=0)
    b_map = jnp.where(own, _make_mm_f32w("s5_spread_b")(b_small, copy), 0.0)
    c_map = jnp.where(own.T, _make_mm_f32w("s5_spread_c")(copy.T, c_small), 0.0)
    return ab_re.reshape(SCAN_ROWS, LANES), ab_im.reshape(SCAN_ROWS, LANES), b_map, c_map


def _rope_tables(length):
    inv_freq = ROPE_BASE ** (-jnp.arange(0, ROPE, 2, dtype=F32) / ROPE)
    ang = jnp.arange(length).astype(F32)[:, None] * inv_freq[None, :]
    return jnp.tile(jnp.cos(ang), (1, LANES // HALF_ROPE)), jnp.tile(jnp.sin(ang), (1, LANES // HALF_ROPE))


def _heads_first(t):
    return t.reshape(t.shape[0], HEADS, -1).transpose(1, 0, 2)


def _local_loss(diff, big, n_real):
    small, wz = diff['small'], diff['wz']
    h = diff['h0']
    length = h.shape[0]
    cos, sin = _rope_tables(length)
    row2 = lambda v: v.reshape(1, -1)
    for l in range(DEPTH):
        w, z = big[l], wz[l]
        p = {k: small[k][l] for k in small if k != 'meta'}
        ffn = lambda h, tag, ln: _make_ffn(tag)(
            h, *[w[f'{tag}_{k}'] for k in ('gate', 'up', 'down')], *[z[f'{tag}_{k}'] for k in ('gate', 'up', 'down')],
            row2(p[f'{ln}_g']), row2(p[f'{ln}_b']))
        h = ffn(h, "ffn1", "ln1")
        cq, ckv, kr1, kr2, xbar, bg, cg, u, gate_a, gate_b, gate_c, _ = _make_projection("w_in", PIN_CUTS)(
            h, w['w_in'], z['w_in'])
        qn, = _make_rowwise(_f_rms, 1, 1, (BF16,), "q_rms")(cq, row2(p['mla_q_norm_g']))
        kvn, = _make_rowwise(_f_rms, 1, 1, (BF16,), "kv_rms")(ckv, row2(p['mla_kv_norm_g']))
        q_nope, q1, q2 = _make_split((0, HEADS * NOPE, HEADS * NOPE + LANES, HEADS * NOPE + 2 * LANES))(
            _make_mm("w_uq", wt=True)(qn, w['w_uq'], z['w_uq']))
        k_nope, val = _make_split((0, HEADS * NOPE, HEADS * (NOPE + V_DIM)))(
            _make_mm("w_ukv", wt=True)(kvn, w['w_ukv'], z['w_ukv']))
        rope = _make_rowwise(_f_rope, 4, 0, (BF16, BF16), "rope", n_nodiff=2)
        q1, q2 = rope(q1, q2, cos, sin)
        k1, k2 = rope(kr1, kr2, cos, sin)
        hpad = jnp.zeros((HEADS, length, HEAD_PAD - QK_DIM), BF16)
        q3 = jnp.concatenate([_heads_first(q_nope.astype(BF16)), _heads_first(q1), _heads_first(q2), hpad], -1)
        shared = lambda t: jnp.broadcast_to(t[None, :, :HALF_ROPE], (HEADS, length, HALF_ROPE))
        k3 = jnp.concatenate([_heads_first(k_nope.astype(BF16)), shared(k1), shared(k2), hpad], -1)
        v3 = _heads_first(val.astype(BF16))
        o3 = _attention(q3, k3, v3)
        y_a = _make_mm("mla_w_o", wt=True)(o3.transpose(1, 0, 2).reshape(length, MIX), w['mla_w_o'], z['mla_w_o'])
        conv = _short_conv(xbar, bg, cg, p['conv_w_full'], row2(p['conv_b']))
        y_b = _make_mm("conv_w_out", wt=True)(conv, w['conv_w_out'], z['conv_w_out'])
        ar, ai, b_map, c_map = _s5_discretize(p['s5_a_re'], p['s5_a_im'], p['s5_log_dt'], p['s5_b_re'], p['s5_b_im'],
                                              p['s5_c_re'], p['s5_c_im'])
        bu = _make_bd_in("s5_b")(u, b_map)
        states = _s5_scan(ar, ai, bu.reshape(length, 2 * SCAN_ROWS, LANES)).reshape(length, 2 * S5_CH)
        y_ssm = _make_bd_out("s5_c")(states, c_map)
        zed, = _make_rowwise(_f_gelu_skip, 2, 1, (F32,), "s5_gelu")(y_ssm, u, row2(p['s5_d']))
        t = _make_mm("s5_w_glu")(zed, w['s5_w_glu'], z['s5_w_glu'])
        glu, = _make_rowwise(_f_glu, 2, 1, (BF16,), "s5_glu")(zed, t, row2(p['s5_b_glu']))
        y_c = _make_mm("s5_w_out", wt=True)(glu, w['s5_w_out'], z['s5_w_out'])
        mixed, = _make_rowwise(_f_merge, 6, 0, (BF16,), "merge")(gate_a, gate_b, gate_c, y_a, y_b, y_c)
        mix_out = _make_mm("w_o")(mixed, w['w_o'], z['w_o'])
        h, = _make_rowwise(_f_ln_full, 2, 2, (F32,), "mix_ln")(h, mix_out, row2(p['ln2_g']), row2(p['ln2_b']))
        h = ffn(h, "ffn2", "ln3")
    return _make_loss(n_real)(h, diff['target'])


def kernel(x, meta, ffn1_w_gate, ffn1_w_up, ffn1_w_down, ln1_g, ln1_b, w_in, mla_q_norm_g, mla_w_uq, mla_kv_norm_g, mla_w_ukv, mla_w_o, conv_w, conv_b, conv_w_out, s5_a_re, s5_a_im, s5_log_dt, s5_b_re, s5_b_im, s5_c_re, s5_c_im, s5_d, s5_w_glu, s5_b_glu, s5_w_out, w_o, ln2_g, ln2_b, ffn2_w_gate, ffn2_w_up, ffn2_w_down, ln3_g, ln3_b, loss_target, m_meta, m_ffn1_w_gate, m_ffn1_w_up, m_ffn1_w_down, m_ln1_g, m_ln1_b, m_w_in, m_mla_q_norm_g, m_mla_w_uq, m_mla_kv_norm_g, m_mla_w_ukv, m_mla_w_o, m_conv_w, m_conv_b, m_conv_w_out, m_s5_a_re, m_s5_a_im, m_s5_log_dt, m_s5_b_re, m_s5_b_im, m_s5_c_re, m_s5_c_im, m_s5_d, m_s5_w_glu, m_s5_b_glu, m_s5_w_out, m_w_o, m_ln2_g, m_ln2_b, m_ffn2_w_gate, m_ffn2_w_up, m_ffn2_w_down, m_ln3_g, m_ln3_b, v_meta, v_ffn1_w_gate, v_ffn1_w_up, v_ffn1_w_down, v_ln1_g, v_ln1_b, v_w_in, v_mla_q_norm_g, v_mla_w_uq, v_mla_kv_norm_g, v_mla_w_ukv, v_mla_w_o, v_conv_w, v_conv_b, v_conv_w_out, v_s5_a_re, v_s5_a_im, v_s5_log_dt, v_s5_b_re, v_s5_b_im, v_s5_c_re, v_s5_c_im, v_s5_d, v_s5_w_glu, v_s5_b_glu, v_s5_w_out, v_w_o, v_ln2_g, v_ln2_b, v_ffn2_w_gate, v_ffn2_w_up, v_ffn2_w_down, v_ln3_g, v_ln3_b):
    args = locals()
    w = {n: args[n] for n in WEIGHT_NAMES}
    m = {n: args["m_" + n] for n in WEIGHT_NAMES}
    v = {n: args["v_" + n] for n in WEIGHT_NAMES}
    me = 4 * lax.axis_index("x") + 2 * lax.axis_index("y") + lax.axis_index("c")
    seq = x.shape[1]
    n_real = N_META + seq
    length = -(-n_real // LANES) * LANES

    shard_shapes = {n: w[n].shape for n in BIG}
    small_shards = _all_gather([_pack_rows([w[n] for n in SMALL_SHARDED])], "gather_small",
                               sequencer=True)[0].reshape(N_DEV, -1)
    big = [{} for _ in range(DEPTH)]
    for l in range(DEPTH):
        for p, groups in enumerate(PIECES):
            packed = _pack_groups(w, l, groups, BF16)
            if (l, p) == (0, 1):
                packed = lax.optimization_barrier((gathered, packed))[1]
            gathered = _all_gather(packed, f"gather_weights_layer{l}_piece{p}", sequencer=True)
            big[l].update(_assemble(gathered, groups, shard_shapes))
    meta_full = _full_weight(small_shards[:, :meta.size].reshape(N_DEV, 1, *meta.shape), 2)[0]
    conv_w_full = _full_weight(small_shards[:, meta.size:meta.size + conv_w.size].reshape(N_DEV, *conv_w.shape), 2)

    small = {n: w[n] for n in SMALL_NAMES if n not in SMALL_SHARDED}
    small['conv_w_full'] = conv_w_full
    small['meta'] = meta_full
    wz = jax.tree.map(lambda t: jnp.zeros(t.shape, BF16), big)
    pad_rows = length - n_real

    def loss_fn(diff):
        h0 = jnp.concatenate([diff['small']['meta'], diff['x'], jnp.zeros((pad_rows, D_MODEL), F32)], axis=0)
        target = jnp.pad(loss_target[0], ((N_META, pad_rows), (0, 0)))
        return _local_loss(dict(h0=h0, small=diff['small'], wz=diff['wz'], target=target), big, n_real)

    loss_local, grads = jax.value_and_grad(loss_fn)(dict(x=x[0], small=small, wz=wz))

    small_names = [n for n in SMALL_NAMES if n not in SMALL_SHARDED] + ['conv_w_full', 'meta']
    small_flat = _pack_rows([loss_local.reshape(1)] + [grads['small'][n] for n in small_names])
    rows_each = -(-small_flat.shape[0] // (8 * N_DEV)) * 8
    small_flat = jnp.pad(small_flat, ((0, rows_each * N_DEV - small_flat.shape[0]), (0, 0)))
    layer_sums = [{} for _ in range(DEPTH)]
    for l in reversed(range(DEPTH)):
        for p, groups in reversed(list(enumerate(PIECES))):
            last = l == 0 and p == 0
            extra = [small_flat.reshape(N_DEV, rows_each, PACK_COLS)] if last else []
            sums = _reduce_scatter(_disassemble(grads['wz'][l], groups, BF16) + extra, f"grads_layer{l}_piece{p}_",
                                   sequencer=True)
            layer_sums[l].update(_unpack_groups(sums[:len(groups)], groups, shard_shapes))
            if last:
                small_sum = sums[-1]
    small_all, = _all_gather([small_sum], "gather_small_grads")
    loss, *small_sums = _unpack_rows(small_all, [()] + [grads['small'][n].shape for n in small_names])
    small_grads = dict(zip(small_names, small_sums))
    g = {}
    for name in SMALL_NAMES:
        if name == 'meta':
            g[name] = lax.dynamic_slice_in_dim(small_grads['meta'], me * meta.shape[1], meta.shape[1], axis=1)
        elif name == 'conv_w':
            g[name] = lax.dynamic_slice_in_dim(small_grads['conv_w_full'], me * conv_w.shape[2], conv_w.shape[2], axis=2)
        else:
            g[name] = small_grads[name]

    delta, new_m, new_v = {}, {}, {}
    for name in BIG:
        per_layer = [s[name].T if name in TRANSPOSED else s[name] for s in layer_sums]
        g[name], delta[name], new_m[name], new_v[name] = _adamw_layers(w[name], per_layer, m[name], v[name], f"adamw_{name}")
    shapes = [w[n].shape for n in SMALL_NAMES]
    d, nm, nv = _adamw(*[_pack_rows([t[n] for n in SMALL_NAMES]) for t in (w, g, m, v)], "adamw_small")
    for out, buf in ((delta, d), (new_m, nm), (new_v, nv)):
        out.update(zip(SMALL_NAMES, _unpack_rows(buf, shapes)))

    return (loss, grads['x'][None], *[g[n] for n in WEIGHT_NAMES], *[delta[n] for n in WEIGHT_NAMES],
            *[new_m[n] for n in WEIGHT_NAMES], *[new_v[n] for n in WEIGHT_NAMES])
```

```python
import jax
import jax.numpy as jnp
import numpy as np
from jax import lax
from jax.experimental import pallas as pl
from jax.experimental.pallas import tpu as pltpu
from jax.experimental.pallas import tpu_sc as plsc

F32 = jnp.float32
BF16 = jnp.bfloat16

D_MODEL = 1024
DEPTH = 2
N_META = 16
HEADS = 8
V_DIM = 64
NOPE = 64
ROPE = 32
HALF_ROPE = ROPE // 2
QK_DIM = NOPE + ROPE
Q_RANK = 384
KV_RANK = 256
MIX = 512
CONV_K = 3
S5_GROUPS = 32
S5_GROUP = 16
S5_STATE = 64
S5_CH = S5_GROUPS * S5_STATE
D_FF = 2816
ALPHA = (2.0 * DEPTH) ** 0.25
LN_EPS = 1e-5
RMS_EPS = 1e-6
ROPE_BASE = 10000.0
IN_SPLITS = (Q_RANK, KV_RANK, ROPE, MIX, MIX, MIX, MIX, 3 * D_MODEL)
D_IN = sum(IN_SPLITS)
ADAM_LR, ADAM_B1, ADAM_B2, ADAM_EPS, ADAM_WD, ADAM_STEP = 0.001, 0.9, 0.999, 1e-08, 0.01, 10

N_DEV = 8
AXES = ("x", "y", "c")
LANES = 128
PACK_COLS = 1024
HEAD_PAD = 128
VMEM_LIMIT = 48 * 1024 * 1024

PIN_CQ, PIN_CKV, PIN_KR1, PIN_KR2, PIN_XBAR, PIN_BG, PIN_CG, PIN_U, PIN_GATES, PIN_END = (
    0, 384, 640, 768, 896, 1408, 1920, 2432, 2944, 6016)
D_IN_PAD = 6144

WEIGHT_NAMES = ['meta', 'ffn1_w_gate', 'ffn1_w_up', 'ffn1_w_down', 'ln1_g', 'ln1_b', 'w_in', 'mla_q_norm_g', 'mla_w_uq',
                'mla_kv_norm_g', 'mla_w_ukv', 'mla_w_o', 'conv_w', 'conv_b', 'conv_w_out', 's5_a_re', 's5_a_im',
                's5_log_dt', 's5_b_re', 's5_b_im', 's5_c_re', 's5_c_im', 's5_d', 's5_w_glu', 's5_b_glu', 's5_w_out',
                'w_o', 'ln2_g', 'ln2_b', 'ffn2_w_gate', 'ffn2_w_up', 'ffn2_w_down', 'ln3_g', 'ln3_b']
BIG = {'ffn1_w_gate': 2, 'ffn1_w_up': 2, 'ffn1_w_down': 1, 'w_in': 2, 'mla_w_uq': 2, 'mla_w_ukv': 2, 'mla_w_o': 2,
       'conv_w_out': 2, 's5_w_glu': 1, 's5_w_out': 2, 'w_o': 1, 'ffn2_w_gate': 2, 'ffn2_w_up': 2, 'ffn2_w_down': 1}
SMALL_SHARDED = ('meta', 'conv_w')
SMALL_NAMES = [n for n in WEIGHT_NAMES if n not in BIG]


def _divisor_tile(n, limit, mult):
    best = None
    for t in range(mult, min(n, limit) + 1, mult):
        if n % t == 0:
            best = t
    return best if best is not None else n


def _params(*sem):
    return pltpu.CompilerParams(dimension_semantics=sem, vmem_limit_bytes=VMEM_LIMIT)


def _matmul(a, b, *, ta=False, tb=False, out_dtype=F32, add=None, name):
    m, k = (a.shape[1], a.shape[0]) if ta else a.shape
    n = b.shape[0] if tb else b.shape[1]
    assert (b.shape[1] if tb else b.shape[0]) == k, (a.shape, b.shape, ta, tb)
    both_bf16 = a.dtype == BF16 and b.dtype == BF16
    deep = both_bf16 and not ta and 1408 < k <= D_IN_PAD
    tm = (_divisor_tile(m, 1408, LANES) if ta else
          _divisor_tile(m, (1088 if k <= D_FF else 544) if deep else (1088 if a.dtype != BF16 else 2176), 16))
    tn = _divisor_tile(n, 512, LANES)
    tk = k if deep else _divisor_tile(k, 2176 if (ta and both_bf16) else 1408, 16 if ta else LANES)
    nk = k // tk
    dims = (((0 if ta else 1,), (1 if tb else 0,)), ((), ()))

    in_place = jnp.dtype(out_dtype) == jnp.dtype(F32)

    def kern(a_ref, b_ref, *rest):
        add_ref = rest[0] if add is not None else None
        o_ref, *scratch = rest[1:] if add is not None else rest
        kk = pl.program_id(2)
        part = lax.dot_general(a_ref[...].astype(BF16), b_ref[...].astype(BF16), dims, preferred_element_type=F32)
        first = lambda: part if add_ref is None else part + add_ref[...].astype(F32)
        if nk == 1:
            o_ref[...] = first().astype(o_ref.dtype)
            return
        acc_ref = o_ref if in_place else scratch[0]

        @pl.when(kk == 0)
        def _():
            acc_ref[...] = first()

        @pl.when(kk > 0)
        def _():
            acc_ref[...] += part

        if not in_place:
            @pl.when(kk == nk - 1)
            def _():
                o_ref[...] = acc_ref[...].astype(o_ref.dtype)

    a_spec = pl.BlockSpec((tk, tm), lambda i, j, kk: (kk, i)) if ta else pl.BlockSpec((tm, tk), lambda i, j, kk: (i, kk))
    b_spec = pl.BlockSpec((tn, tk), lambda i, j, kk: (j, kk)) if tb else pl.BlockSpec((tk, tn), lambda i, j, kk: (kk, j))
    o_spec = pl.BlockSpec((tm, tn), lambda i, j, kk: (i, j))
    return pl.pallas_call(
        kern, name=name, grid=(m // tm, n // tn, nk),
        in_specs=[a_spec, b_spec] + ([o_spec] if add is not None else []), out_specs=o_spec,
        out_shape=jax.ShapeDtypeStruct((m, n), out_dtype),
        scratch_shapes=[] if (nk == 1 or in_place) else [pltpu.VMEM((tm, tn), F32)],
        compiler_params=_params("parallel", "parallel", "arbitrary"),
    )(a, b, *([add] if add is not None else []))


def _make_mm(name, wt=False):
    @jax.custom_vjp
    def mm(x, w, wz):
        return _matmul(x, w, tb=wt, name=name + "_fwd")

    def fwd(x, w, wz):
        return _matmul(x, w, tb=wt, name=name + "_fwd"), (x, w)

    def bwd(res, dy):
        x, w = res
        wz_dtype = BF16
        dx = _matmul(dy, w, tb=not wt, out_dtype=x.dtype, name=name + "_dx")
        dw = (_matmul(dy, x, ta=True, out_dtype=wz_dtype, name=name + "_dw") if wt
              else _matmul(x, dy, ta=True, out_dtype=wz_dtype, name=name + "_dw"))
        return dx, jnp.zeros_like(w), dw

    mm.defvjp(fwd, bwd)
    return mm


def _make_mm_f32w(name):
    @jax.custom_vjp
    def mm(x, w):
        return _matmul(x, w, name=name + "_fwd")

    def fwd(x, w):
        return _matmul(x, w, name=name + "_fwd"), (x, w)

    def bwd(res, dy):
        x, w = res
        return (_matmul(dy, w, tb=True, out_dtype=x.dtype, name=name + "_dx"),
                _matmul(x, dy, ta=True, name=name + "_dw"))

    mm.defvjp(fwd, bwd)
    return mm


BD_BLOCKS = 4
BD_PARTS = 2


def _bd_call(a, b, out_shape, a_blk, b_blk, o_blk, a_idx, b_idx, o_idx, dims, reduce_parts, name):
    def kern(a_ref, b_ref, o_ref):
        part = lax.dot_general(a_ref[...].astype(BF16), b_ref[...].astype(BF16), dims, preferred_element_type=F32)
        if not reduce_parts:
            o_ref[...] = part.astype(o_ref.dtype)
            return

        @pl.when(pl.program_id(1) == 0)
        def _():
            o_ref[...] = part

        @pl.when(pl.program_id(1) > 0)
        def _():
            o_ref[...] += part

    return pl.pallas_call(
        kern, name=name, grid=(BD_BLOCKS, BD_PARTS),
        in_specs=[pl.BlockSpec(a_blk, a_idx), pl.BlockSpec(b_blk, b_idx)], out_specs=pl.BlockSpec(o_blk, o_idx),
        out_shape=jax.ShapeDtypeStruct(out_shape, F32),
        compiler_params=_params("parallel", "arbitrary" if reduce_parts else "parallel"),
    )(a, b)


def _make_bd_in(name):
    wide = lambda j, p: (0, BD_BLOCKS * p + j)
    thin = lambda j, p: (0, j)
    diag = lambda j, p: (j, BD_BLOCKS * p + j)
    nn, nt, tn = (((1,), (0,)), ((), ())), (((1,), (1,)), ((), ())), (((0,), (0,)), ((), ()))

    def run(x, w):
        length, cols = x.shape[0], w.shape[1] // (BD_BLOCKS * BD_PARTS)
        return _bd_call(x, w, (length, w.shape[1]), (length, LANES), (LANES, cols), (length, cols), thin, diag, wide,
                        nn, False, name + "_fwd")

    @jax.custom_vjp
    def mm(x, w):
        return run(x, w)

    def fwd(x, w):
        return run(x, w), (x, w)

    def bwd(res, dy):
        x, w = res
        length, cols = x.shape[0], w.shape[1] // (BD_BLOCKS * BD_PARTS)
        dx = _bd_call(dy, w, x.shape, (length, cols), (LANES, cols), (length, LANES), wide, diag, thin, nt, True,
                      name + "_dx")
        dw = _bd_call(x, dy, w.shape, (length, LANES), (length, cols), (LANES, cols), thin, wide, diag, tn, False,
                      name + "_dw")
        return dx, dw

    mm.defvjp(fwd, bwd)
    return mm


def _make_bd_out(name):
    wide = lambda j, p: (0, BD_BLOCKS * p + j)
    thin = lambda j, p: (0, j)
    diag = lambda j, p: (BD_BLOCKS * p + j, j)
    nn, nt, tn = (((1,), (0,)), ((), ())), (((1,), (1,)), ((), ())), (((0,), (0,)), ((), ()))

    def run(x, w):
        length, cols = x.shape[0], w.shape[0] // (BD_BLOCKS * BD_PARTS)
        return _bd_call(x, w, (length, w.shape[1]), (length, cols), (cols, LANES), (length, LANES), wide, diag, thin,
                        nn, True, name + "_fwd")

    @jax.custom_vjp
    def mm(x, w):
        return run(x, w)

    def fwd(x, w):
        return run(x, w), (x, w)

    def bwd(res, dy):
        x, w = res
        length, cols = x.shape[0], w.shape[0] // (BD_BLOCKS * BD_PARTS)
        dx = _bd_call(dy, w, x.shape, (length, LANES), (cols, LANES), (length, cols), thin, diag, wide, nt, False,
                      name + "_dx")
        dw = _bd_call(x, dy, w.shape, (length, cols), (length, LANES), (cols, LANES), wide, thin, diag, tn, False,
                      name + "_dw")
        return dx, dw

    mm.defvjp(fwd, bwd)
    return mm


def _row_tile(rows, widths):
    limit = max(16, (6 * 1024 * 1024 // 4) // max(1, sum(widths)))
    return _divisor_tile(rows, limit, 16)


def _make_rowwise(f, n_rows, n_pars, out_dtypes, name, n_nodiff=0, grad_dtypes=None):
    n_out = len(out_dtypes)
    n_diff = n_rows - n_nodiff

    def run_fwd(rows, pars):
        length = rows[0].shape[0]
        shapes = jax.eval_shape(lambda *a: f(*a), *[jax.ShapeDtypeStruct((16, r.shape[1]), F32) for r in rows],
                                *[jax.ShapeDtypeStruct(p.shape, F32) for p in pars])
        widths = [s.shape[1] for s in shapes]
        tm = _row_tile(length, [r.shape[1] for r in rows] + widths)

        def kern(*refs):
            ins = [r[...].astype(F32) for r in refs[:n_rows + n_pars]]
            outs = f(*ins)
            for o_ref, o in zip(refs[n_rows + n_pars:], outs):
                o_ref[...] = o.astype(o_ref.dtype)

        return pl.pallas_call(
            kern, name=name + "_fwd", grid=(length // tm,),
            in_specs=[pl.BlockSpec((tm, r.shape[1]), lambda i: (i, 0)) for r in rows]
            + [pl.BlockSpec(p.shape, lambda i: (0, 0)) for p in pars],
            out_specs=[pl.BlockSpec((tm, w), lambda i: (i, 0)) for w in widths],
            out_shape=[jax.ShapeDtypeStruct((length, w), dt) for w, dt in zip(widths, out_dtypes)],
            compiler_params=_params("parallel"),
        )(*rows, *pars)

    def run_bwd(rows, pars, cts):
        length = rows[0].shape[0]
        tm = _row_tile(length, [r.shape[1] for r in rows] * 2 + [c.shape[1] for c in cts] * 2)

        def kern(*refs):
            ins = [r[...].astype(F32) for r in refs[:n_rows + n_pars]]
            ct = [r[...].astype(F32) for r in refs[n_rows + n_pars:n_rows + n_pars + n_out]]
            out_refs = refs[n_rows + n_pars + n_out:]
            nodiff = ins[n_diff:n_rows]
            _, vjp = jax.vjp(lambda *a: f(*a[:n_diff], *nodiff, *a[n_diff:]), *ins[:n_diff], *ins[n_rows:])
            grads = vjp(tuple(ct))
            for o_ref, g in zip(out_refs[:n_diff], grads[:n_diff]):
                o_ref[...] = g.astype(o_ref.dtype)
            first = pl.program_id(0) == 0
            for o_ref, g in zip(out_refs[n_diff:], grads[n_diff:]):
                @pl.when(first)
                def _(o_ref=o_ref, g=g):
                    o_ref[...] = g

                @pl.when(jnp.logical_not(first))
                def _(o_ref=o_ref, g=g):
                    o_ref[...] += g

        return pl.pallas_call(
            kern, name=name + "_bwd", grid=(length // tm,),
            in_specs=[pl.BlockSpec((tm, r.shape[1]), lambda i: (i, 0)) for r in rows]
            + [pl.BlockSpec(p.shape, lambda i: (0, 0)) for p in pars]
            + [pl.BlockSpec((tm, c.shape[1]), lambda i: (i, 0)) for c in cts],
            out_specs=[pl.BlockSpec((tm, r.shape[1]), lambda i: (i, 0)) for r in rows[:n_diff]]
            + [pl.BlockSpec(p.shape, lambda i: (0, 0)) for p in pars],
            out_shape=[jax.ShapeDtypeStruct(r.shape, r.dtype if grad_dtypes is None else grad_dtypes[i])
                       for i, r in enumerate(rows[:n_diff])]
            + [jax.ShapeDtypeStruct(p.shape, F32) for p in pars],
            compiler_params=_params("arbitrary"),
        )(*rows, *pars, *cts)

    @jax.custom_vjp
    def op(*args):
        return tuple(run_fwd(args[:n_rows], args[n_rows:]))

    def fwd(*args):
        return tuple(run_fwd(args[:n_rows], args[n_rows:])), args

    def bwd(args, cts):
        grads = run_bwd(args[:n_rows], args[n_rows:], cts)
        zeros = [jnp.zeros_like(r) for r in args[n_diff:n_rows]]
        return (*grads[:n_diff], *zeros, *grads[n_diff:])

    op.defvjp(fwd, bwd)
    op.run_fwd, op.run_bwd = run_fwd, run_bwd
    return op


def _layer_norm(z, g, b):
    mu = jnp.mean(z, axis=-1, keepdims=True)
    d = z - mu
    var = jnp.mean(d * d, axis=-1, keepdims=True)
    return d * lax.rsqrt(var + LN_EPS) * g + b


def _f_ln_half(h, f, g, b):
    return (_layer_norm(ALPHA * h + 0.5 * f, g, b),)


def _f_ln_full(h, f, g, b):
    return (_layer_norm(ALPHA * h + f, g, b),)


def _f_rms(x, g):
    return (x * lax.rsqrt(jnp.mean(x * x, axis=-1, keepdims=True) + RMS_EPS) * g,)


def _f_rope(x1, x2, cos, sin):
    return x1 * cos - x2 * sin, x2 * cos + x1 * sin


def _f_gelu_skip(y, u, d):
    return (jax.nn.gelu(y + d * u),)


def _f_glu(z, t, b):
    return (z * jax.nn.sigmoid(t + b),)


def _f_merge(ga, gb, gc, ya, yb, yc):
    return (jax.nn.sigmoid(ga) * ya + jax.nn.sigmoid(gb) * yb + jax.nn.sigmoid(gc) * yc,)


def _f_swiglu(gate, up):
    return (jax.nn.silu(gate) * up,)


def _swiglu_bwd(gate, up, dact, name):
    length, width = gate.shape
    tm = _divisor_tile(length, 128, 16)

    def kern(g_ref, u_ref, d_ref, dg_ref, du_ref):
        g, d = g_ref[...], d_ref[...].astype(F32)
        s = jax.nn.sigmoid(g)
        dg_ref[...] = (d * u_ref[...] * (s * (1.0 + g * (1.0 - s)))).astype(dg_ref.dtype)
        du_ref[...] = (d * (g * s)).astype(du_ref.dtype)

    blk = pl.BlockSpec((tm, width), lambda i: (i, 0))
    out = jax.ShapeDtypeStruct(gate.shape, BF16)
    return pl.pallas_call(kern, name=name, grid=(length // tm,), in_specs=[blk] * 3, out_specs=[blk] * 2,
                          out_shape=[out, out], compiler_params=_params("parallel"))(gate, up, dact)


def _f_cast(x):
    return (x,)


def _make_ffn(tag):
    ln = _make_rowwise(_f_ln_half, 2, 2, (F32,), f"{tag}_ln", grad_dtypes=(F32, BF16))
    cast = _make_rowwise(_f_cast, 1, 0, (BF16,), f"{tag}_cast")
    swiglu = _make_rowwise(_f_swiglu, 2, 0, (BF16,), f"{tag}_swiglu", grad_dtypes=(BF16, BF16))

    def forward(h, w_gate, w_up, w_down, g, b):
        hb, = cast.run_fwd((h,), ())
        gate = _matmul(hb, w_gate, tb=True, name=f"{tag}_gate_fwd")
        up = _matmul(hb, w_up, tb=True, name=f"{tag}_up_fwd")
        act, = swiglu.run_fwd((gate, up), ())
        f = _matmul(act, w_down, name=f"{tag}_down_fwd")
        y, = ln.run_fwd((h, f), (g, b))
        return y, (h, hb, gate, up, act, f, w_gate, w_up, w_down, g, b)

    @jax.custom_vjp
    def block(h, w_gate, w_up, w_down, z_gate, z_up, z_down, g, b):
        return forward(h, w_gate, w_up, w_down, g, b)[0]

    def fwd(h, w_gate, w_up, w_down, z_gate, z_up, z_down, g, b):
        return forward(h, w_gate, w_up, w_down, g, b)

    def bwd(res, dy):
        h, hb, gate, up, act, f, w_gate, w_up, w_down, g, b = res
        dh, df, dg, db = ln.run_bwd((h, f), (g, b), (dy,))
        dact = _matmul(df, w_down, tb=True, out_dtype=BF16, name=f"{tag}_down_dx")
        dw_down = _matmul(act, df, ta=True, out_dtype=BF16, name=f"{tag}_down_dw")
        dgate, dup = _swiglu_bwd(gate, up, dact, f"{tag}_swiglu_bwd")
        dw_gate = _matmul(dgate, hb, ta=True, out_dtype=BF16, name=f"{tag}_gate_dw")
        dw_up = _matmul(dup, hb, ta=True, out_dtype=BF16, name=f"{tag}_up_dw")
        dh = _matmul(dgate, w_gate, add=dh, name=f"{tag}_gate_dx")
        dh = _matmul(dup, w_up, add=dh, name=f"{tag}_up_dx")
        zero = jnp.zeros_like
        return dh, zero(w_gate), zero(w_up), zero(w_down), dw_gate, dw_up, dw_down, dg, db

    block.defvjp(fwd, bwd)
    return block


def _attn_scores(q, k, q_block, tq):
    length = k.shape[0]
    s = lax.dot_general(q, k, (((1,), (1,)), ((), ())), preferred_element_type=F32) * (QK_DIM ** -0.5)
    row = q_block * tq + lax.broadcasted_iota(jnp.int32, (tq, length), 0)
    col = lax.broadcasted_iota(jnp.int32, (tq, length), 1)
    s = jnp.where(col <= row, s, -1e30)
    e = jnp.exp(s - jnp.max(s, axis=1, keepdims=True))
    return e * (1.0 / jnp.sum(e, axis=1, keepdims=True))


ATTN_SEGMENTS = 4


def _attn_tiles(length):
    seg = length // ATTN_SEGMENTS
    return seg, _divisor_tile(seg, 272, 16)


def _attn_fwd(q3, k3, v3):
    heads, length, _ = q3.shape
    seg, tq = _attn_tiles(length)
    outs = []
    for s in range(ATTN_SEGMENTS):
        kmax, base = (s + 1) * seg, s * (seg // tq)

        def kern(q_ref, k_ref, v_ref, o_ref, base=base):
            p = _attn_scores(q_ref[0], k_ref[0], base + pl.program_id(1), tq)
            o_ref[0] = jnp.dot(p.astype(BF16), v_ref[0], preferred_element_type=F32).astype(o_ref.dtype)

        outs.append(pl.pallas_call(
            kern, name=f"attn_fwd_seg{s}", grid=(heads, seg // tq),
            in_specs=[pl.BlockSpec((1, tq, HEAD_PAD), lambda h, i, base=base: (h, base + i, 0)),
                      pl.BlockSpec((1, kmax, HEAD_PAD), lambda h, i: (h, 0, 0)),
                      pl.BlockSpec((1, kmax, V_DIM), lambda h, i: (h, 0, 0))],
            out_specs=pl.BlockSpec((1, tq, V_DIM), lambda h, i: (h, i, 0)),
            out_shape=jax.ShapeDtypeStruct((heads, seg, V_DIM), F32),
            compiler_params=_params("parallel", "parallel"),
        )(q3, k3, v3))
    return jnp.concatenate(outs, axis=1)


def _attn_bwd(q3, k3, v3, do3):
    heads, length, _ = q3.shape
    seg, tq = _attn_tiles(length)
    dk = jnp.zeros((heads, length, HEAD_PAD), F32)
    dv = jnp.zeros((heads, length, V_DIM), F32)
    dqs = [None] * ATTN_SEGMENTS
    for s in reversed(range(ATTN_SEGMENTS)):
        kmax, base = (s + 1) * seg, s * (seg // tq)

        def kern(q_ref, k_ref, v_ref, do_ref, dk_in, dv_in, dq_ref, dk_ref, dv_ref, base=base):
            i = pl.program_id(1)
            q, k, v, do = q_ref[0], k_ref[0], v_ref[0], do_ref[0].astype(BF16)
            p = _attn_scores(q, k, base + i, tq)
            dp = lax.dot_general(do, v, (((1,), (1,)), ((), ())), preferred_element_type=F32)
            ds = (p * (dp - jnp.sum(p * dp, axis=1, keepdims=True)) * (QK_DIM ** -0.5)).astype(BF16)
            dq_ref[0] = jnp.dot(ds, k, preferred_element_type=F32)
            dk_part = lax.dot_general(ds, q, (((0,), (0,)), ((), ())), preferred_element_type=F32)
            dv_part = lax.dot_general(p.astype(BF16), do, (((0,), (0,)), ((), ())), preferred_element_type=F32)

            @pl.when(i == 0)
            def _():
                dk_ref[0] = dk_in[0] + dk_part
                dv_ref[0] = dv_in[0] + dv_part

            @pl.when(i > 0)
            def _():
                dk_ref[0] += dk_part
                dv_ref[0] += dv_part

        q_blk = pl.BlockSpec((1, tq, HEAD_PAD), lambda h, i, base=base: (h, base + i, 0))
        k_blk = pl.BlockSpec((1, kmax, HEAD_PAD), lambda h, i: (h, 0, 0))
        v_blk = pl.BlockSpec((1, kmax, V_DIM), lambda h, i: (h, 0, 0))
        dqs[s], dk, dv = pl.pallas_call(
            kern, name=f"attn_bwd_seg{s}", grid=(heads, seg // tq),
            in_specs=[q_blk, k_blk, v_blk, pl.BlockSpec((1, tq, V_DIM), lambda h, i, base=base: (h, base + i, 0)),
                      k_blk, v_blk],
            out_specs=[pl.BlockSpec((1, tq, HEAD_PAD), lambda h, i: (h, i, 0)), k_blk, v_blk],
            out_shape=[jax.ShapeDtypeStruct((heads, seg, HEAD_PAD), F32), jax.ShapeDtypeStruct(dk.shape, F32),
                       jax.ShapeDtypeStruct(dv.shape, F32)],
            input_output_aliases={4: 1, 5: 2}, compiler_params=_params("parallel", "arbitrary"),
        )(q3, k3, v3, do3, dk, dv)
    return jnp.concatenate(dqs, axis=1), dk, dv


@jax.custom_vjp
def _attention(q3, k3, v3):
    return _attn_fwd(q3, k3, v3)


def _attention_fwd(q3, k3, v3):
    return _attn_fwd(q3, k3, v3), (q3, k3, v3)


def _attention_bwd(res, do3):
    q3, k3, v3 = res
    dq, dk, dv = _attn_bwd(q3, k3, v3, do3)
    return dq.astype(q3.dtype), dk.astype(k3.dtype), dv.astype(v3.dtype)


_attention.defvjp(_attention_fwd, _attention_bwd)


def _conv_terms(x, c, w_ref, cb):
    u = c * x
    row = lax.broadcasted_iota(jnp.int32, u.shape, 0)
    u1 = jnp.where(row >= 1, pltpu.roll(u, 1, 0), 0.0)
    u2 = jnp.where(row >= 2, pltpu.roll(u, 2, 0), 0.0)
    y = cb + w_ref[0:1, :] * u2 + w_ref[1:2, :] * u1 + w_ref[2:3, :] * u
    return u, u1, u2, y


def _conv_specs(length):
    col = pl.BlockSpec((length, LANES), lambda j: (0, j))
    return col, pl.BlockSpec((CONV_K, LANES), lambda j: (0, j)), pl.BlockSpec((1, LANES), lambda j: (0, j))


def _conv_fwd(x, b, c, w, cb):
    length = x.shape[0]

    def kern(x_ref, b_ref, c_ref, w_ref, cb_ref, o_ref):
        _, _, _, y = _conv_terms(x_ref[...], c_ref[...], w_ref, cb_ref[...])
        o_ref[...] = b_ref[...] * y

    col, wspec, bspec = _conv_specs(length)
    return pl.pallas_call(
        kern, name="conv_fwd", grid=(MIX // LANES,), in_specs=[col, col, col, wspec, bspec], out_specs=col,
        out_shape=jax.ShapeDtypeStruct((length, MIX), F32), compiler_params=_params("parallel"),
    )(x, b, c, w, cb)


def _conv_bwd(x, b, c, w, cb, do):
    length = x.shape[0]

    def kern(x_ref, b_ref, c_ref, w_ref, cb_ref, do_ref, dx_ref, db_ref, dc_ref, dw_ref, dcb_ref):
        x, c, do = x_ref[...], c_ref[...], do_ref[...]
        u, u1, u2, y = _conv_terms(x, c, w_ref, cb_ref[...])
        db_ref[...] = do * y
        dy = do * b_ref[...]
        row = lax.broadcasted_iota(jnp.int32, dy.shape, 0)
        dy1 = jnp.where(row < length - 1, pltpu.roll(dy, length - 1, 0), 0.0)
        dy2 = jnp.where(row < length - 2, pltpu.roll(dy, length - 2, 0), 0.0)
        du = w_ref[2:3, :] * dy + w_ref[1:2, :] * dy1 + w_ref[0:1, :] * dy2
        dx_ref[...] = du * c
        dc_ref[...] = du * x
        dw_ref[0:1, :] = jnp.sum(dy * u2, axis=0, keepdims=True)
        dw_ref[1:2, :] = jnp.sum(dy * u1, axis=0, keepdims=True)
        dw_ref[2:3, :] = jnp.sum(dy * u, axis=0, keepdims=True)
        dcb_ref[...] = jnp.sum(dy, axis=0, keepdims=True)

    col, wspec, bspec = _conv_specs(length)
    big = jax.ShapeDtypeStruct((length, MIX), F32)
    return pl.pallas_call(
        kern, name="conv_bwd", grid=(MIX // LANES,), in_specs=[col, col, col, wspec, bspec, col],
        out_specs=[col, col, col, wspec, bspec],
        out_shape=[big, big, big, jax.ShapeDtypeStruct((CONV_K, MIX), F32), jax.ShapeDtypeStruct((1, MIX), F32)],
        compiler_params=_params("parallel"),
    )(x, b, c, w, cb, do)


@jax.custom_vjp
def _short_conv(x, b, c, w, cb):
    return _conv_fwd(x, b, c, w, cb)


def _short_conv_fwd(x, b, c, w, cb):
    return _conv_fwd(x, b, c, w, cb), (x, b, c, w, cb)


def _short_conv_bwd(res, do):
    return tuple(_conv_bwd(*res, do))


_short_conv.defvjp(_short_conv_fwd, _short_conv_bwd)


SCAN_ROWS = S5_CH // LANES
SCAN_TC = 136


def _scan_fwd(ar, ai, b):
    length = b.shape[0]
    tc = _divisor_tile(length, SCAN_TC, 8)

    def kern(ar_ref, ai_ref, b_ref, x_ref, sr, si):
        @pl.when(pl.program_id(0) == 0)
        def _():
            sr[...] = jnp.zeros_like(sr)
            si[...] = jnp.zeros_like(si)

        a_re, a_im = ar_ref[...], ai_ref[...]

        def body(t, carry):
            xr, xi = carry
            nr = a_re * xr - a_im * xi + b_ref[t, 0:SCAN_ROWS, :]
            ni = a_re * xi + a_im * xr + b_ref[t, SCAN_ROWS:2 * SCAN_ROWS, :]
            x_ref[t, 0:SCAN_ROWS, :] = nr
            x_ref[t, SCAN_ROWS:2 * SCAN_ROWS, :] = ni
            return nr, ni

        xr, xi = lax.fori_loop(0, tc, body, (sr[...], si[...]), unroll=4)
        sr[...] = xr
        si[...] = xi

    par = pl.BlockSpec((SCAN_ROWS, LANES), lambda i: (0, 0))
    blk = pl.BlockSpec((tc, 2 * SCAN_ROWS, LANES), lambda i: (i, 0, 0))
    return pl.pallas_call(
        kern, name="s5_scan_fwd", grid=(length // tc,), in_specs=[par, par, blk], out_specs=blk,
        out_shape=jax.ShapeDtypeStruct(b.shape, F32), scratch_shapes=[pltpu.VMEM((SCAN_ROWS, LANES), F32)] * 2,
        compiler_params=_params("arbitrary"),
    )(ar, ai, b)


def _scan_bwd(ar, ai, x, dx):
    length = x.shape[0]
    tc = _divisor_tile(length, SCAN_TC, 8)
    n_blk = length // tc
    re, im = slice(0, SCAN_ROWS), slice(SCAN_ROWS, 2 * SCAN_ROWS)

    def kern(ar_ref, ai_ref, x_ref, dx_ref, db_ref, dar_ref, dai_ref, lr_s, li_s):
        @pl.when(pl.program_id(0) == 0)
        def _():
            lr_s[...] = jnp.zeros_like(lr_s)
            li_s[...] = jnp.zeros_like(li_s)
            dar_ref[...] = jnp.zeros_like(dar_ref)
            dai_ref[...] = jnp.zeros_like(dai_ref)

        a_re, a_im = ar_ref[...], ai_ref[...]

        def body(j, carry):
            t = tc - 1 - j
            lr, li, gr, gi = carry
            x_re, x_im = x_ref[t, re, :], x_ref[t, im, :]
            gr = gr + (lr * x_re + li * x_im)
            gi = gi + (li * x_re - lr * x_im)
            nlr = dx_ref[t, re, :] + (a_re * lr + a_im * li)
            nli = dx_ref[t, im, :] + (a_re * li - a_im * lr)
            db_ref[t, re, :] = nlr
            db_ref[t, im, :] = nli
            return nlr, nli, gr, gi

        lr, li, gr, gi = lax.fori_loop(0, tc, body, (lr_s[...], li_s[...], dar_ref[...], dai_ref[...]), unroll=4)
        lr_s[...] = lr
        li_s[...] = li
        dar_ref[...] = gr
        dai_ref[...] = gi

    par = pl.BlockSpec((SCAN_ROWS, LANES), lambda i: (0, 0))
    blk = pl.BlockSpec((tc, 2 * SCAN_ROWS, LANES), lambda i: (n_blk - 1 - i, 0, 0))
    pout = jax.ShapeDtypeStruct((SCAN_ROWS, LANES), F32)
    return pl.pallas_call(
        kern, name="s5_scan_bwd", grid=(n_blk,), in_specs=[par, par, blk, blk],
        out_specs=[blk, par, par], out_shape=[jax.ShapeDtypeStruct(x.shape, F32), pout, pout],
        scratch_shapes=[pltpu.VMEM((SCAN_ROWS, LANES), F32)] * 2, compiler_params=_params("arbitrary"),
    )(ar, ai, x, dx)


@jax.custom_vjp
def _s5_scan(ar, ai, b):
    return _scan_fwd(ar, ai, b)


def _s5_scan_fwd(ar, ai, b):
    x = _scan_fwd(ar, ai, b)
    return x, (ar, ai, x)


def _s5_scan_bwd(res, dx):
    ar, ai, x = res
    db, dar, dai = _scan_bwd(ar, ai, x, dx)
    return dar, dai, db


_s5_scan.defvjp(_s5_scan_fwd, _s5_scan_bwd)


def _loss_call(y, target, n_real):
    length = y.shape[0]
    tm = _divisor_tile(length, 544, 16)

    def kern(y_ref, t_ref, loss_ref, dy_ref):
        i = pl.program_id(0)
        row = i * tm + lax.broadcasted_iota(jnp.int32, (tm, 1), 0)
        keep = jnp.logical_and(row >= N_META, row < n_real)
        err = jnp.where(keep, y_ref[...] - t_ref[...], 0.0)
        dy_ref[...] = err * (1.0 / D_MODEL)
        part = 0.5 * jnp.sum(jnp.mean(err * err, axis=-1, keepdims=True), axis=0, keepdims=True)

        @pl.when(i == 0)
        def _():
            loss_ref[...] = jnp.zeros_like(loss_ref)

        loss_ref[...] += part

    blk = pl.BlockSpec((tm, D_MODEL), lambda i: (i, 0))
    return pl.pallas_call(
        kern, name="loss_head", grid=(length // tm,), in_specs=[blk, blk],
        out_specs=[pl.BlockSpec((8, LANES), lambda i: (0, 0)), blk],
        out_shape=[jax.ShapeDtypeStruct((8, LANES), F32), jax.ShapeDtypeStruct(y.shape, F32)],
        compiler_params=_params("arbitrary"),
    )(y, target)


def _make_loss(n_real):
    @jax.custom_vjp
    def loss(y, target):
        return _loss_call(y, target, n_real)[0][0, 0]

    def fwd(y, target):
        total, dy = _loss_call(y, target, n_real)
        return total[0, 0], dy

    def bwd(dy, ct):
        return dy * ct, jnp.zeros_like(dy)

    loss.defvjp(fwd, bwd)
    return loss


HBM_SPEC = pl.BlockSpec(memory_space=pl.ANY)
MESH_ID = pl.DeviceIdType.MESH


SC_MESH = dict(axis_name="sequencer", num_cores=1)
GATHER_ID, SLOT_ID = 1, 2


def _handshake(peers):
    barrier = pltpu.get_barrier_semaphore()
    for peer in peers:
        pl.semaphore_signal(barrier, inc=1, device_id=peer, device_id_type=MESH_ID)
    pl.semaphore_wait(barrier, len(peers))


def _exchange_call(body, name, ins, out_types, n_sems, sequencer_id):
    n_in, n_out = len(ins), len(out_types)
    sems = [pltpu.SemaphoreType.DMA((n_sems,)), pltpu.SemaphoreType.DMA((n_sems,)), pltpu.SemaphoreType.DMA((n_in,))]
    if sequencer_id is None:
        def on_core(*refs):
            body(lambda peers: None, refs[:n_in], refs[n_in:n_in + n_out], *refs[n_in + n_out:])

        return pl.pallas_call(on_core, name=name, out_shape=out_types, in_specs=[HBM_SPEC] * n_in,
                              out_specs=[HBM_SPEC] * n_out, scratch_shapes=sems)(*ins)

    def on_sequencer(*refs):
        body(_handshake, refs[:n_in], refs[n_in:n_in + n_out], *refs[n_in + n_out:])

    return pl.kernel(on_sequencer, name=name, out_type=out_types, mesh=plsc.ScalarSubcoreMesh(**SC_MESH),
                     scratch_types=sems, compiler_params=pltpu.CompilerParams(collective_id=sequencer_id))(*ins)


def _all_gather(shards, name, sequencer=False):
    n = len(shards)

    def body(handshake, x_refs, out_refs, send_sems, recv_sems, local_sems):
        x, y, c = lax.axis_index("x"), lax.axis_index("y"), lax.axis_index("c")
        me, sibling = (x, y, c), (x, y, 1 - c)
        chips = [(1 - x, y), (x, 1 - y), (1 - x, 1 - y)]
        handshake([sibling] + [(*chip, c) for chip in chips])

        def copy(b, k, block, to, from_input=False):
            px, py, pc = block
            slot = out_refs[b].at[4 * px + 2 * py + pc]
            return pltpu.make_async_remote_copy(
                src_ref=x_refs[b] if from_input else slot, dst_ref=slot,
                send_sem=send_sems.at[7 * b + k], recv_sem=recv_sems.at[7 * b + k], device_id=to, device_id_type=MESH_ID)

        mine = [pltpu.make_async_copy(x_refs[b], out_refs[b].at[4 * x + 2 * y + c], local_sems.at[b]) for b in range(n)]
        for cp in mine:
            cp.start()
        first = []
        for b in range(n):
            first.append(copy(b, 0, me, sibling, from_input=True))
            first += [copy(b, 1 + j, me, (*chip, c), from_input=True) for j, chip in enumerate(chips)]
        for cp in first:
            cp.start()
        passed = []
        for j, chip in enumerate(chips):
            for b in range(n):
                copy(b, 1 + j, (*chip, c), me).wait_recv()
                passed.append(copy(b, 4 + j, (*chip, c), sibling))
                passed[-1].start()
        for b in range(n):
            copy(b, 0, sibling, me).wait_recv()
            for j, chip in enumerate(chips):
                copy(b, 4 + j, (*chip, 1 - c), me).wait_recv()
        for cp in first + passed:
            cp.wait_send()
        for cp in mine:
            cp.wait()

    out_types = [jax.ShapeDtypeStruct((N_DEV, *s.shape), s.dtype) for s in shards]
    return _exchange_call(body, name, shards, out_types, 7 * n, GATHER_ID if sequencer else None)


def _slot_exchange(bufs, name, sequencer=False):
    def body(handshake, ins, outs, send_sems, recv_sems, local_sems):
        x, y, c = lax.axis_index("x"), lax.axis_index("y"), lax.axis_index("c")
        me = 4 * x + 2 * y + c
        flips = [(dx, dy, dc) for dx in (0, 1) for dy in (0, 1) for dc in (0, 1)][1:]
        peers = [(1 - x if dx else x, 1 - y if dy else y, 1 - c if dc else c) for dx, dy, dc in flips]
        handshake(peers)
        own = [pltpu.make_async_copy(src.at[me], dst.at[me], local_sems.at[b]) for b, (src, dst) in enumerate(zip(ins, outs))]
        copies = []
        for b, (src, dst) in enumerate(zip(ins, outs)):
            for k, (px, py, pc) in enumerate(peers):
                copies.append(pltpu.make_async_remote_copy(
                    src_ref=src.at[4 * px + 2 * py + pc], dst_ref=dst.at[me],
                    send_sem=send_sems.at[7 * b + k], recv_sem=recv_sems.at[7 * b + k],
                    device_id=(px, py, pc), device_id_type=MESH_ID))
        for cp in own + copies:
            cp.start()
        for cp in copies + own:
            cp.wait()

    out_types = [jax.ShapeDtypeStruct(b.shape, b.dtype) for b in bufs]
    return _exchange_call(body, name, bufs, out_types, 7 * len(bufs), SLOT_ID if sequencer else None)


def _slot_sum(slots, name):
    _, rows, cols = slots.shape
    tm = _divisor_tile(rows, 512, 16)

    def kern(s_ref, o_ref):
        total = s_ref[0].astype(F32)
        for d in range(1, N_DEV):
            total = total + s_ref[d].astype(F32)
        o_ref[...] = total

    return pl.pallas_call(
        kern, name=name, grid=(rows // tm,), in_specs=[pl.BlockSpec((N_DEV, tm, cols), lambda i: (0, i, 0))],
        out_specs=pl.BlockSpec((tm, cols), lambda i: (i, 0)), out_shape=jax.ShapeDtypeStruct((rows, cols), F32),
        compiler_params=_params("parallel"),
    )(slots)


def _reduce_scatter(bufs, tag, sequencer):
    arrived = _slot_exchange(bufs, f"rs_exchange_{tag}", sequencer)
    return [_slot_sum(a, f"rs_sum_{tag}{i}") for i, a in enumerate(arrived)]


def _adamw(w, g, m, v, name):
    rows, cols = w.shape
    tm = _divisor_tile(rows, max(8, (512 * 1024) // cols // 8 * 8), 8)

    def kern(w_ref, g_ref, m_ref, v_ref, d_ref, nm_ref, nv_ref):
        g = g_ref[...]
        m = ADAM_B1 * m_ref[...] + (1.0 - ADAM_B1) * g
        v = ADAM_B2 * v_ref[...] + (1.0 - ADAM_B2) * (g * g)
        m_hat = m / (1.0 - ADAM_B1 ** ADAM_STEP)
        v_hat = v / (1.0 - ADAM_B2 ** ADAM_STEP)
        d_ref[...] = -ADAM_LR * (m_hat / (jnp.sqrt(v_hat) + ADAM_EPS) + ADAM_WD * w_ref[...])
        nm_ref[...] = m
        nv_ref[...] = v

    blk = pl.BlockSpec((tm, cols), lambda i: (i, 0))
    out = jax.ShapeDtypeStruct(w.shape, F32)
    return pl.pallas_call(
        kern, name=name, grid=(rows // tm,), in_specs=[blk] * 4, out_specs=[blk] * 3, out_shape=[out] * 3,
        compiler_params=_params("parallel"),
    )(w, g, m, v)


def _adamw_layers(w, g_layers, m, v, name):
    depth, rows, cols = w.shape
    tm = _divisor_tile(rows, max(8, (512 * 1024) // cols // 8 * 8), 8)

    def kern(w_ref, m_ref, v_ref, *refs):
        g_refs, (g_out, d_ref, nm_ref, nv_ref) = refs[:depth], refs[depth:]
        layer = pl.program_id(0)
        g = g_refs[0][...]
        for l in range(1, depth):
            g = jnp.where(layer == l, g_refs[l][...], g)
        m = ADAM_B1 * m_ref[0] + (1.0 - ADAM_B1) * g
        v = ADAM_B2 * v_ref[0] + (1.0 - ADAM_B2) * (g * g)
        m_hat = m / (1.0 - ADAM_B1 ** ADAM_STEP)
        v_hat = v / (1.0 - ADAM_B2 ** ADAM_STEP)
        g_out[0] = g
        d_ref[0] = -ADAM_LR * (m_hat / (jnp.sqrt(v_hat) + ADAM_EPS) + ADAM_WD * w_ref[0])
        nm_ref[0] = m
        nv_ref[0] = v

    blk = pl.BlockSpec((1, tm, cols), lambda l, i: (l, i, 0))
    out = jax.ShapeDtypeStruct(w.shape, F32)
    return pl.pallas_call(
        kern, name=name, grid=(depth, rows // tm),
        in_specs=[blk] * 3 + [pl.BlockSpec((tm, cols), lambda l, i: (i, 0))] * depth,
        out_specs=[blk] * 4, out_shape=[out] * 4, compiler_params=_params("parallel", "parallel"),
    )(w, m, v, *g_layers)


TRANSPOSED = ('ffn1_w_gate', 'ffn1_w_up', 'ffn2_w_gate', 'ffn2_w_up', 'w_in', 'mla_w_uq', 'mla_w_ukv', 'mla_w_o',
              'conv_w_out', 's5_w_out')
PIECES = ((('ffn1_w_gate',), ('ffn1_w_up',), ('ffn1_w_down',)),
          (('w_in', 'w_o'), ('mla_w_o', 'conv_w_out', 's5_w_out', 's5_w_glu'), ('mla_w_uq',), ('mla_w_ukv',)),
          (('ffn2_w_gate',), ('ffn2_w_up',), ('ffn2_w_down',)))
PIN_CUTS = (PIN_CQ, PIN_CKV, PIN_KR1, PIN_KR2, PIN_XBAR, PIN_BG, PIN_CG, PIN_U, PIN_GATES, PIN_GATES + D_MODEL,
            PIN_GATES + 2 * D_MODEL, PIN_END, D_IN_PAD)
PIN_PIECES = ((PIN_CQ, Q_RANK), (PIN_CKV, KV_RANK), (PIN_KR1, HALF_ROPE), (PIN_KR2, HALF_ROPE), (PIN_XBAR, 4 * MIX),
              (PIN_GATES, 3 * D_MODEL))


def _make_split(cuts):
    @jax.custom_vjp
    def split(t):
        return tuple(t[:, a:b] for a, b in zip(cuts[:-1], cuts[1:]))

    def fwd(t):
        return split(t), None

    def bwd(_, cts):
        return (jnp.concatenate(cts, axis=1),)

    split.defvjp(fwd, bwd)
    return split


def _make_projection(tag, cuts):
    cast = _make_rowwise(_f_cast, 1, 0, (BF16,), f"{tag}_cast")

    def forward(h, w):
        hb, = cast.run_fwd((h,), ())
        full = _matmul(hb, w, tb=True, name=f"{tag}_fwd")
        return tuple(full[:, a:b] for a, b in zip(cuts[:-1], cuts[1:])), (hb, w)

    @jax.custom_vjp
    def proj(h, w, wz):
        return forward(h, w)[0]

    def fwd(h, w, wz):
        return forward(h, w)

    def bwd(res, cts):
        hb, w = res
        d_full = jnp.concatenate([c.astype(BF16) for c in cts], axis=1)
        dw = _matmul(d_full, hb, ta=True, out_dtype=BF16, name=f"{tag}_dw")
        return _matmul(d_full, w, name=f"{tag}_dx"), jnp.zeros_like(w), dw

    proj.defvjp(fwd, bwd)
    return proj


def _travel_shape(name, shape):
    return (shape[2], shape[1]) if name in TRANSPOSED else (shape[1], shape[2])


def _pack_groups(tensors, layer, groups, dtype):
    def view(n):
        t = tensors[n][layer]
        return (t.T if n in TRANSPOSED else t).astype(dtype)
    return [jnp.concatenate([view(n) for n in grp], axis=0) for grp in groups]


def _unpack_groups(bufs, groups, shard_shapes):
    out = {}
    for buf, grp in zip(bufs, groups):
        at = 0
        for n in grp:
            r, _ = _travel_shape(n, shard_shapes[n])
            out[n] = buf[..., at:at + r, :]
            at += r
    return out


def _pack_rows(arrays):
    flat = jnp.concatenate([a.reshape(-1) for a in arrays])
    rows = -(-flat.shape[0] // PACK_COLS)
    rows = -(-rows // 8) * 8
    return jnp.pad(flat, (0, rows * PACK_COLS - flat.shape[0])).reshape(rows, PACK_COLS)


def _unpack_rows(buf, shapes):
    flat = buf.reshape(-1)
    out, at = [], 0
    for s in shapes:
        n = int(np.prod(s))
        out.append(flat[at:at + n].reshape(s))
        at += n
    return out


def _full_weight(t, axis):
    if axis == 1:
        return jnp.moveaxis(t, 0, 1).reshape(t.shape[1], N_DEV * t.shape[2], t.shape[3])
    return jnp.moveaxis(t, 0, 2).reshape(t.shape[1], t.shape[2], N_DEV * t.shape[3])


def _disassemble(d, groups, dtype):
    heads = lambda t: t.reshape(HEADS, -1, t.shape[1])
    full = {}
    for tag in ('ffn1', 'ffn2'):
        if f'{tag}_gate' in d:
            full.update({f'{tag}_w_gate': d[f'{tag}_gate'], f'{tag}_w_up': d[f'{tag}_up'], f'{tag}_w_down': d[f'{tag}_down']})
    if 'w_in' in d:
        w_in, uq, ukv = d['w_in'], d['w_uq'], d['w_ukv']
        full.update(
            w_in=jnp.concatenate([w_in[a:a + n] for a, n in PIN_PIECES], axis=0),
            mla_w_uq=jnp.concatenate([heads(uq[:HEADS * NOPE]), heads(uq[HEADS * NOPE:HEADS * NOPE + LANES]),
                                      heads(uq[HEADS * NOPE + LANES:])], axis=1).reshape(HEADS * QK_DIM, Q_RANK),
            mla_w_ukv=jnp.concatenate([heads(ukv[:HEADS * NOPE]), heads(ukv[HEADS * NOPE:])],
                                      axis=1).reshape(HEADS * (NOPE + V_DIM), KV_RANK),
            mla_w_o=d['mla_w_o'], conv_w_out=d['conv_w_out'], s5_w_glu=d['s5_w_glu'], s5_w_out=d['s5_w_out'], w_o=d['w_o'])
    return [jnp.concatenate([full[n].reshape(N_DEV, -1, full[n].shape[-1]).astype(dtype) for n in grp], axis=1)
            for grp in groups]


def _assemble(gathered, groups, shard_shapes):
    full = {n: t.reshape(N_DEV * t.shape[1], t.shape[2])
            for n, t in _unpack_groups(gathered, groups, shard_shapes).items()}
    out = {}
    for tag in ('ffn1', 'ffn2'):
        if f'{tag}_w_gate' in full:
            out.update({f'{tag}_gate': full[f'{tag}_w_gate'], f'{tag}_up': full[f'{tag}_w_up'],
                        f'{tag}_down': full[f'{tag}_w_down']})
    if 'w_in' not in full:
        return out
    w_in = full['w_in']
    cuts = np.cumsum((0,) + IN_SPLITS)
    cq, ckv, kr, xbar, bg, cg, u, gates = [w_in[a:b] for a, b in zip(cuts[:-1], cuts[1:])]
    pad = lambda t, n: jnp.pad(t, ((0, n - t.shape[0]), (0, 0)))
    w_in_packed = jnp.concatenate(
        [cq, ckv, pad(kr[:HALF_ROPE], LANES), pad(kr[HALF_ROPE:], LANES), xbar, bg, cg, u, gates,
         jnp.zeros((D_IN_PAD - PIN_END, D_MODEL), w_in.dtype)], axis=0)
    uq = full['mla_w_uq'].reshape(HEADS, QK_DIM, Q_RANK)
    w_uq = jnp.concatenate([uq[:, :NOPE].reshape(HEADS * NOPE, Q_RANK),
                            uq[:, NOPE:NOPE + HALF_ROPE].reshape(HEADS * HALF_ROPE, Q_RANK),
                            uq[:, NOPE + HALF_ROPE:].reshape(HEADS * HALF_ROPE, Q_RANK)], axis=0)
    ukv = full['mla_w_ukv'].reshape(HEADS, NOPE + V_DIM, KV_RANK)
    w_ukv = jnp.concatenate([ukv[:, :NOPE].reshape(HEADS * NOPE, KV_RANK),
                             ukv[:, NOPE:].reshape(HEADS * V_DIM, KV_RANK)], axis=0)
    out.update(w_in=w_in_packed, w_uq=w_uq, w_ukv=w_ukv, mla_w_o=full['mla_w_o'], conv_w_out=full['conv_w_out'],
               s5_w_glu=full['s5_w_glu'], s5_w_out=full['s5_w_out'], w_o=full['w_o'])
    return out


def _s5_discretize(a_re, a_im, log_dt, b_re, b_im, c_re, c_im):
    dt = jnp.exp(log_dt)[:, None]
    mag = jnp.exp(dt * a_re)
    ab_re, ab_im = mag * jnp.cos(dt * a_im), mag * jnp.sin(dt * a_im)
    den = a_re * a_re + a_im * a_im
    nr, ni = ab_re - 1.0, ab_im
    coef_re = (nr * a_re + ni * a_im) / den
    coef_im = (ni * a_re - nr * a_im) / den
    bb_re = coef_re[..., None] * b_re - coef_im[..., None] * b_im
    bb_im = coef_re[..., None] * b_im + coef_im[..., None] * b_re
    unit = jnp.arange(MIX)[:, None]
    chan = jnp.arange(2 * S5_CH)[None, :]
    pair = jnp.arange(2 * S5_STATE)[:, None]
    own = unit // S5_GROUP == (chan % S5_CH) // S5_STATE
    copy = jnp.logical_and(pair // S5_STATE == chan // S5_CH, pair % S5_STATE == chan % S5_STATE).astype(F32)
    flat_b = lambda bb: bb.transpose(0, 2, 1).reshape(MIX, S5_STATE)
    flat_c = lambda cc: cc.transpose(2, 0, 1).reshape(S5_STATE, MIX)
    b_small = jnp.concatenate([flat_b(bb_re), flat_b(bb_im)], axis=1)
    c_small = jnp.concatenate([flat_c(c_re), -flat_c(c_im)], axis=0)
    b_map = jnp.where(own, _make_mm_f32w("s5_spread_b")(b_small, copy), 0.0)
    c_map = jnp.where(own.T, _make_mm_f32w("s5_spread_c")(copy.T, c_small), 0.0)
    return ab_re.reshape(SCAN_ROWS, LANES), ab_im.reshape(SCAN_ROWS, LANES), b_map, c_map


def _rope_tables(length):
    inv_freq = ROPE_BASE ** (-jnp.arange(0, ROPE, 2, dtype=F32) / ROPE)
    ang = jnp.arange(length).astype(F32)[:, None] * inv_freq[None, :]
    return jnp.tile(jnp.cos(ang), (1, LANES // HALF_ROPE)), jnp.tile(jnp.sin(ang), (1, LANES // HALF_ROPE))


def _heads_first(t):
    return t.reshape(t.shape[0], HEADS, -1).transpose(1, 0, 2)


def _local_loss(diff, big, n_real):
    small, wz = diff['small'], diff['wz']
    h = diff['h0']
    length = h.shape[0]
    cos, sin = _rope_tables(length)
    row2 = lambda v: v.reshape(1, -1)
    for l in range(DEPTH):
        w, z = big[l], wz[l]
        p = {k: small[k][l] for k in small if k != 'meta'}
        ffn = lambda h, tag, ln: _make_ffn(tag)(
            h, *[w[f'{tag}_{k}'] for k in ('gate', 'up', 'down')], *[z[f'{tag}_{k}'] for k in ('gate', 'up', 'down')],
            row2(p[f'{ln}_g']), row2(p[f'{ln}_b']))
        h = ffn(h, "ffn1", "ln1")
        cq, ckv, kr1, kr2, xbar, bg, cg, u, gate_a, gate_b, gate_c, _ = _make_projection("w_in", PIN_CUTS)(
            h, w['w_in'], z['w_in'])
        qn, = _make_rowwise(_f_rms, 1, 1, (BF16,), "q_rms")(cq, row2(p['mla_q_norm_g']))
        kvn, = _make_rowwise(_f_rms, 1, 1, (BF16,), "kv_rms")(ckv, row2(p['mla_kv_norm_g']))
        q_nope, q1, q2 = _make_split((0, HEADS * NOPE, HEADS * NOPE + LANES, HEADS * NOPE + 2 * LANES))(
            _make_mm("w_uq", wt=True)(qn, w['w_uq'], z['w_uq']))
        k_nope, val = _make_split((0, HEADS * NOPE, HEADS * (NOPE + V_DIM)))(
            _make_mm("w_ukv", wt=True)(kvn, w['w_ukv'], z['w_ukv']))
        rope = _make_rowwise(_f_rope, 4, 0, (BF16, BF16), "rope", n_nodiff=2)
        q1, q2 = rope(q1, q2, cos, sin)
        k1, k2 = rope(kr1, kr2, cos, sin)
        hpad = jnp.zeros((HEADS, length, HEAD_PAD - QK_DIM), BF16)
        q3 = jnp.concatenate([_heads_first(q_nope.astype(BF16)), _heads_first(q1), _heads_first(q2), hpad], -1)
        shared = lambda t: jnp.broadcast_to(t[None, :, :HALF_ROPE], (HEADS, length, HALF_ROPE))
        k3 = jnp.concatenate([_heads_first(k_nope.astype(BF16)), shared(k1), shared(k2), hpad], -1)
        v3 = _heads_first(val.astype(BF16))
        o3 = _attention(q3, k3, v3)
        y_a = _make_mm("mla_w_o", wt=True)(o3.transpose(1, 0, 2).reshape(length, MIX), w['mla_w_o'], z['mla_w_o'])
        conv = _short_conv(xbar, bg, cg, p['conv_w_full'], row2(p['conv_b']))
        y_b = _make_mm("conv_w_out", wt=True)(conv, w['conv_w_out'], z['conv_w_out'])
        ar, ai, b_map, c_map = _s5_discretize(p['s5_a_re'], p['s5_a_im'], p['s5_log_dt'], p['s5_b_re'], p['s5_b_im'],
                                              p['s5_c_re'], p['s5_c_im'])
        bu = _make_bd_in("s5_b")(u, b_map)
        states = _s5_scan(ar, ai, bu.reshape(length, 2 * SCAN_ROWS, LANES)).reshape(length, 2 * S5_CH)
        y_ssm = _make_bd_out("s5_c")(states, c_map)
        zed, = _make_rowwise(_f_gelu_skip, 2, 1, (F32,), "s5_gelu")(y_ssm, u, row2(p['s5_d']))
        t = _make_mm("s5_w_glu")(zed, w['s5_w_glu'], z['s5_w_glu'])
        glu, = _make_rowwise(_f_glu, 2, 1, (BF16,), "s5_glu")(zed, t, row2(p['s5_b_glu']))
        y_c = _make_mm("s5_w_out", wt=True)(glu, w['s5_w_out'], z['s5_w_out'])
        mixed, = _make_rowwise(_f_merge, 6, 0, (BF16,), "merge")(gate_a, gate_b, gate_c, y_a, y_b, y_c)
        mix_out = _make_mm("w_o")(mixed, w['w_o'], z['w_o'])
        h, = _make_rowwise(_f_ln_full, 2, 2, (F32,), "mix_ln")(h, mix_out, row2(p['ln2_g']), row2(p['ln2_b']))
        h = ffn(h, "ffn2", "ln3")
    return _make_loss(n_real)(h, diff['target'])


def kernel(x, meta, ffn1_w_gate, ffn1_w_up, ffn1_w_down, ln1_g, ln1_b, w_in, mla_q_norm_g, mla_w_uq, mla_kv_norm_g, mla_w_ukv, mla_w_o, conv_w, conv_b, conv_w_out, s5_a_re, s5_a_im, s5_log_dt, s5_b_re, s5_b_im, s5_c_re, s5_c_im, s5_d, s5_w_glu, s5_b_glu, s5_w_out, w_o, ln2_g, ln2_b, ffn2_w_gate, ffn2_w_up, ffn2_w_down, ln3_g, ln3_b, loss_target, m_meta, m_ffn1_w_gate, m_ffn1_w_up, m_ffn1_w_down, m_ln1_g, m_ln1_b, m_w_in, m_mla_q_norm_g, m_mla_w_uq, m_mla_kv_norm_g, m_mla_w_ukv, m_mla_w_o, m_conv_w, m_conv_b, m_conv_w_out, m_s5_a_re, m_s5_a_im, m_s5_log_dt, m_s5_b_re, m_s5_b_im, m_s5_c_re, m_s5_c_im, m_s5_d, m_s5_w_glu, m_s5_b_glu, m_s5_w_out, m_w_o, m_ln2_g, m_ln2_b, m_ffn2_w_gate, m_ffn2_w_up, m_ffn2_w_down, m_ln3_g, m_ln3_b, v_meta, v_ffn1_w_gate, v_ffn1_w_up, v_ffn1_w_down, v_ln1_g, v_ln1_b, v_w_in, v_mla_q_norm_g, v_mla_w_uq, v_mla_kv_norm_g, v_mla_w_ukv, v_mla_w_o, v_conv_w, v_conv_b, v_conv_w_out, v_s5_a_re, v_s5_a_im, v_s5_log_dt, v_s5_b_re, v_s5_b_im, v_s5_c_re, v_s5_c_im, v_s5_d, v_s5_w_glu, v_s5_b_glu, v_s5_w_out, v_w_o, v_ln2_g, v_ln2_b, v_ffn2_w_gate, v_ffn2_w_up, v_ffn2_w_down, v_ln3_g, v_ln3_b):
    args = locals()
    w = {n: args[n] for n in WEIGHT_NAMES}
    m = {n: args["m_" + n] for n in WEIGHT_NAMES}
    v = {n: args["v_" + n] for n in WEIGHT_NAMES}
    me = 4 * lax.axis_index("x") + 2 * lax.axis_index("y") + lax.axis_index("c")
    seq = x.shape[1]
    n_real = N_META + seq
    length = -(-n_real // LANES) * LANES

    shard_shapes = {n: w[n].shape for n in BIG}
    small_shards = _all_gather([_pack_rows([w[n] for n in SMALL_SHARDED])], "gather_small",
                               sequencer=True)[0].reshape(N_DEV, -1)
    big = [{} for _ in range(DEPTH)]
    for l in range(DEPTH):
        for p, groups in enumerate(PIECES):
            packed = _pack_groups(w, l, groups, BF16)
            if (l, p) == (0, 1):
                packed = lax.optimization_barrier((gathered, packed))[1]
            gathered = _all_gather(packed, f"gather_weights_layer{l}_piece{p}", sequencer=True)
            big[l].update(_assemble(gathered, groups, shard_shapes))
    meta_full = _full_weight(small_shards[:, :meta.size].reshape(N_DEV, 1, *meta.shape), 2)[0]
    conv_w_full = _full_weight(small_shards[:, meta.size:meta.size + conv_w.size].reshape(N_DEV, *conv_w.shape), 2)

    small = {n: w[n] for n in SMALL_NAMES if n not in SMALL_SHARDED}
    small['conv_w_full'] = conv_w_full
    small['meta'] = meta_full
    wz = jax.tree.map(lambda t: jnp.zeros(t.shape, BF16), big)
    pad_rows = length - n_real

    def loss_fn(diff):
        h0 = jnp.concatenate([diff['small']['meta'], diff['x'], jnp.zeros((pad_rows, D_MODEL), F32)], axis=0)
        target = jnp.pad(loss_target[0], ((N_META, pad_rows), (0, 0)))
        return _local_loss(dict(h0=h0, small=diff['small'], wz=diff['wz'], target=target), big, n_real)

    loss_local, grads = jax.value_and_grad(loss_fn)(dict(x=x[0], small=small, wz=wz))

    small_names = [n for n in SMALL_NAMES if n not in SMALL_SHARDED] + ['conv_w_full', 'meta']
    small_flat = _pack_rows([loss_local.reshape(1)] + [grads['small'][n] for n in small_names])
    rows_each = -(-small_flat.shape[0] // (8 * N_DEV)) * 8
    small_flat = jnp.pad(small_flat, ((0, rows_each * N_DEV - small_flat.shape[0]), (0, 0)))
    layer_sums = [{} for _ in range(DEPTH)]
    for l in reversed(range(DEPTH)):
        for p, groups in reversed(list(enumerate(PIECES))):
            last = l == 0 and p == 0
            extra = [small_flat.reshape(N_DEV, rows_each, PACK_COLS)] if last else []
            sums = _reduce_scatter(_disassemble(grads['wz'][l], groups, BF16) + extra, f"grads_layer{l}_piece{p}_",
                                   sequencer=True)
            layer_sums[l].update(_unpack_groups(sums[:len(groups)], groups, shard_shapes))
            if last:
                small_sum = sums[-1]
    small_all, = _all_gather([small_sum], "gather_small_grads")
    loss, *small_sums = _unpack_rows(small_all, [()] + [grads['small'][n].shape for n in small_names])
    small_grads = dict(zip(small_names, small_sums))
    g = {}
    for name in SMALL_NAMES:
        if name == 'meta':
            g[name] = lax.dynamic_slice_in_dim(small_grads['meta'], me * meta.shape[1], meta.shape[1], axis=1)
        elif name == 'conv_w':
            g[name] = lax.dynamic_slice_in_dim(small_grads['conv_w_full'], me * conv_w.shape[2], conv_w.shape[2], axis=2)
        else:
            g[name] = small_grads[name]

    delta, new_m, new_v = {}, {}, {}
    for name in BIG:
        per_layer = [s[name].T if name in TRANSPOSED else s[name] for s in layer_sums]
        g[name], delta[name], new_m[name], new_v[name] = _adamw_layers(w[name], per_layer, m[name], v[name], f"adamw_{name}")
    shapes = [w[n].shape for n in SMALL_NAMES]
    d, nm, nv = _adamw(*[_pack_rows([t[n] for n in SMALL_NAMES]) for t in (w, g, m, v)], "adamw_small")
    for out, buf in ((delta, d), (new_m, nm), (new_v, nv)):
        out.update(zip(SMALL_NAMES, _unpack_rows(buf, shapes)))

    return (loss, grads['x'][None], *[g[n] for n in WEIGHT_NAMES], *[delta[n] for n in WEIGHT_NAMES],
            *[new_m[n] for n in WEIGHT_NAMES], *[new_v[n] for n in WEIGHT_NAMES])
```

```python
import jax
import jax.numpy as jnp
import numpy as np
from jax import lax
from jax.experimental import pallas as pl
from jax.experimental.pallas import tpu as pltpu
from jax.experimental.pallas import tpu_sc as plsc

F32 = jnp.float32
BF16 = jnp.bfloat16

D_MODEL = 1024
DEPTH = 2
N_META = 16
HEADS = 8
V_DIM = 64
NOPE = 64
ROPE = 32
HALF_ROPE = ROPE // 2
QK_DIM = NOPE + ROPE
Q_RANK = 384
KV_RANK = 256
MIX = 512
CONV_K = 3
S5_GROUPS = 32
S5_GROUP = 16
S5_STATE = 64
S5_CH = S5_GROUPS * S5_STATE
D_FF = 2816
ALPHA = (2.0 * DEPTH) ** 0.25
LN_EPS = 1e-5
RMS_EPS = 1e-6
ROPE_BASE = 10000.0
IN_SPLITS = (Q_RANK, KV_RANK, ROPE, MIX, MIX, MIX, MIX, 3 * D_MODEL)
D_IN = sum(IN_SPLITS)
ADAM_LR, ADAM_B1, ADAM_B2, ADAM_EPS, ADAM_WD, ADAM_STEP = 0.001, 0.9, 0.999, 1e-08, 0.01, 10

N_DEV = 8
AXES = ("x", "y", "c")
LANES = 128
PACK_COLS = 1024
HEAD_PAD = 128
VMEM_LIMIT = 48 * 1024 * 1024

PIN_CQ, PIN_CKV, PIN_KR1, PIN_KR2, PIN_XBAR, PIN_BG, PIN_CG, PIN_U, PIN_GATES, PIN_END = (
    0, 384, 640, 768, 896, 1408, 1920, 2432, 2944, 6016)
D_IN_PAD = 6144

WEIGHT_NAMES = ['meta', 'ffn1_w_gate', 'ffn1_w_up', 'ffn1_w_down', 'ln1_g', 'ln1_b', 'w_in', 'mla_q_norm_g', 'mla_w_uq',
                'mla_kv_norm_g', 'mla_w_ukv', 'mla_w_o', 'conv_w', 'conv_b', 'conv_w_out', 's5_a_re', 's5_a_im',
                's5_log_dt', 's5_b_re', 's5_b_im', 's5_c_re', 's5_c_im', 's5_d', 's5_w_glu', 's5_b_glu', 's5_w_out',
                'w_o', 'ln2_g', 'ln2_b', 'ffn2_w_gate', 'ffn2_w_up', 'ffn2_w_down', 'ln3_g', 'ln3_b']
BIG = {'ffn1_w_gate': 2, 'ffn1_w_up': 2, 'ffn1_w_down': 1, 'w_in': 2, 'mla_w_uq': 2, 'mla_w_ukv': 2, 'mla_w_o': 2,
       'conv_w_out': 2, 's5_w_glu': 1, 's5_w_out': 2, 'w_o': 1, 'ffn2_w_gate': 2, 'ffn2_w_up': 2, 'ffn2_w_down': 1}
SMALL_SHARDED = ('meta', 'conv_w')
SMALL_NAMES = [n for n in WEIGHT_NAMES if n not in BIG]


def _divisor_tile(n, limit, mult):
    best = None
    for t in range(mult, min(n, limit) + 1, mult):
        if n % t == 0:
            best = t
    return best if best is not None else n


def _params(*sem):
    return pltpu.CompilerParams(dimension_semantics=sem, vmem_limit_bytes=VMEM_LIMIT)


def _matmul(a, b, *, ta=False, tb=False, out_dtype=F32, add=None, name):
    m, k = (a.shape[1], a.shape[0]) if ta else a.shape
    n = b.shape[0] if tb else b.shape[1]
    assert (b.shape[1] if tb else b.shape[0]) == k, (a.shape, b.shape, ta, tb)
    both_bf16 = a.dtype == BF16 and b.dtype == BF16
    deep = both_bf16 and not ta and 1408 < k <= D_FF
    tm = (_divisor_tile(m, 1408, LANES) if ta else
          _divisor_tile(m, 1088 if (deep or a.dtype != BF16) else 2176, 16))
    tn = _divisor_tile(n, 512, LANES)
    tk = k if deep else _divisor_tile(k, 2176 if (ta and both_bf16) else 1408, 16 if ta else LANES)
    nk = k // tk
    dims = (((0 if ta else 1,), (1 if tb else 0,)), ((), ()))

    in_place = jnp.dtype(out_dtype) == jnp.dtype(F32)

    def kern(a_ref, b_ref, *rest):
        add_ref = rest[0] if add is not None else None
        o_ref, *scratch = rest[1:] if add is not None else rest
        kk = pl.program_id(2)
        part = lax.dot_general(a_ref[...].astype(BF16), b_ref[...].astype(BF16), dims, preferred_element_type=F32)
        first = lambda: part if add_ref is None else part + add_ref[...].astype(F32)
        if nk == 1:
            o_ref[...] = first().astype(o_ref.dtype)
            return
        acc_ref = o_ref if in_place else scratch[0]

        @pl.when(kk == 0)
        def _():
            acc_ref[...] = first()

        @pl.when(kk > 0)
        def _():
            acc_ref[...] += part

        if not in_place:
            @pl.when(kk == nk - 1)
            def _():
                o_ref[...] = acc_ref[...].astype(o_ref.dtype)

    a_spec = pl.BlockSpec((tk, tm), lambda i, j, kk: (kk, i)) if ta else pl.BlockSpec((tm, tk), lambda i, j, kk: (i, kk))
    b_spec = pl.BlockSpec((tn, tk), lambda i, j, kk: (j, kk)) if tb else pl.BlockSpec((tk, tn), lambda i, j, kk: (kk, j))
    o_spec = pl.BlockSpec((tm, tn), lambda i, j, kk: (i, j))
    return pl.pallas_call(
        kern, name=name, grid=(m // tm, n // tn, nk),
        in_specs=[a_spec, b_spec] + ([o_spec] if add is not None else []), out_specs=o_spec,
        out_shape=jax.ShapeDtypeStruct((m, n), out_dtype),
        scratch_shapes=[] if (nk == 1 or in_place) else [pltpu.VMEM((tm, tn), F32)],
        compiler_params=_params("parallel", "parallel", "arbitrary"),
    )(a, b, *([add] if add is not None else []))


def _make_mm(name, wt=False):
    @jax.custom_vjp
    def mm(x, w, wz):
        return _matmul(x, w, tb=wt, name=name + "_fwd")

    def fwd(x, w, wz):
        return _matmul(x, w, tb=wt, name=name + "_fwd"), (x, w)

    def bwd(res, dy):
        x, w = res
        wz_dtype = BF16
        dx = _matmul(dy, w, tb=not wt, out_dtype=x.dtype, name=name + "_dx")
        dw = (_matmul(dy, x, ta=True, out_dtype=wz_dtype, name=name + "_dw") if wt
              else _matmul(x, dy, ta=True, out_dtype=wz_dtype, name=name + "_dw"))
        return dx, jnp.zeros_like(w), dw

    mm.defvjp(fwd, bwd)
    return mm


def _make_mm_f32w(name):
    @jax.custom_vjp
    def mm(x, w):
        return _matmul(x, w, name=name + "_fwd")

    def fwd(x, w):
        return _matmul(x, w, name=name + "_fwd"), (x, w)

    def bwd(res, dy):
        x, w = res
        return (_matmul(dy, w, tb=True, out_dtype=x.dtype, name=name + "_dx"),
                _matmul(x, dy, ta=True, name=name + "_dw"))

    mm.defvjp(fwd, bwd)
    return mm


BD_BLOCKS = 4
BD_PARTS = 2


def _bd_call(a, b, out_shape, a_blk, b_blk, o_blk, a_idx, b_idx, o_idx, dims, reduce_parts, name):
    def kern(a_ref, b_ref, o_ref):
        part = lax.dot_general(a_ref[...].astype(BF16), b_ref[...].astype(BF16), dims, preferred_element_type=F32)
        if not reduce_parts:
            o_ref[...] = part.astype(o_ref.dtype)
            return

        @pl.when(pl.program_id(1) == 0)
        def _():
            o_ref[...] = part

        @pl.when(pl.program_id(1) > 0)
        def _():
            o_ref[...] += part

    return pl.pallas_call(
        kern, name=name, grid=(BD_BLOCKS, BD_PARTS),
        in_specs=[pl.BlockSpec(a_blk, a_idx), pl.BlockSpec(b_blk, b_idx)], out_specs=pl.BlockSpec(o_blk, o_idx),
        out_shape=jax.ShapeDtypeStruct(out_shape, F32),
        compiler_params=_params("parallel", "arbitrary" if reduce_parts else "parallel"),
    )(a, b)


def _make_bd_in(name):
    wide = lambda j, p: (0, BD_BLOCKS * p + j)
    thin = lambda j, p: (0, j)
    diag = lambda j, p: (j, BD_BLOCKS * p + j)
    nn, nt, tn = (((1,), (0,)), ((), ())), (((1,), (1,)), ((), ())), (((0,), (0,)), ((), ()))

    def run(x, w):
        length, cols = x.shape[0], w.shape[1] // (BD_BLOCKS * BD_PARTS)
        return _bd_call(x, w, (length, w.shape[1]), (length, LANES), (LANES, cols), (length, cols), thin, diag, wide,
                        nn, False, name + "_fwd")

    @jax.custom_vjp
    def mm(x, w):
        return run(x, w)

    def fwd(x, w):
        return run(x, w), (x, w)

    def bwd(res, dy):
        x, w = res
        length, cols = x.shape[0], w.shape[1] // (BD_BLOCKS * BD_PARTS)
        dx = _bd_call(dy, w, x.shape, (length, cols), (LANES, cols), (length, LANES), wide, diag, thin, nt, True,
                      name + "_dx")
        dw = _bd_call(x, dy, w.shape, (length, LANES), (length, cols), (LANES, cols), thin, wide, diag, tn, False,
                      name + "_dw")
        return dx, dw

    mm.defvjp(fwd, bwd)
    return mm


def _make_bd_out(name):
    wide = lambda j, p: (0, BD_BLOCKS * p + j)
    thin = lambda j, p: (0, j)
    diag = lambda j, p: (BD_BLOCKS * p + j, j)
    nn, nt, tn = (((1,), (0,)), ((), ())), (((1,), (1,)), ((), ())), (((0,), (0,)), ((), ()))

    def run(x, w):
        length, cols = x.shape[0], w.shape[0] // (BD_BLOCKS * BD_PARTS)
        return _bd_call(x, w, (length, w.shape[1]), (length, cols), (cols, LANES), (length, LANES), wide, diag, thin,
                        nn, True, name + "_fwd")

    @jax.custom_vjp
    def mm(x, w):
        return run(x, w)

    def fwd(x, w):
        return run(x, w), (x, w)

    def bwd(res, dy):
        x, w = res
        length, cols = x.shape[0], w.shape[0] // (BD_BLOCKS * BD_PARTS)
        dx = _bd_call(dy, w, x.shape, (length, LANES), (cols, LANES), (length, cols), thin, diag, wide, nt, False,
                      name + "_dx")
        dw = _bd_call(x, dy, w.shape, (length, cols), (length, LANES), (cols, LANES), wide, thin, diag, tn, False,
                      name + "_dw")
        return dx, dw

    mm.defvjp(fwd, bwd)
    return mm


def _row_tile(rows, widths):
    limit = max(16, (6 * 1024 * 1024 // 4) // max(1, sum(widths)))
    return _divisor_tile(rows, limit, 16)


def _make_rowwise(f, n_rows, n_pars, out_dtypes, name, n_nodiff=0, grad_dtypes=None):
    n_out = len(out_dtypes)
    n_diff = n_rows - n_nodiff

    def run_fwd(rows, pars):
        length = rows[0].shape[0]
        shapes = jax.eval_shape(lambda *a: f(*a), *[jax.ShapeDtypeStruct((16, r.shape[1]), F32) for r in rows],
                                *[jax.ShapeDtypeStruct(p.shape, F32) for p in pars])
        widths = [s.shape[1] for s in shapes]
        tm = _row_tile(length, [r.shape[1] for r in rows] + widths)

        def kern(*refs):
            ins = [r[...].astype(F32) for r in refs[:n_rows + n_pars]]
            outs = f(*ins)
            for o_ref, o in zip(refs[n_rows + n_pars:], outs):
                o_ref[...] = o.astype(o_ref.dtype)

        return pl.pallas_call(
            kern, name=name + "_fwd", grid=(length // tm,),
            in_specs=[pl.BlockSpec((tm, r.shape[1]), lambda i: (i, 0)) for r in rows]
            + [pl.BlockSpec(p.shape, lambda i: (0, 0)) for p in pars],
            out_specs=[pl.BlockSpec((tm, w), lambda i: (i, 0)) for w in widths],
            out_shape=[jax.ShapeDtypeStruct((length, w), dt) for w, dt in zip(widths, out_dtypes)],
            compiler_params=_params("parallel"),
        )(*rows, *pars)

    def run_bwd(rows, pars, cts):
        length = rows[0].shape[0]
        tm = _row_tile(length, [r.shape[1] for r in rows] * 2 + [c.shape[1] for c in cts] * 2)

        def kern(*refs):
            ins = [r[...].astype(F32) for r in refs[:n_rows + n_pars]]
            ct = [r[...].astype(F32) for r in refs[n_rows + n_pars:n_rows + n_pars + n_out]]
            out_refs = refs[n_rows + n_pars + n_out:]
            nodiff = ins[n_diff:n_rows]
            _, vjp = jax.vjp(lambda *a: f(*a[:n_diff], *nodiff, *a[n_diff:]), *ins[:n_diff], *ins[n_rows:])
            grads = vjp(tuple(ct))
            for o_ref, g in zip(out_refs[:n_diff], grads[:n_diff]):
                o_ref[...] = g.astype(o_ref.dtype)
            first = pl.program_id(0) == 0
            for o_ref, g in zip(out_refs[n_diff:], grads[n_diff:]):
                @pl.when(first)
                def _(o_ref=o_ref, g=g):
                    o_ref[...] = g

                @pl.when(jnp.logical_not(first))
                def _(o_ref=o_ref, g=g):
                    o_ref[...] += g

        return pl.pallas_call(
            kern, name=name + "_bwd", grid=(length // tm,),
            in_specs=[pl.BlockSpec((tm, r.shape[1]), lambda i: (i, 0)) for r in rows]
            + [pl.BlockSpec(p.shape, lambda i: (0, 0)) for p in pars]
            + [pl.BlockSpec((tm, c.shape[1]), lambda i: (i, 0)) for c in cts],
            out_specs=[pl.BlockSpec((tm, r.shape[1]), lambda i: (i, 0)) for r in rows[:n_diff]]
            + [pl.BlockSpec(p.shape, lambda i: (0, 0)) for p in pars],
            out_shape=[jax.ShapeDtypeStruct(r.shape, r.dtype if grad_dtypes is None else grad_dtypes[i])
                       for i, r in enumerate(rows[:n_diff])]
            + [jax.ShapeDtypeStruct(p.shape, F32) for p in pars],
            compiler_params=_params("arbitrary"),
        )(*rows, *pars, *cts)

    @jax.custom_vjp
    def op(*args):
        return tuple(run_fwd(args[:n_rows], args[n_rows:]))

    def fwd(*args):
        return tuple(run_fwd(args[:n_rows], args[n_rows:])), args

    def bwd(args, cts):
        grads = run_bwd(args[:n_rows], args[n_rows:], cts)
        zeros = [jnp.zeros_like(r) for r in args[n_diff:n_rows]]
        return (*grads[:n_diff], *zeros, *grads[n_diff:])

    op.defvjp(fwd, bwd)
    op.run_fwd, op.run_bwd = run_fwd, run_bwd
    return op


def _layer_norm(z, g, b):
    mu = jnp.mean(z, axis=-1, keepdims=True)
    d = z - mu
    var = jnp.mean(d * d, axis=-1, keepdims=True)
    return d * lax.rsqrt(var + LN_EPS) * g + b


def _f_ln_half(h, f, g, b):
    return (_layer_norm(ALPHA * h + 0.5 * f, g, b),)


def _f_ln_full(h, f, g, b):
    return (_layer_norm(ALPHA * h + f, g, b),)


def _f_rms(x, g):
    return (x * lax.rsqrt(jnp.mean(x * x, axis=-1, keepdims=True) + RMS_EPS) * g,)


def _f_rope(x1, x2, cos, sin):
    return x1 * cos - x2 * sin, x2 * cos + x1 * sin


def _f_gelu_skip(y, u, d):
    return (jax.nn.gelu(y + d * u),)


def _f_glu(z, t, b):
    return (z * jax.nn.sigmoid(t + b),)


def _f_merge(ga, gb, gc, ya, yb, yc):
    return (jax.nn.sigmoid(ga) * ya + jax.nn.sigmoid(gb) * yb + jax.nn.sigmoid(gc) * yc,)


def _f_swiglu(gate, up):
    return (jax.nn.silu(gate) * up,)


def _swiglu_bwd(gate, up, dact, name):
    length, width = gate.shape
    tm = _divisor_tile(length, 128, 16)

    def kern(g_ref, u_ref, d_ref, dg_ref, du_ref):
        g, d = g_ref[...], d_ref[...].astype(F32)
        s = jax.nn.sigmoid(g)
        dg_ref[...] = (d * u_ref[...] * (s * (1.0 + g * (1.0 - s)))).astype(dg_ref.dtype)
        du_ref[...] = (d * (g * s)).astype(du_ref.dtype)

    blk = pl.BlockSpec((tm, width), lambda i: (i, 0))
    out = jax.ShapeDtypeStruct(gate.shape, BF16)
    return pl.pallas_call(kern, name=name, grid=(length // tm,), in_specs=[blk] * 3, out_specs=[blk] * 2,
                          out_shape=[out, out], compiler_params=_params("parallel"))(gate, up, dact)


def _f_cast(x):
    return (x,)


def _make_ffn(tag):
    ln = _make_rowwise(_f_ln_half, 2, 2, (F32,), f"{tag}_ln", grad_dtypes=(F32, BF16))
    cast = _make_rowwise(_f_cast, 1, 0, (BF16,), f"{tag}_cast")
    swiglu = _make_rowwise(_f_swiglu, 2, 0, (BF16,), f"{tag}_swiglu", grad_dtypes=(BF16, BF16))

    def forward(h, w_gate, w_up, w_down, g, b):
        hb, = cast.run_fwd((h,), ())
        gate = _matmul(hb, w_gate, tb=True, name=f"{tag}_gate_fwd")
        up = _matmul(hb, w_up, tb=True, name=f"{tag}_up_fwd")
        act, = swiglu.run_fwd((gate, up), ())
        f = _matmul(act, w_down, name=f"{tag}_down_fwd")
        y, = ln.run_fwd((h, f), (g, b))
        return y, (h, hb, gate, up, act, f, w_gate, w_up, w_down, g, b)

    @jax.custom_vjp
    def block(h, w_gate, w_up, w_down, z_gate, z_up, z_down, g, b):
        return forward(h, w_gate, w_up, w_down, g, b)[0]

    def fwd(h, w_gate, w_up, w_down, z_gate, z_up, z_down, g, b):
        return forward(h, w_gate, w_up, w_down, g, b)

    def bwd(res, dy):
        h, hb, gate, up, act, f, w_gate, w_up, w_down, g, b = res
        dh, df, dg, db = ln.run_bwd((h, f), (g, b), (dy,))
        dact = _matmul(df, w_down, tb=True, out_dtype=BF16, name=f"{tag}_down_dx")
        dw_down = _matmul(act, df, ta=True, out_dtype=BF16, name=f"{tag}_down_dw")
        dgate, dup = _swiglu_bwd(gate, up, dact, f"{tag}_swiglu_bwd")
        dw_gate = _matmul(dgate, hb, ta=True, out_dtype=BF16, name=f"{tag}_gate_dw")
        dw_up = _matmul(dup, hb, ta=True, out_dtype=BF16, name=f"{tag}_up_dw")
        dh = _matmul(dgate, w_gate, add=dh, name=f"{tag}_gate_dx")
        dh = _matmul(dup, w_up, add=dh, name=f"{tag}_up_dx")
        zero = jnp.zeros_like
        return dh, zero(w_gate), zero(w_up), zero(w_down), dw_gate, dw_up, dw_down, dg, db

    block.defvjp(fwd, bwd)
    return block


def _attn_scores(q, k, q_block, tq):
    length = k.shape[0]
    s = lax.dot_general(q, k, (((1,), (1,)), ((), ())), preferred_element_type=F32) * (QK_DIM ** -0.5)
    row = q_block * tq + lax.broadcasted_iota(jnp.int32, (tq, length), 0)
    col = lax.broadcasted_iota(jnp.int32, (tq, length), 1)
    s = jnp.where(col <= row, s, -1e30)
    e = jnp.exp(s - jnp.max(s, axis=1, keepdims=True))
    return e * (1.0 / jnp.sum(e, axis=1, keepdims=True))


ATTN_SEGMENTS = 4


def _attn_tiles(length):
    seg = length // ATTN_SEGMENTS
    return seg, _divisor_tile(seg, 272, 16)


def _attn_fwd(q3, k3, v3):
    heads, length, _ = q3.shape
    seg, tq = _attn_tiles(length)
    outs = []
    for s in range(ATTN_SEGMENTS):
        kmax, base = (s + 1) * seg, s * (seg // tq)

        def kern(q_ref, k_ref, v_ref, o_ref, base=base):
            p = _attn_scores(q_ref[0], k_ref[0], base + pl.program_id(1), tq)
            o_ref[0] = jnp.dot(p.astype(BF16), v_ref[0], preferred_element_type=F32).astype(o_ref.dtype)

        outs.append(pl.pallas_call(
            kern, name=f"attn_fwd_seg{s}", grid=(heads, seg // tq),
            in_specs=[pl.BlockSpec((1, tq, HEAD_PAD), lambda h, i, base=base: (h, base + i, 0)),
                      pl.BlockSpec((1, kmax, HEAD_PAD), lambda h, i: (h, 0, 0)),
                      pl.BlockSpec((1, kmax, V_DIM), lambda h, i: (h, 0, 0))],
            out_specs=pl.BlockSpec((1, tq, V_DIM), lambda h, i: (h, i, 0)),
            out_shape=jax.ShapeDtypeStruct((heads, seg, V_DIM), F32),
            compiler_params=_params("parallel", "parallel"),
        )(q3, k3, v3))
    return jnp.concatenate(outs, axis=1)


def _attn_bwd(q3, k3, v3, do3):
    heads, length, _ = q3.shape
    seg, tq = _attn_tiles(length)
    dk = jnp.zeros((heads, length, HEAD_PAD), F32)
    dv = jnp.zeros((heads, length, V_DIM), F32)
    dqs = [None] * ATTN_SEGMENTS
    for s in reversed(range(ATTN_SEGMENTS)):
        kmax, base = (s + 1) * seg, s * (seg // tq)

        def kern(q_ref, k_ref, v_ref, do_ref, dk_in, dv_in, dq_ref, dk_ref, dv_ref, base=base):
            i = pl.program_id(1)
            q, k, v, do = q_ref[0], k_ref[0], v_ref[0], do_ref[0].astype(BF16)
            p = _attn_scores(q, k, base + i, tq)
            dp = lax.dot_general(do, v, (((1,), (1,)), ((), ())), preferred_element_type=F32)
            ds = (p * (dp - jnp.sum(p * dp, axis=1, keepdims=True)) * (QK_DIM ** -0.5)).astype(BF16)
            dq_ref[0] = jnp.dot(ds, k, preferred_element_type=F32)
            dk_part = lax.dot_general(ds, q, (((0,), (0,)), ((), ())), preferred_element_type=F32)
            dv_part = lax.dot_general(p.astype(BF16), do, (((0,), (0,)), ((), ())), preferred_element_type=F32)

            @pl.when(i == 0)
            def _():
                dk_ref[0] = dk_in[0] + dk_part
                dv_ref[0] = dv_in[0] + dv_part

            @pl.when(i > 0)
            def _():
                dk_ref[0] += dk_part
                dv_ref[0] += dv_part

        q_blk = pl.BlockSpec((1, tq, HEAD_PAD), lambda h, i, base=base: (h, base + i, 0))
        k_blk = pl.BlockSpec((1, kmax, HEAD_PAD), lambda h, i: (h, 0, 0))
        v_blk = pl.BlockSpec((1, kmax, V_DIM), lambda h, i: (h, 0, 0))
        dqs[s], dk, dv = pl.pallas_call(
            kern, name=f"attn_bwd_seg{s}", grid=(heads, seg // tq),
            in_specs=[q_blk, k_blk, v_blk, pl.BlockSpec((1, tq, V_DIM), lambda h, i, base=base: (h, base + i, 0)),
                      k_blk, v_blk],
            out_specs=[pl.BlockSpec((1, tq, HEAD_PAD), lambda h, i: (h, i, 0)), k_blk, v_blk],
            out_shape=[jax.ShapeDtypeStruct((heads, seg, HEAD_PAD), F32), jax.ShapeDtypeStruct(dk.shape, F32),
                       jax.ShapeDtypeStruct(dv.shape, F32)],
            input_output_aliases={4: 1, 5: 2}, compiler_params=_params("parallel", "arbitrary"),
        )(q3, k3, v3, do3, dk, dv)
    return jnp.concatenate(dqs, axis=1), dk, dv


@jax.custom_vjp
def _attention(q3, k3, v3):
    return _attn_fwd(q3, k3, v3)


def _attention_fwd(q3, k3, v3):
    return _attn_fwd(q3, k3, v3), (q3, k3, v3)


def _attention_bwd(res, do3):
    q3, k3, v3 = res
    dq, dk, dv = _attn_bwd(q3, k3, v3, do3)
    return dq.astype(q3.dtype), dk.astype(k3.dtype), dv.astype(v3.dtype)


_attention.defvjp(_attention_fwd, _attention_bwd)


def _conv_terms(x, c, w_ref, cb):
    u = c * x
    row = lax.broadcasted_iota(jnp.int32, u.shape, 0)
    u1 = jnp.where(row >= 1, pltpu.roll(u, 1, 0), 0.0)
    u2 = jnp.where(row >= 2, pltpu.roll(u, 2, 0), 0.0)
    y = cb + w_ref[0:1, :] * u2 + w_ref[1:2, :] * u1 + w_ref[2:3, :] * u
    return u, u1, u2, y


def _conv_specs(length):
    col = pl.BlockSpec((length, LANES), lambda j: (0, j))
    return col, pl.BlockSpec((CONV_K, LANES), lambda j: (0, j)), pl.BlockSpec((1, LANES), lambda j: (0, j))


def _conv_fwd(x, b, c, w, cb):
    length = x.shape[0]

    def kern(x_ref, b_ref, c_ref, w_ref, cb_ref, o_ref):
        _, _, _, y = _conv_terms(x_ref[...], c_ref[...], w_ref, cb_ref[...])
        o_ref[...] = b_ref[...] * y

    col, wspec, bspec = _conv_specs(length)
    return pl.pallas_call(
        kern, name="conv_fwd", grid=(MIX // LANES,), in_specs=[col, col, col, wspec, bspec], out_specs=col,
        out_shape=jax.ShapeDtypeStruct((length, MIX), F32), compiler_params=_params("parallel"),
    )(x, b, c, w, cb)


def _conv_bwd(x, b, c, w, cb, do):
    length = x.shape[0]

    def kern(x_ref, b_ref, c_ref, w_ref, cb_ref, do_ref, dx_ref, db_ref, dc_ref, dw_ref, dcb_ref):
        x, c, do = x_ref[...], c_ref[...], do_ref[...]
        u, u1, u2, y = _conv_terms(x, c, w_ref, cb_ref[...])
        db_ref[...] = do * y
        dy = do * b_ref[...]
        row = lax.broadcasted_iota(jnp.int32, dy.shape, 0)
        dy1 = jnp.where(row < length - 1, pltpu.roll(dy, length - 1, 0), 0.0)
        dy2 = jnp.where(row < length - 2, pltpu.roll(dy, length - 2, 0), 0.0)
        du = w_ref[2:3, :] * dy + w_ref[1:2, :] * dy1 + w_ref[0:1, :] * dy2
        dx_ref[...] = du * c
        dc_ref[...] = du * x
        dw_ref[0:1, :] = jnp.sum(dy * u2, axis=0, keepdims=True)
        dw_ref[1:2, :] = jnp.sum(dy * u1, axis=0, keepdims=True)
        dw_ref[2:3, :] = jnp.sum(dy * u, axis=0, keepdims=True)
        dcb_ref[...] = jnp.sum(dy, axis=0, keepdims=True)

    col, wspec, bspec = _conv_specs(length)
    big = jax.ShapeDtypeStruct((length, MIX), F32)
    return pl.pallas_call(
        kern, name="conv_bwd", grid=(MIX // LANES,), in_specs=[col, col, col, wspec, bspec, col],
        out_specs=[col, col, col, wspec, bspec],
        out_shape=[big, big, big, jax.ShapeDtypeStruct((CONV_K, MIX), F32), jax.ShapeDtypeStruct((1, MIX), F32)],
        compiler_params=_params("parallel"),
    )(x, b, c, w, cb, do)


@jax.custom_vjp
def _short_conv(x, b, c, w, cb):
    return _conv_fwd(x, b, c, w, cb)


def _short_conv_fwd(x, b, c, w, cb):
    return _conv_fwd(x, b, c, w, cb), (x, b, c, w, cb)


def _short_conv_bwd(res, do):
    return tuple(_conv_bwd(*res, do))


_short_conv.defvjp(_short_conv_fwd, _short_conv_bwd)


SCAN_ROWS = S5_CH // LANES
SCAN_TC = 136


def _scan_fwd(ar, ai, b):
    length = b.shape[0]
    tc = _divisor_tile(length, SCAN_TC, 8)

    def kern(ar_ref, ai_ref, b_ref, x_ref, sr, si):
        @pl.when(pl.program_id(0) == 0)
        def _():
            sr[...] = jnp.zeros_like(sr)
            si[...] = jnp.zeros_like(si)

        a_re, a_im = ar_ref[...], ai_ref[...]

        def body(t, carry):
            xr, xi = carry
            nr = a_re * xr - a_im * xi + b_ref[t, 0:SCAN_ROWS, :]
            ni = a_re * xi + a_im * xr + b_ref[t, SCAN_ROWS:2 * SCAN_ROWS, :]
            x_ref[t, 0:SCAN_ROWS, :] = nr
            x_ref[t, SCAN_ROWS:2 * SCAN_ROWS, :] = ni
            return nr, ni

        xr, xi = lax.fori_loop(0, tc, body, (sr[...], si[...]), unroll=4)
        sr[...] = xr
        si[...] = xi

    par = pl.BlockSpec((SCAN_ROWS, LANES), lambda i: (0, 0))
    blk = pl.BlockSpec((tc, 2 * SCAN_ROWS, LANES), lambda i: (i, 0, 0))
    return pl.pallas_call(
        kern, name="s5_scan_fwd", grid=(length // tc,), in_specs=[par, par, blk], out_specs=blk,
        out_shape=jax.ShapeDtypeStruct(b.shape, F32), scratch_shapes=[pltpu.VMEM((SCAN_ROWS, LANES), F32)] * 2,
        compiler_params=_params("arbitrary"),
    )(ar, ai, b)


def _scan_bwd(ar, ai, x, dx):
    length = x.shape[0]
    tc = _divisor_tile(length, SCAN_TC, 8)
    n_blk = length // tc
    re, im = slice(0, SCAN_ROWS), slice(SCAN_ROWS, 2 * SCAN_ROWS)

    def kern(ar_ref, ai_ref, x_ref, dx_ref, db_ref, dar_ref, dai_ref, lr_s, li_s):
        @pl.when(pl.program_id(0) == 0)
        def _():
            lr_s[...] = jnp.zeros_like(lr_s)
            li_s[...] = jnp.zeros_like(li_s)
            dar_ref[...] = jnp.zeros_like(dar_ref)
            dai_ref[...] = jnp.zeros_like(dai_ref)

        a_re, a_im = ar_ref[...], ai_ref[...]

        def body(j, carry):
            t = tc - 1 - j
            lr, li, gr, gi = carry
            x_re, x_im = x_ref[t, re, :], x_ref[t, im, :]
            gr = gr + (lr * x_re + li * x_im)
            gi = gi + (li * x_re - lr * x_im)
            nlr = dx_ref[t, re, :] + (a_re * lr + a_im * li)
            nli = dx_ref[t, im, :] + (a_re * li - a_im * lr)
            db_ref[t, re, :] = nlr
            db_ref[t, im, :] = nli
            return nlr, nli, gr, gi

        lr, li, gr, gi = lax.fori_loop(0, tc, body, (lr_s[...], li_s[...], dar_ref[...], dai_ref[...]), unroll=4)
        lr_s[...] = lr
        li_s[...] = li
        dar_ref[...] = gr
        dai_ref[...] = gi

    par = pl.BlockSpec((SCAN_ROWS, LANES), lambda i: (0, 0))
    blk = pl.BlockSpec((tc, 2 * SCAN_ROWS, LANES), lambda i: (n_blk - 1 - i, 0, 0))
    pout = jax.ShapeDtypeStruct((SCAN_ROWS, LANES), F32)
    return pl.pallas_call(
        kern, name="s5_scan_bwd", grid=(n_blk,), in_specs=[par, par, blk, blk],
        out_specs=[blk, par, par], out_shape=[jax.ShapeDtypeStruct(x.shape, F32), pout, pout],
        scratch_shapes=[pltpu.VMEM((SCAN_ROWS, LANES), F32)] * 2, compiler_params=_params("arbitrary"),
    )(ar, ai, x, dx)


@jax.custom_vjp
def _s5_scan(ar, ai, b):
    return _scan_fwd(ar, ai, b)


def _s5_scan_fwd(ar, ai, b):
    x = _scan_fwd(ar, ai, b)
    return x, (ar, ai, x)


def _s5_scan_bwd(res, dx):
    ar, ai, x = res
    db, dar, dai = _scan_bwd(ar, ai, x, dx)
    return dar, dai, db


_s5_scan.defvjp(_s5_scan_fwd, _s5_scan_bwd)


def _loss_call(y, target, n_real):
    length = y.shape[0]
    tm = _divisor_tile(length, 544, 16)

    def kern(y_ref, t_ref, loss_ref, dy_ref):
        i = pl.program_id(0)
        row = i * tm + lax.broadcasted_iota(jnp.int32, (tm, 1), 0)
        keep = jnp.logical_and(row >= N_META, row < n_real)
        err = jnp.where(keep, y_ref[...] - t_ref[...], 0.0)
        dy_ref[...] = err * (1.0 / D_MODEL)
        part = 0.5 * jnp.sum(jnp.mean(err * err, axis=-1, keepdims=True), axis=0, keepdims=True)

        @pl.when(i == 0)
        def _():
            loss_ref[...] = jnp.zeros_like(loss_ref)

        loss_ref[...] += part

    blk = pl.BlockSpec((tm, D_MODEL), lambda i: (i, 0))
    return pl.pallas_call(
        kern, name="loss_head", grid=(length // tm,), in_specs=[blk, blk],
        out_specs=[pl.BlockSpec((8, LANES), lambda i: (0, 0)), blk],
        out_shape=[jax.ShapeDtypeStruct((8, LANES), F32), jax.ShapeDtypeStruct(y.shape, F32)],
        compiler_params=_params("arbitrary"),
    )(y, target)


def _make_loss(n_real):
    @jax.custom_vjp
    def loss(y, target):
        return _loss_call(y, target, n_real)[0][0, 0]

    def fwd(y, target):
        total, dy = _loss_call(y, target, n_real)
        return total[0, 0], dy

    def bwd(dy, ct):
        return dy * ct, jnp.zeros_like(dy)

    loss.defvjp(fwd, bwd)
    return loss


HBM_SPEC = pl.BlockSpec(memory_space=pl.ANY)
MESH_ID = pl.DeviceIdType.MESH


SC_MESH = dict(axis_name="sequencer", num_cores=1)
GATHER_ID, SLOT_ID = 1, 2


def _handshake(peers):
    barrier = pltpu.get_barrier_semaphore()
    for peer in peers:
        pl.semaphore_signal(barrier, inc=1, device_id=peer, device_id_type=MESH_ID)
    pl.semaphore_wait(barrier, len(peers))


def _exchange_call(body, name, ins, out_types, n_sems, sequencer_id):
    n_in, n_out = len(ins), len(out_types)
    sems = [pltpu.SemaphoreType.DMA((n_sems,)), pltpu.SemaphoreType.DMA((n_sems,)), pltpu.SemaphoreType.DMA((n_in,))]
    if sequencer_id is None:
        def on_core(*refs):
            body(lambda peers: None, refs[:n_in], refs[n_in:n_in + n_out], *refs[n_in + n_out:])

        return pl.pallas_call(on_core, name=name, out_shape=out_types, in_specs=[HBM_SPEC] * n_in,
                              out_specs=[HBM_SPEC] * n_out, scratch_shapes=sems)(*ins)

    def on_sequencer(*refs):
        body(_handshake, refs[:n_in], refs[n_in:n_in + n_out], *refs[n_in + n_out:])

    return pl.kernel(on_sequencer, name=name, out_type=out_types, mesh=plsc.ScalarSubcoreMesh(**SC_MESH),
                     scratch_types=sems, compiler_params=pltpu.CompilerParams(collective_id=sequencer_id))(*ins)


def _all_gather(shards, name, sequencer=False):
    n = len(shards)

    def body(handshake, x_refs, out_refs, send_sems, recv_sems, local_sems):
        x, y, c = lax.axis_index("x"), lax.axis_index("y"), lax.axis_index("c")
        me, sibling = (x, y, c), (x, y, 1 - c)
        chips = [(1 - x, y), (x, 1 - y), (1 - x, 1 - y)]
        handshake([sibling] + [(*chip, c) for chip in chips])

        def copy(b, k, block, to, from_input=False):
            px, py, pc = block
            slot = out_refs[b].at[4 * px + 2 * py + pc]
            return pltpu.make_async_remote_copy(
                src_ref=x_refs[b] if from_input else slot, dst_ref=slot,
                send_sem=send_sems.at[7 * b + k], recv_sem=recv_sems.at[7 * b + k], device_id=to, device_id_type=MESH_ID)

        mine = [pltpu.make_async_copy(x_refs[b], out_refs[b].at[4 * x + 2 * y + c], local_sems.at[b]) for b in range(n)]
        for cp in mine:
            cp.start()
        first = []
        for b in range(n):
            first.append(copy(b, 0, me, sibling, from_input=True))
            first += [copy(b, 1 + j, me, (*chip, c), from_input=True) for j, chip in enumerate(chips)]
        for cp in first:
            cp.start()
        passed = []
        for j, chip in enumerate(chips):
            for b in range(n):
                copy(b, 1 + j, (*chip, c), me).wait_recv()
                passed.append(copy(b, 4 + j, (*chip, c), sibling))
                passed[-1].start()
        for b in range(n):
            copy(b, 0, sibling, me).wait_recv()
            for j, chip in enumerate(chips):
                copy(b, 4 + j, (*chip, 1 - c), me).wait_recv()
        for cp in first + passed:
            cp.wait_send()
        for cp in mine:
            cp.wait()

    out_types = [jax.ShapeDtypeStruct((N_DEV, *s.shape), s.dtype) for s in shards]
    return _exchange_call(body, name, shards, out_types, 7 * n, GATHER_ID if sequencer else None)


def _slot_exchange(bufs, name, sequencer=False):
    def body(handshake, ins, outs, send_sems, recv_sems, local_sems):
        x, y, c = lax.axis_index("x"), lax.axis_index("y"), lax.axis_index("c")
        me = 4 * x + 2 * y + c
        flips = [(dx, dy, dc) for dx in (0, 1) for dy in (0, 1) for dc in (0, 1)][1:]
        peers = [(1 - x if dx else x, 1 - y if dy else y, 1 - c if dc else c) for dx, dy, dc in flips]
        handshake(peers)
        own = [pltpu.make_async_copy(src.at[me], dst.at[me], local_sems.at[b]) for b, (src, dst) in enumerate(zip(ins, outs))]
        copies = []
        for b, (src, dst) in enumerate(zip(ins, outs)):
            for k, (px, py, pc) in enumerate(peers):
                copies.append(pltpu.make_async_remote_copy(
                    src_ref=src.at[4 * px + 2 * py + pc], dst_ref=dst.at[me],
                    send_sem=send_sems.at[7 * b + k], recv_sem=recv_sems.at[7 * b + k],
                    device_id=(px, py, pc), device_id_type=MESH_ID))
        for cp in own + copies:
            cp.start()
        for cp in copies + own:
            cp.wait()

    out_types = [jax.ShapeDtypeStruct(b.shape, b.dtype) for b in bufs]
    return _exchange_call(body, name, bufs, out_types, 7 * len(bufs), SLOT_ID if sequencer else None)


def _slot_sum(slots, name):
    _, rows, cols = slots.shape
    tm = _divisor_tile(rows, 512, 16)

    def kern(s_ref, o_ref):
        total = s_ref[0].astype(F32)
        for d in range(1, N_DEV):
            total = total + s_ref[d].astype(F32)
        o_ref[...] = total

    return pl.pallas_call(
        kern, name=name, grid=(rows // tm,), in_specs=[pl.BlockSpec((N_DEV, tm, cols), lambda i: (0, i, 0))],
        out_specs=pl.BlockSpec((tm, cols), lambda i: (i, 0)), out_shape=jax.ShapeDtypeStruct((rows, cols), F32),
        compiler_params=_params("parallel"),
    )(slots)


def _reduce_scatter(bufs, tag, sequencer):
    arrived = _slot_exchange(bufs, f"rs_exchange_{tag}", sequencer)
    return [_slot_sum(a, f"rs_sum_{tag}{i}") for i, a in enumerate(arrived)]


def _adamw(w, g, m, v, name):
    rows, cols = w.shape
    tm = _divisor_tile(rows, max(8, (512 * 1024) // cols // 8 * 8), 8)

    def kern(w_ref, g_ref, m_ref, v_ref, d_ref, nm_ref, nv_ref):
        g = g_ref[...]
        m = ADAM_B1 * m_ref[...] + (1.0 - ADAM_B1) * g
        v = ADAM_B2 * v_ref[...] + (1.0 - ADAM_B2) * (g * g)
        m_hat = m / (1.0 - ADAM_B1 ** ADAM_STEP)
        v_hat = v / (1.0 - ADAM_B2 ** ADAM_STEP)
        d_ref[...] = -ADAM_LR * (m_hat / (jnp.sqrt(v_hat) + ADAM_EPS) + ADAM_WD * w_ref[...])
        nm_ref[...] = m
        nv_ref[...] = v

    blk = pl.BlockSpec((tm, cols), lambda i: (i, 0))
    out = jax.ShapeDtypeStruct(w.shape, F32)
    return pl.pallas_call(
        kern, name=name, grid=(rows // tm,), in_specs=[blk] * 4, out_specs=[blk] * 3, out_shape=[out] * 3,
        compiler_params=_params("parallel"),
    )(w, g, m, v)


def _adamw_layers(w, g_layers, m, v, name):
    depth, rows, cols = w.shape
    tm = _divisor_tile(rows, max(8, (512 * 1024) // cols // 8 * 8), 8)

    def kern(w_ref, m_ref, v_ref, *refs):
        g_refs, (g_out, d_ref, nm_ref, nv_ref) = refs[:depth], refs[depth:]
        layer = pl.program_id(0)
        g = g_refs[0][...]
        for l in range(1, depth):
            g = jnp.where(layer == l, g_refs[l][...], g)
        m = ADAM_B1 * m_ref[0] + (1.0 - ADAM_B1) * g
        v = ADAM_B2 * v_ref[0] + (1.0 - ADAM_B2) * (g * g)
        m_hat = m / (1.0 - ADAM_B1 ** ADAM_STEP)
        v_hat = v / (1.0 - ADAM_B2 ** ADAM_STEP)
        g_out[0] = g
        d_ref[0] = -ADAM_LR * (m_hat / (jnp.sqrt(v_hat) + ADAM_EPS) + ADAM_WD * w_ref[0])
        nm_ref[0] = m
        nv_ref[0] = v

    blk = pl.BlockSpec((1, tm, cols), lambda l, i: (l, i, 0))
    out = jax.ShapeDtypeStruct(w.shape, F32)
    return pl.pallas_call(
        kern, name=name, grid=(depth, rows // tm),
        in_specs=[blk] * 3 + [pl.BlockSpec((tm, cols), lambda l, i: (i, 0))] * depth,
        out_specs=[blk] * 4, out_shape=[out] * 4, compiler_params=_params("parallel", "parallel"),
    )(w, m, v, *g_layers)


TRANSPOSED = ('ffn1_w_gate', 'ffn1_w_up', 'ffn2_w_gate', 'ffn2_w_up', 'w_in', 'mla_w_uq', 'mla_w_ukv', 'mla_w_o',
              'conv_w_out', 's5_w_out')
PIECES = ((('ffn1_w_gate',), ('ffn1_w_up',), ('ffn1_w_down',)),
          (('w_in', 'w_o'), ('mla_w_o', 'conv_w_out', 's5_w_out', 's5_w_glu'), ('mla_w_uq',), ('mla_w_ukv',)),
          (('ffn2_w_gate',), ('ffn2_w_up',), ('ffn2_w_down',)))
PIN_CUTS = (PIN_CQ, PIN_CKV, PIN_KR1, PIN_KR2, PIN_XBAR, PIN_BG, PIN_CG, PIN_U, PIN_GATES, PIN_GATES + D_MODEL,
            PIN_GATES + 2 * D_MODEL, PIN_END, D_IN_PAD)
PIN_PIECES = ((PIN_CQ, Q_RANK), (PIN_CKV, KV_RANK), (PIN_KR1, HALF_ROPE), (PIN_KR2, HALF_ROPE), (PIN_XBAR, 4 * MIX),
              (PIN_GATES, 3 * D_MODEL))


def _make_split(cuts):
    @jax.custom_vjp
    def split(t):
        return tuple(t[:, a:b] for a, b in zip(cuts[:-1], cuts[1:]))

    def fwd(t):
        return split(t), None

    def bwd(_, cts):
        return (jnp.concatenate(cts, axis=1),)

    split.defvjp(fwd, bwd)
    return split


def _make_projection(tag, cuts):
    cast = _make_rowwise(_f_cast, 1, 0, (BF16,), f"{tag}_cast")

    def forward(h, w):
        hb, = cast.run_fwd((h,), ())
        full = _matmul(hb, w, tb=True, name=f"{tag}_fwd")
        return tuple(full[:, a:b] for a, b in zip(cuts[:-1], cuts[1:])), (hb, w)

    @jax.custom_vjp
    def proj(h, w, wz):
        return forward(h, w)[0]

    def fwd(h, w, wz):
        return forward(h, w)

    def bwd(res, cts):
        hb, w = res
        d_full = jnp.concatenate([c.astype(BF16) for c in cts], axis=1)
        dw = _matmul(d_full, hb, ta=True, out_dtype=BF16, name=f"{tag}_dw")
        return _matmul(d_full, w, name=f"{tag}_dx"), jnp.zeros_like(w), dw

    proj.defvjp(fwd, bwd)
    return proj


def _travel_shape(name, shape):
    return (shape[2], shape[1]) if name in TRANSPOSED else (shape[1], shape[2])


def _pack_groups(tensors, layer, groups, dtype):
    def view(n):
        t = tensors[n][layer]
        return (t.T if n in TRANSPOSED else t).astype(dtype)
    return [jnp.concatenate([view(n) for n in grp], axis=0) for grp in groups]


def _unpack_groups(bufs, groups, shard_shapes):
    out = {}
    for buf, grp in zip(bufs, groups):
        at = 0
        for n in grp:
            r, _ = _travel_shape(n, shard_shapes[n])
            out[n] = buf[..., at:at + r, :]
            at += r
    return out


def _pack_rows(arrays):
    flat = jnp.concatenate([a.reshape(-1) for a in arrays])
    rows = -(-flat.shape[0] // PACK_COLS)
    rows = -(-rows // 8) * 8
    return jnp.pad(flat, (0, rows * PACK_COLS - flat.shape[0])).reshape(rows, PACK_COLS)


def _unpack_rows(buf, shapes):
    flat = buf.reshape(-1)
    out, at = [], 0
    for s in shapes:
        n = int(np.prod(s))
        out.append(flat[at:at + n].reshape(s))
        at += n
    return out


def _full_weight(t, axis):
    if axis == 1:
        return jnp.moveaxis(t, 0, 1).reshape(t.shape[1], N_DEV * t.shape[2], t.shape[3])
    return jnp.moveaxis(t, 0, 2).reshape(t.shape[1], t.shape[2], N_DEV * t.shape[3])


def _disassemble(d, groups, dtype):
    heads = lambda t: t.reshape(HEADS, -1, t.shape[1])
    full = {}
    for tag in ('ffn1', 'ffn2'):
        if f'{tag}_gate' in d:
            full.update({f'{tag}_w_gate': d[f'{tag}_gate'], f'{tag}_w_up': d[f'{tag}_up'], f'{tag}_w_down': d[f'{tag}_down']})
    if 'w_in' in d:
        w_in, uq, ukv = d['w_in'], d['w_uq'], d['w_ukv']
        full.update(
            w_in=jnp.concatenate([w_in[a:a + n] for a, n in PIN_PIECES], axis=0),
            mla_w_uq=jnp.concatenate([heads(uq[:HEADS * NOPE]), heads(uq[HEADS * NOPE:HEADS * NOPE + LANES]),
                                      heads(uq[HEADS * NOPE + LANES:])], axis=1).reshape(HEADS * QK_DIM, Q_RANK),
            mla_w_ukv=jnp.concatenate([heads(ukv[:HEADS * NOPE]), heads(ukv[HEADS * NOPE:])],
                                      axis=1).reshape(HEADS * (NOPE + V_DIM), KV_RANK),
            mla_w_o=d['mla_w_o'], conv_w_out=d['conv_w_out'], s5_w_glu=d['s5_w_glu'], s5_w_out=d['s5_w_out'], w_o=d['w_o'])
    return [jnp.concatenate([full[n].reshape(N_DEV, -1, full[n].shape[-1]).astype(dtype) for n in grp], axis=1)
            for grp in groups]


def _assemble(gathered, groups, shard_shapes):
    full = {n: t.reshape(N_DEV * t.shape[1], t.shape[2])
            for n, t in _unpack_groups(gathered, groups, shard_shapes).items()}
    out = {}
    for tag in ('ffn1', 'ffn2'):
        if f'{tag}_w_gate' in full:
            out.update({f'{tag}_gate': full[f'{tag}_w_gate'], f'{tag}_up': full[f'{tag}_w_up'],
                        f'{tag}_down': full[f'{tag}_w_down']})
    if 'w_in' not in full:
        return out
    w_in = full['w_in']
    cuts = np.cumsum((0,) + IN_SPLITS)
    cq, ckv, kr, xbar, bg, cg, u, gates = [w_in[a:b] for a, b in zip(cuts[:-1], cuts[1:])]
    pad = lambda t, n: jnp.pad(t, ((0, n - t.shape[0]), (0, 0)))
    w_in_packed = jnp.concatenate(
        [cq, ckv, pad(kr[:HALF_ROPE], LANES), pad(kr[HALF_ROPE:], LANES), xbar, bg, cg, u, gates,
         jnp.zeros((D_IN_PAD - PIN_END, D_MODEL), w_in.dtype)], axis=0)
    uq = full['mla_w_uq'].reshape(HEADS, QK_DIM, Q_RANK)
    w_uq = jnp.concatenate([uq[:, :NOPE].reshape(HEADS * NOPE, Q_RANK),
                            uq[:, NOPE:NOPE + HALF_ROPE].reshape(HEADS * HALF_ROPE, Q_RANK),
                            uq[:, NOPE + HALF_ROPE:].reshape(HEADS * HALF_ROPE, Q_RANK)], axis=0)
    ukv = full['mla_w_ukv'].reshape(HEADS, NOPE + V_DIM, KV_RANK)
    w_ukv = jnp.concatenate([ukv[:, :NOPE].reshape(HEADS * NOPE, KV_RANK),
                             ukv[:, NOPE:].reshape(HEADS * V_DIM, KV_RANK)], axis=0)
    out.update(w_in=w_in_packed, w_uq=w_uq, w_ukv=w_ukv, mla_w_o=full['mla_w_o'], conv_w_out=full['conv_w_out'],
               s5_w_glu=full['s5_w_glu'], s5_w_out=full['s5_w_out'], w_o=full['w_o'])
    return out


def _s5_discretize(a_re, a_im, log_dt, b_re, b_im, c_re, c_im):
    dt = jnp.exp(log_dt)[:, None]
    mag = jnp.exp(dt * a_re)
    ab_re, ab_im = mag * jnp.cos(dt * a_im), mag * jnp.sin(dt * a_im)
    den = a_re * a_re + a_im * a_im
    nr, ni = ab_re - 1.0, ab_im
    coef_re = (nr * a_re + ni * a_im) / den
    coef_im = (ni * a_re - nr * a_im) / den
    bb_re = coef_re[..., None] * b_re - coef_im[..., None] * b_im
    bb_im = coef_re[..., None] * b_im + coef_im[..., None] * b_re
    unit = jnp.arange(MIX)[:, None]
    chan = jnp.arange(2 * S5_CH)[None, :]
    pair = jnp.arange(2 * S5_STATE)[:, None]
    own = unit // S5_GROUP == (chan % S5_CH) // S5_STATE
    copy = jnp.logical_and(pair // S5_STATE == chan // S5_CH, pair % S5_STATE == chan % S5_STATE).astype(F32)
    flat_b = lambda bb: bb.transpose(0, 2, 1).reshape(MIX, S5_STATE)
    flat_c = lambda cc: cc.transpose(2, 0, 1).reshape(S5_STATE, MIX)
    b_small = jnp.concatenate([flat_b(bb_re), flat_b(bb_im)], axis=1)
    c_small = jnp.concatenate([flat_c(c_re), -flat_c(c_im)], axis=0)
    b_map = jnp.where(own, _make_mm_f32w("s5_spread_b")(b_small, copy), 0.0)
    c_map = jnp.where(own.T, _make_mm_f32w("s5_spread_c")(copy.T, c_small), 0.0)
    return ab_re.reshape(SCAN_ROWS, LANES), ab_im.reshape(SCAN_ROWS, LANES), b_map, c_map


def _rope_tables(length):
    inv_freq = ROPE_BASE ** (-jnp.arange(0, ROPE, 2, dtype=F32) / ROPE)
    ang = jnp.arange(length).astype(F32)[:, None] * inv_freq[None, :]
    return jnp.tile(jnp.cos(ang), (1, LANES // HALF_ROPE)), jnp.tile(jnp.sin(ang), (1, LANES // HALF_ROPE))


def _heads_first(t):
    return t.reshape(t.shape[0], HEADS, -1).transpose(1, 0, 2)


def _local_loss(diff, big, n_real):
    small, wz = diff['small'], diff['wz']
    h = diff['h0']
    length = h.shape[0]
    cos, sin = _rope_tables(length)
    row2 = lambda v: v.reshape(1, -1)
    for l in range(DEPTH):
        w, z = big[l], wz[l]
        p = {k: small[k][l] for k in small if k != 'meta'}
        ffn = lambda h, tag, ln: _make_ffn(tag)(
            h, *[w[f'{tag}_{k}'] for k in ('gate', 'up', 'down')], *[z[f'{tag}_{k}'] for k in ('gate', 'up', 'down')],
            row2(p[f'{ln}_g']), row2(p[f'{ln}_b']))
        h = ffn(h, "ffn1", "ln1")
        cq, ckv, kr1, kr2, xbar, bg, cg, u, gate_a, gate_b, gate_c, _ = _make_projection("w_in", PIN_CUTS)(
            h, w['w_in'], z['w_in'])
        qn, = _make_rowwise(_f_rms, 1, 1, (BF16,), "q_rms")(cq, row2(p['mla_q_norm_g']))
        kvn, = _make_rowwise(_f_rms, 1, 1, (BF16,), "kv_rms")(ckv, row2(p['mla_kv_norm_g']))
        q_nope, q1, q2 = _make_split((0, HEADS * NOPE, HEADS * NOPE + LANES, HEADS * NOPE + 2 * LANES))(
            _make_mm("w_uq", wt=True)(qn, w['w_uq'], z['w_uq']))
        k_nope, val = _make_split((0, HEADS * NOPE, HEADS * (NOPE + V_DIM)))(
            _make_mm("w_ukv", wt=True)(kvn, w['w_ukv'], z['w_ukv']))
        rope = _make_rowwise(_f_rope, 4, 0, (BF16, BF16), "rope", n_nodiff=2)
        q1, q2 = rope(q1, q2, cos, sin)
        k1, k2 = rope(kr1, kr2, cos, sin)
        hpad = jnp.zeros((HEADS, length, HEAD_PAD - QK_DIM), BF16)
        q3 = jnp.concatenate([_heads_first(q_nope.astype(BF16)), _heads_first(q1), _heads_first(q2), hpad], -1)
        shared = lambda t: jnp.broadcast_to(t[None, :, :HALF_ROPE], (HEADS, length, HALF_ROPE))
        k3 = jnp.concatenate([_heads_first(k_nope.astype(BF16)), shared(k1), shared(k2), hpad], -1)
        v3 = _heads_first(val.astype(BF16))
        o3 = _attention(q3, k3, v3)
        y_a = _make_mm("mla_w_o", wt=True)(o3.transpose(1, 0, 2).reshape(length, MIX), w['mla_w_o'], z['mla_w_o'])
        conv = _short_conv(xbar, bg, cg, p['conv_w_full'], row2(p['conv_b']))
        y_b = _make_mm("conv_w_out", wt=True)(conv, w['conv_w_out'], z['conv_w_out'])
        ar, ai, b_map, c_map = _s5_discretize(p['s5_a_re'], p['s5_a_im'], p['s5_log_dt'], p['s5_b_re'], p['s5_b_im'],
                                              p['s5_c_re'], p['s5_c_im'])
        bu = _make_bd_in("s5_b")(u, b_map)
        states = _s5_scan(ar, ai, bu.reshape(length, 2 * SCAN_ROWS, LANES)).reshape(length, 2 * S5_CH)
        y_ssm = _make_bd_out("s5_c")(states, c_map)
        zed, = _make_rowwise(_f_gelu_skip, 2, 1, (F32,), "s5_gelu")(y_ssm, u, row2(p['s5_d']))
        t = _make_mm("s5_w_glu")(zed, w['s5_w_glu'], z['s5_w_glu'])
        glu, = _make_rowwise(_f_glu, 2, 1, (BF16,), "s5_glu")(zed, t, row2(p['s5_b_glu']))
        y_c = _make_mm("s5_w_out", wt=True)(glu, w['s5_w_out'], z['s5_w_out'])
        mixed, = _make_rowwise(_f_merge, 6, 0, (BF16,), "merge")(gate_a, gate_b, gate_c, y_a, y_b, y_c)
        mix_out = _make_mm("w_o")(mixed, w['w_o'], z['w_o'])
        h, = _make_rowwise(_f_ln_full, 2, 2, (F32,), "mix_ln")(h, mix_out, row2(p['ln2_g']), row2(p['ln2_b']))
        h = ffn(h, "ffn2", "ln3")
    return _make_loss(n_real)(h, diff['target'])


def kernel(x, meta, ffn1_w_gate, ffn1_w_up, ffn1_w_down, ln1_g, ln1_b, w_in, mla_q_norm_g, mla_w_uq, mla_kv_norm_g, mla_w_ukv, mla_w_o, conv_w, conv_b, conv_w_out, s5_a_re, s5_a_im, s5_log_dt, s5_b_re, s5_b_im, s5_c_re, s5_c_im, s5_d, s5_w_glu, s5_b_glu, s5_w_out, w_o, ln2_g, ln2_b, ffn2_w_gate, ffn2_w_up, ffn2_w_down, ln3_g, ln3_b, loss_target, m_meta, m_ffn1_w_gate, m_ffn1_w_up, m_ffn1_w_down, m_ln1_g, m_ln1_b, m_w_in, m_mla_q_norm_g, m_mla_w_uq, m_mla_kv_norm_g, m_mla_w_ukv, m_mla_w_o, m_conv_w, m_conv_b, m_conv_w_out, m_s5_a_re, m_s5_a_im, m_s5_log_dt, m_s5_b_re, m_s5_b_im, m_s5_c_re, m_s5_c_im, m_s5_d, m_s5_w_glu, m_s5_b_glu, m_s5_w_out, m_w_o, m_ln2_g, m_ln2_b, m_ffn2_w_gate, m_ffn2_w_up, m_ffn2_w_down, m_ln3_g, m_ln3_b, v_meta, v_ffn1_w_gate, v_ffn1_w_up, v_ffn1_w_down, v_ln1_g, v_ln1_b, v_w_in, v_mla_q_norm_g, v_mla_w_uq, v_mla_kv_norm_g, v_mla_w_ukv, v_mla_w_o, v_conv_w, v_conv_b, v_conv_w_out, v_s5_a_re, v_s5_a_im, v_s5_log_dt, v_s5_b_re, v_s5_b_im, v_s5_c_re, v_s5_c_im, v_s5_d, v_s5_w_glu, v_s5_b_glu, v_s5_w_out, v_w_o, v_ln2_g, v_ln2_b, v_ffn2_w_gate, v_ffn2_w_up, v_ffn2_w_down, v_ln3_g, v_ln3_b):
    args = locals()
    w = {n: args[n] for n in WEIGHT_NAMES}
    m = {n: args["m_" + n] for n in WEIGHT_NAMES}
    v = {n: args["v_" + n] for n in WEIGHT_NAMES}
    me = 4 * lax.axis_index("x") + 2 * lax.axis_index("y") + lax.axis_index("c")
    seq = x.shape[1]
    n_real = N_META + seq
    length = -(-n_real // LANES) * LANES

    shard_shapes = {n: w[n].shape for n in BIG}
    small_shards = _all_gather([_pack_rows([w[n] for n in SMALL_SHARDED])], "gather_small",
                               sequencer=True)[0].reshape(N_DEV, -1)
    big = [{} for _ in range(DEPTH)]
    for l in range(DEPTH):
        for p, groups in enumerate(PIECES):
            packed = _pack_groups(w, l, groups, BF16)
            if (l, p) == (0, 1):
                packed = lax.optimization_barrier((gathered, packed))[1]
            gathered = _all_gather(packed, f"gather_weights_layer{l}_piece{p}", sequencer=True)
            big[l].update(_assemble(gathered, groups, shard_shapes))
    meta_full = _full_weight(small_shards[:, :meta.size].reshape(N_DEV, 1, *meta.shape), 2)[0]
    conv_w_full = _full_weight(small_shards[:, meta.size:meta.size + conv_w.size].reshape(N_DEV, *conv_w.shape), 2)

    small = {n: w[n] for n in SMALL_NAMES if n not in SMALL_SHARDED}
    small['conv_w_full'] = conv_w_full
    small['meta'] = meta_full
    wz = jax.tree.map(lambda t: jnp.zeros(t.shape, BF16), big)
    pad_rows = length - n_real

    def loss_fn(diff):
        h0 = jnp.concatenate([diff['small']['meta'], diff['x'], jnp.zeros((pad_rows, D_MODEL), F32)], axis=0)
        target = jnp.pad(loss_target[0], ((N_META, pad_rows), (0, 0)))
        return _local_loss(dict(h0=h0, small=diff['small'], wz=diff['wz'], target=target), big, n_real)

    loss_local, grads = jax.value_and_grad(loss_fn)(dict(x=x[0], small=small, wz=wz))

    small_names = [n for n in SMALL_NAMES if n not in SMALL_SHARDED] + ['conv_w_full', 'meta']
    small_flat = _pack_rows([loss_local.reshape(1)] + [grads['small'][n] for n in small_names])
    rows_each = -(-small_flat.shape[0] // (8 * N_DEV)) * 8
    small_flat = jnp.pad(small_flat, ((0, rows_each * N_DEV - small_flat.shape[0]), (0, 0)))
    layer_sums = [{} for _ in range(DEPTH)]
    for l in reversed(range(DEPTH)):
        for p, groups in reversed(list(enumerate(PIECES))):
            last = l == 0 and p == 0
            extra = [small_flat.reshape(N_DEV, rows_each, PACK_COLS)] if last else []
            sums = _reduce_scatter(_disassemble(grads['wz'][l], groups, BF16) + extra, f"grads_layer{l}_piece{p}_",
                                   sequencer=True)
            layer_sums[l].update(_unpack_groups(sums[:len(groups)], groups, shard_shapes))
            if last:
                small_sum = sums[-1]
    small_all, = _all_gather([small_sum], "gather_small_grads")
    loss, *small_sums = _unpack_rows(small_all, [()] + [grads['small'][n].shape for n in small_names])
    small_grads = dict(zip(small_names, small_sums))
    g = {}
    for name in SMALL_NAMES:
        if name == 'meta':
            g[name] = lax.dynamic_slice_in_dim(small_grads['meta'], me * meta.shape[1], meta.shape[1], axis=1)
        elif name == 'conv_w':
            g[name] = lax.dynamic_slice_in_dim(small_grads['conv_w_full'], me * conv_w.shape[2], conv_w.shape[2], axis=2)
        else:
            g[name] = small_grads[name]

    delta, new_m, new_v = {}, {}, {}
    for name in BIG:
        per_layer = [s[name].T if name in TRANSPOSED else s[name] for s in layer_sums]
        g[name], delta[name], new_m[name], new_v[name] = _adamw_layers(w[name], per_layer, m[name], v[name], f"adamw_{name}")
    for name in SMALL_NAMES:
        two_d = lambda t: t.reshape(-1, t.shape[-1])
        d, nm, nv = _adamw(two_d(w[name]), two_d(g[name]), two_d(m[name]), two_d(v[name]), f"adamw_{name}")
        delta[name], new_m[name], new_v[name] = (t.reshape(w[name].shape) for t in (d, nm, nv))

    return (loss, grads['x'][None], *[g[n] for n in WEIGHT_NAMES], *[delta[n] for n in WEIGHT_NAMES],
            *[new_m[n] for n in WEIGHT_NAMES], *[new_v[n] for n in WEIGHT_NAMES])
```
